```python
import math
import numpy as np
import jax
import jax.numpy as jnp
from jax import lax

D_MODEL = 1024
BATCH = 8
SEQ = 8192
DEPTH = 2
DEC_BATCH = 2
DEC_SEQ = 8192
PAST_LEN = 128

DN_HEADS = 4
DN_HEAD_DIM = 128
DN_WIDTH = DN_HEADS * DN_HEAD_DIM
DN_CONV = 4
DN_CHUNK = 64
S5_GROUP = 16
S5_WIDTH = 512
S5_GROUPS = S5_WIDTH // S5_GROUP
S5_STATE = 64
DF_HEADS = 4
DF_QK_DIM = 64
DF_V_DIM = 128
DF_WIDTH = DF_HEADS * DF_V_DIM
Q_BLOCK = 128
ROPE_THETA = 10000.0
MEM_LEN = 256
XA_HEADS = 4
XA_HEAD_DIM = 128
XA_WIDTH = XA_HEADS * XA_HEAD_DIM
N_EXPERTS = 32
TOP_K = 4
D_FF = 1024
SWIGLU_LIMIT = 7.0
SWIGLU_ALPHA = 1.702
MOE_BLOCK = 256
N_BRANCH = 3
DEEPNORM_ALPHA = (2 * DEPTH) ** 0.25
DEEPNORM_BETA = (8 * DEPTH) ** -0.25
LN_EPS = 1e-5
RMS_EPS = 1e-6
IN_SPLITS = (DN_WIDTH, DN_WIDTH, DN_WIDTH, DN_WIDTH, 2 * DN_HEADS, 2 * DN_HEADS, S5_WIDTH,
             DF_HEADS * DF_QK_DIM, DF_HEADS * DF_QK_DIM, DF_HEADS * DF_QK_DIM, DF_HEADS * DF_QK_DIM,
             DF_WIDTH, N_BRANCH * D_MODEL)
D_IN = 4 * DN_WIDTH + 4 * DN_HEADS + S5_WIDTH + 4 * DF_HEADS * DF_QK_DIM + DF_WIDTH + N_BRANCH * D_MODEL

kernel_name = 'hybrid_deltanet_s5_diffattn_moe_encoder'


def layer_norm(x, g, b):
    xf = x.astype(jnp.float32)
    mu = jnp.mean(xf, -1, keepdims=True)
    var = jnp.mean(jnp.square(xf - mu), -1, keepdims=True)
    return ((xf - mu) * lax.rsqrt(var + LN_EPS) * g.astype(jnp.float32) + b.astype(jnp.float32)).astype(x.dtype)


def rms_norm(x, g):
    xf = x.astype(jnp.float32)
    return xf * lax.rsqrt(jnp.mean(jnp.square(xf), -1, keepdims=True) + RMS_EPS) * g.astype(jnp.float32)


def l2_normalize(x):
    return x * lax.rsqrt(jnp.sum(jnp.square(x), -1, keepdims=True) + 1e-6)


def rope(x, pos):
    half = x.shape[-1] // 2
    inv_freq = jnp.power(ROPE_THETA, -jnp.arange(half, dtype=jnp.float32) / half)
    ang = pos[:, None] * inv_freq[None, :]
    cos = jnp.cos(ang)[None, :, None, :]
    sin = jnp.sin(ang)[None, :, None, :]
    xf = x.astype(jnp.float32)
    x1, x2 = xf[..., :half], xf[..., half:]
    return jnp.concatenate([x1 * cos - x2 * sin, x2 * cos + x1 * sin], -1).astype(x.dtype)


def short_conv(x, w):
    ch = x.shape[-1]
    left = DN_CONV // 2
    return lax.conv_general_dilated(x, w[:, None, :].astype(x.dtype), window_strides=(1,),
                                    padding=[(left, DN_CONV - 1 - left)],
                                    dimension_numbers=('NWC', 'WIO', 'NWC'),
                                    feature_group_count=ch)


def gated_delta_chunked(q, k, v, g, beta):
    bn, hn, sn, dk = q.shape
    dv = v.shape[-1]
    nc = sn // DN_CHUNK
    q = q.reshape(bn, hn, nc, DN_CHUNK, dk)
    k = k.reshape(bn, hn, nc, DN_CHUNK, dk)
    v = v.reshape(bn, hn, nc, DN_CHUNK, dv)
    g = jnp.cumsum(g.reshape(bn, hn, nc, DN_CHUNK), axis=-1)
    beta = beta.reshape(bn, hn, nc, DN_CHUNK)
    kb = k * beta[..., None]
    vb = v * beta[..., None]
    tri = jnp.tril(jnp.ones((DN_CHUNK, DN_CHUNK), dtype=bool))
    diff = g[..., :, None] - g[..., None, :]
    decay = jnp.where(tri, jnp.exp(jnp.where(tri, diff, 0.0)), 0.0)
    strict = jnp.tril(jnp.einsum('bhncd,bhnmd->bhncm', kb, k) * decay, -1)
    eye = jnp.eye(DN_CHUNK, dtype=jnp.float32)
    tmat = lax.linalg.triangular_solve(eye + strict, jnp.broadcast_to(eye, strict.shape),
                                       left_side=True, lower=True, unit_diagonal=True)
    u = jnp.einsum('bhncm,bhnme->bhnce', tmat, vb)
    w = jnp.einsum('bhncm,bhnmd->bhncd', tmat, kb * jnp.exp(g)[..., None])
    intra = jnp.einsum('bhncd,bhnmd->bhncm', q, k) * decay
    qg = q * jnp.exp(g)[..., None]
    g_last = g[..., -1]
    kd = k * jnp.exp(g_last[..., None] - g)[..., None]
    xs = tuple(jnp.moveaxis(t, 2, 0) for t in (qg, kd, u, w, intra, g_last))

    def step(state, inp):
        qg_i, kd_i, u_i, w_i, intra_i, gl_i = inp
        v_new = u_i - jnp.einsum('bhcd,bhde->bhce', w_i, state)
        o_i = jnp.einsum('bhcd,bhde->bhce', qg_i, state) + jnp.einsum('bhcm,bhme->bhce', intra_i, v_new)
        state = state * jnp.exp(gl_i)[..., None, None] + jnp.einsum('bhcd,bhce->bhde', kd_i, v_new)
        return state, o_i

    s0 = jnp.zeros((bn, hn, dk, dv), jnp.float32)
    _, o = lax.scan(step, s0, xs)
    return jnp.moveaxis(o, 0, 2).reshape(bn, hn, sn, dv)


def deltanet_branch(q, k, v, z, b_pre, a_pre, conv_w, a_log, dt_bias, norm_g):
    bn, sn, _ = q.shape
    qkv = jax.nn.silu(short_conv(jnp.concatenate([q, k, v], -1), conv_w))
    q, k, v = jnp.split(qkv, 3, axis=-1)

    def heads(t):
        return t.astype(jnp.float32).reshape(bn, sn, DN_HEADS, DN_HEAD_DIM).transpose(0, 2, 1, 3)

    q = l2_normalize(heads(q)) * DN_HEAD_DIM ** -0.5
    k = l2_normalize(heads(k))
    v = heads(v)
    beta = jax.nn.sigmoid(b_pre.astype(jnp.float32)).reshape(bn, sn, 2, DN_HEADS).transpose(2, 0, 3, 1)
    a_in = a_pre.astype(jnp.float32).reshape(bn, sn, 2, DN_HEADS).transpose(2, 0, 3, 1)
    gdec = -jnp.exp(a_log.astype(jnp.float32))[:, None, :, None] * jax.nn.softplus(
        a_in + dt_bias.astype(jnp.float32)[:, None, :, None])
    o_f = gated_delta_chunked(q, k, v, gdec[0], beta[0])

    def fl(t):
        return jnp.flip(t, axis=2)

    o_b = fl(gated_delta_chunked(fl(q), fl(k), fl(v), fl(gdec[1]), fl(beta[1])))
    o = (o_f + o_b).transpose(0, 2, 1, 3)
    zz = z.astype(jnp.float32).reshape(bn, sn, DN_HEADS, DN_HEAD_DIM)
    o = rms_norm(o, norm_g) * jax.nn.silu(zz)
    return o.reshape(bn, sn, DN_WIDTH).astype(z.dtype)


def _ssm_combine(e1, e2):
    a1, b1 = e1
    a2, b2 = e2
    return a1 * a2, a2 * b1 + b2


def s5_branch(u, lam_re, lam_im, log_dt, b_re, b_im, c_re, c_im, d_skip, glu_w, glu_b):
    bn, sn, _ = u.shape
    uf = u.astype(jnp.float32).reshape(bn, sn, S5_GROUPS, S5_GROUP)
    y = uf * d_skip.astype(jnp.float32).reshape(S5_GROUPS, S5_GROUP)
    uc = uf.astype(jnp.complex64)
    for d in range(2):
        lam = lax.complex(lam_re[d].astype(jnp.float32), lam_im[d].astype(jnp.float32))
        dt = jnp.exp(log_dt[d].astype(jnp.float32))[:, None]
        lam_bar = jnp.exp(lam * dt)
        b_bar = ((lam_bar - 1.0) / lam)[:, :, None] * lax.complex(b_re[d].astype(jnp.float32),
                                                                 b_im[d].astype(jnp.float32))
        bu = jnp.einsum('bsgc,gpc->bsgp', uc, b_bar)
        a = jnp.broadcast_to(lam_bar[None, None], (1, sn, S5_GROUPS, S5_STATE))
        _, states = lax.associative_scan(_ssm_combine, (a, bu), reverse=(d == 1), axis=1)
        c = lax.complex(c_re[d].astype(jnp.float32), c_im[d].astype(jnp.float32))
        y = y + jnp.real(jnp.einsum('bsgp,gcp->bsgc', states, c))
    y = jax.nn.gelu(y.reshape(bn, sn, S5_WIDTH))
    y = y * jax.nn.sigmoid(y @ glu_w.astype(jnp.float32) + glu_b.astype(jnp.float32))
    return y.astype(u.dtype)


def diff_attention_branch(q1, q2, k1, k2, v, lq1, lk1, lq2, lk2, norm_g, lam_init, pos):
    bn, sn, _ = q1.shape

    def heads(t, dh):
        return t.reshape(bn, sn, DF_HEADS, dh)

    q1 = rope(heads(q1, DF_QK_DIM), pos)
    q2 = rope(heads(q2, DF_QK_DIM), pos)
    k1 = rope(heads(k1, DF_QK_DIM), pos)
    k2 = rope(heads(k2, DF_QK_DIM), pos)
    v = heads(v, DF_V_DIM)
    lam = (jnp.exp(jnp.sum(lq1.astype(jnp.float32) * lk1.astype(jnp.float32)))
           - jnp.exp(jnp.sum(lq2.astype(jnp.float32) * lk2.astype(jnp.float32))) + lam_init)
    scale = DF_QK_DIM ** -0.5
    nb = sn // Q_BLOCK

    def blk(t):
        return jnp.moveaxis(t.reshape(bn, nb, Q_BLOCK, DF_HEADS, DF_QK_DIM), 1, 0)

    def one_block(qs):
        qa, qb = qs
        p1 = jax.nn.softmax(jnp.einsum('bqhd,bkhd->bhqk', qa, k1).astype(jnp.float32) * scale, axis=-1)
        p2 = jax.nn.softmax(jnp.einsum('bqhd,bkhd->bhqk', qb, k2).astype(jnp.float32) * scale, axis=-1)
        att = (p1 - lam * p2).astype(v.dtype)
        return jnp.einsum('bhqk,bkhe->bqhe', att, v)

    o = lax.map(one_block, (blk(q1), blk(q2)))
    o = jnp.moveaxis(o, 0, 1).reshape(bn, sn, DF_HEADS, DF_V_DIM)
    o = rms_norm(o, norm_g) * (1.0 - lam_init)
    return o.reshape(bn, sn, DF_WIDTH).astype(v.dtype)


def memory_cross_attention(x, mem, wq, wkv, wo):
    bn, sn, _ = x.shape
    q = (x @ wq).reshape(bn, sn, XA_HEADS, XA_HEAD_DIM)
    kv = (mem @ wkv).reshape(bn, mem.shape[1], 2, XA_HEADS, XA_HEAD_DIM)
    k, v = kv[:, :, 0], kv[:, :, 1]
    s = jnp.einsum('bqhd,bkhd->bhqk', q, k).astype(jnp.float32) * XA_HEAD_DIM ** -0.5
    p = jax.nn.softmax(s, axis=-1).astype(x.dtype)
    o = jnp.einsum('bhqk,bkhd->bqhd', p, v).reshape(bn, sn, XA_WIDTH)
    return o @ wo


def moe(x, router_w, router_b, w_gu, b_gu, w_down, b_down):
    bn, sn, dm = x.shape
    nt = bn * sn
    xt = x.reshape(nt, dm)
    logits = (xt @ router_w).astype(jnp.float32) + router_b.astype(jnp.float32)
    top_v, top_i = lax.top_k(logits, TOP_K)
    gate = jax.nn.softmax(top_v, axis=-1).astype(x.dtype)
    m = nt * TOP_K
    nblk = -(-m // MOE_BLOCK) + N_EXPERTS
    rows = nblk * MOE_BLOCK
    e_flat = top_i.reshape(m)
    w_flat = gate.reshape(m)
    order = jnp.argsort(e_flat)
    e_sorted = e_flat[order]
    tok_sorted = (order // TOP_K).astype(jnp.int32)
    counts = jnp.bincount(e_flat, length=N_EXPERTS)
    padded = (counts + MOE_BLOCK - 1) // MOE_BLOCK * MOE_BLOCK
    pad_end = jnp.cumsum(padded)
    pad_start = pad_end - padded
    cnt_start = jnp.cumsum(counts) - counts
    dest = pad_start[e_sorted] + jnp.arange(m, dtype=jnp.int32) - cnt_start[e_sorted]
    row_tok = jnp.full((rows,), nt, jnp.int32).at[dest].set(tok_sorted)
    row_w = jnp.zeros((rows,), x.dtype).at[dest].set(w_flat[order])
    blk_e = jnp.minimum(jnp.searchsorted(pad_end, jnp.arange(nblk, dtype=pad_end.dtype) * MOE_BLOCK,
                                         side='right'), N_EXPERTS - 1)
    x_rows = jnp.concatenate([xt, jnp.zeros((1, dm), xt.dtype)], 0)[row_tok].reshape(nblk, MOE_BLOCK, dm)

    def expert_block(args):
        xb, e = args
        h = xb @ w_gu[e] + b_gu[e]
        gt = jnp.minimum(h[:, 0::2], SWIGLU_LIMIT)
        up = jnp.clip(h[:, 1::2], -SWIGLU_LIMIT, SWIGLU_LIMIT)
        act = (up + 1.0) * gt * jax.nn.sigmoid(SWIGLU_ALPHA * gt)
        return act @ w_down[e] + b_down[e]

    y_rows = lax.map(expert_block, (x_rows, blk_e)).reshape(rows, dm) * row_w[:, None]
    y = jnp.zeros((nt + 1, dm), x.dtype).at[row_tok].add(y_rows)[:nt]
    return y.reshape(bn, sn, dm)


def encoder_layer(x, mem, l, pos, w_in, b_gate, dn_conv, dn_A_log, dn_dt_bias, dn_norm,
                  s5_lam_re, s5_lam_im, s5_log_dt, s5_B_re, s5_B_im, s5_C_re, s5_C_im, s5_D,
                  s5_glu_w, s5_glu_b, df_lq1, df_lk1, df_lq2, df_lk2, df_norm, w_branch, w_out,
                  ln1_g, ln1_b, xa_wq, xa_wkv, xa_wo, ln2_g, ln2_b, router_w, router_b,
                  e_w_gu, e_b_gu, e_w_down, e_b_down, ln3_g, ln3_b):
    bn, sn, dm = x.shape
    split_points = [int(i) for i in np.cumsum(IN_SPLITS)[:-1]]
    proj = x @ w_in[l]
    (dn_q, dn_k, dn_v, dn_z, dn_b, dn_a, s5_u, dq1, dq2, dk1, dk2, dv,
     gate_pre) = jnp.split(proj, split_points, axis=-1)
    o_a = deltanet_branch(dn_q, dn_k, dn_v, dn_z, dn_b, dn_a, dn_conv[l], dn_A_log[l], dn_dt_bias[l], dn_norm[l])
    o_b = s5_branch(s5_u, s5_lam_re[l], s5_lam_im[l], s5_log_dt[l], s5_B_re[l], s5_B_im[l],
                    s5_C_re[l], s5_C_im[l], s5_D[l], s5_glu_w[l], s5_glu_b[l])
    lam_init = 0.8 - 0.6 * math.exp(-0.3 * l)
    o_c = diff_attention_branch(dq1, dq2, dk1, dk2, dv, df_lq1[l], df_lk1[l], df_lq2[l], df_lk2[l],
                                df_norm[l], lam_init, pos)
    gates = jax.nn.sigmoid(gate_pre + b_gate[l]).reshape(bn, sn, N_BRANCH, dm)
    merged = (gates[:, :, 0] * (o_a @ w_branch[l, 0]) + gates[:, :, 1] * (o_b @ w_branch[l, 1])
              + gates[:, :, 2] * (o_c @ w_branch[l, 2]))
    x = layer_norm(DEEPNORM_ALPHA * x + merged @ w_out[l], ln1_g[l], ln1_b[l])
    h = memory_cross_attention(x, mem, xa_wq[l], xa_wkv[l], xa_wo[l])
    x = layer_norm(DEEPNORM_ALPHA * x + h, ln2_g[l], ln2_b[l])
    h = moe(x, router_w[l], router_b[l], e_w_gu[l], e_b_gu[l], e_w_down[l], e_b_down[l])
    x = layer_norm(DEEPNORM_ALPHA * x + h, ln3_g[l], ln3_b[l])
    return x


def setup_inputs(seed: int = 0) -> dict:
    key = jax.random.key(seed)
    keys = jax.random.split(key, 64)
    counter = [0]

    def nk():
        k = keys[counter[0]]
        counter[0] += 1
        return k

    def nrm(shape, scale):
        return scale * jax.random.normal(nk(), shape, jnp.float32)

    def unif(shape, lo, hi):
        return jax.random.uniform(nk(), shape, jnp.float32, lo, hi)

    L = DEPTH
    D = D_MODEL
    dn_dt = jnp.exp(unif((L, 2, DN_HEADS), math.log(1e-3), math.log(1e-1)))
    inp = {}
    inp['x_prompt'] = nrm((BATCH, SEQ, D), 1.0)
    inp['x_sample'] = nrm((DEC_BATCH, DEC_SEQ, D), 1.0)
    inp['mem_prompt'] = nrm((BATCH, MEM_LEN, D), 1.0)
    inp['mem_sample'] = nrm((DEC_BATCH, MEM_LEN, D), 1.0)
    inp['w_in'] = nrm((L, D, D_IN), D ** -0.5)
    inp['b_gate'] = nrm((L, N_BRANCH * D), 0.02)
    inp['dn_conv'] = nrm((L, DN_CONV, 3 * DN_WIDTH), DN_CONV ** -0.5)
    inp['dn_A_log'] = jnp.log(unif((L, 2, DN_HEADS), 1.0, 16.0))
    inp['dn_dt_bias'] = dn_dt + jnp.log(-jnp.expm1(-dn_dt))
    inp['dn_norm'] = 1.0 + nrm((L, DN_HEAD_DIM), 0.02)
    inp['s5_lam_re'] = -0.5 + nrm((L, 2, S5_GROUPS, S5_STATE), 0.01)
    inp['s5_lam_im'] = math.pi * jnp.arange(S5_STATE, dtype=jnp.float32) + nrm((L, 2, S5_GROUPS, S5_STATE), 0.01)
    inp['s5_log_dt'] = unif((L, 2, S5_GROUPS), math.log(1e-3), math.log(1e-1))
    inp['s5_B_re'] = nrm((L, 2, S5_GROUPS, S5_STATE, S5_GROUP), (2 * S5_GROUP) ** -0.5)
    inp['s5_B_im'] = nrm((L, 2, S5_GROUPS, S5_STATE, S5_GROUP), (2 * S5_GROUP) ** -0.5)
    inp['s5_C_re'] = nrm((L, 2, S5_GROUPS, S5_GROUP, S5_STATE), S5_STATE ** -0.5)
    inp['s5_C_im'] = nrm((L, 2, S5_GROUPS, S5_GROUP, S5_STATE), S5_STATE ** -0.5)
    inp['s5_D'] = nrm((L, S5_WIDTH), 1.0)
    inp['s5_glu_w'] = nrm((L, S5_WIDTH, S5_WIDTH), S5_WIDTH ** -0.5)
    inp['s5_glu_b'] = nrm((L, S5_WIDTH), 0.02)
    inp['df_lq1'] = nrm((L, DF_QK_DIM), 0.1)
    inp['df_lk1'] = nrm((L, DF_QK_DIM), 0.1)
    inp['df_lq2'] = nrm((L, DF_QK_DIM), 0.1)
    inp['df_lk2'] = nrm((L, DF_QK_DIM), 0.1)
    inp['df_norm'] = 1.0 + nrm((L, DF_V_DIM), 0.02)
    inp['w_branch'] = nrm((L, N_BRANCH, DN_WIDTH, D), DN_WIDTH ** -0.5)
    inp['w_out'] = nrm((L, D, D), D ** -0.5 * DEEPNORM_BETA)
    inp['ln1_g'] = 1.0 + nrm((L, D), 0.02)
    inp['ln1_b'] = nrm((L, D), 0.02)
    inp['xa_wq'] = nrm((L, D, XA_WIDTH), D ** -0.5)
    inp['xa_wkv'] = nrm((L, D, 2 * XA_WIDTH), D ** -0.5)
    inp['xa_wo'] = nrm((L, XA_WIDTH, D), XA_WIDTH ** -0.5 * DEEPNORM_BETA)
    inp['ln2_g'] = 1.0 + nrm((L, D), 0.02)
    inp['ln2_b'] = nrm((L, D), 0.02)
    inp['router_w'] = nrm((L, D, N_EXPERTS), D ** -0.5)
    inp['router_b'] = nrm((L, N_EXPERTS), 0.01)
    inp['e_w_gu'] = nrm((L, N_EXPERTS, D, 2 * D_FF), D ** -0.5)
    inp['e_b_gu'] = nrm((L, N_EXPERTS, 2 * D_FF), 0.02)
    inp['e_w_down'] = nrm((L, N_EXPERTS, D_FF, D), D_FF ** -0.5 * DEEPNORM_BETA)
    inp['e_b_down'] = nrm((L, N_EXPERTS, D), 0.02)
    inp['ln3_g'] = 1.0 + nrm((L, D), 0.02)
    inp['ln3_b'] = nrm((L, D), 0.02)
    return inp


def reference(x_prompt, x_sample, mem_prompt, mem_sample, w_in, b_gate, dn_conv, dn_A_log, dn_dt_bias,
              dn_norm, s5_lam_re, s5_lam_im, s5_log_dt, s5_B_re, s5_B_im, s5_C_re, s5_C_im, s5_D,
              s5_glu_w, s5_glu_b, df_lq1, df_lk1, df_lq2, df_lk2, df_norm, w_branch, w_out,
              ln1_g, ln1_b, xa_wq, xa_wkv, xa_wo, ln2_g, ln2_b, router_w, router_b,
              e_w_gu, e_b_gu, e_w_down, e_b_down, ln3_g, ln3_b):
    def trunk(x, mem):
        pos = jnp.arange(x.shape[1], dtype=jnp.float32)
        for l in range(DEPTH):
            x = encoder_layer(x, mem, l, pos, w_in, b_gate, dn_conv, dn_A_log, dn_dt_bias, dn_norm,
                              s5_lam_re, s5_lam_im, s5_log_dt, s5_B_re, s5_B_im, s5_C_re, s5_C_im, s5_D,
                              s5_glu_w, s5_glu_b, df_lq1, df_lk1, df_lq2, df_lk2, df_norm, w_branch, w_out,
                              ln1_g, ln1_b, xa_wq, xa_wkv, xa_wo, ln2_g, ln2_b, router_w, router_b,
                              e_w_gu, e_b_gu, e_w_down, e_b_down, ln3_g, ln3_b)
        return x

    y_prompt = trunk(x_prompt, mem_prompt)
    y_sample = trunk(x_sample, mem_sample)
    return (y_prompt, y_sample)
```

```python
import functools
import math

import numpy as np
import jax
import jax.numpy as jnp
from jax import lax
from jax.experimental import pallas as pl
from jax.experimental.pallas import tpu as pltpu

F32 = jnp.float32
BF16 = jnp.bfloat16
HI = lax.Precision.HIGHEST

D_MODEL = 1024
DN_HEADS = 4
DN_DIM = 128
DN_WIDTH = DN_HEADS * DN_DIM
DN_CONV = 4
S5_GROUP = 16
S5_WIDTH = 512
S5_GROUPS = S5_WIDTH // S5_GROUP
S5_STATE = 64
DF_HEADS = 4
DF_QK = 64
DF_V = 128
DF_WIDTH = DF_HEADS * DF_V
ROPE_THETA = 10000.0
XA_HEADS = 4
XA_DIM = 128
XA_WIDTH = XA_HEADS * XA_DIM
N_EXPERTS = 32
TOP_K = 4
D_FF = 1024
SWIGLU_LIMIT = 7.0
SWIGLU_ALPHA = 1.702
N_BRANCH = 3
DEPTH = 2
DEEPNORM_ALPHA = (2 * DEPTH) ** 0.25
LN_EPS = 1e-5
RMS_EPS = 1e-6
LOG2E = 1.4426950408889634

LANES = 128
SUBLANES = 8
VMEM_LIMIT = 56 * 1024 * 1024

COL_DN_QKV = 0
COL_DN_Z = 1536
COL_S5_U = 2048
COL_QQ = 2560
COL_KK = 3072
COL_DV = 3584
COL_GATE = 4096
COL_BA = 7168
PROJ_WIDTH = 7296

TILE_MM = 1024
TILE_DN_PREP = 512
TILE_DN_SCAN = 256
DN_CHUNK = 64
S5_CHUNK = 64
TILE_ROPE = 512
TILE_Q = 256
TILE_K = 512
TILE_MERGE = 256
TILE_XA = 512
TILE_ROUTER = 512
MOE_BLOCK = 512
TILE_COMBINE = 256


def _in_proj_perm():
    o_dn_b, o_dn_a, o_s5 = 2048, 2056, 2064
    o_q1, o_q2, o_k1, o_k2, o_dv, o_gate = 2576, 2832, 3088, 3344, 3600, 4112
    perm = np.full((PROJ_WIDTH,), -1, np.int64)
    perm[0:2048] = np.arange(2048)
    perm[COL_S5_U:COL_S5_U + 512] = o_s5 + np.arange(512)
    for h in range(DF_HEADS):
        j = np.arange(DF_QK)
        perm[COL_QQ + h * 128 + j] = o_q1 + h * DF_QK + j
        perm[COL_QQ + h * 128 + 64 + j] = o_q2 + h * DF_QK + j
        perm[COL_KK + h * 128 + j] = o_k1 + h * DF_QK + j
        perm[COL_KK + h * 128 + 64 + j] = o_k2 + h * DF_QK + j
    perm[COL_DV:COL_DV + 512] = o_dv + np.arange(512)
    perm[COL_GATE:COL_GATE + 3072] = o_gate + np.arange(3072)
    perm[COL_BA:COL_BA + 8] = o_dn_b + np.arange(8)
    perm[COL_BA + 8:COL_BA + 16] = o_dn_a + np.arange(8)
    return perm


_IN_PERM = _in_proj_perm()


def _cparams(*sem):
    return pltpu.CompilerParams(dimension_semantics=sem, vmem_limit_bytes=VMEM_LIMIT)


def _bdot(a, b):
    return jnp.dot(a.astype(BF16), b.astype(BF16), preferred_element_type=F32)


def _bdot_nt(a, b):
    return lax.dot_general(a.astype(BF16), b.astype(BF16), (((1,), (1,)), ((), ())),
                           preferred_element_type=F32)


def _bdot_tn(a, b):
    return lax.dot_general(a.astype(BF16), b.astype(BF16), (((0,), (0,)), ((), ())),
                           preferred_element_type=F32)


def _sigmoid(x):
    return 1.0 / (1.0 + jnp.exp(-x))


def _silu(x):
    return x * _sigmoid(x)


def _mm_kernel(x_ref, w_ref, o_ref):
    o_ref[...] = jnp.dot(x_ref[...].astype(BF16), w_ref[...],
                         preferred_element_type=F32).astype(o_ref.dtype)


def _matmul(x, w, *, tm, tn, out_dtype=F32):
    n, k = x.shape
    m = w.shape[1]
    tm = min(tm, n)
    tn = min(tn, m)
    return pl.pallas_call(
        _mm_kernel,
        grid=(n // tm, m // tn),
        in_specs=[pl.BlockSpec((tm, k), lambda i, j: (i, 0)),
                  pl.BlockSpec((k, tn), lambda i, j: (0, j))],
        out_specs=pl.BlockSpec((tm, tn), lambda i, j: (i, j)),
        out_shape=jax.ShapeDtypeStruct((n, m), out_dtype),
        compiler_params=_cparams("parallel", "arbitrary"),
        name="matmul",
    )(x, w)


def _dn_prep_kernel(x_ref, prev_ref, next_ref, ba_ref, cw_ref, nega_ref, dtb_ref,
                    qkv_ref, bg_ref, *, tile, chunk):
    i = pl.program_id(1)
    last = pl.num_programs(1) - 1
    x = x_ref[0]
    prev = jnp.where(i > 0, prev_ref[0], 0.0)
    nxt = jnp.where(i < last, next_ref[0], 0.0)
    ext = jnp.concatenate([prev, x, nxt], axis=0)
    cw = cw_ref[...]
    acc = cw[0:1] * ext[6:6 + tile]
    for j in range(1, DN_CONV):
        acc = acc + cw[j:j + 1] * ext[6 + j:6 + j + tile]
    act = _silu(acc)
    for h in range(DN_HEADS):
        lo = h * DN_DIM
        qh = act[:, lo:lo + DN_DIM]
        qn = qh * lax.rsqrt(jnp.sum(qh * qh, axis=-1, keepdims=True) + 1e-6)
        qkv_ref[0, :, lo:lo + DN_DIM] = qn * (DN_DIM ** -0.5)
        kh = act[:, DN_WIDTH + lo:DN_WIDTH + lo + DN_DIM]
        qkv_ref[0, :, DN_WIDTH + lo:DN_WIDTH + lo + DN_DIM] = kh * lax.rsqrt(
            jnp.sum(kh * kh, axis=-1, keepdims=True) + 1e-6)
    qkv_ref[0, :, 2 * DN_WIDTH:] = act[:, 2 * DN_WIDTH:]

    ba = ba_ref[0]
    beta = _sigmoid(ba)
    z = ba + dtb_ref[...]
    softplus = jnp.maximum(z, 0.0) + jnp.log(1.0 + jnp.exp(-jnp.abs(z)))
    g = nega_ref[...] * softplus
    row = lax.broadcasted_iota(jnp.int32, (chunk, chunk), 0)
    col = lax.broadcasted_iota(jnp.int32, (chunk, chunk), 1)
    lower = (row >= col).astype(F32)
    upper = (row <= col).astype(F32)
    lane = lax.broadcasted_iota(jnp.int32, (chunk, LANES), 1)
    fwd_lane = lane < 8 + DN_HEADS
    for c in range(tile // chunk):
        gc = g[c * chunk:(c + 1) * chunk]
        cf = jnp.dot(lower, gc, precision=HI, preferred_element_type=F32)
        cb = jnp.dot(upper, gc, precision=HI, preferred_element_type=F32)
        gcum = jnp.where(fwd_lane, cf, cb)
        bg_ref[0, c * chunk:(c + 1) * chunk, :] = jnp.where(
            lane < 8, beta[c * chunk:(c + 1) * chunk], gcum)


def _dn_prep(proj, conv_w, nega, dtb, *, chunk):
    b, s, _ = proj.shape
    tile = min(TILE_DN_PREP, s)
    nt = s // tile
    r8 = tile // SUBLANES
    kern = functools.partial(_dn_prep_kernel, tile=tile, chunk=chunk)
    return pl.pallas_call(
        kern,
        grid=(b, nt),
        in_specs=[
            pl.BlockSpec((1, tile, 3 * DN_WIDTH), lambda bi, i: (bi, i, 0)),
            pl.BlockSpec((1, SUBLANES, 3 * DN_WIDTH),
                         lambda bi, i: (bi, jnp.maximum(i * r8 - 1, 0), 0)),
            pl.BlockSpec((1, SUBLANES, 3 * DN_WIDTH),
                         lambda bi, i: (bi, jnp.minimum((i + 1) * r8, s // SUBLANES - 1), 0)),
            pl.BlockSpec((1, tile, LANES), lambda bi, i: (bi, i, COL_BA // LANES)),
            pl.BlockSpec((DN_CONV, 3 * DN_WIDTH), lambda bi, i: (0, 0)),
            pl.BlockSpec((1, LANES), lambda bi, i: (0, 0)),
            pl.BlockSpec((1, LANES), lambda bi, i: (0, 0)),
        ],
        out_specs=[pl.BlockSpec((1, tile, 3 * DN_WIDTH), lambda bi, i: (bi, i, 0)),
                   pl.BlockSpec((1, tile, LANES), lambda bi, i: (bi, i, 0))],
        out_shape=[jax.ShapeDtypeStruct((b, s, 3 * DN_WIDTH), F32),
                   jax.ShapeDtypeStruct((b, s, LANES), F32)],
        compiler_params=_cparams("parallel", "parallel"),
        name="dn_prep",
    )(proj, proj, proj, proj, conv_w, nega, dtb)


def _dn_chain(x_ref, bgc, rows, r0, o_ref, st_ref, *, d, h, chunk):
    lo = h * DN_DIM
    q = x_ref[0, pl.ds(r0, chunk), lo:lo + DN_DIM]
    k = x_ref[0, pl.ds(r0, chunk), DN_WIDTH + lo:DN_WIDTH + lo + DN_DIM]
    v = x_ref[0, pl.ds(r0, chunk), 2 * DN_WIDTH + lo:2 * DN_WIDTH + lo + DN_DIM]
    li = d * DN_HEADS + h
    beta = bgc[:, li:li + 1]
    gcol = bgc[:, 8 + li:8 + li + 1]
    grow = rows[8 + li:8 + li + 1, :]
    gtot = gcol[chunk - 1:chunk] if d == 0 else gcol[0:1]
    ri = lax.broadcasted_iota(jnp.int32, (chunk, chunk), 0)
    ci = lax.broadcasted_iota(jnp.int32, (chunk, chunk), 1)
    incl = (ri >= ci) if d == 0 else (ri <= ci)
    strict = (ri > ci) if d == 0 else (ri < ci)

    eg = jnp.exp(gcol)
    kb = k * beta
    vb = v * beta
    kbg = kb * eg
    qg = q * eg
    kd = k * jnp.exp(gtot - gcol)
    diff = gcol - grow
    decay = jnp.where(incl, jnp.exp(jnp.where(incl, diff, 0.0)), 0.0)
    qk = _bdot_nt(jnp.concatenate([q, kb], axis=0), k)
    intra = qk[:chunk] * decay
    a = jnp.where(strict, qk[chunk:] * decay, 0.0)
    p = -a
    r = p
    for _ in range(int(math.log2(chunk)) - 1):
        p = _bdot(p, p)
        r = r + p + _bdot(r, p)
    x2 = jnp.concatenate([vb, kbg], axis=1)
    uw = x2 + _bdot(r, x2)
    u = uw[:, :DN_DIM]
    w = uw[:, DN_DIM:]
    st = st_ref[li]
    ws = _bdot(jnp.concatenate([w, qg], axis=0), st)
    v_new = u - ws[:chunk]
    o_ref[0, pl.ds(r0, chunk), lo:lo + DN_DIM] = ws[chunk:] + _bdot(intra, v_new)
    st_ref[li] = st * jnp.exp(gtot) + _bdot_tn(kd, v_new)


def _dn_scan_kernel(xf_ref, xb_ref, gf_ref, gb_ref, rf_ref, rb_ref, of_ref, ob_ref, st_ref,
                    *, tile, chunk):
    @pl.when(pl.program_id(1) == 0)
    def _():
        st_ref[...] = jnp.zeros_like(st_ref)

    nct = tile // chunk

    def body(j, carry):
        for d in range(2):
            c = j if d == 0 else nct - 1 - j
            r0 = pl.multiple_of(c * chunk, chunk)
            x_ref, g_ref, r_ref, o_ref = ((xf_ref, gf_ref, rf_ref, of_ref) if d == 0
                                          else (xb_ref, gb_ref, rb_ref, ob_ref))
            bgc = g_ref[0, pl.ds(r0, chunk), :]
            rows = r_ref[0, c]
            for h in range(DN_HEADS):
                _dn_chain(x_ref, bgc, rows, r0, o_ref, st_ref, d=d, h=h, chunk=chunk)
        return carry

    lax.fori_loop(0, nct, body, 0)


def _dn_scan(qkv, bg, bgr, *, chunk):
    b, s, _ = qkv.shape
    tile = min(TILE_DN_SCAN, s)
    nt = s // tile
    nct = tile // chunk
    kern = functools.partial(_dn_scan_kernel, tile=tile, chunk=chunk)
    fwd = lambda bi, i: (bi, i, 0)
    bwd = lambda bi, i: (bi, nt - 1 - i, 0)
    fwd4 = lambda bi, i: (bi, i, 0, 0)
    bwd4 = lambda bi, i: (bi, nt - 1 - i, 0, 0)
    return pl.pallas_call(
        kern,
        grid=(b, nt),
        in_specs=[pl.BlockSpec((1, tile, 3 * DN_WIDTH), fwd),
                  pl.BlockSpec((1, tile, 3 * DN_WIDTH), bwd),
                  pl.BlockSpec((1, tile, LANES), fwd),
                  pl.BlockSpec((1, tile, LANES), bwd),
                  pl.BlockSpec((1, nct, 16, chunk), fwd4),
                  pl.BlockSpec((1, nct, 16, chunk), bwd4)],
        out_specs=[pl.BlockSpec((1, tile, DN_WIDTH), fwd),
                   pl.BlockSpec((1, tile, DN_WIDTH), bwd)],
        out_shape=[jax.ShapeDtypeStruct((b, s, DN_WIDTH), F32),
                   jax.ShapeDtypeStruct((b, s, DN_WIDTH), F32)],
        scratch_shapes=[pltpu.VMEM((2 * DN_HEADS, DN_DIM, DN_DIM), F32)],
        compiler_params=_cparams("parallel", "arbitrary"),
        name="dn_scan",
    )(qkv, qkv, bg, bg, bgr, bgr)


def _s5_operators(lam_re, lam_im, log_dt, b_re, b_im, c_re, c_im, d_skip, *, chunk, nc):
    t = chunk
    levels = max(1, int(math.log2(nc)))
    tau = jnp.arange(t + 1, dtype=F32)
    kern = []
    lamp = []
    for d in range(2):
        lam = lax.complex(lam_re[d], lam_im[d])
        dt = jnp.exp(log_dt[d])[:, None]
        lam_dt = lam * dt
        lam_bar = jnp.exp(lam_dt)
        bbar = ((lam_bar - 1.0) / lam)[:, :, None] * lax.complex(b_re[d], b_im[d])
        cc = lax.complex(c_re[d], c_im[d])
        pw = jnp.exp(lam_dt[:, None, :] * tau[None, :, None].astype(jnp.complex64))
        kern.append((pw, bbar, cc))
        steps = (t * 2.0 ** jnp.arange(levels, dtype=F32)).astype(jnp.complex64)
        lamp.append(jnp.exp(lam_dt[:, None, :] * steps[None, :, None]))
    (pwf, bbf, ccf), (pwb, bbb, ccb) = kern
    ef = jnp.einsum('gsp,gpc->gscp', pwf[:, t - 1::-1, :], bbf)
    eb = jnp.einsum('gsp,gpc->gscp', pwb[:, :t, :], bbb)
    w1 = jnp.concatenate([jnp.real(ef), jnp.real(eb), jnp.imag(ef), jnp.imag(eb)], axis=-1)
    w1 = w1.reshape(S5_GROUPS, t * S5_GROUP, 4 * S5_STATE)
    kf = jnp.real(jnp.einsum('gcp,gtp,gpd->gtcd', ccf, pwf[:, :t, :], bbf))
    kb = jnp.real(jnp.einsum('gcp,gtp,gpd->gtcd', ccb, pwb[:, :t, :], bbb))
    si = jnp.arange(t)[:, None]
    ti = jnp.arange(t)[None, :]
    lag_f = jnp.clip(ti - si, 0, t - 1)
    lag_b = jnp.clip(si - ti, 0, t - 1)
    toep = (jnp.where((ti >= si)[None, :, :, None, None], kf[:, lag_f], 0.0)
            + jnp.where((si >= ti)[None, :, :, None, None], kb[:, lag_b], 0.0))
    eye_t = jnp.eye(t, dtype=F32)[None, :, :, None, None]
    eye_c = jnp.eye(S5_GROUP, dtype=F32)[None, None, None, :, :]
    toep = toep + eye_t * eye_c * d_skip.reshape(S5_GROUPS, 1, 1, S5_GROUP, 1)
    w_intra = toep.transpose(0, 1, 4, 2, 3).reshape(S5_GROUPS, t * S5_GROUP, t * S5_GROUP)
    mf = jnp.einsum('gcp,gtp->gptc', ccf, pwf[:, 1:t + 1, :])
    mb = jnp.einsum('gcp,gtp->gptc', ccb, pwb[:, t:0:-1, :])
    w_state = jnp.concatenate([jnp.real(mf), jnp.real(mb), -jnp.imag(mf), -jnp.imag(mb)], axis=1)
    w_state = w_state.reshape(S5_GROUPS, 4 * S5_STATE, t * S5_GROUP)
    w3 = jnp.concatenate([w_intra, w_state], axis=1)
    lam_r = jnp.concatenate([jnp.real(lamp[0]), jnp.real(lamp[1])], axis=-1)
    lam_i = jnp.concatenate([jnp.imag(lamp[0]), jnp.imag(lamp[1])], axis=-1)
    lamp = jnp.stack([lam_r, lam_i], axis=2)
    return w1.astype(BF16), w3.astype(BF16), lamp.astype(F32)


def _s5_kernel(u_ref, w1_ref, w3_ref, lam_ref, y_ref, *, nc, width, levels):
    u = u_ref[0, 0]
    loc = jnp.dot(u, w1_ref[0], preferred_element_type=F32)
    re = loc[:, :2 * S5_STATE]
    im = loc[:, 2 * S5_STATE:]
    row = lax.broadcasted_iota(jnp.int32, (nc, 2 * S5_STATE), 0)
    fwd = lax.broadcasted_iota(jnp.int32, (nc, 2 * S5_STATE), 1) < S5_STATE

    def shifted(x, d):
        down = jnp.where(row >= d, pltpu.roll(x, d, 0), 0.0)
        up = jnp.where(row < nc - d, pltpu.roll(x, nc - d, 0), 0.0)
        return jnp.where(fwd, down, up)

    for k in range(levels):
        d = 2 ** k
        if d >= nc:
            break
        lr = lam_ref[0, k, 0:1, :]
        li = lam_ref[0, k, 1:2, :]
        sr = shifted(re, d)
        si = shifted(im, d)
        re, im = re + sr * lr - si * li, im + sr * li + si * lr
    ent = jnp.concatenate([shifted(re, 1), shifted(im, 1)], axis=1)
    y = jnp.dot(u, w3_ref[0, :width, :], preferred_element_type=F32)
    y = y + jnp.dot(ent.astype(BF16), w3_ref[0, width:, :], preferred_element_type=F32)
    y_ref[0, 0] = y


def _s5_ssm(u5, w1, w3, lamp):
    g, b, nc, width = u5.shape
    levels = lamp.shape[1]
    kern = functools.partial(_s5_kernel, nc=nc, width=width, levels=levels)
    return pl.pallas_call(
        kern,
        grid=(g, b),
        in_specs=[pl.BlockSpec((1, 1, nc, width), lambda gi, bi: (gi, bi, 0, 0)),
                  pl.BlockSpec((1, width, 4 * S5_STATE), lambda gi, bi: (gi, 0, 0)),
                  pl.BlockSpec((1, width + 4 * S5_STATE, width), lambda gi, bi: (gi, 0, 0)),
                  pl.BlockSpec((1, levels, 2, 2 * S5_STATE), lambda gi, bi: (gi, 0, 0, 0))],
        out_specs=pl.BlockSpec((1, 1, nc, width), lambda gi, bi: (gi, bi, 0, 0)),
        out_shape=jax.ShapeDtypeStruct((g, b, nc, width), F32),
        compiler_params=_cparams("parallel", "parallel"),
        name="s5_ssm",
    )(u5, w1, w3, lamp)


def _rope_kernel(qq_ref, kk_ref, v_ref, cos_ref, sin_ref, qa_ref, qb_ref, ko_ref, vo_ref):
    width = DF_HEADS * 2 * DF_QK
    cos = jnp.concatenate([cos_ref[...]] * (width // LANES), axis=1)
    sin = jnp.concatenate([sin_ref[...]] * (width // LANES), axis=1)
    lane = lax.broadcasted_iota(jnp.int32, (1, width), 1)
    first_half = (lane % DF_QK) < (DF_QK // 2)

    def rot(x):
        partner = jnp.where(first_half, pltpu.roll(x, width - DF_QK // 2, 1),
                            pltpu.roll(x, DF_QK // 2, 1))
        return x * cos + partner * sin

    q = rot(qq_ref[0]) * (DF_QK ** -0.5 * LOG2E)
    is_a = (lane % (2 * DF_QK)) < DF_QK
    qa_ref[0] = jnp.where(is_a, q, 0.0).astype(BF16)
    qb_ref[0] = jnp.where(is_a, 0.0, q).astype(BF16)
    ko_ref[0] = rot(kk_ref[0]).astype(BF16)
    vo_ref[0] = v_ref[0].astype(BF16)


def _rope_prep(proj, cos_t, sin_t):
    b, s, _ = proj.shape
    tile = min(TILE_ROPE, s)
    w = DF_WIDTH
    spec = lambda cb: pl.BlockSpec((1, tile, w), lambda bi, i: (bi, i, cb))
    out = pl.BlockSpec((1, tile, w), lambda bi, i: (bi, i, 0))
    tab = pl.BlockSpec((tile, LANES), lambda bi, i: (i, 0))
    return pl.pallas_call(
        _rope_kernel,
        grid=(b, s // tile),
        in_specs=[spec(COL_QQ // w), spec(COL_KK // w), spec(COL_DV // w), tab, tab],
        out_specs=[out, out, out, out],
        out_shape=[jax.ShapeDtypeStruct((b, s, w), BF16)] * 4,
        compiler_params=_cparams("parallel", "parallel"),
        name="rope_prep",
    )(proj, proj, proj, cos_t, sin_t)


def _diff_attn_kernel(qa_ref, qb_ref, k_ref, v_ref, lq1_ref, lk1_ref, lq2_ref, lk2_ref, ng_ref,
                      o_ref, s1_ref, s2_ref, *, tq, tk, nk, lam_init):
    qa = qa_ref[0]
    qb = qb_ref[0]
    nt_dims = (((1,), (1,)), ((), ()))

    def fold(x):
        return [x[:, j * LANES:(j + 1) * LANES] for j in range(tk // LANES)]

    def scores(c, carry):
        m1, m2 = carry
        kc = k_ref[0, pl.ds(pl.multiple_of(c * tk, tk), tk), :]
        s1 = lax.dot_general(qa, kc, nt_dims, preferred_element_type=F32)
        s2 = lax.dot_general(qb, kc, nt_dims, preferred_element_type=F32)
        s1_ref[c] = s1
        s2_ref[c] = s2
        for part in fold(s1):
            m1 = jnp.maximum(m1, part)
        for part in fold(s2):
            m2 = jnp.maximum(m2, part)
        return m1, m2

    neg = jnp.full((tq, LANES), -jnp.inf, F32)
    m1, m2 = lax.fori_loop(0, nk, scores, (neg, neg))
    m1 = jnp.max(m1, axis=-1, keepdims=True)
    m2 = jnp.max(m2, axis=-1, keepdims=True)

    def expsum(c, carry):
        l1, l2 = carry
        e1 = jnp.exp2(s1_ref[c] - m1)
        e2 = jnp.exp2(s2_ref[c] - m2)
        s1_ref[c] = e1
        s2_ref[c] = e2
        for part in fold(e1):
            l1 = l1 + part
        for part in fold(e2):
            l2 = l2 + part
        return l1, l2

    zero = jnp.zeros((tq, LANES), F32)
    l1, l2 = lax.fori_loop(0, nk, expsum, (zero, zero))
    lam = (jnp.exp(jnp.sum(lq1_ref[...] * lk1_ref[...], axis=-1, keepdims=True))
           - jnp.exp(jnp.sum(lq2_ref[...] * lk2_ref[...], axis=-1, keepdims=True)) + lam_init)
    r1 = 1.0 / jnp.sum(l1, axis=-1, keepdims=True)
    r2 = lam / jnp.sum(l2, axis=-1, keepdims=True)

    def weighted(c, acc):
        att = s1_ref[c] * r1 - s2_ref[c] * r2
        vc = v_ref[0, pl.ds(pl.multiple_of(c * tk, tk), tk), :]
        return acc + jnp.dot(att.astype(BF16), vc, preferred_element_type=F32)

    acc = lax.fori_loop(0, nk, weighted, jnp.zeros((tq, DF_V), F32))
    ms = jnp.mean(acc * acc, axis=-1, keepdims=True)
    o_ref[0] = acc * lax.rsqrt(ms + RMS_EPS) * ng_ref[...] * (1.0 - lam_init)


def _diff_attn(qa, qb, kk, vv, lq1, lk1, lq2, lk2, norm_g, lam_init):
    b, s, _ = qa.shape
    tq = min(TILE_Q, s)
    tk = min(TILE_K, s)
    nk = s // tk
    kern = functools.partial(_diff_attn_kernel, tq=tq, tk=tk, nk=nk, lam_init=lam_init)
    qspec = pl.BlockSpec((1, tq, LANES), lambda bi, h, i: (bi, i, h))
    kspec = pl.BlockSpec((1, s, LANES), lambda bi, h, i: (bi, 0, h))
    vec64 = pl.BlockSpec((1, DF_QK), lambda bi, h, i: (0, 0))
    return pl.pallas_call(
        kern,
        grid=(b, DF_HEADS, s // tq),
        in_specs=[qspec, qspec, kspec, kspec, vec64, vec64, vec64, vec64,
                  pl.BlockSpec((1, DF_V), lambda bi, h, i: (0, 0))],
        out_specs=pl.BlockSpec((1, tq, DF_V), lambda bi, h, i: (bi, i, h)),
        out_shape=jax.ShapeDtypeStruct((b, s, DF_WIDTH), F32),
        scratch_shapes=[pltpu.VMEM((nk, tq, tk), F32), pltpu.VMEM((nk, tq, tk), F32)],
        compiler_params=_cparams("parallel", "parallel", "arbitrary"),
        name="diff_attn",
    )(qa, qb, kk, vv, lq1, lk1, lq2, lk2, norm_g)


def _layer_norm(x, g, b):
    mu = jnp.mean(x, axis=-1, keepdims=True)
    xc = x - mu
    var = jnp.mean(xc * xc, axis=-1, keepdims=True)
    return xc * lax.rsqrt(var + LN_EPS) * g + b


def _merge_kernel(x_ref, of_ref, ob_ref, z_ref, y5_ref, oc_ref, g0_ref, g1_ref, g2_ref,
                  bgate_ref, dnn_ref, gluw_ref, glub_ref, wbr_ref, wout_ref, lng_ref, lnb_ref,
                  o_ref):
    o = of_ref[0] + ob_ref[0]
    z = z_ref[0]
    parts = []
    for h in range(DN_HEADS):
        oh = o[:, h * DN_DIM:(h + 1) * DN_DIM]
        ms = jnp.mean(oh * oh, axis=-1, keepdims=True)
        parts.append(oh * lax.rsqrt(ms + RMS_EPS) * dnn_ref[...])
    o_a = jnp.concatenate(parts, axis=1) * _silu(z)
    y = jax.nn.gelu(y5_ref[0])
    o_b = y * _sigmoid(_bdot(y, gluw_ref[...]) + glub_ref[...])
    o_c = oc_ref[0]
    bg = bgate_ref[...]
    merged = (_sigmoid(g0_ref[0] + bg[:, 0:D_MODEL]) * _bdot(o_a, wbr_ref[0])
              + _sigmoid(g1_ref[0] + bg[:, D_MODEL:2 * D_MODEL]) * _bdot(o_b, wbr_ref[1])
              + _sigmoid(g2_ref[0] + bg[:, 2 * D_MODEL:]) * _bdot(o_c, wbr_ref[2]))
    hres = DEEPNORM_ALPHA * x_ref[0] + _bdot(merged, wout_ref[...])
    o_ref[0] = _layer_norm(hres, lng_ref[...], lnb_ref[...])


def _merge(x, o_f, o_b, proj, y5, o_c, b_gate, dn_norm, glu_w, glu_b, w_branch, w_out, ln_g, ln_b):
    b, s, _ = x.shape
    tile = min(TILE_MERGE, s)
    row = lambda w, cb=0: pl.BlockSpec((1, tile, w), lambda bi, i: (bi, i, cb))
    full = lambda shape: pl.BlockSpec(shape, lambda bi, i: (0,) * len(shape))
    gcb = COL_GATE // D_MODEL
    return pl.pallas_call(
        _merge_kernel,
        grid=(b, s // tile),
        in_specs=[row(D_MODEL), row(DN_WIDTH), row(DN_WIDTH), row(DN_WIDTH, COL_DN_Z // DN_WIDTH),
                  row(S5_WIDTH), row(DF_WIDTH),
                  row(D_MODEL, gcb), row(D_MODEL, gcb + 1), row(D_MODEL, gcb + 2),
                  full((1, N_BRANCH * D_MODEL)), full((1, DN_DIM)),
                  full((S5_WIDTH, S5_WIDTH)), full((1, S5_WIDTH)),
                  full((N_BRANCH, DN_WIDTH, D_MODEL)), full((D_MODEL, D_MODEL)),
                  full((1, D_MODEL)), full((1, D_MODEL))],
        out_specs=row(D_MODEL),
        out_shape=jax.ShapeDtypeStruct((b, s, D_MODEL), F32),
        compiler_params=_cparams("parallel", "parallel"),
        name="merge",
    )(x, o_f, o_b, proj, y5, o_c, proj, proj, proj, b_gate, dn_norm, glu_w, glu_b,
      w_branch, w_out, ln_g, ln_b)


def _xattn_kernel(x_ref, kv_ref, wq_ref, wo_ref, lng_ref, lnb_ref, o_ref):
    x = x_ref[0]
    q = _bdot(x, wq_ref[...])
    outs = []
    for h in range(XA_HEADS):
        qh = q[:, h * XA_DIM:(h + 1) * XA_DIM]
        kh = kv_ref[0, :, h * XA_DIM:(h + 1) * XA_DIM]
        vh = kv_ref[0, :, XA_WIDTH + h * XA_DIM:XA_WIDTH + (h + 1) * XA_DIM]
        sc = _bdot_nt(qh, kh) * (XA_DIM ** -0.5)
        sc = sc - jnp.max(sc, axis=-1, keepdims=True)
        e = jnp.exp(sc)
        pr = e / jnp.sum(e, axis=-1, keepdims=True)
        outs.append(_bdot(pr, vh))
    o = jnp.concatenate(outs, axis=1)
    hres = DEEPNORM_ALPHA * x + _bdot(o, wo_ref[...])
    o_ref[0] = _layer_norm(hres, lng_ref[...], lnb_ref[...])


def _xattn(x, kv, wq, wo, ln_g, ln_b):
    b, s, _ = x.shape
    tile = min(TILE_XA, s)
    mlen = kv.shape[1]
    row = pl.BlockSpec((1, tile, D_MODEL), lambda bi, i: (bi, i, 0))
    full = lambda shape: pl.BlockSpec(shape, lambda bi, i: (0,) * len(shape))
    return pl.pallas_call(
        _xattn_kernel,
        grid=(b, s // tile),
        in_specs=[row, pl.BlockSpec((1, mlen, 2 * XA_WIDTH), lambda bi, i: (bi, 0, 0)),
                  full((D_MODEL, XA_WIDTH)), full((XA_WIDTH, D_MODEL)),
                  full((1, D_MODEL)), full((1, D_MODEL))],
        out_specs=row,
        out_shape=jax.ShapeDtypeStruct((b, s, D_MODEL), F32),
        compiler_params=_cparams("parallel", "parallel"),
        name="xattn",
    )(x, kv, wq, wo, ln_g, ln_b)


def _router_kernel(x_ref, rw_ref, rb_ref, idx_ref, gate_ref, xb_ref):
    x = x_ref[...]
    xb_ref[...] = x.astype(BF16)
    logits = lax.dot_general(rw_ref[...], x, (((1,), (1,)), ((), ())), precision=HI,
                             preferred_element_type=F32) + rb_ref[...]
    tn = logits.shape[1]
    eid = lax.broadcasted_iota(jnp.int32, (N_EXPERTS, tn), 0)
    vals, ids = [], []
    cur = logits
    for _ in range(TOP_K):
        m = jnp.max(cur, axis=0, keepdims=True)
        sel = jnp.min(jnp.where(cur == m, eid, N_EXPERTS), axis=0, keepdims=True)
        vals.append(m)
        ids.append(sel)
        cur = jnp.where(eid == sel, -jnp.inf, cur)
    top_v = jnp.concatenate(vals, axis=0)
    e = jnp.exp(top_v - top_v[0:1])
    gate_ref[...] = e / jnp.sum(e, axis=0, keepdims=True)
    idx_ref[...] = jnp.concatenate(ids, axis=0)


def _router(x2d, router_w_t, router_b):
    n, _ = x2d.shape
    tile = min(TILE_ROUTER, n)
    return pl.pallas_call(
        _router_kernel,
        grid=(n // tile,),
        in_specs=[pl.BlockSpec((tile, D_MODEL), lambda i: (i, 0)),
                  pl.BlockSpec((N_EXPERTS, D_MODEL), lambda i: (0, 0)),
                  pl.BlockSpec((N_EXPERTS, 1), lambda i: (0, 0))],
        out_specs=[pl.BlockSpec((TOP_K, tile), lambda i: (0, i)),
                   pl.BlockSpec((TOP_K, tile), lambda i: (0, i)),
                   pl.BlockSpec((tile, D_MODEL), lambda i: (i, 0))],
        out_shape=[jax.ShapeDtypeStruct((TOP_K, n), jnp.int32),
                   jax.ShapeDtypeStruct((TOP_K, n), F32),
                   jax.ShapeDtypeStruct((n, D_MODEL), BF16)],
        compiler_params=_cparams("parallel"),
        name="router",
    )(x2d, router_w_t, router_b)


def _rank_kernel(idx_ref, rank_ref, cnt_ref, run_ref, *, tile):
    @pl.when(pl.program_id(0) == 0)
    def _():
        run_ref[...] = jnp.zeros_like(run_ref)

    idx = idx_ref[...]
    eid = lax.broadcasted_iota(jnp.int32, (N_EXPERTS, tile), 0)
    si = lax.broadcasted_iota(jnp.int32, (tile, tile), 0)
    ti = lax.broadcasted_iota(jnp.int32, (tile, tile), 1)
    before = (si < ti).astype(BF16)
    run = run_ref[...]
    ranks = []
    for k in range(TOP_K):
        onehot = eid == idx[k:k + 1]
        oh = onehot.astype(BF16)
        within = jnp.dot(oh, before, preferred_element_type=F32)
        ranks.append(jnp.sum(jnp.where(onehot, within + run, 0.0), axis=0, keepdims=True))
        run = run + jnp.sum(onehot.astype(F32), axis=1, keepdims=True)
    rank_ref[...] = jnp.concatenate(ranks, axis=0).astype(jnp.int32)
    run_ref[...] = run
    cnt_ref[...] = run.astype(jnp.int32)


def _rank(top_i):
    _, n = top_i.shape
    tile = min(TILE_ROUTER, n)
    kern = functools.partial(_rank_kernel, tile=tile)
    return pl.pallas_call(
        kern,
        grid=(n // tile,),
        in_specs=[pl.BlockSpec((TOP_K, tile), lambda i: (0, i))],
        out_specs=[pl.BlockSpec((TOP_K, tile), lambda i: (0, i)),
                   pl.BlockSpec((N_EXPERTS, 1), lambda i: (0, 0))],
        out_shape=[jax.ShapeDtypeStruct((TOP_K, n), jnp.int32),
                   jax.ShapeDtypeStruct((N_EXPERTS, 1), jnp.int32)],
        scratch_shapes=[pltpu.VMEM((N_EXPERTS, 1), F32)],
        compiler_params=_cparams("arbitrary"),
        name="moe_rank",
    )(top_i)


def _expert_kernel(be_ref, x_ref, wg_ref, wu_ref, bg_ref, bu_ref, wd_ref, bd_ref, o_ref):
    del be_ref
    x = x_ref[...]
    hg = jnp.dot(x, wg_ref[0], preferred_element_type=F32) + bg_ref[0]
    hu = jnp.dot(x, wu_ref[0], preferred_element_type=F32) + bu_ref[0]
    gt = jnp.minimum(hg, SWIGLU_LIMIT)
    up = jnp.clip(hu, -SWIGLU_LIMIT, SWIGLU_LIMIT)
    act = (up + 1.0) * gt * _sigmoid(SWIGLU_ALPHA * gt)
    o_ref[...] = jnp.dot(act.astype(BF16), wd_ref[0], preferred_element_type=F32) + bd_ref[0]


def _experts(x_rows, blk_e, w_g, w_u, b_g, b_u, w_d, b_d, *, block):
    rows, _ = x_rows.shape
    wspec = lambda k, m: pl.BlockSpec((1, k, m), lambda i, be: (be[i], 0, 0))
    grid_spec = pltpu.PrefetchScalarGridSpec(
        num_scalar_prefetch=1,
        grid=(rows // block,),
        in_specs=[pl.BlockSpec((block, D_MODEL), lambda i, be: (i, 0)),
                  wspec(D_MODEL, D_FF), wspec(D_MODEL, D_FF), wspec(1, D_FF), wspec(1, D_FF),
                  wspec(D_FF, D_MODEL), wspec(1, D_MODEL)],
        out_specs=pl.BlockSpec((block, D_MODEL), lambda i, be: (i, 0)),
    )
    return pl.pallas_call(
        _expert_kernel,
        grid_spec=grid_spec,
        out_shape=jax.ShapeDtypeStruct((rows, D_MODEL), F32),
        compiler_params=_cparams("arbitrary"),
        name="moe_experts",
    )(blk_e, x_rows, w_g, w_u, b_g, b_u, w_d, b_d)


def _combine_kernel(x_ref, y_ref, gate_ref, lng_ref, lnb_ref, o_ref):
    gate = gate_ref[...]
    h = y_ref[0] * gate[:, 0:1].astype(F32)
    for k in range(1, TOP_K):
        h = h + y_ref[k] * gate[:, k:k + 1]
    o_ref[...] = _layer_norm(DEEPNORM_ALPHA * x_ref[...] + h, lng_ref[...], lnb_ref[...])


def _combine(x2d, y_g, gate_t, ln_g, ln_b):
    n, _ = x2d.shape
    tile = min(TILE_COMBINE, n)
    return pl.pallas_call(
        _combine_kernel,
        grid=(n // tile,),
        in_specs=[pl.BlockSpec((tile, D_MODEL), lambda i: (i, 0)),
                  pl.BlockSpec((TOP_K, tile, D_MODEL), lambda i: (0, i, 0)),
                  pl.BlockSpec((tile, TOP_K), lambda i: (i, 0)),
                  pl.BlockSpec((1, D_MODEL), lambda i: (0, 0)),
                  pl.BlockSpec((1, D_MODEL), lambda i: (0, 0))],
        out_specs=pl.BlockSpec((tile, D_MODEL), lambda i: (i, 0)),
        out_shape=jax.ShapeDtypeStruct((n, D_MODEL), F32),
        compiler_params=_cparams("parallel"),
        name="moe_combine",
    )(x2d, y_g, gate_t, ln_g, ln_b)


def _moe(x, router_w, router_b, w_gu, b_gu, w_down, b_down, ln_g, ln_b):
    b, s, dm = x.shape
    n = b * s
    x2d = x.reshape(n, dm)
    top_i, gate, xb = _router(x2d, router_w.T, router_b.reshape(N_EXPERTS, 1))
    rank, counts = _rank(top_i)
    block = MOE_BLOCK
    m = n * TOP_K
    nblk = -(-m // block) + N_EXPERTS
    rows = nblk * block
    counts = counts.reshape(N_EXPERTS)
    padded = (counts + block - 1) // block * block
    pad_end = jnp.cumsum(padded)
    pad_start = pad_end - padded
    dest = pad_start[top_i] + rank
    tok = jnp.broadcast_to(jnp.arange(n, dtype=jnp.int32)[None, :], (TOP_K, n))
    row_tok = jnp.full((rows,), n, jnp.int32).at[dest.reshape(-1)].set(tok.reshape(-1))
    blk_e = jnp.minimum(jnp.searchsorted(pad_end, jnp.arange(nblk, dtype=pad_end.dtype) * block,
                                         side='right'), N_EXPERTS - 1).astype(jnp.int32)
    x_rows = jnp.concatenate([xb, jnp.zeros((1, dm), BF16)], axis=0)[row_tok]
    w_g = w_gu[:, :, 0::2].astype(BF16)
    w_u = w_gu[:, :, 1::2].astype(BF16)
    b_g = b_gu[:, None, 0::2]
    b_u = b_gu[:, None, 1::2]
    y_rows = _experts(x_rows, blk_e, w_g, w_u, b_g, b_u, w_down.astype(BF16),
                      b_down[:, None, :], block=block)
    y_g = y_rows[dest]
    out = _combine(x2d, y_g, gate.T, ln_g.reshape(1, dm), ln_b.reshape(1, dm))
    return out.reshape(b, s, dm)


def _rope_tables(s):
    half = DF_QK // 2
    inv_freq = jnp.power(ROPE_THETA, -jnp.arange(half, dtype=F32) / half)
    ang = jnp.arange(s, dtype=F32)[:, None] * inv_freq[None, :]
    cos = jnp.cos(ang)
    sin = jnp.sin(ang)
    cos_t = jnp.concatenate([cos, cos, cos, cos], axis=1)
    sin_t = jnp.concatenate([-sin, sin, -sin, sin], axis=1)
    return cos_t, sin_t


def _layer(x, mem, l, tabs, p):
    b, s, dm = x.shape
    n = b * s
    w_in = jnp.concatenate([p['w_in'][l], jnp.zeros((dm, 1), F32)], axis=1)[:, _IN_PERM]
    proj = _matmul(x.reshape(n, dm), w_in.astype(BF16), tm=TILE_MM, tn=LANES * 19)
    proj = proj.reshape(b, s, PROJ_WIDTH)

    pad8 = jnp.zeros((8,), F32)
    pad112 = jnp.zeros((LANES - 16,), F32)
    nega = jnp.concatenate([pad8, -jnp.exp(p['dn_A_log'][l].reshape(8)), pad112]).reshape(1, LANES)
    dtb = jnp.concatenate([pad8, p['dn_dt_bias'][l].reshape(8), pad112]).reshape(1, LANES)
    qkv, bg = _dn_prep(proj, p['dn_conv'][l], nega, dtb, chunk=DN_CHUNK)
    bgr = bg[:, :, :16].reshape(b, s // DN_CHUNK, DN_CHUNK, 16).transpose(0, 1, 3, 2)
    o_f, o_b = _dn_scan(qkv, bg, bgr, chunk=DN_CHUNK)

    t5 = min(S5_CHUNK, s)
    nc = s // t5
    w1, w3, lamp = _s5_operators(p['s5_lam_re'][l], p['s5_lam_im'][l], p['s5_log_dt'][l],
                                 p['s5_B_re'][l], p['s5_B_im'][l], p['s5_C_re'][l],
                                 p['s5_C_im'][l], p['s5_D'][l], chunk=t5, nc=nc)
    u5 = proj[:, :, COL_S5_U:COL_S5_U + S5_WIDTH].reshape(b, nc, t5, S5_GROUPS, S5_GROUP)
    u5 = u5.transpose(3, 0, 1, 2, 4).reshape(S5_GROUPS, b, nc, t5 * S5_GROUP).astype(BF16)
    y5 = _s5_ssm(u5, w1, w3, lamp)
    y5 = y5.reshape(S5_GROUPS, b, nc, t5, S5_GROUP).transpose(1, 2, 3, 0, 4).reshape(b, s, S5_WIDTH)

    lam_init = 0.8 - 0.6 * math.exp(-0.3 * l)
    qa, qb, kk, vv = _rope_prep(proj, *tabs)
    vec = lambda name: p[name][l].reshape(1, DF_QK)
    o_c = _diff_attn(qa, qb, kk, vv, vec('df_lq1'), vec('df_lk1'), vec('df_lq2'), vec('df_lk2'),
                     p['df_norm'][l].reshape(1, DF_V), lam_init)

    x = _merge(x, o_f, o_b, proj, y5, o_c, p['b_gate'][l].reshape(1, -1),
               p['dn_norm'][l].reshape(1, DN_DIM), p['s5_glu_w'][l].astype(BF16),
               p['s5_glu_b'][l].reshape(1, -1), p['w_branch'][l].astype(BF16),
               p['w_out'][l].astype(BF16), p['ln1_g'][l].reshape(1, dm), p['ln1_b'][l].reshape(1, dm))

    mlen = mem.shape[1]
    kv = _matmul(mem.reshape(b * mlen, dm), p['xa_wkv'][l].astype(BF16), tm=512, tn=1024)
    x = _xattn(x, kv.reshape(b, mlen, 2 * XA_WIDTH), p['xa_wq'][l].astype(BF16),
               p['xa_wo'][l].astype(BF16), p['ln2_g'][l].reshape(1, dm), p['ln2_b'][l].reshape(1, dm))

    x = _moe(x, p['router_w'][l], p['router_b'][l], p['e_w_gu'][l], p['e_b_gu'][l],
             p['e_w_down'][l], p['e_b_down'][l], p['ln3_g'][l], p['ln3_b'][l])
    return x


def kernel(x_prompt, x_sample, mem_prompt, mem_sample, w_in, b_gate, dn_conv, dn_A_log, dn_dt_bias,
           dn_norm, s5_lam_re, s5_lam_im, s5_log_dt, s5_B_re, s5_B_im, s5_C_re, s5_C_im, s5_D,
           s5_glu_w, s5_glu_b, df_lq1, df_lk1, df_lq2, df_lk2, df_norm, w_branch, w_out,
           ln1_g, ln1_b, xa_wq, xa_wkv, xa_wo, ln2_g, ln2_b, router_w, router_b,
           e_w_gu, e_b_gu, e_w_down, e_b_down, ln3_g, ln3_b):
    p = dict(w_in=w_in, b_gate=b_gate, dn_conv=dn_conv, dn_A_log=dn_A_log, dn_dt_bias=dn_dt_bias,
             dn_norm=dn_norm, s5_lam_re=s5_lam_re, s5_lam_im=s5_lam_im, s5_log_dt=s5_log_dt,
             s5_B_re=s5_B_re, s5_B_im=s5_B_im, s5_C_re=s5_C_re, s5_C_im=s5_C_im, s5_D=s5_D,
             s5_glu_w=s5_glu_w, s5_glu_b=s5_glu_b, df_lq1=df_lq1, df_lk1=df_lk1, df_lq2=df_lq2,
             df_lk2=df_lk2, df_norm=df_norm, w_branch=w_branch, w_out=w_out, ln1_g=ln1_g,
             ln1_b=ln1_b, xa_wq=xa_wq, xa_wkv=xa_wkv, xa_wo=xa_wo, ln2_g=ln2_g, ln2_b=ln2_b,
             router_w=router_w, router_b=router_b, e_w_gu=e_w_gu, e_b_gu=e_b_gu,
             e_w_down=e_w_down, e_b_down=e_b_down, ln3_g=ln3_g, ln3_b=ln3_b)
    nb = x_prompt.shape[0]
    x = jnp.concatenate([x_prompt, x_sample], axis=0)
    mem = jnp.concatenate([mem_prompt, mem_sample], axis=0)
    tabs = _rope_tables(x.shape[1])
    for l in range(DEPTH):
        x = _layer(x, mem, l, tabs, p)
    return (x[:nb], x[nb:])
```

```python
import functools
import math

import numpy as np
import jax
import jax.numpy as jnp
from jax import lax
from jax.experimental import pallas as pl
from jax.experimental.pallas import tpu as pltpu

F32 = jnp.float32
BF16 = jnp.bfloat16
HI = lax.Precision.HIGHEST

D_MODEL = 1024
DN_HEADS = 4
DN_DIM = 128
DN_WIDTH = DN_HEADS * DN_DIM
DN_CONV = 4
S5_GROUP = 16
S5_WIDTH = 512
S5_GROUPS = S5_WIDTH // S5_GROUP
S5_STATE = 64
DF_HEADS = 4
DF_QK = 64
DF_V = 128
DF_WIDTH = DF_HEADS * DF_V
ROPE_THETA = 10000.0
XA_HEADS = 4
XA_DIM = 128
XA_WIDTH = XA_HEADS * XA_DIM
N_EXPERTS = 32
TOP_K = 4
D_FF = 1024
SWIGLU_LIMIT = 7.0
SWIGLU_ALPHA = 1.702
N_BRANCH = 3
DEPTH = 2
DEEPNORM_ALPHA = (2 * DEPTH) ** 0.25
LN_EPS = 1e-5
RMS_EPS = 1e-6
LOG2E = 1.4426950408889634

LANES = 128
SUBLANES = 8
VMEM_LIMIT = 56 * 1024 * 1024

COL_DN_QKV = 0
COL_DN_Z = 1536
COL_S5_U = 2048
COL_QQ = 2560
COL_KK = 3072
COL_DV = 3584
COL_GATE = 4096
COL_BA = 7168
PROJ_WIDTH = 7296

TILE_MM = 1024
TILE_DN_PREP = 512
TILE_DN_SCAN = 256
DN_CHUNK = 64
S5_CHUNK = 64
TILE_ROPE = 512
TILE_Q = 256
TILE_K = 512
TILE_MERGE = 256
TILE_XA = 512
TILE_ROUTER = 512
MOE_BLOCK = 512
TILE_COMBINE = 256


def _in_proj_perm():
    o_dn_b, o_dn_a, o_s5 = 2048, 2056, 2064
    o_q1, o_q2, o_k1, o_k2, o_dv, o_gate = 2576, 2832, 3088, 3344, 3600, 4112
    perm = np.full((PROJ_WIDTH,), -1, np.int64)
    perm[0:2048] = np.arange(2048)
    perm[COL_S5_U:COL_S5_U + 512] = o_s5 + np.arange(512)
    for h in range(DF_HEADS):
        j = np.arange(DF_QK)
        perm[COL_QQ + h * 128 + j] = o_q1 + h * DF_QK + j
        perm[COL_QQ + h * 128 + 64 + j] = o_q2 + h * DF_QK + j
        perm[COL_KK + h * 128 + j] = o_k1 + h * DF_QK + j
        perm[COL_KK + h * 128 + 64 + j] = o_k2 + h * DF_QK + j
    perm[COL_DV:COL_DV + 512] = o_dv + np.arange(512)
    perm[COL_GATE:COL_GATE + 3072] = o_gate + np.arange(3072)
    perm[COL_BA:COL_BA + 8] = o_dn_b + np.arange(8)
    perm[COL_BA + 8:COL_BA + 16] = o_dn_a + np.arange(8)
    return perm


_IN_PERM = _in_proj_perm()


def _cparams(*sem):
    return pltpu.CompilerParams(dimension_semantics=sem, vmem_limit_bytes=VMEM_LIMIT)


def _bdot(a, b):
    return jnp.dot(a.astype(BF16), b.astype(BF16), preferred_element_type=F32)


def _bdot_nt(a, b):
    return lax.dot_general(a.astype(BF16), b.astype(BF16), (((1,), (1,)), ((), ())),
                           preferred_element_type=F32)


def _bdot_tn(a, b):
    return lax.dot_general(a.astype(BF16), b.astype(BF16), (((0,), (0,)), ((), ())),
                           preferred_element_type=F32)


def _sigmoid(x):
    return 1.0 / (1.0 + jnp.exp(-x))


def _silu(x):
    return x * _sigmoid(x)


def _mm_kernel(x_ref, w_ref, o_ref):
    o_ref[...] = jnp.dot(x_ref[...].astype(BF16), w_ref[...],
                         preferred_element_type=F32).astype(o_ref.dtype)


def _matmul(x, w, *, tm, tn, out_dtype=F32):
    n, k = x.shape
    m = w.shape[1]
    tm = min(tm, n)
    tn = min(tn, m)
    return pl.pallas_call(
        _mm_kernel,
        grid=(n // tm, m // tn),
        in_specs=[pl.BlockSpec((tm, k), lambda i, j: (i, 0)),
                  pl.BlockSpec((k, tn), lambda i, j: (0, j))],
        out_specs=pl.BlockSpec((tm, tn), lambda i, j: (i, j)),
        out_shape=jax.ShapeDtypeStruct((n, m), out_dtype),
        compiler_params=_cparams("parallel", "arbitrary"),
        name="matmul",
    )(x, w)


def _dn_prep_kernel(x_ref, prev_ref, next_ref, ba_ref, cw_ref, nega_ref, dtb_ref,
                    qkv_ref, bg_ref, *, tile, chunk):
    i = pl.program_id(1)
    last = pl.num_programs(1) - 1
    x = x_ref[0]
    prev = jnp.where(i > 0, prev_ref[0], 0.0)
    nxt = jnp.where(i < last, next_ref[0], 0.0)
    ext = jnp.concatenate([prev, x, nxt], axis=0)
    cw = cw_ref[...]
    acc = cw[0:1] * ext[6:6 + tile]
    for j in range(1, DN_CONV):
        acc = acc + cw[j:j + 1] * ext[6 + j:6 + j + tile]
    act = _silu(acc)
    for h in range(DN_HEADS):
        lo = h * DN_DIM
        qh = act[:, lo:lo + DN_DIM]
        qn = qh * lax.rsqrt(jnp.sum(qh * qh, axis=-1, keepdims=True) + 1e-6)
        qkv_ref[0, :, lo:lo + DN_DIM] = qn * (DN_DIM ** -0.5)
        kh = act[:, DN_WIDTH + lo:DN_WIDTH + lo + DN_DIM]
        qkv_ref[0, :, DN_WIDTH + lo:DN_WIDTH + lo + DN_DIM] = kh * lax.rsqrt(
            jnp.sum(kh * kh, axis=-1, keepdims=True) + 1e-6)
    qkv_ref[0, :, 2 * DN_WIDTH:] = act[:, 2 * DN_WIDTH:]

    ba = ba_ref[0]
    beta = _sigmoid(ba)
    z = ba + dtb_ref[...]
    softplus = jnp.maximum(z, 0.0) + jnp.log(1.0 + jnp.exp(-jnp.abs(z)))
    g = nega_ref[...] * softplus
    row = lax.broadcasted_iota(jnp.int32, (chunk, chunk), 0)
    col = lax.broadcasted_iota(jnp.int32, (chunk, chunk), 1)
    lower = (row >= col).astype(F32)
    upper = (row <= col).astype(F32)
    lane = lax.broadcasted_iota(jnp.int32, (chunk, LANES), 1)
    fwd_lane = lane < 8 + DN_HEADS
    for c in range(tile // chunk):
        gc = g[c * chunk:(c + 1) * chunk]
        cf = jnp.dot(lower, gc, precision=HI, preferred_element_type=F32)
        cb = jnp.dot(upper, gc, precision=HI, preferred_element_type=F32)
        gcum = jnp.where(fwd_lane, cf, cb)
        bg_ref[0, c * chunk:(c + 1) * chunk, :] = jnp.where(
            lane < 8, beta[c * chunk:(c + 1) * chunk], gcum)


def _dn_prep(proj, conv_w, nega, dtb, *, chunk):
    b, s, _ = proj.shape
    tile = min(TILE_DN_PREP, s)
    nt = s // tile
    r8 = tile // SUBLANES
    kern = functools.partial(_dn_prep_kernel, tile=tile, chunk=chunk)
    return pl.pallas_call(
        kern,
        grid=(b, nt),
        in_specs=[
            pl.BlockSpec((1, tile, 3 * DN_WIDTH), lambda bi, i: (bi, i, 0)),
            pl.BlockSpec((1, SUBLANES, 3 * DN_WIDTH),
                         lambda bi, i: (bi, jnp.maximum(i * r8 - 1, 0), 0)),
            pl.BlockSpec((1, SUBLANES, 3 * DN_WIDTH),
                         lambda bi, i: (bi, jnp.minimum((i + 1) * r8, s // SUBLANES - 1), 0)),
            pl.BlockSpec((1, tile, LANES), lambda bi, i: (bi, i, COL_BA // LANES)),
            pl.BlockSpec((DN_CONV, 3 * DN_WIDTH), lambda bi, i: (0, 0)),
            pl.BlockSpec((1, LANES), lambda bi, i: (0, 0)),
            pl.BlockSpec((1, LANES), lambda bi, i: (0, 0)),
        ],
        out_specs=[pl.BlockSpec((1, tile, 3 * DN_WIDTH), lambda bi, i: (bi, i, 0)),
                   pl.BlockSpec((1, tile, LANES), lambda bi, i: (bi, i, 0))],
        out_shape=[jax.ShapeDtypeStruct((b, s, 3 * DN_WIDTH), F32),
                   jax.ShapeDtypeStruct((b, s, LANES), F32)],
        compiler_params=_cparams("parallel", "parallel"),
        name="dn_prep",
    )(proj, proj, proj, proj, conv_w, nega, dtb)


def _dn_step(chains, chunk):
    ri = lax.broadcasted_iota(jnp.int32, (chunk, chunk), 0)
    ci = lax.broadcasted_iota(jnp.int32, (chunk, chunk), 1)
    pre = []
    for d, h, q, k, v, st, bgc, rows in chains:
        li = d * DN_HEADS + h
        beta = bgc[:, li:li + 1]
        gcol = bgc[:, 8 + li:8 + li + 1]
        grow = rows[8 + li:8 + li + 1, :]
        gtot = gcol[chunk - 1:chunk] if d == 0 else gcol[0:1]
        incl = (ri >= ci) if d == 0 else (ri <= ci)
        strict = (ri > ci) if d == 0 else (ri < ci)
        eg = jnp.exp(gcol)
        kb = k * beta
        decay = jnp.where(incl, jnp.exp(jnp.where(incl, gcol - grow, 0.0)), 0.0)
        pre.append(dict(k=k, st=st, kb=kb, qg=q * eg, kd=k * jnp.exp(gtot - gcol), decay=decay,
                        strict=strict, x2=jnp.concatenate([v * beta, kb * eg], axis=1),
                        sdec=jnp.exp(gtot), qkb=jnp.concatenate([q, kb], axis=0)))
    qk = [_bdot_nt(c['qkb'], c['k']) for c in pre]
    intra = [x[:chunk] * c['decay'] for x, c in zip(qk, pre)]
    p = [-jnp.where(c['strict'], x[chunk:] * c['decay'], 0.0) for x, c in zip(qk, pre)]
    r = p
    p = [_bdot(x, x) for x in p]
    for _ in range(int(math.log2(chunk)) - 2):
        rp = [_bdot(jnp.concatenate([ri_, pi_], axis=0), pi_) for ri_, pi_ in zip(r, p)]
        r = [ri_ + pi_ + x[:chunk] for ri_, pi_, x in zip(r, p, rp)]
        p = [x[chunk:] for x in rp]
    r = [ri_ + pi_ + _bdot(ri_, pi_) for ri_, pi_ in zip(r, p)]
    uw = [c['x2'] + _bdot(ri_, c['x2']) for ri_, c in zip(r, pre)]
    ws = [_bdot(jnp.concatenate([x[:, DN_DIM:], c['qg']], axis=0), c['st'])
          for x, c in zip(uw, pre)]
    v_new = [x[:, :DN_DIM] - y[:chunk] for x, y in zip(uw, ws)]
    out = [y[chunk:] + _bdot(it, vn) for y, it, vn in zip(ws, intra, v_new)]
    st_new = [c['st'] * c['sdec'] + _bdot_tn(c['kd'], vn) for c, vn in zip(pre, v_new)]
    return list(zip(out, st_new))


def _dn_scan_kernel(xf_ref, xb_ref, gf_ref, gb_ref, rf_ref, rb_ref, of_ref, ob_ref, st_ref,
                    *, tile, chunk):
    @pl.when(pl.program_id(1) == 0)
    def _():
        st_ref[...] = jnp.zeros_like(st_ref)

    nct = tile // chunk

    def body(j, carry):
        loaded = []
        for d in range(2):
            c = j if d == 0 else nct - 1 - j
            r0 = pl.multiple_of(c * chunk, chunk)
            x_ref, g_ref, r_ref = (xf_ref, gf_ref, rf_ref) if d == 0 else (xb_ref, gb_ref, rb_ref)
            bgc = g_ref[0, pl.ds(r0, chunk), :]
            rows = r_ref[0, c]
            for h in range(DN_HEADS):
                lo = h * DN_DIM
                q = x_ref[0, pl.ds(r0, chunk), lo:lo + DN_DIM]
                k = x_ref[0, pl.ds(r0, chunk), DN_WIDTH + lo:DN_WIDTH + lo + DN_DIM]
                v = x_ref[0, pl.ds(r0, chunk), 2 * DN_WIDTH + lo:2 * DN_WIDTH + lo + DN_DIM]
                loaded.append((d, h, r0, q, k, v, st_ref[d * DN_HEADS + h], bgc, rows))
        results = _dn_step([(d, h, q, k, v, st, bgc, rows)
                            for d, h, r0, q, k, v, st, bgc, rows in loaded], chunk)
        for (d, h, r0, *_), (out, st_new) in zip(loaded, results):
            o_ref = of_ref if d == 0 else ob_ref
            o_ref[0, pl.ds(r0, chunk), h * DN_DIM:(h + 1) * DN_DIM] = out
            st_ref[d * DN_HEADS + h] = st_new
        return carry

    lax.fori_loop(0, nct, body, 0)


def _dn_scan(qkv, bg, bgr, *, chunk):
    b, s, _ = qkv.shape
    tile = min(TILE_DN_SCAN, s)
    nt = s // tile
    nct = tile // chunk
    kern = functools.partial(_dn_scan_kernel, tile=tile, chunk=chunk)
    fwd = lambda bi, i: (bi, i, 0)
    bwd = lambda bi, i: (bi, nt - 1 - i, 0)
    fwd4 = lambda bi, i: (bi, i, 0, 0)
    bwd4 = lambda bi, i: (bi, nt - 1 - i, 0, 0)
    return pl.pallas_call(
        kern,
        grid=(b, nt),
        in_specs=[pl.BlockSpec((1, tile, 3 * DN_WIDTH), fwd),
                  pl.BlockSpec((1, tile, 3 * DN_WIDTH), bwd),
                  pl.BlockSpec((1, tile, LANES), fwd),
                  pl.BlockSpec((1, tile, LANES), bwd),
                  pl.BlockSpec((1, nct, 16, chunk), fwd4),
                  pl.BlockSpec((1, nct, 16, chunk), bwd4)],
        out_specs=[pl.BlockSpec((1, tile, DN_WIDTH), fwd),
                   pl.BlockSpec((1, tile, DN_WIDTH), bwd)],
        out_shape=[jax.ShapeDtypeStruct((b, s, DN_WIDTH), F32),
                   jax.ShapeDtypeStruct((b, s, DN_WIDTH), F32)],
        scratch_shapes=[pltpu.VMEM((2 * DN_HEADS, DN_DIM, DN_DIM), F32)],
        compiler_params=_cparams("parallel", "arbitrary"),
        name="dn_scan",
    )(qkv, qkv, bg, bg, bgr, bgr)


def _s5_operators(lam_re, lam_im, log_dt, b_re, b_im, c_re, c_im, d_skip, *, chunk, nc):
    t = chunk
    levels = max(1, int(math.log2(nc)))
    tau = jnp.arange(t + 1, dtype=F32)
    kern = []
    lamp = []
    for d in range(2):
        lam = lax.complex(lam_re[d], lam_im[d])
        dt = jnp.exp(log_dt[d])[:, None]
        lam_dt = lam * dt
        lam_bar = jnp.exp(lam_dt)
        bbar = ((lam_bar - 1.0) / lam)[:, :, None] * lax.complex(b_re[d], b_im[d])
        cc = lax.complex(c_re[d], c_im[d])
        pw = jnp.exp(lam_dt[:, None, :] * tau[None, :, None].astype(jnp.complex64))
        kern.append((pw, bbar, cc))
        steps = (t * 2.0 ** jnp.arange(levels, dtype=F32)).astype(jnp.complex64)
        lamp.append(jnp.exp(lam_dt[:, None, :] * steps[None, :, None]))
    (pwf, bbf, ccf), (pwb, bbb, ccb) = kern
    ef = jnp.einsum('gsp,gpc->gscp', pwf[:, t - 1::-1, :], bbf)
    eb = jnp.einsum('gsp,gpc->gscp', pwb[:, :t, :], bbb)
    w1 = jnp.concatenate([jnp.real(ef), jnp.real(eb), jnp.imag(ef), jnp.imag(eb)], axis=-1)
    w1 = w1.reshape(S5_GROUPS, t * S5_GROUP, 4 * S5_STATE)
    kf = jnp.real(jnp.einsum('gcp,gtp,gpd->gtcd', ccf, pwf[:, :t, :], bbf))
    kb = jnp.real(jnp.einsum('gcp,gtp,gpd->gtcd', ccb, pwb[:, :t, :], bbb))
    si = jnp.arange(t)[:, None]
    ti = jnp.arange(t)[None, :]
    lag_f = jnp.clip(ti - si, 0, t - 1)
    lag_b = jnp.clip(si - ti, 0, t - 1)
    toep = (jnp.where((ti >= si)[None, :, :, None, None], kf[:, lag_f], 0.0)
            + jnp.where((si >= ti)[None, :, :, None, None], kb[:, lag_b], 0.0))
    eye_t = jnp.eye(t, dtype=F32)[None, :, :, None, None]
    eye_c = jnp.eye(S5_GROUP, dtype=F32)[None, None, None, :, :]
    toep = toep + eye_t * eye_c * d_skip.reshape(S5_GROUPS, 1, 1, S5_GROUP, 1)
    w_intra = toep.transpose(0, 1, 4, 2, 3).reshape(S5_GROUPS, t * S5_GROUP, t * S5_GROUP)
    mf = jnp.einsum('gcp,gtp->gptc', ccf, pwf[:, 1:t + 1, :])
    mb = jnp.einsum('gcp,gtp->gptc', ccb, pwb[:, t:0:-1, :])
    w_state = jnp.concatenate([jnp.real(mf), jnp.real(mb), -jnp.imag(mf), -jnp.imag(mb)], axis=1)
    w_state = w_state.reshape(S5_GROUPS, 4 * S5_STATE, t * S5_GROUP)
    w3 = jnp.concatenate([w_intra, w_state], axis=1)
    lam_r = jnp.concatenate([jnp.real(lamp[0]), jnp.real(lamp[1])], axis=-1)
    lam_i = jnp.concatenate([jnp.imag(lamp[0]), jnp.imag(lamp[1])], axis=-1)
    lamp = jnp.stack([lam_r, lam_i], axis=2)
    return w1.astype(BF16), w3.astype(BF16), lamp.astype(F32)


def _s5_kernel(u_ref, w1_ref, w3_ref, lam_ref, y_ref, *, nc, width, levels):
    u = u_ref[0, 0]
    loc = jnp.dot(u, w1_ref[0], preferred_element_type=F32)
    re = loc[:, :2 * S5_STATE]
    im = loc[:, 2 * S5_STATE:]
    row = lax.broadcasted_iota(jnp.int32, (nc, 2 * S5_STATE), 0)
    fwd = lax.broadcasted_iota(jnp.int32, (nc, 2 * S5_STATE), 1) < S5_STATE

    def shifted(x, d):
        down = jnp.where(row >= d, pltpu.roll(x, d, 0), 0.0)
        up = jnp.where(row < nc - d, pltpu.roll(x, nc - d, 0), 0.0)
        return jnp.where(fwd, down, up)

    for k in range(levels):
        d = 2 ** k
        if d >= nc:
            break
        lr = lam_ref[0, k, 0:1, :]
        li = lam_ref[0, k, 1:2, :]
        sr = shifted(re, d)
        si = shifted(im, d)
        re, im = re + sr * lr - si * li, im + sr * li + si * lr
    ent = jnp.concatenate([shifted(re, 1), shifted(im, 1)], axis=1)
    y = jnp.dot(u, w3_ref[0, :width, :], preferred_element_type=F32)
    y = y + jnp.dot(ent.astype(BF16), w3_ref[0, width:, :], preferred_element_type=F32)
    y_ref[0, 0] = y


def _s5_ssm(u5, w1, w3, lamp):
    g, b, nc, width = u5.shape
    levels = lamp.shape[1]
    kern = functools.partial(_s5_kernel, nc=nc, width=width, levels=levels)
    return pl.pallas_call(
        kern,
        grid=(g, b),
        in_specs=[pl.BlockSpec((1, 1, nc, width), lambda gi, bi: (gi, bi, 0, 0)),
                  pl.BlockSpec((1, width, 4 * S5_STATE), lambda gi, bi: (gi, 0, 0)),
                  pl.BlockSpec((1, width + 4 * S5_STATE, width), lambda gi, bi: (gi, 0, 0)),
                  pl.BlockSpec((1, levels, 2, 2 * S5_STATE), lambda gi, bi: (gi, 0, 0, 0))],
        out_specs=pl.BlockSpec((1, 1, nc, width), lambda gi, bi: (gi, bi, 0, 0)),
        out_shape=jax.ShapeDtypeStruct((g, b, nc, width), F32),
        compiler_params=_cparams("parallel", "parallel"),
        name="s5_ssm",
    )(u5, w1, w3, lamp)


def _rope_kernel(qq_ref, kk_ref, v_ref, cos_ref, sin_ref, qa_ref, qb_ref, ko_ref, vo_ref):
    width = DF_HEADS * 2 * DF_QK
    cos = jnp.concatenate([cos_ref[...]] * (width // LANES), axis=1)
    sin = jnp.concatenate([sin_ref[...]] * (width // LANES), axis=1)
    lane = lax.broadcasted_iota(jnp.int32, (1, width), 1)
    first_half = (lane % DF_QK) < (DF_QK // 2)

    def rot(x):
        partner = jnp.where(first_half, pltpu.roll(x, width - DF_QK // 2, 1),
                            pltpu.roll(x, DF_QK // 2, 1))
        return x * cos + partner * sin

    q = rot(qq_ref[0]) * (DF_QK ** -0.5 * LOG2E)
    is_a = (lane % (2 * DF_QK)) < DF_QK
    qa_ref[0] = jnp.where(is_a, q, 0.0).astype(BF16)
    qb_ref[0] = jnp.where(is_a, 0.0, q).astype(BF16)
    ko_ref[0] = rot(kk_ref[0]).astype(BF16)
    vo_ref[0] = v_ref[0].astype(BF16)


def _rope_prep(proj, cos_t, sin_t):
    b, s, _ = proj.shape
    tile = min(TILE_ROPE, s)
    w = DF_WIDTH
    spec = lambda cb: pl.BlockSpec((1, tile, w), lambda bi, i: (bi, i, cb))
    out = pl.BlockSpec((1, tile, w), lambda bi, i: (bi, i, 0))
    tab = pl.BlockSpec((tile, LANES), lambda bi, i: (i, 0))
    return pl.pallas_call(
        _rope_kernel,
        grid=(b, s // tile),
        in_specs=[spec(COL_QQ // w), spec(COL_KK // w), spec(COL_DV // w), tab, tab],
        out_specs=[out, out, out, out],
        out_shape=[jax.ShapeDtypeStruct((b, s, w), BF16)] * 4,
        compiler_params=_cparams("parallel", "parallel"),
        name="rope_prep",
    )(proj, proj, proj, cos_t, sin_t)


def _diff_attn_kernel(qa_ref, qb_ref, k_ref, v_ref, lq1_ref, lk1_ref, lq2_ref, lk2_ref, ng_ref,
                      o_ref, s_ref, *, tq, tk, nk, lam_init):
    q2 = (qa_ref[0], qb_ref[0])
    nt_dims = (((1,), (1,)), ((), ()))

    def fold(x, op, init):
        for j in range(tk // LANES):
            init = op(init, x[:, j * LANES:(j + 1) * LANES])
        return init

    m = [jnp.full((tq, LANES), -jnp.inf, F32)] * 2
    for c in range(nk):
        kc = k_ref[0, c * tk:(c + 1) * tk, :]
        for i in range(2):
            sc = lax.dot_general(q2[i], kc, nt_dims, preferred_element_type=F32)
            s_ref[c, i * tq:(i + 1) * tq, :] = sc
            m[i] = fold(sc, jnp.maximum, m[i])
    m = [jnp.max(x, axis=-1, keepdims=True) for x in m]

    l = [jnp.zeros((tq, LANES), F32)] * 2
    acc = [jnp.zeros((tq, DF_V), F32)] * 2
    for c in range(nk):
        vc = v_ref[0, c * tk:(c + 1) * tk, :]
        for i in range(2):
            e = jnp.exp2(s_ref[c, i * tq:(i + 1) * tq, :] - m[i])
            l[i] = fold(e, jnp.add, l[i])
            acc[i] = acc[i] + jnp.dot(e.astype(BF16), vc, preferred_element_type=F32)
    lam = (jnp.exp(jnp.sum(lq1_ref[...] * lk1_ref[...], axis=-1, keepdims=True))
           - jnp.exp(jnp.sum(lq2_ref[...] * lk2_ref[...], axis=-1, keepdims=True)) + lam_init)
    l = [jnp.sum(x, axis=-1, keepdims=True) for x in l]
    acc = acc[0] * (1.0 / l[0]) - acc[1] * (lam / l[1])
    ms = jnp.mean(acc * acc, axis=-1, keepdims=True)
    o_ref[0] = acc * lax.rsqrt(ms + RMS_EPS) * ng_ref[...] * (1.0 - lam_init)


def _diff_attn(qa, qb, kk, vv, lq1, lk1, lq2, lk2, norm_g, lam_init):
    b, s, _ = qa.shape
    tq = min(TILE_Q, s)
    tk = min(TILE_K, s)
    nk = s // tk
    kern = functools.partial(_diff_attn_kernel, tq=tq, tk=tk, nk=nk, lam_init=lam_init)
    qspec = pl.BlockSpec((1, tq, LANES), lambda bi, h, i: (bi, i, h))
    kspec = pl.BlockSpec((1, s, LANES), lambda bi, h, i: (bi, 0, h))
    vec64 = pl.BlockSpec((1, DF_QK), lambda bi, h, i: (0, 0))
    return pl.pallas_call(
        kern,
        grid=(b, DF_HEADS, s // tq),
        in_specs=[qspec, qspec, kspec, kspec, vec64, vec64, vec64, vec64,
                  pl.BlockSpec((1, DF_V), lambda bi, h, i: (0, 0))],
        out_specs=pl.BlockSpec((1, tq, DF_V), lambda bi, h, i: (bi, i, h)),
        out_shape=jax.ShapeDtypeStruct((b, s, DF_WIDTH), F32),
        scratch_shapes=[pltpu.VMEM((nk, 2 * tq, tk), F32)],
        compiler_params=_cparams("parallel", "parallel", "arbitrary"),
        name="diff_attn",
    )(qa, qb, kk, vv, lq1, lk1, lq2, lk2, norm_g)


def _layer_norm(x, g, b):
    mu = jnp.mean(x, axis=-1, keepdims=True)
    xc = x - mu
    var = jnp.mean(xc * xc, axis=-1, keepdims=True)
    return xc * lax.rsqrt(var + LN_EPS) * g + b


def _merge_kernel(x_ref, of_ref, ob_ref, z_ref, y5_ref, oc_ref, g0_ref, g1_ref, g2_ref,
                  bgate_ref, dnn_ref, gluw_ref, glub_ref, wbr_ref, wout_ref, lng_ref, lnb_ref,
                  o_ref):
    o = of_ref[0] + ob_ref[0]
    z = z_ref[0]
    parts = []
    for h in range(DN_HEADS):
        oh = o[:, h * DN_DIM:(h + 1) * DN_DIM]
        ms = jnp.mean(oh * oh, axis=-1, keepdims=True)
        parts.append(oh * lax.rsqrt(ms + RMS_EPS) * dnn_ref[...])
    o_a = jnp.concatenate(parts, axis=1) * _silu(z)
    y = jax.nn.gelu(y5_ref[0])
    o_b = y * _sigmoid(_bdot(y, gluw_ref[...]) + glub_ref[...])
    o_c = oc_ref[0]
    bg = bgate_ref[...]
    merged = (_sigmoid(g0_ref[0] + bg[:, 0:D_MODEL]) * _bdot(o_a, wbr_ref[0])
              + _sigmoid(g1_ref[0] + bg[:, D_MODEL:2 * D_MODEL]) * _bdot(o_b, wbr_ref[1])
              + _sigmoid(g2_ref[0] + bg[:, 2 * D_MODEL:]) * _bdot(o_c, wbr_ref[2]))
    hres = DEEPNORM_ALPHA * x_ref[0] + _bdot(merged, wout_ref[...])
    o_ref[0] = _layer_norm(hres, lng_ref[...], lnb_ref[...])


def _merge(x, o_f, o_b, proj, y5, o_c, b_gate, dn_norm, glu_w, glu_b, w_branch, w_out, ln_g, ln_b):
    b, s, _ = x.shape
    tile = min(TILE_MERGE, s)
    row = lambda w, cb=0: pl.BlockSpec((1, tile, w), lambda bi, i: (bi, i, cb))
    full = lambda shape: pl.BlockSpec(shape, lambda bi, i: (0,) * len(shape))
    gcb = COL_GATE // D_MODEL
    return pl.pallas_call(
        _merge_kernel,
        grid=(b, s // tile),
        in_specs=[row(D_MODEL), row(DN_WIDTH), row(DN_WIDTH), row(DN_WIDTH, COL_DN_Z // DN_WIDTH),
                  row(S5_WIDTH), row(DF_WIDTH),
                  row(D_MODEL, gcb), row(D_MODEL, gcb + 1), row(D_MODEL, gcb + 2),
                  full((1, N_BRANCH * D_MODEL)), full((1, DN_DIM)),
                  full((S5_WIDTH, S5_WIDTH)), full((1, S5_WIDTH)),
                  full((N_BRANCH, DN_WIDTH, D_MODEL)), full((D_MODEL, D_MODEL)),
                  full((1, D_MODEL)), full((1, D_MODEL))],
        out_specs=row(D_MODEL),
        out_shape=jax.ShapeDtypeStruct((b, s, D_MODEL), F32),
        compiler_params=_cparams("parallel", "parallel"),
        name="merge",
    )(x, o_f, o_b, proj, y5, o_c, proj, proj, proj, b_gate, dn_norm, glu_w, glu_b,
      w_branch, w_out, ln_g, ln_b)


def _xattn_kernel(x_ref, kv_ref, wq_ref, wo_ref, lng_ref, lnb_ref, o_ref):
    x = x_ref[0]
    q = _bdot(x, wq_ref[...])
    outs = []
    for h in range(XA_HEADS):
        qh = q[:, h * XA_DIM:(h + 1) * XA_DIM]
        kh = kv_ref[0, :, h * XA_DIM:(h + 1) * XA_DIM]
        vh = kv_ref[0, :, XA_WIDTH + h * XA_DIM:XA_WIDTH + (h + 1) * XA_DIM]
        sc = _bdot_nt(qh, kh) * (XA_DIM ** -0.5)
        sc = sc - jnp.max(sc, axis=-1, keepdims=True)
        e = jnp.exp(sc)
        pr = e / jnp.sum(e, axis=-1, keepdims=True)
        outs.append(_bdot(pr, vh))
    o = jnp.concatenate(outs, axis=1)
    hres = DEEPNORM_ALPHA * x + _bdot(o, wo_ref[...])
    o_ref[0] = _layer_norm(hres, lng_ref[...], lnb_ref[...])


def _xattn(x, kv, wq, wo, ln_g, ln_b):
    b, s, _ = x.shape
    tile = min(TILE_XA, s)
    mlen = kv.shape[1]
    row = pl.BlockSpec((1, tile, D_MODEL), lambda bi, i: (bi, i, 0))
    full = lambda shape: pl.BlockSpec(shape, lambda bi, i: (0,) * len(shape))
    return pl.pallas_call(
        _xattn_kernel,
        grid=(b, s // tile),
        in_specs=[row, pl.BlockSpec((1, mlen, 2 * XA_WIDTH), lambda bi, i: (bi, 0, 0)),
                  full((D_MODEL, XA_WIDTH)), full((XA_WIDTH, D_MODEL)),
                  full((1, D_MODEL)), full((1, D_MODEL))],
        out_specs=row,
        out_shape=jax.ShapeDtypeStruct((b, s, D_MODEL), F32),
        compiler_params=_cparams("parallel", "parallel"),
        name="xattn",
    )(x, kv, wq, wo, ln_g, ln_b)


def _router_kernel(x_ref, rw_ref, rb_ref, idx_ref, gate_ref, xb_ref):
    x = x_ref[...]
    xb_ref[...] = x.astype(BF16)
    logits = lax.dot_general(rw_ref[...], x, (((1,), (1,)), ((), ())), precision=HI,
                             preferred_element_type=F32) + rb_ref[...]
    tn = logits.shape[1]
    eid = lax.broadcasted_iota(jnp.int32, (N_EXPERTS, tn), 0)
    vals, ids = [], []
    cur = logits
    for _ in range(TOP_K):
        m = jnp.max(cur, axis=0, keepdims=True)
        sel = jnp.min(jnp.where(cur == m, eid, N_EXPERTS), axis=0, keepdims=True)
        vals.append(m)
        ids.append(sel)
        cur = jnp.where(eid == sel, -jnp.inf, cur)
    top_v = jnp.concatenate(vals, axis=0)
    e = jnp.exp(top_v - top_v[0:1])
    gate_ref[...] = e / jnp.sum(e, axis=0, keepdims=True)
    idx_ref[...] = jnp.concatenate(ids, axis=0)


def _router(x2d, router_w_t, router_b):
    n, _ = x2d.shape
    tile = min(TILE_ROUTER, n)
    return pl.pallas_call(
        _router_kernel,
        grid=(n // tile,),
        in_specs=[pl.BlockSpec((tile, D_MODEL), lambda i: (i, 0)),
                  pl.BlockSpec((N_EXPERTS, D_MODEL), lambda i: (0, 0)),
                  pl.BlockSpec((N_EXPERTS, 1), lambda i: (0, 0))],
        out_specs=[pl.BlockSpec((TOP_K, tile), lambda i: (0, i)),
                   pl.BlockSpec((TOP_K, tile), lambda i: (0, i)),
                   pl.BlockSpec((tile, D_MODEL), lambda i: (i, 0))],
        out_shape=[jax.ShapeDtypeStruct((TOP_K, n), jnp.int32),
                   jax.ShapeDtypeStruct((TOP_K, n), F32),
                   jax.ShapeDtypeStruct((n, D_MODEL), BF16)],
        compiler_params=_cparams("parallel"),
        name="router",
    )(x2d, router_w_t, router_b)


def _rank_kernel(idx_ref, rank_ref, cnt_ref, run_ref, *, tile):
    @pl.when(pl.program_id(0) == 0)
    def _():
        run_ref[...] = jnp.zeros_like(run_ref)

    idx = idx_ref[...]
    eid = lax.broadcasted_iota(jnp.int32, (N_EXPERTS, tile), 0)
    si = lax.broadcasted_iota(jnp.int32, (tile, tile), 0)
    ti = lax.broadcasted_iota(jnp.int32, (tile, tile), 1)
    before = (si < ti).astype(BF16)
    run = run_ref[...]
    ranks = []
    for k in range(TOP_K):
        onehot = eid == idx[k:k + 1]
        oh = onehot.astype(BF16)
        within = jnp.dot(oh, before, preferred_element_type=F32)
        ranks.append(jnp.sum(jnp.where(onehot, within + run, 0.0), axis=0, keepdims=True))
        run = run + jnp.sum(onehot.astype(F32), axis=1, keepdims=True)
    rank_ref[...] = jnp.concatenate(ranks, axis=0).astype(jnp.int32)
    run_ref[...] = run
    cnt_ref[...] = run.astype(jnp.int32)


def _rank(top_i):
    _, n = top_i.shape
    tile = min(TILE_ROUTER, n)
    kern = functools.partial(_rank_kernel, tile=tile)
    return pl.pallas_call(
        kern,
        grid=(n // tile,),
        in_specs=[pl.BlockSpec((TOP_K, tile), lambda i: (0, i))],
        out_specs=[pl.BlockSpec((TOP_K, tile), lambda i: (0, i)),
                   pl.BlockSpec((N_EXPERTS, 1), lambda i: (0, 0))],
        out_shape=[jax.ShapeDtypeStruct((TOP_K, n), jnp.int32),
                   jax.ShapeDtypeStruct((N_EXPERTS, 1), jnp.int32)],
        scratch_shapes=[pltpu.VMEM((N_EXPERTS, 1), F32)],
        compiler_params=_cparams("arbitrary"),
        name="moe_rank",
    )(top_i)


def _expert_kernel(be_ref, x_ref, wg_ref, wu_ref, bg_ref, bu_ref, wd_ref, bd_ref, o_ref):
    del be_ref
    x = x_ref[...]
    hg = jnp.dot(x, wg_ref[0], preferred_element_type=F32) + bg_ref[0]
    hu = jnp.dot(x, wu_ref[0], preferred_element_type=F32) + bu_ref[0]
    gt = jnp.minimum(hg, SWIGLU_LIMIT)
    up = jnp.clip(hu, -SWIGLU_LIMIT, SWIGLU_LIMIT)
    act = (up + 1.0) * gt * _sigmoid(SWIGLU_ALPHA * gt)
    o_ref[...] = jnp.dot(act.astype(BF16), wd_ref[0], preferred_element_type=F32) + bd_ref[0]


def _experts(x_rows, blk_e, w_g, w_u, b_g, b_u, w_d, b_d, *, block):
    rows, _ = x_rows.shape
    wspec = lambda k, m: pl.BlockSpec((1, k, m), lambda i, be: (be[i], 0, 0))
    grid_spec = pltpu.PrefetchScalarGridSpec(
        num_scalar_prefetch=1,
        grid=(rows // block,),
        in_specs=[pl.BlockSpec((block, D_MODEL), lambda i, be: (i, 0)),
                  wspec(D_MODEL, D_FF), wspec(D_MODEL, D_FF), wspec(1, D_FF), wspec(1, D_FF),
                  wspec(D_FF, D_MODEL), wspec(1, D_MODEL)],
        out_specs=pl.BlockSpec((block, D_MODEL), lambda i, be: (i, 0)),
    )
    return pl.pallas_call(
        _expert_kernel,
        grid_spec=grid_spec,
        out_shape=jax.ShapeDtypeStruct((rows, D_MODEL), F32),
        compiler_params=_cparams("arbitrary"),
        name="moe_experts",
    )(blk_e, x_rows, w_g, w_u, b_g, b_u, w_d, b_d)


def _combine_kernel(x_ref, y_ref, gate_ref, lng_ref, lnb_ref, o_ref):
    gate = gate_ref[...]
    h = y_ref[0] * gate[:, 0:1].astype(F32)
    for k in range(1, TOP_K):
        h = h + y_ref[k] * gate[:, k:k + 1]
    o_ref[...] = _layer_norm(DEEPNORM_ALPHA * x_ref[...] + h, lng_ref[...], lnb_ref[...])


def _combine(x2d, y_g, gate_t, ln_g, ln_b):
    n, _ = x2d.shape
    tile = min(TILE_COMBINE, n)
    return pl.pallas_call(
        _combine_kernel,
        grid=(n // tile,),
        in_specs=[pl.BlockSpec((tile, D_MODEL), lambda i: (i, 0)),
                  pl.BlockSpec((TOP_K, tile, D_MODEL), lambda i: (0, i, 0)),
                  pl.BlockSpec((tile, TOP_K), lambda i: (i, 0)),
                  pl.BlockSpec((1, D_MODEL), lambda i: (0, 0)),
                  pl.BlockSpec((1, D_MODEL), lambda i: (0, 0))],
        out_specs=pl.BlockSpec((tile, D_MODEL), lambda i: (i, 0)),
        out_shape=jax.ShapeDtypeStruct((n, D_MODEL), F32),
        compiler_params=_cparams("parallel"),
        name="moe_combine",
    )(x2d, y_g, gate_t, ln_g, ln_b)


def _split_kernel(w_ref, se_ref, so_ref, g_ref, u_ref):
    w = w_ref[0].astype(BF16)
    g_ref[0] = jnp.dot(w, se_ref[...], preferred_element_type=F32).astype(BF16)
    u_ref[0] = jnp.dot(w, so_ref[...], preferred_element_type=F32).astype(BF16)


def _split_gate_up(w_gu):
    e, dm, f2 = w_gu.shape
    f = f2 // 2
    tr = 256
    col = jnp.arange(f2, dtype=jnp.int32)[:, None]
    out = jnp.arange(f, dtype=jnp.int32)[None, :]
    sel_even = (col == 2 * out).astype(BF16)
    sel_odd = (col == 2 * out + 1).astype(BF16)
    sel = pl.BlockSpec((f2, f), lambda ei, i: (0, 0))
    ospec = pl.BlockSpec((1, tr, f), lambda ei, i: (ei, i, 0))
    return pl.pallas_call(
        _split_kernel,
        grid=(e, dm // tr),
        in_specs=[pl.BlockSpec((1, tr, f2), lambda ei, i: (ei, i, 0)), sel, sel],
        out_specs=[ospec, ospec],
        out_shape=[jax.ShapeDtypeStruct((e, dm, f), BF16)] * 2,
        compiler_params=_cparams("parallel", "parallel"),
        name="split_gate_up",
    )(w_gu, sel_even, sel_odd)


def _moe(x, router_w, router_b, w_gu, b_gu, w_down, b_down, ln_g, ln_b):
    b, s, dm = x.shape
    n = b * s
    x2d = x.reshape(n, dm)
    top_i, gate, xb = _router(x2d, router_w.T, router_b.reshape(N_EXPERTS, 1))
    rank, counts = _rank(top_i)
    block = MOE_BLOCK
    m = n * TOP_K
    nblk = -(-m // block) + N_EXPERTS
    rows = nblk * block
    counts = counts.reshape(N_EXPERTS)
    padded = (counts + block - 1) // block * block
    pad_end = jnp.cumsum(padded)
    pad_start = pad_end - padded
    dest = pad_start[top_i] + rank
    tok = jnp.broadcast_to(jnp.arange(n, dtype=jnp.int32)[None, :], (TOP_K, n))
    row_tok = jnp.full((rows,), n, jnp.int32).at[dest.reshape(-1)].set(tok.reshape(-1))
    blk_e = jnp.minimum(jnp.searchsorted(pad_end, jnp.arange(nblk, dtype=pad_end.dtype) * block,
                                         side='right'), N_EXPERTS - 1).astype(jnp.int32)
    x_rows = jnp.concatenate([xb, jnp.zeros((1, dm), BF16)], axis=0)[row_tok]
    w_g, w_u = _split_gate_up(w_gu)
    b_g = b_gu[:, None, 0::2]
    b_u = b_gu[:, None, 1::2]
    y_rows = _experts(x_rows, blk_e, w_g, w_u, b_g, b_u, w_down.astype(BF16),
                      b_down[:, None, :], block=block)
    y_g = y_rows[dest]
    out = _combine(x2d, y_g, gate.T, ln_g.reshape(1, dm), ln_b.reshape(1, dm))
    return out.reshape(b, s, dm)


def _rope_tables(s):
    half = DF_QK // 2
    inv_freq = jnp.power(ROPE_THETA, -jnp.arange(half, dtype=F32) / half)
    ang = jnp.arange(s, dtype=F32)[:, None] * inv_freq[None, :]
    cos = jnp.cos(ang)
    sin = jnp.sin(ang)
    cos_t = jnp.concatenate([cos, cos, cos, cos], axis=1)
    sin_t = jnp.concatenate([-sin, sin, -sin, sin], axis=1)
    return cos_t, sin_t


def _layer(x, mem, l, tabs, p):
    b, s, dm = x.shape
    n = b * s
    w_in = jnp.concatenate([p['w_in'][l], jnp.zeros((dm, 1), F32)], axis=1)[:, _IN_PERM]
    proj = _matmul(x.reshape(n, dm), w_in.astype(BF16), tm=TILE_MM, tn=LANES * 19)
    proj = proj.reshape(b, s, PROJ_WIDTH)

    pad8 = jnp.zeros((8,), F32)
    pad112 = jnp.zeros((LANES - 16,), F32)
    nega = jnp.concatenate([pad8, -jnp.exp(p['dn_A_log'][l].reshape(8)), pad112]).reshape(1, LANES)
    dtb = jnp.concatenate([pad8, p['dn_dt_bias'][l].reshape(8), pad112]).reshape(1, LANES)
    qkv, bg = _dn_prep(proj, p['dn_conv'][l], nega, dtb, chunk=DN_CHUNK)
    bgr = bg[:, :, :16].reshape(b, s // DN_CHUNK, DN_CHUNK, 16).transpose(0, 1, 3, 2)
    o_f, o_b = _dn_scan(qkv, bg, bgr, chunk=DN_CHUNK)

    t5 = min(S5_CHUNK, s)
    nc = s // t5
    w1, w3, lamp = _s5_operators(p['s5_lam_re'][l], p['s5_lam_im'][l], p['s5_log_dt'][l],
                                 p['s5_B_re'][l], p['s5_B_im'][l], p['s5_C_re'][l],
                                 p['s5_C_im'][l], p['s5_D'][l], chunk=t5, nc=nc)
    u5 = proj[:, :, COL_S5_U:COL_S5_U + S5_WIDTH].reshape(b, nc, t5, S5_GROUPS, S5_GROUP)
    u5 = u5.transpose(3, 0, 1, 2, 4).reshape(S5_GROUPS, b, nc, t5 * S5_GROUP).astype(BF16)
    y5 = _s5_ssm(u5, w1, w3, lamp)
    y5 = y5.reshape(S5_GROUPS, b, nc, t5, S5_GROUP).transpose(1, 2, 3, 0, 4).reshape(b, s, S5_WIDTH)

    lam_init = 0.8 - 0.6 * math.exp(-0.3 * l)
    qa, qb, kk, vv = _rope_prep(proj, *tabs)
    vec = lambda name: p[name][l].reshape(1, DF_QK)
    o_c = _diff_attn(qa, qb, kk, vv, vec('df_lq1'), vec('df_lk1'), vec('df_lq2'), vec('df_lk2'),
                     p['df_norm'][l].reshape(1, DF_V), lam_init)

    x = _merge(x, o_f, o_b, proj, y5, o_c, p['b_gate'][l].reshape(1, -1),
               p['dn_norm'][l].reshape(1, DN_DIM), p['s5_glu_w'][l].astype(BF16),
               p['s5_glu_b'][l].reshape(1, -1), p['w_branch'][l].astype(BF16),
               p['w_out'][l].astype(BF16), p['ln1_g'][l].reshape(1, dm), p['ln1_b'][l].reshape(1, dm))

    mlen = mem.shape[1]
    kv = _matmul(mem.reshape(b * mlen, dm), p['xa_wkv'][l].astype(BF16), tm=512, tn=1024)
    x = _xattn(x, kv.reshape(b, mlen, 2 * XA_WIDTH), p['xa_wq'][l].astype(BF16),
               p['xa_wo'][l].astype(BF16), p['ln2_g'][l].reshape(1, dm), p['ln2_b'][l].reshape(1, dm))

    x = _moe(x, p['router_w'][l], p['router_b'][l], p['e_w_gu'][l], p['e_b_gu'][l],
             p['e_w_down'][l], p['e_b_down'][l], p['ln3_g'][l], p['ln3_b'][l])
    return x


def kernel(x_prompt, x_sample, mem_prompt, mem_sample, w_in, b_gate, dn_conv, dn_A_log, dn_dt_bias,
           dn_norm, s5_lam_re, s5_lam_im, s5_log_dt, s5_B_re, s5_B_im, s5_C_re, s5_C_im, s5_D,
           s5_glu_w, s5_glu_b, df_lq1, df_lk1, df_lq2, df_lk2, df_norm, w_branch, w_out,
           ln1_g, ln1_b, xa_wq, xa_wkv, xa_wo, ln2_g, ln2_b, router_w, router_b,
           e_w_gu, e_b_gu, e_w_down, e_b_down, ln3_g, ln3_b):
    p = dict(w_in=w_in, b_gate=b_gate, dn_conv=dn_conv, dn_A_log=dn_A_log, dn_dt_bias=dn_dt_bias,
             dn_norm=dn_norm, s5_lam_re=s5_lam_re, s5_lam_im=s5_lam_im, s5_log_dt=s5_log_dt,
             s5_B_re=s5_B_re, s5_B_im=s5_B_im, s5_C_re=s5_C_re, s5_C_im=s5_C_im, s5_D=s5_D,
             s5_glu_w=s5_glu_w, s5_glu_b=s5_glu_b, df_lq1=df_lq1, df_lk1=df_lk1, df_lq2=df_lq2,
             df_lk2=df_lk2, df_norm=df_norm, w_branch=w_branch, w_out=w_out, ln1_g=ln1_g,
             ln1_b=ln1_b, xa_wq=xa_wq, xa_wkv=xa_wkv, xa_wo=xa_wo, ln2_g=ln2_g, ln2_b=ln2_b,
             router_w=router_w, router_b=router_b, e_w_gu=e_w_gu, e_b_gu=e_b_gu,
             e_w_down=e_w_down, e_b_down=e_b_down, ln3_g=ln3_g, ln3_b=ln3_b)
    nb = x_prompt.shape[0]
    x = jnp.concatenate([x_prompt, x_sample], axis=0)
    mem = jnp.concatenate([mem_prompt, mem_sample], axis=0)
    tabs = _rope_tables(x.shape[1])
    for l in range(DEPTH):
        x = _layer(x, mem, l, tabs, p)
    return (x[:nb], x[nb:])
```

```python
import functools
import math

import numpy as np
import jax
import jax.numpy as jnp
from jax import lax
from jax.experimental import pallas as pl
from jax.experimental.pallas import tpu as pltpu

F32 = jnp.float32
BF16 = jnp.bfloat16
HI = lax.Precision.HIGHEST

D_MODEL = 1024
DN_HEADS = 4
DN_DIM = 128
DN_WIDTH = DN_HEADS * DN_DIM
DN_CONV = 4
S5_GROUP = 16
S5_WIDTH = 512
S5_GROUPS = S5_WIDTH // S5_GROUP
S5_STATE = 64
DF_HEADS = 4
DF_QK = 64
DF_V = 128
DF_WIDTH = DF_HEADS * DF_V
ROPE_THETA = 10000.0
XA_HEADS = 4
XA_DIM = 128
XA_WIDTH = XA_HEADS * XA_DIM
N_EXPERTS = 32
TOP_K = 4
D_FF = 1024
SWIGLU_LIMIT = 7.0
SWIGLU_ALPHA = 1.702
N_BRANCH = 3
DEPTH = 2
DEEPNORM_ALPHA = (2 * DEPTH) ** 0.25
LN_EPS = 1e-5
RMS_EPS = 1e-6
LOG2E = 1.4426950408889634

LANES = 128
SUBLANES = 8
VMEM_LIMIT = 56 * 1024 * 1024

COL_DN_QKV = 0
COL_DN_Z = 1536
COL_S5_U = 2048
COL_QQ = 2560
COL_KK = 3072
COL_DV = 3584
COL_GATE = 4096
COL_BA = 7168
PROJ_WIDTH = 7296

TILE_MM = 1024
TILE_DN_PREP = 512
TILE_DN_SCAN = 256
DN_CHUNK = 64
S5_CHUNK = 64
TILE_ROPE = 512
TILE_Q = 256
TILE_K = 512
TILE_MERGE = 256
TILE_XA = 512
TILE_ROUTER = 512
MOE_BLOCK = 512
TILE_COMBINE = 256


def _in_proj_perm():
    o_dn_b, o_dn_a, o_s5 = 2048, 2056, 2064
    o_q1, o_q2, o_k1, o_k2, o_dv, o_gate = 2576, 2832, 3088, 3344, 3600, 4112
    perm = np.full((PROJ_WIDTH,), -1, np.int64)
    perm[0:2048] = np.arange(2048)
    perm[COL_S5_U:COL_S5_U + 512] = o_s5 + np.arange(512)
    for h in range(DF_HEADS):
        j = np.arange(DF_QK)
        perm[COL_QQ + h * 128 + j] = o_q1 + h * DF_QK + j
        perm[COL_QQ + h * 128 + 64 + j] = o_q2 + h * DF_QK + j
        perm[COL_KK + h * 128 + j] = o_k1 + h * DF_QK + j
        perm[COL_KK + h * 128 + 64 + j] = o_k2 + h * DF_QK + j
    perm[COL_DV:COL_DV + 512] = o_dv + np.arange(512)
    perm[COL_GATE:COL_GATE + 3072] = o_gate + np.arange(3072)
    perm[COL_BA:COL_BA + 8] = o_dn_b + np.arange(8)
    perm[COL_BA + 8:COL_BA + 16] = o_dn_a + np.arange(8)
    return perm


_IN_PERM = _in_proj_perm()


def _cparams(*sem):
    return pltpu.CompilerParams(dimension_semantics=sem, vmem_limit_bytes=VMEM_LIMIT)


def _bdot(a, b):
    return jnp.dot(a.astype(BF16), b.astype(BF16), preferred_element_type=F32)


def _bdot_nt(a, b):
    return lax.dot_general(a.astype(BF16), b.astype(BF16), (((1,), (1,)), ((), ())),
                           preferred_element_type=F32)


def _bdot_tn(a, b):
    return lax.dot_general(a.astype(BF16), b.astype(BF16), (((0,), (0,)), ((), ())),
                           preferred_element_type=F32)


def _sigmoid(x):
    return 1.0 / (1.0 + jnp.exp(-x))


def _silu(x):
    return x * _sigmoid(x)


def _mm_kernel(x_ref, w_ref, o_ref):
    o_ref[...] = jnp.dot(x_ref[...].astype(BF16), w_ref[...],
                         preferred_element_type=F32).astype(o_ref.dtype)


def _matmul(x, w, *, tm, tn, out_dtype=F32):
    n, k = x.shape
    m = w.shape[1]
    tm = min(tm, n)
    tn = min(tn, m)
    return pl.pallas_call(
        _mm_kernel,
        grid=(n // tm, m // tn),
        in_specs=[pl.BlockSpec((tm, k), lambda i, j: (i, 0)),
                  pl.BlockSpec((k, tn), lambda i, j: (0, j))],
        out_specs=pl.BlockSpec((tm, tn), lambda i, j: (i, j)),
        out_shape=jax.ShapeDtypeStruct((n, m), out_dtype),
        compiler_params=_cparams("parallel", "arbitrary"),
        name="matmul",
    )(x, w)


def _dn_prep_kernel(x_ref, prev_ref, next_ref, ba_ref, cw_ref, nega_ref, dtb_ref,
                    qkv_ref, bg_ref, *, tile, chunk):
    i = pl.program_id(1)
    last = pl.num_programs(1) - 1
    x = x_ref[0]
    prev = jnp.where(i > 0, prev_ref[0], 0.0)
    nxt = jnp.where(i < last, next_ref[0], 0.0)
    ext = jnp.concatenate([prev, x, nxt], axis=0)
    cw = cw_ref[...]
    acc = cw[0:1] * ext[6:6 + tile]
    for j in range(1, DN_CONV):
        acc = acc + cw[j:j + 1] * ext[6 + j:6 + j + tile]
    act = _silu(acc)
    for h in range(DN_HEADS):
        lo = h * DN_DIM
        qh = act[:, lo:lo + DN_DIM]
        qn = qh * lax.rsqrt(jnp.sum(qh * qh, axis=-1, keepdims=True) + 1e-6)
        qkv_ref[0, :, lo:lo + DN_DIM] = qn * (DN_DIM ** -0.5)
        kh = act[:, DN_WIDTH + lo:DN_WIDTH + lo + DN_DIM]
        qkv_ref[0, :, DN_WIDTH + lo:DN_WIDTH + lo + DN_DIM] = kh * lax.rsqrt(
            jnp.sum(kh * kh, axis=-1, keepdims=True) + 1e-6)
    qkv_ref[0, :, 2 * DN_WIDTH:] = act[:, 2 * DN_WIDTH:]

    ba = ba_ref[0]
    beta = _sigmoid(ba)
    z = ba + dtb_ref[...]
    softplus = jnp.maximum(z, 0.0) + jnp.log(1.0 + jnp.exp(-jnp.abs(z)))
    g = nega_ref[...] * softplus
    row = lax.broadcasted_iota(jnp.int32, (chunk, chunk), 0)
    col = lax.broadcasted_iota(jnp.int32, (chunk, chunk), 1)
    lower = (row >= col).astype(F32)
    upper = (row <= col).astype(F32)
    lane = lax.broadcasted_iota(jnp.int32, (chunk, LANES), 1)
    fwd_lane = lane < 8 + DN_HEADS
    for c in range(tile // chunk):
        gc = g[c * chunk:(c + 1) * chunk]
        cf = jnp.dot(lower, gc, precision=HI, preferred_element_type=F32)
        cb = jnp.dot(upper, gc, precision=HI, preferred_element_type=F32)
        gcum = jnp.where(fwd_lane, cf, cb)
        bg_ref[0, c * chunk:(c + 1) * chunk, :] = jnp.where(
            lane < 8, beta[c * chunk:(c + 1) * chunk], gcum)


def _dn_prep(proj, conv_w, nega, dtb, *, chunk):
    b, s, _ = proj.shape
    tile = min(TILE_DN_PREP, s)
    nt = s // tile
    r8 = tile // SUBLANES
    kern = functools.partial(_dn_prep_kernel, tile=tile, chunk=chunk)
    return pl.pallas_call(
        kern,
        grid=(b, nt),
        in_specs=[
            pl.BlockSpec((1, tile, 3 * DN_WIDTH), lambda bi, i: (bi, i, 0)),
            pl.BlockSpec((1, SUBLANES, 3 * DN_WIDTH),
                         lambda bi, i: (bi, jnp.maximum(i * r8 - 1, 0), 0)),
            pl.BlockSpec((1, SUBLANES, 3 * DN_WIDTH),
                         lambda bi, i: (bi, jnp.minimum((i + 1) * r8, s // SUBLANES - 1), 0)),
            pl.BlockSpec((1, tile, LANES), lambda bi, i: (bi, i, COL_BA // LANES)),
            pl.BlockSpec((DN_CONV, 3 * DN_WIDTH), lambda bi, i: (0, 0)),
            pl.BlockSpec((1, LANES), lambda bi, i: (0, 0)),
            pl.BlockSpec((1, LANES), lambda bi, i: (0, 0)),
        ],
        out_specs=[pl.BlockSpec((1, tile, 3 * DN_WIDTH), lambda bi, i: (bi, i, 0)),
                   pl.BlockSpec((1, tile, LANES), lambda bi, i: (bi, i, 0))],
        out_shape=[jax.ShapeDtypeStruct((b, s, 3 * DN_WIDTH), F32),
                   jax.ShapeDtypeStruct((b, s, LANES), F32)],
        compiler_params=_cparams("parallel", "parallel"),
        name="dn_prep",
    )(proj, proj, proj, proj, conv_w, nega, dtb)


def _dn_step(chains, chunk):
    ri = lax.broadcasted_iota(jnp.int32, (chunk, chunk), 0)
    ci = lax.broadcasted_iota(jnp.int32, (chunk, chunk), 1)
    pre = []
    for d, h, q, k, v, st, bgc, rows in chains:
        li = d * DN_HEADS + h
        beta = bgc[:, li:li + 1]
        gcol = bgc[:, 8 + li:8 + li + 1]
        grow = rows[8 + li:8 + li + 1, :]
        gtot = gcol[chunk - 1:chunk] if d == 0 else gcol[0:1]
        incl = (ri >= ci) if d == 0 else (ri <= ci)
        strict = (ri > ci) if d == 0 else (ri < ci)
        eg = jnp.exp(gcol)
        kb = k * beta
        decay = jnp.where(incl, jnp.exp(jnp.where(incl, gcol - grow, 0.0)), 0.0)
        pre.append(dict(k=k, st=st, kb=kb, qg=q * eg, kd=k * jnp.exp(gtot - gcol), decay=decay,
                        strict=strict, x2=jnp.concatenate([v * beta, kb * eg], axis=1),
                        sdec=jnp.exp(gtot), qkb=jnp.concatenate([q, kb], axis=0)))
    qk = [_bdot_nt(c['qkb'], c['k']) for c in pre]
    intra = [x[:chunk] * c['decay'] for x, c in zip(qk, pre)]
    p = [-jnp.where(c['strict'], x[chunk:] * c['decay'], 0.0) for x, c in zip(qk, pre)]
    r = p
    p = [_bdot(x, x) for x in p]
    for _ in range(int(math.log2(chunk)) - 2):
        rp = [_bdot(jnp.concatenate([ri_, pi_], axis=0), pi_) for ri_, pi_ in zip(r, p)]
        r = [ri_ + pi_ + x[:chunk] for ri_, pi_, x in zip(r, p, rp)]
        p = [x[chunk:] for x in rp]
    r = [ri_ + pi_ + _bdot(ri_, pi_) for ri_, pi_ in zip(r, p)]
    uw = [c['x2'] + _bdot(ri_, c['x2']) for ri_, c in zip(r, pre)]
    ws = [_bdot(jnp.concatenate([x[:, DN_DIM:], c['qg']], axis=0), c['st'])
          for x, c in zip(uw, pre)]
    v_new = [x[:, :DN_DIM] - y[:chunk] for x, y in zip(uw, ws)]
    out = [y[chunk:] + _bdot(it, vn) for y, it, vn in zip(ws, intra, v_new)]
    st_new = [c['st'] * c['sdec'] + _bdot_tn(c['kd'], vn) for c, vn in zip(pre, v_new)]
    return list(zip(out, st_new))


def _dn_scan_kernel(xf_ref, xb_ref, gf_ref, gb_ref, rf_ref, rb_ref, of_ref, ob_ref, st_ref,
                    *, tile, chunk):
    @pl.when(pl.program_id(1) == 0)
    def _():
        st_ref[...] = jnp.zeros_like(st_ref)

    nct = tile // chunk

    def body(j, carry):
        loaded = []
        for d in range(2):
            c = j if d == 0 else nct - 1 - j
            r0 = pl.multiple_of(c * chunk, chunk)
            x_ref, g_ref, r_ref = (xf_ref, gf_ref, rf_ref) if d == 0 else (xb_ref, gb_ref, rb_ref)
            bgc = g_ref[0, pl.ds(r0, chunk), :]
            rows = r_ref[0, c]
            for h in range(DN_HEADS):
                lo = h * DN_DIM
                q = x_ref[0, pl.ds(r0, chunk), lo:lo + DN_DIM]
                k = x_ref[0, pl.ds(r0, chunk), DN_WIDTH + lo:DN_WIDTH + lo + DN_DIM]
                v = x_ref[0, pl.ds(r0, chunk), 2 * DN_WIDTH + lo:2 * DN_WIDTH + lo + DN_DIM]
                loaded.append((d, h, r0, q, k, v, st_ref[d * DN_HEADS + h], bgc, rows))
        results = _dn_step([(d, h, q, k, v, st, bgc, rows)
                            for d, h, r0, q, k, v, st, bgc, rows in loaded], chunk)
        for (d, h, r0, *_), (out, st_new) in zip(loaded, results):
            o_ref = of_ref if d == 0 else ob_ref
            o_ref[0, pl.ds(r0, chunk), h * DN_DIM:(h + 1) * DN_DIM] = out
            st_ref[d * DN_HEADS + h] = st_new
        return carry

    lax.fori_loop(0, nct, body, 0)


def _dn_scan(qkv, bg, bgr, *, chunk):
    b, s, _ = qkv.shape
    tile = min(TILE_DN_SCAN, s)
    nt = s // tile
    nct = tile // chunk
    kern = functools.partial(_dn_scan_kernel, tile=tile, chunk=chunk)
    fwd = lambda bi, i: (bi, i, 0)
    bwd = lambda bi, i: (bi, nt - 1 - i, 0)
    fwd4 = lambda bi, i: (bi, i, 0, 0)
    bwd4 = lambda bi, i: (bi, nt - 1 - i, 0, 0)
    return pl.pallas_call(
        kern,
        grid=(b, nt),
        in_specs=[pl.BlockSpec((1, tile, 3 * DN_WIDTH), fwd),
                  pl.BlockSpec((1, tile, 3 * DN_WIDTH), bwd),
                  pl.BlockSpec((1, tile, LANES), fwd),
                  pl.BlockSpec((1, tile, LANES), bwd),
                  pl.BlockSpec((1, nct, 16, chunk), fwd4),
                  pl.BlockSpec((1, nct, 16, chunk), bwd4)],
        out_specs=[pl.BlockSpec((1, tile, DN_WIDTH), fwd),
                   pl.BlockSpec((1, tile, DN_WIDTH), bwd)],
        out_shape=[jax.ShapeDtypeStruct((b, s, DN_WIDTH), F32),
                   jax.ShapeDtypeStruct((b, s, DN_WIDTH), F32)],
        scratch_shapes=[pltpu.VMEM((2 * DN_HEADS, DN_DIM, DN_DIM), F32)],
        compiler_params=_cparams("parallel", "arbitrary"),
        name="dn_scan",
    )(qkv, qkv, bg, bg, bgr, bgr)


def _s5_operators(lam_re, lam_im, log_dt, b_re, b_im, c_re, c_im, d_skip, *, chunk, nc):
    t = chunk
    levels = max(1, int(math.log2(nc)))
    tau = jnp.arange(t + 1, dtype=F32)
    kern = []
    lamp = []
    for d in range(2):
        lam = lax.complex(lam_re[d], lam_im[d])
        dt = jnp.exp(log_dt[d])[:, None]
        lam_dt = lam * dt
        lam_bar = jnp.exp(lam_dt)
        bbar = ((lam_bar - 1.0) / lam)[:, :, None] * lax.complex(b_re[d], b_im[d])
        cc = lax.complex(c_re[d], c_im[d])
        pw = jnp.exp(lam_dt[:, None, :] * tau[None, :, None].astype(jnp.complex64))
        kern.append((pw, bbar, cc))
        steps = (t * 2.0 ** jnp.arange(levels, dtype=F32)).astype(jnp.complex64)
        lamp.append(jnp.exp(lam_dt[:, None, :] * steps[None, :, None]))
    (pwf, bbf, ccf), (pwb, bbb, ccb) = kern
    ef = jnp.einsum('gsp,gpc->gscp', pwf[:, t - 1::-1, :], bbf)
    eb = jnp.einsum('gsp,gpc->gscp', pwb[:, :t, :], bbb)
    w1 = jnp.concatenate([jnp.real(ef), jnp.real(eb), jnp.imag(ef), jnp.imag(eb)], axis=-1)
    w1 = w1.reshape(S5_GROUPS, t * S5_GROUP, 4 * S5_STATE)
    kf = jnp.real(jnp.einsum('gcp,gtp,gpd->gtcd', ccf, pwf[:, :t, :], bbf))
    kb = jnp.real(jnp.einsum('gcp,gtp,gpd->gtcd', ccb, pwb[:, :t, :], bbb))
    si = jnp.arange(t)[:, None]
    ti = jnp.arange(t)[None, :]
    lag_f = jnp.clip(ti - si, 0, t - 1)
    lag_b = jnp.clip(si - ti, 0, t - 1)
    toep = (jnp.where((ti >= si)[None, :, :, None, None], kf[:, lag_f], 0.0)
            + jnp.where((si >= ti)[None, :, :, None, None], kb[:, lag_b], 0.0))
    eye_t = jnp.eye(t, dtype=F32)[None, :, :, None, None]
    eye_c = jnp.eye(S5_GROUP, dtype=F32)[None, None, None, :, :]
    toep = toep + eye_t * eye_c * d_skip.reshape(S5_GROUPS, 1, 1, S5_GROUP, 1)
    w_intra = toep.transpose(0, 1, 4, 2, 3).reshape(S5_GROUPS, t * S5_GROUP, t * S5_GROUP)
    mf = jnp.einsum('gcp,gtp->gptc', ccf, pwf[:, 1:t + 1, :])
    mb = jnp.einsum('gcp,gtp->gptc', ccb, pwb[:, t:0:-1, :])
    w_state = jnp.concatenate([jnp.real(mf), jnp.real(mb), -jnp.imag(mf), -jnp.imag(mb)], axis=1)
    w_state = w_state.reshape(S5_GROUPS, 4 * S5_STATE, t * S5_GROUP)
    w3 = jnp.concatenate([w_intra, w_state], axis=1)
    lam_r = jnp.concatenate([jnp.real(lamp[0]), jnp.real(lamp[1])], axis=-1)
    lam_i = jnp.concatenate([jnp.imag(lamp[0]), jnp.imag(lamp[1])], axis=-1)
    lamp = jnp.stack([lam_r, lam_i], axis=2)
    return w1.astype(BF16), w3.astype(BF16), lamp.astype(F32)


def _s5_kernel(u_ref, w1_ref, w3_ref, lam_ref, y_ref, *, nc, width, levels):
    u = u_ref[0, 0]
    loc = jnp.dot(u, w1_ref[0], preferred_element_type=F32)
    re = loc[:, :2 * S5_STATE]
    im = loc[:, 2 * S5_STATE:]
    row = lax.broadcasted_iota(jnp.int32, (nc, 2 * S5_STATE), 0)
    fwd = lax.broadcasted_iota(jnp.int32, (nc, 2 * S5_STATE), 1) < S5_STATE

    def shifted(x, d):
        down = jnp.where(row >= d, pltpu.roll(x, d, 0), 0.0)
        up = jnp.where(row < nc - d, pltpu.roll(x, nc - d, 0), 0.0)
        return jnp.where(fwd, down, up)

    for k in range(levels):
        d = 2 ** k
        if d >= nc:
            break
        lr = lam_ref[0, k, 0:1, :]
        li = lam_ref[0, k, 1:2, :]
        sr = shifted(re, d)
        si = shifted(im, d)
        re, im = re + sr * lr - si * li, im + sr * li + si * lr
    ent = jnp.concatenate([shifted(re, 1), shifted(im, 1)], axis=1)
    y = jnp.dot(u, w3_ref[0, :width, :], preferred_element_type=F32)
    y = y + jnp.dot(ent.astype(BF16), w3_ref[0, width:, :], preferred_element_type=F32)
    y_ref[0, 0] = y


def _s5_ssm(u5, w1, w3, lamp):
    g, b, nc, width = u5.shape
    levels = lamp.shape[1]
    kern = functools.partial(_s5_kernel, nc=nc, width=width, levels=levels)
    return pl.pallas_call(
        kern,
        grid=(g, b),
        in_specs=[pl.BlockSpec((1, 1, nc, width), lambda gi, bi: (gi, bi, 0, 0)),
                  pl.BlockSpec((1, width, 4 * S5_STATE), lambda gi, bi: (gi, 0, 0)),
                  pl.BlockSpec((1, width + 4 * S5_STATE, width), lambda gi, bi: (gi, 0, 0)),
                  pl.BlockSpec((1, levels, 2, 2 * S5_STATE), lambda gi, bi: (gi, 0, 0, 0))],
        out_specs=pl.BlockSpec((1, 1, nc, width), lambda gi, bi: (gi, bi, 0, 0)),
        out_shape=jax.ShapeDtypeStruct((g, b, nc, width), F32),
        compiler_params=_cparams("parallel", "parallel"),
        name="s5_ssm",
    )(u5, w1, w3, lamp)


def _rope_kernel(qq_ref, kk_ref, v_ref, cos_ref, sin_ref, qa_ref, qb_ref, ko_ref, vo_ref):
    width = DF_HEADS * 2 * DF_QK
    cos = jnp.concatenate([cos_ref[...]] * (width // LANES), axis=1)
    sin = jnp.concatenate([sin_ref[...]] * (width // LANES), axis=1)
    lane = lax.broadcasted_iota(jnp.int32, (1, width), 1)
    first_half = (lane % DF_QK) < (DF_QK // 2)

    def rot(x):
        partner = jnp.where(first_half, pltpu.roll(x, width - DF_QK // 2, 1),
                            pltpu.roll(x, DF_QK // 2, 1))
        return x * cos + partner * sin

    q = rot(qq_ref[0]) * (DF_QK ** -0.5 * LOG2E)
    is_a = (lane % (2 * DF_QK)) < DF_QK
    qa_ref[0] = jnp.where(is_a, q, 0.0).astype(BF16)
    qb_ref[0] = jnp.where(is_a, 0.0, q).astype(BF16)
    ko_ref[0] = rot(kk_ref[0]).astype(BF16)
    vo_ref[0] = v_ref[0].astype(BF16)


def _rope_prep(proj, cos_t, sin_t):
    b, s, _ = proj.shape
    tile = min(TILE_ROPE, s)
    w = DF_WIDTH
    spec = lambda cb: pl.BlockSpec((1, tile, w), lambda bi, i: (bi, i, cb))
    out = pl.BlockSpec((1, tile, w), lambda bi, i: (bi, i, 0))
    tab = pl.BlockSpec((tile, LANES), lambda bi, i: (i, 0))
    return pl.pallas_call(
        _rope_kernel,
        grid=(b, s // tile),
        in_specs=[spec(COL_QQ // w), spec(COL_KK // w), spec(COL_DV // w), tab, tab],
        out_specs=[out, out, out, out],
        out_shape=[jax.ShapeDtypeStruct((b, s, w), BF16)] * 4,
        compiler_params=_cparams("parallel", "parallel"),
        name="rope_prep",
    )(proj, proj, proj, cos_t, sin_t)


def _diff_attn_kernel(qa_ref, qb_ref, k_ref, v_ref, lq1_ref, lk1_ref, lq2_ref, lk2_ref, ng_ref,
                      o_ref, s_ref, *, tq, tk, nk, lam_init):
    q2 = (qa_ref[0], qb_ref[0])
    nt_dims = (((1,), (1,)), ((), ()))

    def fold(x, op, init):
        for j in range(tk // LANES):
            init = op(init, x[:, j * LANES:(j + 1) * LANES])
        return init

    m = [jnp.full((tq, LANES), -jnp.inf, F32)] * 2
    for c in range(nk):
        kc = k_ref[0, c * tk:(c + 1) * tk, :]
        for i in range(2):
            sc = lax.dot_general(q2[i], kc, nt_dims, preferred_element_type=F32)
            s_ref[c, i * tq:(i + 1) * tq, :] = sc
            m[i] = fold(sc, jnp.maximum, m[i])
    m = [jnp.max(x, axis=-1, keepdims=True) for x in m]

    acc = [jnp.zeros((tq, 2 * DF_V), F32)] * 2
    ones = jnp.ones((tk, DF_V), BF16)
    for c in range(nk):
        vc = jnp.concatenate([v_ref[0, c * tk:(c + 1) * tk, :], ones], axis=1)
        for i in range(2):
            e = jnp.exp2((s_ref[c, i * tq:(i + 1) * tq, :] - m[i]).astype(BF16))
            acc[i] = acc[i] + jnp.dot(e, vc, preferred_element_type=F32)
    lam = (jnp.exp(jnp.sum(lq1_ref[...] * lk1_ref[...], axis=-1, keepdims=True))
           - jnp.exp(jnp.sum(lq2_ref[...] * lk2_ref[...], axis=-1, keepdims=True)) + lam_init)
    acc = (acc[0][:, :DF_V] * (1.0 / acc[0][:, DF_V:DF_V + 1])
           - acc[1][:, :DF_V] * (lam / acc[1][:, DF_V:DF_V + 1]))
    ms = jnp.mean(acc * acc, axis=-1, keepdims=True)
    o_ref[0] = acc * lax.rsqrt(ms + RMS_EPS) * ng_ref[...] * (1.0 - lam_init)


def _diff_attn(qa, qb, kk, vv, lq1, lk1, lq2, lk2, norm_g, lam_init):
    b, s, _ = qa.shape
    tq = min(TILE_Q, s)
    tk = min(TILE_K, s)
    nk = s // tk
    kern = functools.partial(_diff_attn_kernel, tq=tq, tk=tk, nk=nk, lam_init=lam_init)
    qspec = pl.BlockSpec((1, tq, LANES), lambda bi, h, i: (bi, i, h))
    kspec = pl.BlockSpec((1, s, LANES), lambda bi, h, i: (bi, 0, h))
    vec64 = pl.BlockSpec((1, DF_QK), lambda bi, h, i: (0, 0))
    return pl.pallas_call(
        kern,
        grid=(b, DF_HEADS, s // tq),
        in_specs=[qspec, qspec, kspec, kspec, vec64, vec64, vec64, vec64,
                  pl.BlockSpec((1, DF_V), lambda bi, h, i: (0, 0))],
        out_specs=pl.BlockSpec((1, tq, DF_V), lambda bi, h, i: (bi, i, h)),
        out_shape=jax.ShapeDtypeStruct((b, s, DF_WIDTH), F32),
        scratch_shapes=[pltpu.VMEM((nk, 2 * tq, tk), F32)],
        compiler_params=_cparams("parallel", "parallel", "arbitrary"),
        name="diff_attn",
    )(qa, qb, kk, vv, lq1, lk1, lq2, lk2, norm_g)


def _layer_norm(x, g, b):
    mu = jnp.mean(x, axis=-1, keepdims=True)
    xc = x - mu
    var = jnp.mean(xc * xc, axis=-1, keepdims=True)
    return xc * lax.rsqrt(var + LN_EPS) * g + b


def _merge_kernel(x_ref, of_ref, ob_ref, z_ref, y5_ref, oc_ref, g0_ref, g1_ref, g2_ref,
                  bgate_ref, dnn_ref, gluw_ref, glub_ref, wbr_ref, wout_ref, lng_ref, lnb_ref,
                  o_ref):
    o = of_ref[0] + ob_ref[0]
    z = z_ref[0]
    parts = []
    for h in range(DN_HEADS):
        oh = o[:, h * DN_DIM:(h + 1) * DN_DIM]
        ms = jnp.mean(oh * oh, axis=-1, keepdims=True)
        parts.append(oh * lax.rsqrt(ms + RMS_EPS) * dnn_ref[...])
    o_a = jnp.concatenate(parts, axis=1) * _silu(z)
    y = jax.nn.gelu(y5_ref[0])
    o_b = y * _sigmoid(_bdot(y, gluw_ref[...]) + glub_ref[...])
    o_c = oc_ref[0]
    bg = bgate_ref[...]
    merged = (_sigmoid(g0_ref[0] + bg[:, 0:D_MODEL]) * _bdot(o_a, wbr_ref[0])
              + _sigmoid(g1_ref[0] + bg[:, D_MODEL:2 * D_MODEL]) * _bdot(o_b, wbr_ref[1])
              + _sigmoid(g2_ref[0] + bg[:, 2 * D_MODEL:]) * _bdot(o_c, wbr_ref[2]))
    hres = DEEPNORM_ALPHA * x_ref[0] + _bdot(merged, wout_ref[...])
    o_ref[0] = _layer_norm(hres, lng_ref[...], lnb_ref[...])


def _merge(x, o_f, o_b, proj, y5, o_c, b_gate, dn_norm, glu_w, glu_b, w_branch, w_out, ln_g, ln_b):
    b, s, _ = x.shape
    tile = min(TILE_MERGE, s)
    row = lambda w, cb=0: pl.BlockSpec((1, tile, w), lambda bi, i: (bi, i, cb))
    full = lambda shape: pl.BlockSpec(shape, lambda bi, i: (0,) * len(shape))
    gcb = COL_GATE // D_MODEL
    return pl.pallas_call(
        _merge_kernel,
        grid=(b, s // tile),
        in_specs=[row(D_MODEL), row(DN_WIDTH), row(DN_WIDTH), row(DN_WIDTH, COL_DN_Z // DN_WIDTH),
                  row(S5_WIDTH), row(DF_WIDTH),
                  row(D_MODEL, gcb), row(D_MODEL, gcb + 1), row(D_MODEL, gcb + 2),
                  full((1, N_BRANCH * D_MODEL)), full((1, DN_DIM)),
                  full((S5_WIDTH, S5_WIDTH)), full((1, S5_WIDTH)),
                  full((N_BRANCH, DN_WIDTH, D_MODEL)), full((D_MODEL, D_MODEL)),
                  full((1, D_MODEL)), full((1, D_MODEL))],
        out_specs=row(D_MODEL),
        out_shape=jax.ShapeDtypeStruct((b, s, D_MODEL), F32),
        compiler_params=_cparams("parallel", "parallel"),
        name="merge",
    )(x, o_f, o_b, proj, y5, o_c, proj, proj, proj, b_gate, dn_norm, glu_w, glu_b,
      w_branch, w_out, ln_g, ln_b)


def _xattn_kernel(x_ref, kv_ref, wq_ref, wo_ref, lng_ref, lnb_ref, o_ref):
    x = x_ref[0]
    q = _bdot(x, wq_ref[...])
    outs = []
    for h in range(XA_HEADS):
        qh = q[:, h * XA_DIM:(h + 1) * XA_DIM]
        kh = kv_ref[0, :, h * XA_DIM:(h + 1) * XA_DIM]
        vh = kv_ref[0, :, XA_WIDTH + h * XA_DIM:XA_WIDTH + (h + 1) * XA_DIM]
        sc = _bdot_nt(qh, kh) * (XA_DIM ** -0.5)
        sc = sc - jnp.max(sc, axis=-1, keepdims=True)
        e = jnp.exp(sc)
        pr = e / jnp.sum(e, axis=-1, keepdims=True)
        outs.append(_bdot(pr, vh))
    o = jnp.concatenate(outs, axis=1)
    hres = DEEPNORM_ALPHA * x + _bdot(o, wo_ref[...])
    o_ref[0] = _layer_norm(hres, lng_ref[...], lnb_ref[...])


def _xattn(x, kv, wq, wo, ln_g, ln_b):
    b, s, _ = x.shape
    tile = min(TILE_XA, s)
    mlen = kv.shape[1]
    row = pl.BlockSpec((1, tile, D_MODEL), lambda bi, i: (bi, i, 0))
    full = lambda shape: pl.BlockSpec(shape, lambda bi, i: (0,) * len(shape))
    return pl.pallas_call(
        _xattn_kernel,
        grid=(b, s // tile),
        in_specs=[row, pl.BlockSpec((1, mlen, 2 * XA_WIDTH), lambda bi, i: (bi, 0, 0)),
                  full((D_MODEL, XA_WIDTH)), full((XA_WIDTH, D_MODEL)),
                  full((1, D_MODEL)), full((1, D_MODEL))],
        out_specs=row,
        out_shape=jax.ShapeDtypeStruct((b, s, D_MODEL), F32),
        compiler_params=_cparams("parallel", "parallel"),
        name="xattn",
    )(x, kv, wq, wo, ln_g, ln_b)


def _router_kernel(x_ref, rw_ref, rb_ref, idx_ref, gate_ref, xb_ref):
    x = x_ref[...]
    xb_ref[...] = x.astype(BF16)
    logits = lax.dot_general(rw_ref[...], x, (((1,), (1,)), ((), ())), precision=HI,
                             preferred_element_type=F32) + rb_ref[...]
    tn = logits.shape[1]
    eid = lax.broadcasted_iota(jnp.int32, (N_EXPERTS, tn), 0)
    vals, ids = [], []
    cur = logits
    for _ in range(TOP_K):
        m = jnp.max(cur, axis=0, keepdims=True)
        sel = jnp.min(jnp.where(cur == m, eid, N_EXPERTS), axis=0, keepdims=True)
        vals.append(m)
        ids.append(sel)
        cur = jnp.where(eid == sel, -jnp.inf, cur)
    top_v = jnp.concatenate(vals, axis=0)
    e = jnp.exp(top_v - top_v[0:1])
    gate_ref[...] = e / jnp.sum(e, axis=0, keepdims=True)
    idx_ref[...] = jnp.concatenate(ids, axis=0)


def _router(x2d, router_w_t, router_b):
    n, _ = x2d.shape
    tile = min(TILE_ROUTER, n)
    return pl.pallas_call(
        _router_kernel,
        grid=(n // tile,),
        in_specs=[pl.BlockSpec((tile, D_MODEL), lambda i: (i, 0)),
                  pl.BlockSpec((N_EXPERTS, D_MODEL), lambda i: (0, 0)),
                  pl.BlockSpec((N_EXPERTS, 1), lambda i: (0, 0))],
        out_specs=[pl.BlockSpec((TOP_K, tile), lambda i: (0, i)),
                   pl.BlockSpec((TOP_K, tile), lambda i: (0, i)),
                   pl.BlockSpec((tile, D_MODEL), lambda i: (i, 0))],
        out_shape=[jax.ShapeDtypeStruct((TOP_K, n), jnp.int32),
                   jax.ShapeDtypeStruct((TOP_K, n), F32),
                   jax.ShapeDtypeStruct((n, D_MODEL), BF16)],
        compiler_params=_cparams("parallel"),
        name="router",
    )(x2d, router_w_t, router_b)


def _rank_kernel(idx_ref, rank_ref, cnt_ref, run_ref, *, tile):
    @pl.when(pl.program_id(0) == 0)
    def _():
        run_ref[...] = jnp.zeros_like(run_ref)

    idx = idx_ref[...]
    eid = lax.broadcasted_iota(jnp.int32, (N_EXPERTS, tile), 0)
    si = lax.broadcasted_iota(jnp.int32, (tile, tile), 0)
    ti = lax.broadcasted_iota(jnp.int32, (tile, tile), 1)
    before = (si < ti).astype(BF16)
    run = run_ref[...]
    ranks = []
    for k in range(TOP_K):
        onehot = eid == idx[k:k + 1]
        oh = onehot.astype(BF16)
        within = jnp.dot(oh, before, preferred_element_type=F32)
        ranks.append(jnp.sum(jnp.where(onehot, within + run, 0.0), axis=0, keepdims=True))
        run = run + jnp.sum(onehot.astype(F32), axis=1, keepdims=True)
    rank_ref[...] = jnp.concatenate(ranks, axis=0).astype(jnp.int32)
    run_ref[...] = run
    cnt_ref[...] = run.astype(jnp.int32)


def _rank(top_i):
    _, n = top_i.shape
    tile = min(TILE_ROUTER, n)
    kern = functools.partial(_rank_kernel, tile=tile)
    return pl.pallas_call(
        kern,
        grid=(n // tile,),
        in_specs=[pl.BlockSpec((TOP_K, tile), lambda i: (0, i))],
        out_specs=[pl.BlockSpec((TOP_K, tile), lambda i: (0, i)),
                   pl.BlockSpec((N_EXPERTS, 1), lambda i: (0, 0))],
        out_shape=[jax.ShapeDtypeStruct((TOP_K, n), jnp.int32),
                   jax.ShapeDtypeStruct((N_EXPERTS, 1), jnp.int32)],
        scratch_shapes=[pltpu.VMEM((N_EXPERTS, 1), F32)],
        compiler_params=_cparams("arbitrary"),
        name="moe_rank",
    )(top_i)


def _expert_kernel(be_ref, x_ref, wg_ref, wu_ref, bg_ref, bu_ref, wd_ref, bd_ref, o_ref):
    del be_ref
    x = x_ref[...]
    hg = jnp.dot(x, wg_ref[0], preferred_element_type=F32) + bg_ref[0]
    hu = jnp.dot(x, wu_ref[0], preferred_element_type=F32) + bu_ref[0]
    gt = jnp.minimum(hg, SWIGLU_LIMIT)
    up = jnp.clip(hu, -SWIGLU_LIMIT, SWIGLU_LIMIT)
    act = (up + 1.0) * gt * _sigmoid(SWIGLU_ALPHA * gt)
    y = jnp.dot(act.astype(BF16), wd_ref[0], preferred_element_type=F32) + bd_ref[0]
    o_ref[...] = y.astype(o_ref.dtype)


def _experts(x_rows, blk_e, w_g, w_u, b_g, b_u, w_d, b_d, *, block):
    rows, _ = x_rows.shape
    wspec = lambda k, m: pl.BlockSpec((1, k, m), lambda i, be: (be[i], 0, 0))
    grid_spec = pltpu.PrefetchScalarGridSpec(
        num_scalar_prefetch=1,
        grid=(rows // block,),
        in_specs=[pl.BlockSpec((block, D_MODEL), lambda i, be: (i, 0)),
                  wspec(D_MODEL, D_FF), wspec(D_MODEL, D_FF), wspec(1, D_FF), wspec(1, D_FF),
                  wspec(D_FF, D_MODEL), wspec(1, D_MODEL)],
        out_specs=pl.BlockSpec((block, D_MODEL), lambda i, be: (i, 0)),
    )
    return pl.pallas_call(
        _expert_kernel,
        grid_spec=grid_spec,
        out_shape=jax.ShapeDtypeStruct((rows, D_MODEL), BF16),
        compiler_params=_cparams("arbitrary"),
        name="moe_experts",
    )(blk_e, x_rows, w_g, w_u, b_g, b_u, w_d, b_d)


def _combine_kernel(x_ref, y_ref, gate_ref, lng_ref, lnb_ref, o_ref):
    gate = gate_ref[...]
    h = y_ref[0].astype(F32) * gate[:, 0:1]
    for k in range(1, TOP_K):
        h = h + y_ref[k].astype(F32) * gate[:, k:k + 1]
    o_ref[...] = _layer_norm(DEEPNORM_ALPHA * x_ref[...] + h, lng_ref[...], lnb_ref[...])


def _combine(x2d, y_g, gate_t, ln_g, ln_b):
    n, _ = x2d.shape
    tile = min(TILE_COMBINE, n)
    return pl.pallas_call(
        _combine_kernel,
        grid=(n // tile,),
        in_specs=[pl.BlockSpec((tile, D_MODEL), lambda i: (i, 0)),
                  pl.BlockSpec((TOP_K, tile, D_MODEL), lambda i: (0, i, 0)),
                  pl.BlockSpec((tile, TOP_K), lambda i: (i, 0)),
                  pl.BlockSpec((1, D_MODEL), lambda i: (0, 0)),
                  pl.BlockSpec((1, D_MODEL), lambda i: (0, 0))],
        out_specs=pl.BlockSpec((tile, D_MODEL), lambda i: (i, 0)),
        out_shape=jax.ShapeDtypeStruct((n, D_MODEL), F32),
        compiler_params=_cparams("parallel"),
        name="moe_combine",
    )(x2d, y_g, gate_t, ln_g, ln_b)


def _split_kernel(w_ref, se_ref, so_ref, g_ref, u_ref):
    w = w_ref[0].astype(BF16)
    g_ref[0] = jnp.dot(w, se_ref[...], preferred_element_type=F32).astype(BF16)
    u_ref[0] = jnp.dot(w, so_ref[...], preferred_element_type=F32).astype(BF16)


def _split_gate_up(w_gu):
    e, dm, f2 = w_gu.shape
    f = f2 // 2
    tr = 256
    col = jnp.arange(f2, dtype=jnp.int32)[:, None]
    out = jnp.arange(f, dtype=jnp.int32)[None, :]
    sel_even = (col == 2 * out).astype(BF16)
    sel_odd = (col == 2 * out + 1).astype(BF16)
    sel = pl.BlockSpec((f2, f), lambda ei, i: (0, 0))
    ospec = pl.BlockSpec((1, tr, f), lambda ei, i: (ei, i, 0))
    return pl.pallas_call(
        _split_kernel,
        grid=(e, dm // tr),
        in_specs=[pl.BlockSpec((1, tr, f2), lambda ei, i: (ei, i, 0)), sel, sel],
        out_specs=[ospec, ospec],
        out_shape=[jax.ShapeDtypeStruct((e, dm, f), BF16)] * 2,
        compiler_params=_cparams("parallel", "parallel"),
        name="split_gate_up",
    )(w_gu, sel_even, sel_odd)


def _moe(x, w):
    b, s, dm = x.shape
    n = b * s
    x2d = x.reshape(n, dm)
    top_i, gate, xb = _router(x2d, w['router_w_t'], w['router_b'])
    rank, counts = _rank(top_i)
    block = MOE_BLOCK
    m = n * TOP_K
    nblk = -(-m // block) + N_EXPERTS
    rows = nblk * block
    counts = counts.reshape(N_EXPERTS)
    padded = (counts + block - 1) // block * block
    pad_end = jnp.cumsum(padded)
    pad_start = pad_end - padded
    dest = pad_start[top_i] + rank
    tok = jnp.broadcast_to(jnp.arange(n, dtype=jnp.int32)[None, :], (TOP_K, n))
    row_tok = jnp.full((rows,), n, jnp.int32).at[dest.reshape(-1)].set(
        tok.reshape(-1), unique_indices=True)
    blk_e = jnp.minimum(jnp.searchsorted(pad_end, jnp.arange(nblk, dtype=pad_end.dtype) * block,
                                         side='right'), N_EXPERTS - 1).astype(jnp.int32)
    x_rows = jnp.concatenate([xb, jnp.zeros((1, dm), BF16)], axis=0)[row_tok]
    y_rows = _experts(x_rows, blk_e, w['e_w_g'], w['e_w_u'], w['e_b_g'], w['e_b_u'], w['e_w_d'],
                      w['e_b_d'], block=block)
    y_g = y_rows[dest]
    out = _combine(x2d, y_g, gate.T, w['ln3_g'], w['ln3_b'])
    return out.reshape(b, s, dm)


def _rope_tables(s):
    half = DF_QK // 2
    inv_freq = jnp.power(ROPE_THETA, -jnp.arange(half, dtype=F32) / half)
    ang = jnp.arange(s, dtype=F32)[:, None] * inv_freq[None, :]
    cos = jnp.cos(ang)
    sin = jnp.sin(ang)
    cos_t = jnp.concatenate([cos, cos, cos, cos], axis=1)
    sin_t = jnp.concatenate([-sin, sin, -sin, sin], axis=1)
    return cos_t, sin_t


def _prepare_layer(l, p, s):
    dm = D_MODEL
    w = {}
    w_in = jnp.concatenate([p['w_in'][l], jnp.zeros((dm, 1), F32)], axis=1)[:, _IN_PERM]
    w['w_in'] = w_in.astype(BF16)
    pad8 = jnp.zeros((8,), F32)
    pad112 = jnp.zeros((LANES - 16,), F32)
    w['nega'] = jnp.concatenate([pad8, -jnp.exp(p['dn_A_log'][l].reshape(8)), pad112]).reshape(1, LANES)
    w['dtb'] = jnp.concatenate([pad8, p['dn_dt_bias'][l].reshape(8), pad112]).reshape(1, LANES)
    w['dn_conv'] = p['dn_conv'][l]
    t5 = min(S5_CHUNK, s)
    w['s5'] = _s5_operators(p['s5_lam_re'][l], p['s5_lam_im'][l], p['s5_log_dt'][l],
                            p['s5_B_re'][l], p['s5_B_im'][l], p['s5_C_re'][l],
                            p['s5_C_im'][l], p['s5_D'][l], chunk=t5, nc=s // t5)
    for name in ('df_lq1', 'df_lk1', 'df_lq2', 'df_lk2'):
        w[name] = p[name][l].reshape(1, DF_QK)
    w['df_norm'] = p['df_norm'][l].reshape(1, DF_V)
    w['lam_init'] = 0.8 - 0.6 * math.exp(-0.3 * l)
    w['b_gate'] = p['b_gate'][l].reshape(1, -1)
    w['dn_norm'] = p['dn_norm'][l].reshape(1, DN_DIM)
    w['s5_glu_w'] = p['s5_glu_w'][l].astype(BF16)
    w['s5_glu_b'] = p['s5_glu_b'][l].reshape(1, -1)
    w['w_branch'] = p['w_branch'][l].astype(BF16)
    w['w_out'] = p['w_out'][l].astype(BF16)
    for name in ('ln1_g', 'ln1_b', 'ln2_g', 'ln2_b', 'ln3_g', 'ln3_b'):
        w[name] = p[name][l].reshape(1, dm)
    w['xa_wkv'] = p['xa_wkv'][l].astype(BF16)
    w['xa_wq'] = p['xa_wq'][l].astype(BF16)
    w['xa_wo'] = p['xa_wo'][l].astype(BF16)
    w['router_w_t'] = p['router_w'][l].T
    w['router_b'] = p['router_b'][l].reshape(N_EXPERTS, 1)
    w['e_w_g'], w['e_w_u'] = _split_gate_up(p['e_w_gu'][l])
    w['e_b_g'] = p['e_b_gu'][l][:, None, 0::2]
    w['e_b_u'] = p['e_b_gu'][l][:, None, 1::2]
    w['e_w_d'] = p['e_w_down'][l].astype(BF16)
    w['e_b_d'] = p['e_b_down'][l][:, None, :]
    return w


def _layer(x, mem, w, tabs):
    b, s, dm = x.shape
    n = b * s
    proj = _matmul(x.reshape(n, dm), w['w_in'], tm=TILE_MM, tn=LANES * 19)
    proj = proj.reshape(b, s, PROJ_WIDTH)

    qkv, bg = _dn_prep(proj, w['dn_conv'], w['nega'], w['dtb'], chunk=DN_CHUNK)
    bgr = bg[:, :, :16].reshape(b, s // DN_CHUNK, DN_CHUNK, 16).transpose(0, 1, 3, 2)
    o_f, o_b = _dn_scan(qkv, bg, bgr, chunk=DN_CHUNK)

    t5 = min(S5_CHUNK, s)
    nc = s // t5
    u5 = proj[:, :, COL_S5_U:COL_S5_U + S5_WIDTH].reshape(b, nc, t5, S5_GROUPS, S5_GROUP)
    u5 = u5.transpose(3, 0, 1, 2, 4).reshape(S5_GROUPS, b, nc, t5 * S5_GROUP).astype(BF16)
    y5 = _s5_ssm(u5, *w['s5'])
    y5 = y5.reshape(S5_GROUPS, b, nc, t5, S5_GROUP).transpose(1, 2, 3, 0, 4).reshape(b, s, S5_WIDTH)

    qa, qb, kk, vv = _rope_prep(proj, *tabs)
    o_c = _diff_attn(qa, qb, kk, vv, w['df_lq1'], w['df_lk1'], w['df_lq2'], w['df_lk2'],
                     w['df_norm'], w['lam_init'])

    x = _merge(x, o_f, o_b, proj, y5, o_c, w['b_gate'], w['dn_norm'], w['s5_glu_w'], w['s5_glu_b'],
               w['w_branch'], w['w_out'], w['ln1_g'], w['ln1_b'])

    mlen = mem.shape[1]
    kv = _matmul(mem.reshape(b * mlen, dm), w['xa_wkv'], tm=512, tn=1024)
    x = _xattn(x, kv.reshape(b, mlen, 2 * XA_WIDTH), w['xa_wq'], w['xa_wo'], w['ln2_g'], w['ln2_b'])
    return _moe(x, w)


def kernel(x_prompt, x_sample, mem_prompt, mem_sample, w_in, b_gate, dn_conv, dn_A_log, dn_dt_bias,
           dn_norm, s5_lam_re, s5_lam_im, s5_log_dt, s5_B_re, s5_B_im, s5_C_re, s5_C_im, s5_D,
           s5_glu_w, s5_glu_b, df_lq1, df_lk1, df_lq2, df_lk2, df_norm, w_branch, w_out,
           ln1_g, ln1_b, xa_wq, xa_wkv, xa_wo, ln2_g, ln2_b, router_w, router_b,
           e_w_gu, e_b_gu, e_w_down, e_b_down, ln3_g, ln3_b):
    p = dict(w_in=w_in, b_gate=b_gate, dn_conv=dn_conv, dn_A_log=dn_A_log, dn_dt_bias=dn_dt_bias,
             dn_norm=dn_norm, s5_lam_re=s5_lam_re, s5_lam_im=s5_lam_im, s5_log_dt=s5_log_dt,
             s5_B_re=s5_B_re, s5_B_im=s5_B_im, s5_C_re=s5_C_re, s5_C_im=s5_C_im, s5_D=s5_D,
             s5_glu_w=s5_glu_w, s5_glu_b=s5_glu_b, df_lq1=df_lq1, df_lk1=df_lk1, df_lq2=df_lq2,
             df_lk2=df_lk2, df_norm=df_norm, w_branch=w_branch, w_out=w_out, ln1_g=ln1_g,
             ln1_b=ln1_b, xa_wq=xa_wq, xa_wkv=xa_wkv, xa_wo=xa_wo, ln2_g=ln2_g, ln2_b=ln2_b,
             router_w=router_w, router_b=router_b, e_w_gu=e_w_gu, e_b_gu=e_b_gu,
             e_w_down=e_w_down, e_b_down=e_b_down, ln3_g=ln3_g, ln3_b=ln3_b)
    s = x_prompt.shape[1]
    assert x_sample.shape[1] == s
    tabs = _rope_tables(s)
    layers = [_prepare_layer(l, p, s) for l in range(DEPTH)]
    outs = []
    for x, mem in ((x_prompt, mem_prompt), (x_sample, mem_sample)):
        for w in layers:
            x = _layer(x, mem, w, tabs)
        outs.append(x)
    return tuple(outs)
```

```python
import functools
import math

import numpy as np
import jax
import jax.numpy as jnp
from jax import lax
from jax.experimental import pallas as pl
from jax.experimental.pallas import tpu as pltpu

F32 = jnp.float32
BF16 = jnp.bfloat16
HI = lax.Precision.HIGHEST

D_MODEL = 1024
DN_HEADS = 4
DN_DIM = 128
DN_WIDTH = DN_HEADS * DN_DIM
DN_CONV = 4
S5_GROUP = 16
S5_WIDTH = 512
S5_GROUPS = S5_WIDTH // S5_GROUP
S5_STATE = 64
DF_HEADS = 4
DF_QK = 64
DF_V = 128
DF_WIDTH = DF_HEADS * DF_V
ROPE_THETA = 10000.0
XA_HEADS = 4
XA_DIM = 128
XA_WIDTH = XA_HEADS * XA_DIM
N_EXPERTS = 32
TOP_K = 4
D_FF = 1024
SWIGLU_LIMIT = 7.0
SWIGLU_ALPHA = 1.702
N_BRANCH = 3
DEPTH = 2
DEEPNORM_ALPHA = (2 * DEPTH) ** 0.25
LN_EPS = 1e-5
RMS_EPS = 1e-6
LOG2E = 1.4426950408889634

LANES = 128
SUBLANES = 8
VMEM_LIMIT = 56 * 1024 * 1024

COL_DN_QKV = 0
COL_DN_Z = 1536
COL_S5_U = 2048
COL_QQ = 2560
COL_KK = 3072
COL_DV = 3584
COL_GATE = 4096
COL_BA = 7168
PROJ_WIDTH = 7296

TILE_MM = 1024
TILE_DN_PREP = 512
TILE_DN_SCAN = 256
DN_CHUNK = 64
DN_CHUNKS_PER_TRIP = 2
S5_CHUNK = 32
TILE_ROPE = 512
TILE_Q = 256
TILE_K = 512
TILE_MERGE = 256
TILE_XA = 512
TILE_ROUTER = 512
MOE_BLOCK = 512
TILE_COMBINE = 256


def _in_proj_perm():
    o_dn_b, o_dn_a, o_s5 = 2048, 2056, 2064
    o_q1, o_q2, o_k1, o_k2, o_dv, o_gate = 2576, 2832, 3088, 3344, 3600, 4112
    perm = np.full((PROJ_WIDTH,), -1, np.int64)
    perm[0:2048] = np.arange(2048)
    perm[COL_S5_U:COL_S5_U + 512] = o_s5 + np.arange(512)
    for h in range(DF_HEADS):
        j = np.arange(DF_QK)
        perm[COL_QQ + h * 128 + j] = o_q1 + h * DF_QK + j
        perm[COL_QQ + h * 128 + 64 + j] = o_q2 + h * DF_QK + j
        perm[COL_KK + h * 128 + j] = o_k1 + h * DF_QK + j
        perm[COL_KK + h * 128 + 64 + j] = o_k2 + h * DF_QK + j
    perm[COL_DV:COL_DV + 512] = o_dv + np.arange(512)
    perm[COL_GATE:COL_GATE + 3072] = o_gate + np.arange(3072)
    perm[COL_BA:COL_BA + 8] = o_dn_b + np.arange(8)
    perm[COL_BA + 8:COL_BA + 16] = o_dn_a + np.arange(8)
    return perm


_IN_PERM = _in_proj_perm()


def _pack_in_proj(w_in):
    runs, start = [], 0
    for i in range(1, PROJ_WIDTH + 1):
        if i == PROJ_WIDTH or _IN_PERM[i] != _IN_PERM[i - 1] + (1 if _IN_PERM[i - 1] >= 0 else 0):
            runs.append((start, i))
            start = i
    pieces = []
    for a, b in runs:
        src = int(_IN_PERM[a])
        pieces.append(jnp.zeros((w_in.shape[0], b - a), w_in.dtype) if src < 0
                      else w_in[:, src:src + (b - a)])
    return jnp.concatenate(pieces, axis=1)


def _cparams(*sem):
    return pltpu.CompilerParams(dimension_semantics=sem, vmem_limit_bytes=VMEM_LIMIT)


def _bdot(a, b):
    return jnp.dot(a.astype(BF16), b.astype(BF16), preferred_element_type=F32)


def _bdot_nt(a, b):
    return lax.dot_general(a.astype(BF16), b.astype(BF16), (((1,), (1,)), ((), ())),
                           preferred_element_type=F32)


def _bdot_tn(a, b):
    return lax.dot_general(a.astype(BF16), b.astype(BF16), (((0,), (0,)), ((), ())),
                           preferred_element_type=F32)


def _sigmoid(x):
    return 1.0 / (1.0 + jnp.exp(-x))


def _silu(x):
    return x * _sigmoid(x)


def _mm_kernel(x_ref, w_ref, o_ref):
    o_ref[...] = jnp.dot(x_ref[...].astype(BF16), w_ref[...],
                         preferred_element_type=F32).astype(o_ref.dtype)


def _matmul(x, w, *, tm, tn, out_dtype=F32):
    n, k = x.shape
    m = w.shape[1]
    tm = min(tm, n)
    tn = min(tn, m)
    return pl.pallas_call(
        _mm_kernel,
        grid=(n // tm, m // tn),
        in_specs=[pl.BlockSpec((tm, k), lambda i, j: (i, 0)),
                  pl.BlockSpec((k, tn), lambda i, j: (0, j))],
        out_specs=pl.BlockSpec((tm, tn), lambda i, j: (i, j)),
        out_shape=jax.ShapeDtypeStruct((n, m), out_dtype),
        compiler_params=_cparams("parallel", "arbitrary"),
        name="matmul",
    )(x, w)


def _dn_prep_kernel(x_ref, prev_ref, next_ref, ba_ref, cw_ref, nega_ref, dtb_ref,
                    qkv_ref, bg_ref, *, tile, chunk):
    i = pl.program_id(1)
    last = pl.num_programs(1) - 1
    x = x_ref[0]
    prev = jnp.where(i > 0, prev_ref[0], 0.0)
    nxt = jnp.where(i < last, next_ref[0], 0.0)
    ext = jnp.concatenate([prev, x, nxt], axis=0)
    cw = cw_ref[...]
    acc = cw[0:1] * ext[6:6 + tile]
    for j in range(1, DN_CONV):
        acc = acc + cw[j:j + 1] * ext[6 + j:6 + j + tile]
    act = _silu(acc)
    for h in range(DN_HEADS):
        lo = h * DN_DIM
        qh = act[:, lo:lo + DN_DIM]
        qn = qh * lax.rsqrt(jnp.sum(qh * qh, axis=-1, keepdims=True) + 1e-6)
        qkv_ref[0, :, lo:lo + DN_DIM] = qn * (DN_DIM ** -0.5)
        kh = act[:, DN_WIDTH + lo:DN_WIDTH + lo + DN_DIM]
        qkv_ref[0, :, DN_WIDTH + lo:DN_WIDTH + lo + DN_DIM] = kh * lax.rsqrt(
            jnp.sum(kh * kh, axis=-1, keepdims=True) + 1e-6)
    qkv_ref[0, :, 2 * DN_WIDTH:] = act[:, 2 * DN_WIDTH:]

    ba = ba_ref[0]
    beta = _sigmoid(ba)
    z = ba + dtb_ref[...]
    softplus = jnp.maximum(z, 0.0) + jnp.log(1.0 + jnp.exp(-jnp.abs(z)))
    g = nega_ref[...] * softplus
    row = lax.broadcasted_iota(jnp.int32, (chunk, chunk), 0)
    col = lax.broadcasted_iota(jnp.int32, (chunk, chunk), 1)
    lower = (row >= col).astype(F32)
    upper = (row <= col).astype(F32)
    lane = lax.broadcasted_iota(jnp.int32, (chunk, LANES), 1)
    fwd_lane = lane < 8 + DN_HEADS
    for c in range(tile // chunk):
        gc = g[c * chunk:(c + 1) * chunk]
        cf = jnp.dot(lower, gc, precision=HI, preferred_element_type=F32)
        cb = jnp.dot(upper, gc, precision=HI, preferred_element_type=F32)
        gcum = jnp.where(fwd_lane, cf, cb)
        bg_ref[0, c * chunk:(c + 1) * chunk, :] = jnp.where(
            lane < 8, beta[c * chunk:(c + 1) * chunk], gcum)


def _dn_prep(proj, conv_w, nega, dtb, *, chunk):
    b, s, _ = proj.shape
    tile = min(TILE_DN_PREP, s)
    nt = s // tile
    r8 = tile // SUBLANES
    kern = functools.partial(_dn_prep_kernel, tile=tile, chunk=chunk)
    return pl.pallas_call(
        kern,
        grid=(b, nt),
        in_specs=[
            pl.BlockSpec((1, tile, 3 * DN_WIDTH), lambda bi, i: (bi, i, 0)),
            pl.BlockSpec((1, SUBLANES, 3 * DN_WIDTH),
                         lambda bi, i: (bi, jnp.maximum(i * r8 - 1, 0), 0)),
            pl.BlockSpec((1, SUBLANES, 3 * DN_WIDTH),
                         lambda bi, i: (bi, jnp.minimum((i + 1) * r8, s // SUBLANES - 1), 0)),
            pl.BlockSpec((1, tile, LANES), lambda bi, i: (bi, i, COL_BA // LANES)),
            pl.BlockSpec((DN_CONV, 3 * DN_WIDTH), lambda bi, i: (0, 0)),
            pl.BlockSpec((1, LANES), lambda bi, i: (0, 0)),
            pl.BlockSpec((1, LANES), lambda bi, i: (0, 0)),
        ],
        out_specs=[pl.BlockSpec((1, tile, 3 * DN_WIDTH), lambda bi, i: (bi, i, 0)),
                   pl.BlockSpec((1, tile, LANES), lambda bi, i: (bi, i, 0))],
        out_shape=[jax.ShapeDtypeStruct((b, s, 3 * DN_WIDTH), F32),
                   jax.ShapeDtypeStruct((b, s, LANES), F32)],
        compiler_params=_cparams("parallel", "parallel"),
        name="dn_prep",
    )(proj, proj, proj, proj, conv_w, nega, dtb)


def _dn_local(chains, chunk):
    ri = lax.broadcasted_iota(jnp.int32, (chunk, chunk), 0)
    ci = lax.broadcasted_iota(jnp.int32, (chunk, chunk), 1)
    pre = []
    for d, h, q, k, v, bgc, rows in chains:
        li = d * DN_HEADS + h
        beta = bgc[:, li:li + 1]
        gcol = bgc[:, 8 + li:8 + li + 1]
        grow = rows[8 + li:8 + li + 1, :]
        gtot = gcol[chunk - 1:chunk] if d == 0 else gcol[0:1]
        incl = (ri >= ci) if d == 0 else (ri <= ci)
        strict = (ri > ci) if d == 0 else (ri < ci)
        eg = jnp.exp(gcol)
        kb = k * beta
        decay = jnp.where(incl, jnp.exp(jnp.where(incl, gcol - grow, 0.0)), 0.0)
        pre.append(dict(k=k, qg=q * eg, kd=k * jnp.exp(gtot - gcol), decay=decay, strict=strict,
                        x2=jnp.concatenate([v * beta, kb * eg], axis=1), sdec=jnp.exp(gtot),
                        qkb=jnp.concatenate([q, kb], axis=0)))
    qk = [_bdot_nt(c['qkb'], c['k']) for c in pre]
    intra = [x[:chunk] * c['decay'] for x, c in zip(qk, pre)]
    p = [-jnp.where(c['strict'], x[chunk:] * c['decay'], 0.0) for x, c in zip(qk, pre)]
    r = p
    p = [_bdot(x, x) for x in p]
    for _ in range(int(math.log2(chunk)) - 2):
        rp = [_bdot(jnp.concatenate([ri_, pi_], axis=0), pi_) for ri_, pi_ in zip(r, p)]
        r = [ri_ + pi_ + x[:chunk] for ri_, pi_, x in zip(r, p, rp)]
        p = [x[chunk:] for x in rp]
    r = [ri_ + pi_ + _bdot(ri_, pi_) for ri_, pi_ in zip(r, p)]
    uw = [c['x2'] + _bdot(ri_, c['x2']) for ri_, c in zip(r, pre)]
    return [dict(u=x[:, :DN_DIM], wq=jnp.concatenate([x[:, DN_DIM:], c['qg']], axis=0),
                 intra=it, kd=c['kd'], sdec=c['sdec']) for x, it, c in zip(uw, intra, pre)]


def _dn_state(local, states, chunk):
    ws = [_bdot(c['wq'], st) for c, st in zip(local, states)]
    v_new = [c['u'] - y[:chunk] for c, y in zip(local, ws)]
    out = [y[chunk:] + _bdot(c['intra'], vn) for y, c, vn in zip(ws, local, v_new)]
    st_new = [st * c['sdec'] + _bdot_tn(c['kd'], vn) for st, c, vn in zip(states, local, v_new)]
    return out, st_new


def _dn_scan_kernel(xf_ref, xb_ref, gf_ref, gb_ref, rf_ref, rb_ref, of_ref, ob_ref, st_ref,
                    *, tile, chunk):
    @pl.when(pl.program_id(1) == 0)
    def _():
        st_ref[...] = jnp.zeros_like(st_ref)

    nct = tile // chunk
    nchain = 2 * DN_HEADS

    def body(j, carry):
        chains, where = [], []
        for slot in range(DN_CHUNKS_PER_TRIP):
            for d in range(2):
                c = DN_CHUNKS_PER_TRIP * j + slot
                c = c if d == 0 else nct - 1 - c
                r0 = pl.multiple_of(c * chunk, chunk)
                x_ref, g_ref, r_ref = (xf_ref, gf_ref, rf_ref) if d == 0 else (xb_ref, gb_ref, rb_ref)
                bgc = g_ref[0, pl.ds(r0, chunk), :]
                rows = r_ref[0, c]
                for h in range(DN_HEADS):
                    lo = h * DN_DIM
                    q = x_ref[0, pl.ds(r0, chunk), lo:lo + DN_DIM]
                    k = x_ref[0, pl.ds(r0, chunk), DN_WIDTH + lo:DN_WIDTH + lo + DN_DIM]
                    v = x_ref[0, pl.ds(r0, chunk), 2 * DN_WIDTH + lo:2 * DN_WIDTH + lo + DN_DIM]
                    chains.append((d, h, q, k, v, bgc, rows))
                    where.append((d, h, r0))
        states = [st_ref[i] for i in range(nchain)]
        local = _dn_local(chains, chunk)
        outs = []
        for slot in range(DN_CHUNKS_PER_TRIP):
            out, states = _dn_state(local[slot * nchain:(slot + 1) * nchain], states, chunk)
            outs += out
        for (d, h, r0), out in zip(where, outs):
            o_ref = of_ref if d == 0 else ob_ref
            o_ref[0, pl.ds(r0, chunk), h * DN_DIM:(h + 1) * DN_DIM] = out
        for i in range(nchain):
            st_ref[i] = states[i]
        return carry

    lax.fori_loop(0, nct // DN_CHUNKS_PER_TRIP, body, 0)


def _dn_scan(qkv, bg, bgr, *, chunk):
    b, s, _ = qkv.shape
    tile = min(TILE_DN_SCAN, s)
    nt = s // tile
    nct = tile // chunk
    kern = functools.partial(_dn_scan_kernel, tile=tile, chunk=chunk)
    fwd = lambda bi, i: (bi, i, 0)
    bwd = lambda bi, i: (bi, nt - 1 - i, 0)
    fwd4 = lambda bi, i: (bi, i, 0, 0)
    bwd4 = lambda bi, i: (bi, nt - 1 - i, 0, 0)
    return pl.pallas_call(
        kern,
        grid=(b, nt),
        in_specs=[pl.BlockSpec((1, tile, 3 * DN_WIDTH), fwd),
                  pl.BlockSpec((1, tile, 3 * DN_WIDTH), bwd),
                  pl.BlockSpec((1, tile, LANES), fwd),
                  pl.BlockSpec((1, tile, LANES), bwd),
                  pl.BlockSpec((1, nct, 16, chunk), fwd4),
                  pl.BlockSpec((1, nct, 16, chunk), bwd4)],
        out_specs=[pl.BlockSpec((1, tile, DN_WIDTH), fwd),
                   pl.BlockSpec((1, tile, DN_WIDTH), bwd)],
        out_shape=[jax.ShapeDtypeStruct((b, s, DN_WIDTH), F32),
                   jax.ShapeDtypeStruct((b, s, DN_WIDTH), F32)],
        scratch_shapes=[pltpu.VMEM((2 * DN_HEADS, DN_DIM, DN_DIM), F32)],
        compiler_params=_cparams("parallel", "arbitrary"),
        name="dn_scan",
    )(qkv, qkv, bg, bg, bgr, bgr)


def _s5_operators(lam_re, lam_im, log_dt, b_re, b_im, c_re, c_im, d_skip, *, chunk, nc):
    t = chunk
    levels = max(1, int(math.log2(nc)))
    tau = jnp.arange(t + 1, dtype=F32)
    kern = []
    lamp = []
    for d in range(2):
        lam = lax.complex(lam_re[d], lam_im[d])
        dt = jnp.exp(log_dt[d])[:, None]
        lam_dt = lam * dt
        lam_bar = jnp.exp(lam_dt)
        bbar = ((lam_bar - 1.0) / lam)[:, :, None] * lax.complex(b_re[d], b_im[d])
        cc = lax.complex(c_re[d], c_im[d])
        pw = jnp.exp(lam_dt[:, None, :] * tau[None, :, None].astype(jnp.complex64))
        kern.append((pw, bbar, cc))
        steps = (t * 2.0 ** jnp.arange(levels, dtype=F32)).astype(jnp.complex64)
        lamp.append(jnp.exp(lam_dt[:, None, :] * steps[None, :, None]))
    (pwf, bbf, ccf), (pwb, bbb, ccb) = kern
    ef = jnp.einsum('gsp,gpc->gscp', pwf[:, t - 1::-1, :], bbf)
    eb = jnp.einsum('gsp,gpc->gscp', pwb[:, :t, :], bbb)
    w1 = jnp.concatenate([jnp.real(ef), jnp.real(eb), jnp.imag(ef), jnp.imag(eb)], axis=-1)
    w1 = w1.reshape(S5_GROUPS, t * S5_GROUP, 4 * S5_STATE)
    kf = jnp.real(jnp.einsum('gcp,gtp,gpd->gtcd', ccf, pwf[:, :t, :], bbf))
    kb = jnp.real(jnp.einsum('gcp,gtp,gpd->gtcd', ccb, pwb[:, :t, :], bbb))
    si = jnp.arange(t)[:, None]
    ti = jnp.arange(t)[None, :]
    lag_f = jnp.clip(ti - si, 0, t - 1)
    lag_b = jnp.clip(si - ti, 0, t - 1)
    toep = (jnp.where((ti >= si)[None, :, :, None, None], kf[:, lag_f], 0.0)
            + jnp.where((si >= ti)[None, :, :, None, None], kb[:, lag_b], 0.0))
    eye_t = jnp.eye(t, dtype=F32)[None, :, :, None, None]
    eye_c = jnp.eye(S5_GROUP, dtype=F32)[None, None, None, :, :]
    toep = toep + eye_t * eye_c * d_skip.reshape(S5_GROUPS, 1, 1, S5_GROUP, 1)
    w_intra = toep.transpose(0, 1, 4, 2, 3).reshape(S5_GROUPS, t * S5_GROUP, t * S5_GROUP)
    mf = jnp.einsum('gcp,gtp->gptc', ccf, pwf[:, 1:t + 1, :])
    mb = jnp.einsum('gcp,gtp->gptc', ccb, pwb[:, t:0:-1, :])
    w_state = jnp.concatenate([jnp.real(mf), jnp.real(mb), -jnp.imag(mf), -jnp.imag(mb)], axis=1)
    w_state = w_state.reshape(S5_GROUPS, 4 * S5_STATE, t * S5_GROUP)
    w3 = jnp.concatenate([w_intra, w_state], axis=1)
    lam_r = jnp.concatenate([jnp.real(lamp[0]), jnp.real(lamp[1])], axis=-1)
    lam_i = jnp.concatenate([jnp.imag(lamp[0]), jnp.imag(lamp[1])], axis=-1)
    lamp = jnp.stack([lam_r, lam_i], axis=2)
    return w1.astype(BF16), w3.astype(BF16), lamp.astype(F32)


def _s5_kernel(u_ref, w1_ref, w3_ref, lam_ref, y_ref, *, nc, width, levels):
    u = u_ref[0, 0]
    loc = jnp.dot(u, w1_ref[0], preferred_element_type=F32)
    re = loc[:, :2 * S5_STATE]
    im = loc[:, 2 * S5_STATE:]
    row = lax.broadcasted_iota(jnp.int32, (nc, 2 * S5_STATE), 0)
    fwd = lax.broadcasted_iota(jnp.int32, (nc, 2 * S5_STATE), 1) < S5_STATE

    def shifted(x, d):
        down = jnp.where(row >= d, pltpu.roll(x, d, 0), 0.0)
        up = jnp.where(row < nc - d, pltpu.roll(x, nc - d, 0), 0.0)
        return jnp.where(fwd, down, up)

    for k in range(levels):
        d = 2 ** k
        if d >= nc:
            break
        lr = lam_ref[0, k, 0:1, :]
        li = lam_ref[0, k, 1:2, :]
        sr = shifted(re, d)
        si = shifted(im, d)
        re, im = re + sr * lr - si * li, im + sr * li + si * lr
    ent = jnp.concatenate([shifted(re, 1), shifted(im, 1)], axis=1)
    y = jnp.dot(u, w3_ref[0, :width, :], preferred_element_type=F32)
    y = y + jnp.dot(ent.astype(BF16), w3_ref[0, width:, :], preferred_element_type=F32)
    y_ref[0, 0] = y


def _s5_ssm(u5, w1, w3, lamp):
    g, b, nc, width = u5.shape
    levels = lamp.shape[1]
    kern = functools.partial(_s5_kernel, nc=nc, width=width, levels=levels)
    return pl.pallas_call(
        kern,
        grid=(g, b),
        in_specs=[pl.BlockSpec((1, 1, nc, width), lambda gi, bi: (gi, bi, 0, 0)),
                  pl.BlockSpec((1, width, 4 * S5_STATE), lambda gi, bi: (gi, 0, 0)),
                  pl.BlockSpec((1, width + 4 * S5_STATE, width), lambda gi, bi: (gi, 0, 0)),
                  pl.BlockSpec((1, levels, 2, 2 * S5_STATE), lambda gi, bi: (gi, 0, 0, 0))],
        out_specs=pl.BlockSpec((1, 1, nc, width), lambda gi, bi: (gi, bi, 0, 0)),
        out_shape=jax.ShapeDtypeStruct((g, b, nc, width), F32),
        compiler_params=_cparams("parallel", "parallel"),
        name="s5_ssm",
    )(u5, w1, w3, lamp)


def _rope_kernel(qq_ref, kk_ref, v_ref, cos_ref, sin_ref, qa_ref, qb_ref, ko_ref, vo_ref):
    width = DF_HEADS * 2 * DF_QK
    cos = jnp.concatenate([cos_ref[...]] * (width // LANES), axis=1)
    sin = jnp.concatenate([sin_ref[...]] * (width // LANES), axis=1)
    lane = lax.broadcasted_iota(jnp.int32, (1, width), 1)
    first_half = (lane % DF_QK) < (DF_QK // 2)

    def rot(x):
        partner = jnp.where(first_half, pltpu.roll(x, width - DF_QK // 2, 1),
                            pltpu.roll(x, DF_QK // 2, 1))
        return x * cos + partner * sin

    q = rot(qq_ref[0]) * (DF_QK ** -0.5 * LOG2E)
    is_a = (lane % (2 * DF_QK)) < DF_QK
    qa_ref[0] = jnp.where(is_a, q, 0.0).astype(BF16)
    qb_ref[0] = jnp.where(is_a, 0.0, q).astype(BF16)
    ko_ref[0] = rot(kk_ref[0]).astype(BF16)
    vo_ref[0] = v_ref[0].astype(BF16)


def _rope_prep(proj, cos_t, sin_t):
    b, s, _ = proj.shape
    tile = min(TILE_ROPE, s)
    w = DF_WIDTH
    spec = lambda cb: pl.BlockSpec((1, tile, w), lambda bi, i: (bi, i, cb))
    out = pl.BlockSpec((1, tile, w), lambda bi, i: (bi, i, 0))
    tab = pl.BlockSpec((tile, LANES), lambda bi, i: (i, 0))
    return pl.pallas_call(
        _rope_kernel,
        grid=(b, s // tile),
        in_specs=[spec(COL_QQ // w), spec(COL_KK // w), spec(COL_DV // w), tab, tab],
        out_specs=[out, out, out, out],
        out_shape=[jax.ShapeDtypeStruct((b, s, w), BF16)] * 4,
        compiler_params=_cparams("parallel", "parallel"),
        name="rope_prep",
    )(proj, proj, proj, cos_t, sin_t)


def _diff_attn_kernel(qa_ref, qb_ref, k_ref, v_ref, lq1_ref, lk1_ref, lq2_ref, lk2_ref, ng_ref,
                      o_ref, s_ref, *, tq, tk, nk, lam_init):
    q2 = (qa_ref[0], qb_ref[0])
    nt_dims = (((1,), (1,)), ((), ()))

    def fold(x, op, init):
        for j in range(tk // LANES):
            init = op(init, x[:, j * LANES:(j + 1) * LANES])
        return init

    m = [jnp.full((tq, LANES), -jnp.inf, F32)] * 2
    for c in range(nk):
        kc = k_ref[0, c * tk:(c + 1) * tk, :]
        for i in range(2):
            sc = lax.dot_general(q2[i], kc, nt_dims, preferred_element_type=F32)
            s_ref[c, i * tq:(i + 1) * tq, :] = sc
            m[i] = fold(sc, jnp.maximum, m[i])
    m = [jnp.max(x, axis=-1, keepdims=True) for x in m]

    acc = [jnp.zeros((tq, 2 * DF_V), F32)] * 2
    ones = jnp.ones((tk, DF_V), BF16)
    for c in range(nk):
        vc = jnp.concatenate([v_ref[0, c * tk:(c + 1) * tk, :], ones], axis=1)
        for i in range(2):
            e = jnp.exp2((s_ref[c, i * tq:(i + 1) * tq, :] - m[i]).astype(BF16))
            acc[i] = acc[i] + jnp.dot(e, vc, preferred_element_type=F32)
    lam = (jnp.exp(jnp.sum(lq1_ref[...] * lk1_ref[...], axis=-1, keepdims=True))
           - jnp.exp(jnp.sum(lq2_ref[...] * lk2_ref[...], axis=-1, keepdims=True)) + lam_init)
    acc = (acc[0][:, :DF_V] * (1.0 / acc[0][:, DF_V:DF_V + 1])
           - acc[1][:, :DF_V] * (lam / acc[1][:, DF_V:DF_V + 1]))
    ms = jnp.mean(acc * acc, axis=-1, keepdims=True)
    o_ref[0] = acc * lax.rsqrt(ms + RMS_EPS) * ng_ref[...] * (1.0 - lam_init)


def _diff_attn(qa, qb, kk, vv, lq1, lk1, lq2, lk2, norm_g, lam_init):
    b, s, _ = qa.shape
    tq = min(TILE_Q, s)
    tk = min(TILE_K, s)
    nk = s // tk
    kern = functools.partial(_diff_attn_kernel, tq=tq, tk=tk, nk=nk, lam_init=lam_init)
    qspec = pl.BlockSpec((1, tq, LANES), lambda bi, h, i: (bi, i, h))
    kspec = pl.BlockSpec((1, s, LANES), lambda bi, h, i: (bi, 0, h))
    vec64 = pl.BlockSpec((1, DF_QK), lambda bi, h, i: (0, 0))
    return pl.pallas_call(
        kern,
        grid=(b, DF_HEADS, s // tq),
        in_specs=[qspec, qspec, kspec, kspec, vec64, vec64, vec64, vec64,
                  pl.BlockSpec((1, DF_V), lambda bi, h, i: (0, 0))],
        out_specs=pl.BlockSpec((1, tq, DF_V), lambda bi, h, i: (bi, i, h)),
        out_shape=jax.ShapeDtypeStruct((b, s, DF_WIDTH), F32),
        scratch_shapes=[pltpu.VMEM((nk, 2 * tq, tk), F32)],
        compiler_params=_cparams("parallel", "parallel", "arbitrary"),
        name="diff_attn",
    )(qa, qb, kk, vv, lq1, lk1, lq2, lk2, norm_g)


def _layer_norm(x, g, b):
    mu = jnp.mean(x, axis=-1, keepdims=True)
    xc = x - mu
    var = jnp.mean(xc * xc, axis=-1, keepdims=True)
    return xc * lax.rsqrt(var + LN_EPS) * g + b


def _merge_kernel(x_ref, of_ref, ob_ref, z_ref, y5_ref, oc_ref, g0_ref, g1_ref, g2_ref,
                  bgate_ref, dnn_ref, gluw_ref, glub_ref, wbr_ref, wout_ref, lng_ref, lnb_ref,
                  o_ref):
    o = of_ref[0] + ob_ref[0]
    z = z_ref[0]
    parts = []
    for h in range(DN_HEADS):
        oh = o[:, h * DN_DIM:(h + 1) * DN_DIM]
        ms = jnp.mean(oh * oh, axis=-1, keepdims=True)
        parts.append(oh * lax.rsqrt(ms + RMS_EPS) * dnn_ref[...])
    o_a = jnp.concatenate(parts, axis=1) * _silu(z)
    y = jax.nn.gelu(y5_ref[0])
    o_b = y * _sigmoid(_bdot(y, gluw_ref[...]) + glub_ref[...])
    o_c = oc_ref[0]
    bg = bgate_ref[...]
    merged = (_sigmoid(g0_ref[0] + bg[:, 0:D_MODEL]) * _bdot(o_a, wbr_ref[0])
              + _sigmoid(g1_ref[0] + bg[:, D_MODEL:2 * D_MODEL]) * _bdot(o_b, wbr_ref[1])
              + _sigmoid(g2_ref[0] + bg[:, 2 * D_MODEL:]) * _bdot(o_c, wbr_ref[2]))
    hres = DEEPNORM_ALPHA * x_ref[0] + _bdot(merged, wout_ref[...])
    o_ref[0] = _layer_norm(hres, lng_ref[...], lnb_ref[...])


def _merge(x, o_f, o_b, proj, y5, o_c, b_gate, dn_norm, glu_w, glu_b, w_branch, w_out, ln_g, ln_b):
    b, s, _ = x.shape
    tile = min(TILE_MERGE, s)
    row = lambda w, cb=0: pl.BlockSpec((1, tile, w), lambda bi, i: (bi, i, cb))
    full = lambda shape: pl.BlockSpec(shape, lambda bi, i: (0,) * len(shape))
    gcb = COL_GATE // D_MODEL
    return pl.pallas_call(
        _merge_kernel,
        grid=(b, s // tile),
        in_specs=[row(D_MODEL), row(DN_WIDTH), row(DN_WIDTH), row(DN_WIDTH, COL_DN_Z // DN_WIDTH),
                  row(S5_WIDTH), row(DF_WIDTH),
                  row(D_MODEL, gcb), row(D_MODEL, gcb + 1), row(D_MODEL, gcb + 2),
                  full((1, N_BRANCH * D_MODEL)), full((1, DN_DIM)),
                  full((S5_WIDTH, S5_WIDTH)), full((1, S5_WIDTH)),
                  full((N_BRANCH, DN_WIDTH, D_MODEL)), full((D_MODEL, D_MODEL)),
                  full((1, D_MODEL)), full((1, D_MODEL))],
        out_specs=row(D_MODEL),
        out_shape=jax.ShapeDtypeStruct((b, s, D_MODEL), F32),
        compiler_params=_cparams("parallel", "parallel"),
        name="merge",
    )(x, o_f, o_b, proj, y5, o_c, proj, proj, proj, b_gate, dn_norm, glu_w, glu_b,
      w_branch, w_out, ln_g, ln_b)


def _xattn_kernel(x_ref, kv_ref, wq_ref, wo_ref, lng_ref, lnb_ref, o_ref):
    x = x_ref[0]
    q = _bdot(x, wq_ref[...])
    outs = []
    for h in range(XA_HEADS):
        qh = q[:, h * XA_DIM:(h + 1) * XA_DIM]
        kh = kv_ref[0, :, h * XA_DIM:(h + 1) * XA_DIM]
        vh = kv_ref[0, :, XA_WIDTH + h * XA_DIM:XA_WIDTH + (h + 1) * XA_DIM]
        sc = _bdot_nt(qh, kh) * (XA_DIM ** -0.5)
        sc = sc - jnp.max(sc, axis=-1, keepdims=True)
        e = jnp.exp(sc)
        pr = e / jnp.sum(e, axis=-1, keepdims=True)
        outs.append(_bdot(pr, vh))
    o = jnp.concatenate(outs, axis=1)
    hres = DEEPNORM_ALPHA * x + _bdot(o, wo_ref[...])
    o_ref[0] = _layer_norm(hres, lng_ref[...], lnb_ref[...])


def _xattn(x, kv, wq, wo, ln_g, ln_b):
    b, s, _ = x.shape
    tile = min(TILE_XA, s)
    mlen = kv.shape[1]
    row = pl.BlockSpec((1, tile, D_MODEL), lambda bi, i: (bi, i, 0))
    full = lambda shape: pl.BlockSpec(shape, lambda bi, i: (0,) * len(shape))
    return pl.pallas_call(
        _xattn_kernel,
        grid=(b, s // tile),
        in_specs=[row, pl.BlockSpec((1, mlen, 2 * XA_WIDTH), lambda bi, i: (bi, 0, 0)),
                  full((D_MODEL, XA_WIDTH)), full((XA_WIDTH, D_MODEL)),
                  full((1, D_MODEL)), full((1, D_MODEL))],
        out_specs=row,
        out_shape=jax.ShapeDtypeStruct((b, s, D_MODEL), F32),
        compiler_params=_cparams("parallel", "parallel"),
        name="xattn",
    )(x, kv, wq, wo, ln_g, ln_b)


def _router_kernel(x_ref, rw_ref, rb_ref, idx_ref, gate_ref, xb_ref):
    x = x_ref[...]
    xb_ref[...] = x.astype(BF16)
    logits = lax.dot_general(rw_ref[...], x, (((1,), (1,)), ((), ())), precision=HI,
                             preferred_element_type=F32) + rb_ref[...]
    tn = logits.shape[1]
    eid = lax.broadcasted_iota(jnp.int32, (N_EXPERTS, tn), 0)
    vals, ids = [], []
    cur = logits
    for _ in range(TOP_K):
        m = jnp.max(cur, axis=0, keepdims=True)
        sel = jnp.min(jnp.where(cur == m, eid, N_EXPERTS), axis=0, keepdims=True)
        vals.append(m)
        ids.append(sel)
        cur = jnp.where(eid == sel, -jnp.inf, cur)
    top_v = jnp.concatenate(vals, axis=0)
    e = jnp.exp(top_v - top_v[0:1])
    gate_ref[...] = e / jnp.sum(e, axis=0, keepdims=True)
    idx_ref[...] = jnp.concatenate(ids, axis=0)


def _router(x2d, router_w_t, router_b):
    n, _ = x2d.shape
    tile = min(TILE_ROUTER, n)
    return pl.pallas_call(
        _router_kernel,
        grid=(n // tile,),
        in_specs=[pl.BlockSpec((tile, D_MODEL), lambda i: (i, 0)),
                  pl.BlockSpec((N_EXPERTS, D_MODEL), lambda i: (0, 0)),
                  pl.BlockSpec((N_EXPERTS, 1), lambda i: (0, 0))],
        out_specs=[pl.BlockSpec((TOP_K, tile), lambda i: (0, i)),
                   pl.BlockSpec((TOP_K, tile), lambda i: (0, i)),
                   pl.BlockSpec((tile, D_MODEL), lambda i: (i, 0))],
        out_shape=[jax.ShapeDtypeStruct((TOP_K, n), jnp.int32),
                   jax.ShapeDtypeStruct((TOP_K, n), F32),
                   jax.ShapeDtypeStruct((n, D_MODEL), BF16)],
        compiler_params=_cparams("parallel"),
        name="router",
    )(x2d, router_w_t, router_b)


def _rank_kernel(idx_ref, rank_ref, cnt_ref, run_ref, *, tile):
    @pl.when(pl.program_id(0) == 0)
    def _():
        run_ref[...] = jnp.zeros_like(run_ref)

    idx = idx_ref[...]
    eid = lax.broadcasted_iota(jnp.int32, (N_EXPERTS, tile), 0)
    si = lax.broadcasted_iota(jnp.int32, (tile, tile), 0)
    ti = lax.broadcasted_iota(jnp.int32, (tile, tile), 1)
    before = (si < ti).astype(BF16)
    run = run_ref[...]
    ranks = []
    for k in range(TOP_K):
        onehot = eid == idx[k:k + 1]
        oh = onehot.astype(BF16)
        within = jnp.dot(oh, before, preferred_element_type=F32)
        ranks.append(jnp.sum(jnp.where(onehot, within + run, 0.0), axis=0, keepdims=True))
        run = run + jnp.sum(onehot.astype(F32), axis=1, keepdims=True)
    rank_ref[...] = jnp.concatenate(ranks, axis=0).astype(jnp.int32)
    run_ref[...] = run
    cnt_ref[...] = run.astype(jnp.int32)


def _rank(top_i):
    _, n = top_i.shape
    tile = min(TILE_ROUTER, n)
    kern = functools.partial(_rank_kernel, tile=tile)
    return pl.pallas_call(
        kern,
        grid=(n // tile,),
        in_specs=[pl.BlockSpec((TOP_K, tile), lambda i: (0, i))],
        out_specs=[pl.BlockSpec((TOP_K, tile), lambda i: (0, i)),
                   pl.BlockSpec((N_EXPERTS, 1), lambda i: (0, 0))],
        out_shape=[jax.ShapeDtypeStruct((TOP_K, n), jnp.int32),
                   jax.ShapeDtypeStruct((N_EXPERTS, 1), jnp.int32)],
        scratch_shapes=[pltpu.VMEM((N_EXPERTS, 1), F32)],
        compiler_params=_cparams("arbitrary"),
        name="moe_rank",
    )(top_i)


def _expert_kernel(be_ref, x_ref, wg_ref, wu_ref, bg_ref, bu_ref, wd_ref, bd_ref, o_ref):
    del be_ref
    x = x_ref[...]
    hg = jnp.dot(x, wg_ref[0], preferred_element_type=F32) + bg_ref[0]
    hu = jnp.dot(x, wu_ref[0], preferred_element_type=F32) + bu_ref[0]
    gt = jnp.minimum(hg, SWIGLU_LIMIT)
    up = jnp.clip(hu, -SWIGLU_LIMIT, SWIGLU_LIMIT)
    act = (up + 1.0) * gt * _sigmoid(SWIGLU_ALPHA * gt)
    y = jnp.dot(act.astype(BF16), wd_ref[0], preferred_element_type=F32) + bd_ref[0]
    o_ref[...] = y.astype(o_ref.dtype)


def _experts(x_rows, blk_e, w_g, w_u, b_g, b_u, w_d, b_d, *, block):
    rows, _ = x_rows.shape
    wspec = lambda k, m: pl.BlockSpec((1, k, m), lambda i, be: (be[i], 0, 0))
    grid_spec = pltpu.PrefetchScalarGridSpec(
        num_scalar_prefetch=1,
        grid=(rows // block,),
        in_specs=[pl.BlockSpec((block, D_MODEL), lambda i, be: (i, 0)),
                  wspec(D_MODEL, D_FF), wspec(D_MODEL, D_FF), wspec(1, D_FF), wspec(1, D_FF),
                  wspec(D_FF, D_MODEL), wspec(1, D_MODEL)],
        out_specs=pl.BlockSpec((block, D_MODEL), lambda i, be: (i, 0)),
    )
    return pl.pallas_call(
        _expert_kernel,
        grid_spec=grid_spec,
        out_shape=jax.ShapeDtypeStruct((rows, D_MODEL), BF16),
        compiler_params=_cparams("arbitrary"),
        name="moe_experts",
    )(blk_e, x_rows, w_g, w_u, b_g, b_u, w_d, b_d)


def _combine_kernel(x_ref, y_ref, gate_ref, lng_ref, lnb_ref, o_ref):
    gate = gate_ref[...]
    h = y_ref[0].astype(F32) * gate[:, 0:1]
    for k in range(1, TOP_K):
        h = h + y_ref[k].astype(F32) * gate[:, k:k + 1]
    o_ref[...] = _layer_norm(DEEPNORM_ALPHA * x_ref[...] + h, lng_ref[...], lnb_ref[...])


def _combine(x2d, y_g, gate_t, ln_g, ln_b):
    n, _ = x2d.shape
    tile = min(TILE_COMBINE, n)
    return pl.pallas_call(
        _combine_kernel,
        grid=(n // tile,),
        in_specs=[pl.BlockSpec((tile, D_MODEL), lambda i: (i, 0)),
                  pl.BlockSpec((TOP_K, tile, D_MODEL), lambda i: (0, i, 0)),
                  pl.BlockSpec((tile, TOP_K), lambda i: (i, 0)),
                  pl.BlockSpec((1, D_MODEL), lambda i: (0, 0)),
                  pl.BlockSpec((1, D_MODEL), lambda i: (0, 0))],
        out_specs=pl.BlockSpec((tile, D_MODEL), lambda i: (i, 0)),
        out_shape=jax.ShapeDtypeStruct((n, D_MODEL), F32),
        compiler_params=_cparams("parallel"),
        name="moe_combine",
    )(x2d, y_g, gate_t, ln_g, ln_b)


def _split_kernel(w_ref, se_ref, so_ref, g_ref, u_ref):
    w = w_ref[0].astype(BF16)
    g_ref[0] = jnp.dot(w, se_ref[...], preferred_element_type=F32).astype(BF16)
    u_ref[0] = jnp.dot(w, so_ref[...], preferred_element_type=F32).astype(BF16)


def _split_gate_up(w_gu):
    e, dm, f2 = w_gu.shape
    f = f2 // 2
    tr = 256
    col = jnp.arange(f2, dtype=jnp.int32)[:, None]
    out = jnp.arange(f, dtype=jnp.int32)[None, :]
    sel_even = (col == 2 * out).astype(BF16)
    sel_odd = (col == 2 * out + 1).astype(BF16)
    sel = pl.BlockSpec((f2, f), lambda ei, i: (0, 0))
    ospec = pl.BlockSpec((1, tr, f), lambda ei, i: (ei, i, 0))
    return pl.pallas_call(
        _split_kernel,
        grid=(e, dm // tr),
        in_specs=[pl.BlockSpec((1, tr, f2), lambda ei, i: (ei, i, 0)), sel, sel],
        out_specs=[ospec, ospec],
        out_shape=[jax.ShapeDtypeStruct((e, dm, f), BF16)] * 2,
        compiler_params=_cparams("parallel", "parallel"),
        name="split_gate_up",
    )(w_gu, sel_even, sel_odd)


def _moe(x, w):
    b, s, dm = x.shape
    n = b * s
    x2d = x.reshape(n, dm)
    top_i, gate, xb = _router(x2d, w['router_w_t'], w['router_b'])
    rank, counts = _rank(top_i)
    block = MOE_BLOCK
    m = n * TOP_K
    nblk = -(-m // block) + N_EXPERTS
    rows = nblk * block
    counts = counts.reshape(N_EXPERTS)
    padded = (counts + block - 1) // block * block
    pad_end = jnp.cumsum(padded)
    pad_start = pad_end - padded
    dest = pad_start[top_i] + rank
    tok = jnp.broadcast_to(jnp.arange(n, dtype=jnp.int32)[None, :], (TOP_K, n))
    row_tok = jnp.full((rows,), n, jnp.int32).at[dest.reshape(-1)].set(
        tok.reshape(-1), unique_indices=True)
    blk_start = jnp.arange(nblk, dtype=pad_end.dtype) * block
    blk_e = jnp.minimum(jnp.sum(pad_end[None, :] <= blk_start[:, None], axis=1),
                        N_EXPERTS - 1).astype(jnp.int32)
    x_rows = jnp.concatenate([xb, jnp.zeros((1, dm), BF16)], axis=0)[row_tok]
    y_rows = _experts(x_rows, blk_e, w['e_w_g'], w['e_w_u'], w['e_b_g'], w['e_b_u'], w['e_w_d'],
                      w['e_b_d'], block=block)
    y_g = y_rows[dest]
    out = _combine(x2d, y_g, gate.T, w['ln3_g'], w['ln3_b'])
    return out.reshape(b, s, dm)


def _rope_tables(s):
    half = DF_QK // 2
    inv_freq = jnp.power(ROPE_THETA, -jnp.arange(half, dtype=F32) / half)
    ang = jnp.arange(s, dtype=F32)[:, None] * inv_freq[None, :]
    cos = jnp.cos(ang)
    sin = jnp.sin(ang)
    cos_t = jnp.concatenate([cos, cos, cos, cos], axis=1)
    sin_t = jnp.concatenate([-sin, sin, -sin, sin], axis=1)
    return cos_t, sin_t


def _prepare_layer(l, p, s):
    dm = D_MODEL
    w = {}
    w['w_in'] = _pack_in_proj(p['w_in'][l]).astype(BF16)
    pad8 = jnp.zeros((8,), F32)
    pad112 = jnp.zeros((LANES - 16,), F32)
    w['nega'] = jnp.concatenate([pad8, -jnp.exp(p['dn_A_log'][l].reshape(8)), pad112]).reshape(1, LANES)
    w['dtb'] = jnp.concatenate([pad8, p['dn_dt_bias'][l].reshape(8), pad112]).reshape(1, LANES)
    w['dn_conv'] = p['dn_conv'][l]
    t5 = min(S5_CHUNK, s)
    w['s5'] = _s5_operators(p['s5_lam_re'][l], p['s5_lam_im'][l], p['s5_log_dt'][l],
                            p['s5_B_re'][l], p['s5_B_im'][l], p['s5_C_re'][l],
                            p['s5_C_im'][l], p['s5_D'][l], chunk=t5, nc=s // t5)
    for name in ('df_lq1', 'df_lk1', 'df_lq2', 'df_lk2'):
        w[name] = p[name][l].reshape(1, DF_QK)
    w['df_norm'] = p['df_norm'][l].reshape(1, DF_V)
    w['lam_init'] = 0.8 - 0.6 * math.exp(-0.3 * l)
    w['b_gate'] = p['b_gate'][l].reshape(1, -1)
    w['dn_norm'] = p['dn_norm'][l].reshape(1, DN_DIM)
    w['s5_glu_w'] = p['s5_glu_w'][l].astype(BF16)
    w['s5_glu_b'] = p['s5_glu_b'][l].reshape(1, -1)
    w['w_branch'] = p['w_branch'][l].astype(BF16)
    w['w_out'] = p['w_out'][l].astype(BF16)
    for name in ('ln1_g', 'ln1_b', 'ln2_g', 'ln2_b', 'ln3_g', 'ln3_b'):
        w[name] = p[name][l].reshape(1, dm)
    w['xa_wkv'] = p['xa_wkv'][l].astype(BF16)
    w['xa_wq'] = p['xa_wq'][l].astype(BF16)
    w['xa_wo'] = p['xa_wo'][l].astype(BF16)
    w['router_w_t'] = p['router_w'][l].T
    w['router_b'] = p['router_b'][l].reshape(N_EXPERTS, 1)
    w['e_w_g'], w['e_w_u'] = _split_gate_up(p['e_w_gu'][l])
    w['e_b_g'] = p['e_b_gu'][l][:, None, 0::2]
    w['e_b_u'] = p['e_b_gu'][l][:, None, 1::2]
    w['e_w_d'] = p['e_w_down'][l].astype(BF16)
    w['e_b_d'] = p['e_b_down'][l][:, None, :]
    return w


def _layer(x, mem, w, tabs):
    b, s, dm = x.shape
    n = b * s
    proj = _matmul(x.reshape(n, dm), w['w_in'], tm=TILE_MM, tn=LANES * 19)
    proj = proj.reshape(b, s, PROJ_WIDTH)

    qkv, bg = _dn_prep(proj, w['dn_conv'], w['nega'], w['dtb'], chunk=DN_CHUNK)
    bgr = bg[:, :, :16].reshape(b, s // DN_CHUNK, DN_CHUNK, 16).transpose(0, 1, 3, 2)
    o_f, o_b = _dn_scan(qkv, bg, bgr, chunk=DN_CHUNK)

    t5 = min(S5_CHUNK, s)
    nc = s // t5
    u5 = proj[:, :, COL_S5_U:COL_S5_U + S5_WIDTH].reshape(b, nc, t5, S5_GROUPS, S5_GROUP)
    u5 = u5.transpose(3, 0, 1, 2, 4).reshape(S5_GROUPS, b, nc, t5 * S5_GROUP).astype(BF16)
    y5 = _s5_ssm(u5, *w['s5'])
    y5 = y5.reshape(S5_GROUPS, b, nc, t5, S5_GROUP).transpose(1, 2, 3, 0, 4).reshape(b, s, S5_WIDTH)

    qa, qb, kk, vv = _rope_prep(proj, *tabs)
    o_c = _diff_attn(qa, qb, kk, vv, w['df_lq1'], w['df_lk1'], w['df_lq2'], w['df_lk2'],
                     w['df_norm'], w['lam_init'])

    x = _merge(x, o_f, o_b, proj, y5, o_c, w['b_gate'], w['dn_norm'], w['s5_glu_w'], w['s5_glu_b'],
               w['w_branch'], w['w_out'], w['ln1_g'], w['ln1_b'])

    mlen = mem.shape[1]
    kv = _matmul(mem.reshape(b * mlen, dm), w['xa_wkv'], tm=512, tn=1024)
    x = _xattn(x, kv.reshape(b, mlen, 2 * XA_WIDTH), w['xa_wq'], w['xa_wo'], w['ln2_g'], w['ln2_b'])
    return _moe(x, w)


def kernel(x_prompt, x_sample, mem_prompt, mem_sample, w_in, b_gate, dn_conv, dn_A_log, dn_dt_bias,
           dn_norm, s5_lam_re, s5_lam_im, s5_log_dt, s5_B_re, s5_B_im, s5_C_re, s5_C_im, s5_D,
           s5_glu_w, s5_glu_b, df_lq1, df_lk1, df_lq2, df_lk2, df_norm, w_branch, w_out,
           ln1_g, ln1_b, xa_wq, xa_wkv, xa_wo, ln2_g, ln2_b, router_w, router_b,
           e_w_gu, e_b_gu, e_w_down, e_b_down, ln3_g, ln3_b):
    p = dict(w_in=w_in, b_gate=b_gate, dn_conv=dn_conv, dn_A_log=dn_A_log, dn_dt_bias=dn_dt_bias,
             dn_norm=dn_norm, s5_lam_re=s5_lam_re, s5_lam_im=s5_lam_im, s5_log_dt=s5_log_dt,
             s5_B_re=s5_B_re, s5_B_im=s5_B_im, s5_C_re=s5_C_re, s5_C_im=s5_C_im, s5_D=s5_D,
             s5_glu_w=s5_glu_w, s5_glu_b=s5_glu_b, df_lq1=df_lq1, df_lk1=df_lk1, df_lq2=df_lq2,
             df_lk2=df_lk2, df_norm=df_norm, w_branch=w_branch, w_out=w_out, ln1_g=ln1_g,
             ln1_b=ln1_b, xa_wq=xa_wq, xa_wkv=xa_wkv, xa_wo=xa_wo, ln2_g=ln2_g, ln2_b=ln2_b,
             router_w=router_w, router_b=router_b, e_w_gu=e_w_gu, e_b_gu=e_b_gu,
             e_w_down=e_w_down, e_b_down=e_b_down, ln3_g=ln3_g, ln3_b=ln3_b)
    s = x_prompt.shape[1]
    assert x_sample.shape[1] == s
    tabs = _rope_tables(s)
    layers = [_prepare_layer(l, p, s) for l in range(DEPTH)]
    outs = []
    for x, mem in ((x_prompt, mem_prompt), (x_sample, mem_sample)):
        for w in layers:
            x = _layer(x, mem, w, tabs)
        outs.append(x)
    return tuple(outs)
```

```python
import functools
import math

import numpy as np
import jax
import jax.numpy as jnp
from jax import lax
from jax.experimental import pallas as pl
from jax.experimental.pallas import tpu as pltpu

F32 = jnp.float32
BF16 = jnp.bfloat16
HI = lax.Precision.HIGHEST

D_MODEL = 1024
DN_HEADS = 4
DN_DIM = 128
DN_WIDTH = DN_HEADS * DN_DIM
DN_CONV = 4
S5_GROUP = 16
S5_WIDTH = 512
S5_GROUPS = S5_WIDTH // S5_GROUP
S5_STATE = 64
DF_HEADS = 4
DF_QK = 64
DF_V = 128
DF_WIDTH = DF_HEADS * DF_V
ROPE_THETA = 10000.0
XA_HEADS = 4
XA_DIM = 128
XA_WIDTH = XA_HEADS * XA_DIM
N_EXPERTS = 32
TOP_K = 4
D_FF = 1024
SWIGLU_LIMIT = 7.0
SWIGLU_ALPHA = 1.702
N_BRANCH = 3
DEPTH = 2
DEEPNORM_ALPHA = (2 * DEPTH) ** 0.25
LN_EPS = 1e-5
RMS_EPS = 1e-6
LOG2E = 1.4426950408889634

LANES = 128
SUBLANES = 8
VMEM_LIMIT = 56 * 1024 * 1024

COL_DN_QKV = 0
COL_DN_Z = 1536
COL_S5_U = 2048
COL_QQ = 2560
COL_KK = 3072
COL_DV = 3584
COL_GATE = 4096
COL_BA = 7168
PROJ_WIDTH = 7296

TILE_MM = 1024
TILE_DN_PREP = 512
TILE_DN_SCAN = 256
DN_CHUNK = 64
DN_CHUNKS_PER_TRIP = 2
S5_CHUNK = 32
TILE_ROPE = 512
TILE_Q = 256
TILE_K = 512
TILE_MERGE = 256
TILE_XA = 512
TILE_ROUTER = 512
MOE_BLOCK = 512
TILE_COMBINE = 256


def _in_proj_perm():
    o_dn_b, o_dn_a, o_s5 = 2048, 2056, 2064
    o_q1, o_q2, o_k1, o_k2, o_dv, o_gate = 2576, 2832, 3088, 3344, 3600, 4112
    perm = np.full((PROJ_WIDTH,), -1, np.int64)
    perm[0:2048] = np.arange(2048)
    perm[COL_S5_U:COL_S5_U + 512] = o_s5 + np.arange(512)
    for h in range(DF_HEADS):
        j = np.arange(DF_QK)
        perm[COL_QQ + h * 128 + j] = o_q1 + h * DF_QK + j
        perm[COL_QQ + h * 128 + 64 + j] = o_q2 + h * DF_QK + j
        perm[COL_KK + h * 128 + j] = o_k1 + h * DF_QK + j
        perm[COL_KK + h * 128 + 64 + j] = o_k2 + h * DF_QK + j
    perm[COL_DV:COL_DV + 512] = o_dv + np.arange(512)
    perm[COL_GATE:COL_GATE + 3072] = o_gate + np.arange(3072)
    perm[COL_BA:COL_BA + 8] = o_dn_b + np.arange(8)
    perm[COL_BA + 8:COL_BA + 16] = o_dn_a + np.arange(8)
    return perm


_IN_PERM = _in_proj_perm()


def _pack_in_proj(w_in):
    runs, start = [], 0
    for i in range(1, PROJ_WIDTH + 1):
        if i == PROJ_WIDTH or _IN_PERM[i] != _IN_PERM[i - 1] + (1 if _IN_PERM[i - 1] >= 0 else 0):
            runs.append((start, i))
            start = i
    pieces = []
    for a, b in runs:
        src = int(_IN_PERM[a])
        pieces.append(jnp.zeros((w_in.shape[0], b - a), w_in.dtype) if src < 0
                      else w_in[:, src:src + (b - a)])
    return jnp.concatenate(pieces, axis=1)


def _cparams(*sem):
    return pltpu.CompilerParams(dimension_semantics=sem, vmem_limit_bytes=VMEM_LIMIT)


def _bdot(a, b):
    return jnp.dot(a.astype(BF16), b.astype(BF16), preferred_element_type=F32)


def _bdot_nt(a, b):
    return lax.dot_general(a.astype(BF16), b.astype(BF16), (((1,), (1,)), ((), ())),
                           preferred_element_type=F32)


def _bdot_tn(a, b):
    return lax.dot_general(a.astype(BF16), b.astype(BF16), (((0,), (0,)), ((), ())),
                           preferred_element_type=F32)


def _sigmoid(x):
    return 1.0 / (1.0 + jnp.exp(-x))


def _silu(x):
    return x * _sigmoid(x)


def _mm_kernel(x_ref, w_ref, o_ref):
    o_ref[...] = jnp.dot(x_ref[...].astype(BF16), w_ref[...],
                         preferred_element_type=F32).astype(o_ref.dtype)


def _matmul(x, w, *, tm, tn, out_dtype=F32):
    n, k = x.shape
    m = w.shape[1]
    tm = min(tm, n)
    tn = min(tn, m)
    return pl.pallas_call(
        _mm_kernel,
        grid=(n // tm, m // tn),
        in_specs=[pl.BlockSpec((tm, k), lambda i, j: (i, 0)),
                  pl.BlockSpec((k, tn), lambda i, j: (0, j))],
        out_specs=pl.BlockSpec((tm, tn), lambda i, j: (i, j)),
        out_shape=jax.ShapeDtypeStruct((n, m), out_dtype),
        compiler_params=_cparams("parallel", "arbitrary"),
        name="matmul",
    )(x, w)


def _dn_prep_kernel(x_ref, prev_ref, next_ref, ba_ref, cw_ref, nega_ref, dtb_ref,
                    qkv_ref, bg_ref, *, tile, chunk):
    i = pl.program_id(1)
    last = pl.num_programs(1) - 1
    x = x_ref[0]
    prev = jnp.where(i > 0, prev_ref[0], 0.0)
    nxt = jnp.where(i < last, next_ref[0], 0.0)
    ext = jnp.concatenate([prev, x, nxt], axis=0)
    cw = cw_ref[...]
    acc = cw[0:1] * ext[6:6 + tile]
    for j in range(1, DN_CONV):
        acc = acc + cw[j:j + 1] * ext[6 + j:6 + j + tile]
    act = _silu(acc)
    for h in range(DN_HEADS):
        lo = h * DN_DIM
        qh = act[:, lo:lo + DN_DIM]
        qn = qh * lax.rsqrt(jnp.sum(qh * qh, axis=-1, keepdims=True) + 1e-6)
        qkv_ref[0, :, lo:lo + DN_DIM] = qn * (DN_DIM ** -0.5)
        kh = act[:, DN_WIDTH + lo:DN_WIDTH + lo + DN_DIM]
        qkv_ref[0, :, DN_WIDTH + lo:DN_WIDTH + lo + DN_DIM] = kh * lax.rsqrt(
            jnp.sum(kh * kh, axis=-1, keepdims=True) + 1e-6)
    qkv_ref[0, :, 2 * DN_WIDTH:] = act[:, 2 * DN_WIDTH:]

    ba = ba_ref[0]
    beta = _sigmoid(ba)
    z = ba + dtb_ref[...]
    softplus = jnp.maximum(z, 0.0) + jnp.log(1.0 + jnp.exp(-jnp.abs(z)))
    g = nega_ref[...] * softplus
    row = lax.broadcasted_iota(jnp.int32, (chunk, chunk), 0)
    col = lax.broadcasted_iota(jnp.int32, (chunk, chunk), 1)
    lower = (row >= col).astype(F32)
    upper = (row <= col).astype(F32)
    lane = lax.broadcasted_iota(jnp.int32, (chunk, LANES), 1)
    fwd_lane = lane < 8 + DN_HEADS
    for c in range(tile // chunk):
        gc = g[c * chunk:(c + 1) * chunk]
        cf = jnp.dot(lower, gc, precision=HI, preferred_element_type=F32)
        cb = jnp.dot(upper, gc, precision=HI, preferred_element_type=F32)
        gcum = jnp.where(fwd_lane, cf, cb)
        bg_ref[0, c * chunk:(c + 1) * chunk, :] = jnp.where(
            lane < 8, beta[c * chunk:(c + 1) * chunk], gcum)


def _dn_prep(proj, conv_w, nega, dtb, *, chunk):
    b, s, _ = proj.shape
    tile = min(TILE_DN_PREP, s)
    nt = s // tile
    r8 = tile // SUBLANES
    kern = functools.partial(_dn_prep_kernel, tile=tile, chunk=chunk)
    return pl.pallas_call(
        kern,
        grid=(b, nt),
        in_specs=[
            pl.BlockSpec((1, tile, 3 * DN_WIDTH), lambda bi, i: (bi, i, 0)),
            pl.BlockSpec((1, SUBLANES, 3 * DN_WIDTH),
                         lambda bi, i: (bi, jnp.maximum(i * r8 - 1, 0), 0)),
            pl.BlockSpec((1, SUBLANES, 3 * DN_WIDTH),
                         lambda bi, i: (bi, jnp.minimum((i + 1) * r8, s // SUBLANES - 1), 0)),
            pl.BlockSpec((1, tile, LANES), lambda bi, i: (bi, i, COL_BA // LANES)),
            pl.BlockSpec((DN_CONV, 3 * DN_WIDTH), lambda bi, i: (0, 0)),
            pl.BlockSpec((1, LANES), lambda bi, i: (0, 0)),
            pl.BlockSpec((1, LANES), lambda bi, i: (0, 0)),
        ],
        out_specs=[pl.BlockSpec((1, tile, 3 * DN_WIDTH), lambda bi, i: (bi, i, 0)),
                   pl.BlockSpec((1, tile, LANES), lambda bi, i: (bi, i, 0))],
        out_shape=[jax.ShapeDtypeStruct((b, s, 3 * DN_WIDTH), F32),
                   jax.ShapeDtypeStruct((b, s, LANES), F32)],
        compiler_params=_cparams("parallel", "parallel"),
        name="dn_prep",
    )(proj, proj, proj, proj, conv_w, nega, dtb)


def _dn_local(chains, chunk):
    ri = lax.broadcasted_iota(jnp.int32, (chunk, chunk), 0)
    ci = lax.broadcasted_iota(jnp.int32, (chunk, chunk), 1)
    pre = []
    for d, h, q, k, v, bgc, rows in chains:
        li = d * DN_HEADS + h
        beta = bgc[:, li:li + 1]
        gcol = bgc[:, 8 + li:8 + li + 1]
        grow = rows[8 + li:8 + li + 1, :]
        gtot = gcol[chunk - 1:chunk] if d == 0 else gcol[0:1]
        incl = (ri >= ci) if d == 0 else (ri <= ci)
        strict = (ri > ci) if d == 0 else (ri < ci)
        eg = jnp.exp(gcol)
        kb = k * beta
        decay = jnp.where(incl, jnp.exp(jnp.where(incl, gcol - grow, 0.0)), 0.0)
        pre.append(dict(k=k, qg=q * eg, kd=k * jnp.exp(gtot - gcol), decay=decay, strict=strict,
                        x2=jnp.concatenate([v * beta, kb * eg], axis=1), sdec=jnp.exp(gtot),
                        qkb=jnp.concatenate([q, kb], axis=0)))
    qk = [_bdot_nt(c['qkb'], c['k']) for c in pre]
    intra = [x[:chunk] * c['decay'] for x, c in zip(qk, pre)]
    p = [-jnp.where(c['strict'], x[chunk:] * c['decay'], 0.0) for x, c in zip(qk, pre)]
    r = p
    p = [_bdot(x, x) for x in p]
    for _ in range(int(math.log2(chunk)) - 2):
        rp = [_bdot(jnp.concatenate([ri_, pi_], axis=0), pi_) for ri_, pi_ in zip(r, p)]
        r = [ri_ + pi_ + x[:chunk] for ri_, pi_, x in zip(r, p, rp)]
        p = [x[chunk:] for x in rp]
    r = [ri_ + pi_ + _bdot(ri_, pi_) for ri_, pi_ in zip(r, p)]
    uw = [c['x2'] + _bdot(ri_, c['x2']) for ri_, c in zip(r, pre)]
    return [dict(u=x[:, :DN_DIM], wq=jnp.concatenate([x[:, DN_DIM:], c['qg']], axis=0),
                 intra=it, kd=c['kd'], sdec=c['sdec']) for x, it, c in zip(uw, intra, pre)]


def _dn_state(local, states, chunk):
    ws = [_bdot(c['wq'], st) for c, st in zip(local, states)]
    v_new = [c['u'] - y[:chunk] for c, y in zip(local, ws)]
    out = [y[chunk:] + _bdot(c['intra'], vn) for y, c, vn in zip(ws, local, v_new)]
    st_new = [st * c['sdec'] + _bdot_tn(c['kd'], vn) for st, c, vn in zip(states, local, v_new)]
    return out, st_new


def _dn_scan_kernel(xf_ref, xb_ref, gf_ref, gb_ref, rf_ref, rb_ref, of_ref, ob_ref, st_ref,
                    *, tile, chunk):
    @pl.when(pl.program_id(1) == 0)
    def _():
        st_ref[...] = jnp.zeros_like(st_ref)

    nct = tile // chunk
    nchain = 2 * DN_HEADS

    def body(j, carry):
        chains, where = [], []
        for slot in range(DN_CHUNKS_PER_TRIP):
            for d in range(2):
                c = DN_CHUNKS_PER_TRIP * j + slot
                c = c if d == 0 else nct - 1 - c
                r0 = pl.multiple_of(c * chunk, chunk)
                x_ref, g_ref, r_ref = (xf_ref, gf_ref, rf_ref) if d == 0 else (xb_ref, gb_ref, rb_ref)
                bgc = g_ref[0, pl.ds(r0, chunk), :]
                rows = r_ref[0, c]
                for h in range(DN_HEADS):
                    lo = h * DN_DIM
                    q = x_ref[0, pl.ds(r0, chunk), lo:lo + DN_DIM]
                    k = x_ref[0, pl.ds(r0, chunk), DN_WIDTH + lo:DN_WIDTH + lo + DN_DIM]
                    v = x_ref[0, pl.ds(r0, chunk), 2 * DN_WIDTH + lo:2 * DN_WIDTH + lo + DN_DIM]
                    chains.append((d, h, q, k, v, bgc, rows))
                    where.append((d, h, r0))
        states = [st_ref[i] for i in range(nchain)]
        local = _dn_local(chains, chunk)
        outs = []
        for slot in range(DN_CHUNKS_PER_TRIP):
            out, states = _dn_state(local[slot * nchain:(slot + 1) * nchain], states, chunk)
            outs += out
        for (d, h, r0), out in zip(where, outs):
            o_ref = of_ref if d == 0 else ob_ref
            o_ref[0, pl.ds(r0, chunk), h * DN_DIM:(h + 1) * DN_DIM] = out
        for i in range(nchain):
            st_ref[i] = states[i]
        return carry

    lax.fori_loop(0, nct // DN_CHUNKS_PER_TRIP, body, 0)


def _dn_scan(qkv, bg, bgr, *, chunk):
    b, s, _ = qkv.shape
    tile = min(TILE_DN_SCAN, s)
    nt = s // tile
    nct = tile // chunk
    kern = functools.partial(_dn_scan_kernel, tile=tile, chunk=chunk)
    fwd = lambda bi, i: (bi, i, 0)
    bwd = lambda bi, i: (bi, nt - 1 - i, 0)
    fwd4 = lambda bi, i: (bi, i, 0, 0)
    bwd4 = lambda bi, i: (bi, nt - 1 - i, 0, 0)
    return pl.pallas_call(
        kern,
        grid=(b, nt),
        in_specs=[pl.BlockSpec((1, tile, 3 * DN_WIDTH), fwd),
                  pl.BlockSpec((1, tile, 3 * DN_WIDTH), bwd),
                  pl.BlockSpec((1, tile, LANES), fwd),
                  pl.BlockSpec((1, tile, LANES), bwd),
                  pl.BlockSpec((1, nct, 16, chunk), fwd4),
                  pl.BlockSpec((1, nct, 16, chunk), bwd4)],
        out_specs=[pl.BlockSpec((1, tile, DN_WIDTH), fwd),
                   pl.BlockSpec((1, tile, DN_WIDTH), bwd)],
        out_shape=[jax.ShapeDtypeStruct((b, s, DN_WIDTH), F32),
                   jax.ShapeDtypeStruct((b, s, DN_WIDTH), F32)],
        scratch_shapes=[pltpu.VMEM((2 * DN_HEADS, DN_DIM, DN_DIM), F32)],
        compiler_params=_cparams("parallel", "arbitrary"),
        name="dn_scan",
    )(qkv, qkv, bg, bg, bgr, bgr)


def _s5_operators(lam_re, lam_im, log_dt, b_re, b_im, c_re, c_im, d_skip, *, chunk, nc):
    t = chunk
    levels = max(1, int(math.log2(nc)))
    tau = jnp.arange(t + 1, dtype=F32)
    kern = []
    lamp = []
    for d in range(2):
        lam = lax.complex(lam_re[d], lam_im[d])
        dt = jnp.exp(log_dt[d])[:, None]
        lam_dt = lam * dt
        lam_bar = jnp.exp(lam_dt)
        bbar = ((lam_bar - 1.0) / lam)[:, :, None] * lax.complex(b_re[d], b_im[d])
        cc = lax.complex(c_re[d], c_im[d])
        pw = jnp.exp(lam_dt[:, None, :] * tau[None, :, None].astype(jnp.complex64))
        kern.append((pw, bbar, cc))
        steps = (t * 2.0 ** jnp.arange(levels, dtype=F32)).astype(jnp.complex64)
        lamp.append(jnp.exp(lam_dt[:, None, :] * steps[None, :, None]))
    (pwf, bbf, ccf), (pwb, bbb, ccb) = kern
    ef = jnp.einsum('gsp,gpc->gscp', pwf[:, t - 1::-1, :], bbf)
    eb = jnp.einsum('gsp,gpc->gscp', pwb[:, :t, :], bbb)
    w1 = jnp.concatenate([jnp.real(ef), jnp.real(eb), jnp.imag(ef), jnp.imag(eb)], axis=-1)
    w1 = w1.reshape(S5_GROUPS, t * S5_GROUP, 4 * S5_STATE)
    kf = jnp.real(jnp.einsum('gcp,gtp,gpd->gtcd', ccf, pwf[:, :t, :], bbf))
    kb = jnp.real(jnp.einsum('gcp,gtp,gpd->gtcd', ccb, pwb[:, :t, :], bbb))
    si = jnp.arange(t)[:, None]
    ti = jnp.arange(t)[None, :]
    lag_f = jnp.clip(ti - si, 0, t - 1)
    lag_b = jnp.clip(si - ti, 0, t - 1)
    toep = (jnp.where((ti >= si)[None, :, :, None, None], kf[:, lag_f], 0.0)
            + jnp.where((si >= ti)[None, :, :, None, None], kb[:, lag_b], 0.0))
    eye_t = jnp.eye(t, dtype=F32)[None, :, :, None, None]
    eye_c = jnp.eye(S5_GROUP, dtype=F32)[None, None, None, :, :]
    toep = toep + eye_t * eye_c * d_skip.reshape(S5_GROUPS, 1, 1, S5_GROUP, 1)
    w_intra = toep.transpose(0, 1, 4, 2, 3).reshape(S5_GROUPS, t * S5_GROUP, t * S5_GROUP)
    mf = jnp.einsum('gcp,gtp->gptc', ccf, pwf[:, 1:t + 1, :])
    mb = jnp.einsum('gcp,gtp->gptc', ccb, pwb[:, t:0:-1, :])
    w_state = jnp.concatenate([jnp.real(mf), jnp.real(mb), -jnp.imag(mf), -jnp.imag(mb)], axis=1)
    w_state = w_state.reshape(S5_GROUPS, 4 * S5_STATE, t * S5_GROUP)
    w3 = jnp.concatenate([w_intra, w_state], axis=1)
    lam_r = jnp.concatenate([jnp.real(lamp[0]), jnp.real(lamp[1])], axis=-1)
    lam_i = jnp.concatenate([jnp.imag(lamp[0]), jnp.imag(lamp[1])], axis=-1)
    lamp = jnp.stack([lam_r, lam_i], axis=2)
    return w1.astype(BF16), w3.astype(BF16), lamp.astype(F32)


def _s5_kernel(u_ref, w1_ref, w3_ref, lam_ref, y_ref, *, nc, width, levels):
    u = u_ref[0, 0]
    loc = jnp.dot(u, w1_ref[0], preferred_element_type=F32)
    re = loc[:, :2 * S5_STATE]
    im = loc[:, 2 * S5_STATE:]
    row = lax.broadcasted_iota(jnp.int32, (nc, 2 * S5_STATE), 0)
    fwd = lax.broadcasted_iota(jnp.int32, (nc, 2 * S5_STATE), 1) < S5_STATE

    def shifted(x, d):
        down = jnp.where(row >= d, pltpu.roll(x, d, 0), 0.0)
        up = jnp.where(row < nc - d, pltpu.roll(x, nc - d, 0), 0.0)
        return jnp.where(fwd, down, up)

    for k in range(levels):
        d = 2 ** k
        if d >= nc:
            break
        lr = lam_ref[0, k, 0:1, :]
        li = lam_ref[0, k, 1:2, :]
        sr = shifted(re, d)
        si = shifted(im, d)
        re, im = re + sr * lr - si * li, im + sr * li + si * lr
    ent = jnp.concatenate([shifted(re, 1), shifted(im, 1)], axis=1)
    y = jnp.dot(u, w3_ref[0, :width, :], preferred_element_type=F32)
    y = y + jnp.dot(ent.astype(BF16), w3_ref[0, width:, :], preferred_element_type=F32)
    y_ref[0, 0] = y.astype(y_ref.dtype)


def _s5_ssm(u5, w1, w3, lamp):
    g, b, nc, width = u5.shape
    levels = lamp.shape[1]
    kern = functools.partial(_s5_kernel, nc=nc, width=width, levels=levels)
    return pl.pallas_call(
        kern,
        grid=(g, b),
        in_specs=[pl.BlockSpec((1, 1, nc, width), lambda gi, bi: (gi, bi, 0, 0)),
                  pl.BlockSpec((1, width, 4 * S5_STATE), lambda gi, bi: (gi, 0, 0)),
                  pl.BlockSpec((1, width + 4 * S5_STATE, width), lambda gi, bi: (gi, 0, 0)),
                  pl.BlockSpec((1, levels, 2, 2 * S5_STATE), lambda gi, bi: (gi, 0, 0, 0))],
        out_specs=pl.BlockSpec((1, 1, nc, width), lambda gi, bi: (gi, bi, 0, 0)),
        out_shape=jax.ShapeDtypeStruct((g, b, nc, width), BF16),
        compiler_params=_cparams("parallel", "parallel"),
        name="s5_ssm",
    )(u5, w1, w3, lamp)


def _rope_kernel(qq_ref, kk_ref, v_ref, cos_ref, sin_ref, qa_ref, qb_ref, ko_ref, vo_ref):
    width = DF_HEADS * 2 * DF_QK
    cos = jnp.concatenate([cos_ref[...]] * (width // LANES), axis=1)
    sin = jnp.concatenate([sin_ref[...]] * (width // LANES), axis=1)
    lane = lax.broadcasted_iota(jnp.int32, (1, width), 1)
    first_half = (lane % DF_QK) < (DF_QK // 2)

    def rot(x):
        partner = jnp.where(first_half, pltpu.roll(x, width - DF_QK // 2, 1),
                            pltpu.roll(x, DF_QK // 2, 1))
        return x * cos + partner * sin

    q = rot(qq_ref[0]) * (DF_QK ** -0.5 * LOG2E)
    is_a = (lane % (2 * DF_QK)) < DF_QK
    qa_ref[0] = jnp.where(is_a, q, 0.0).astype(BF16)
    qb_ref[0] = jnp.where(is_a, 0.0, q).astype(BF16)
    ko_ref[0] = rot(kk_ref[0]).astype(BF16)
    vo_ref[0] = v_ref[0].astype(BF16)


def _rope_prep(proj, cos_t, sin_t):
    b, s, _ = proj.shape
    tile = min(TILE_ROPE, s)
    w = DF_WIDTH
    spec = lambda cb: pl.BlockSpec((1, tile, w), lambda bi, i: (bi, i, cb))
    out = pl.BlockSpec((1, tile, w), lambda bi, i: (bi, i, 0))
    tab = pl.BlockSpec((tile, LANES), lambda bi, i: (i, 0))
    return pl.pallas_call(
        _rope_kernel,
        grid=(b, s // tile),
        in_specs=[spec(COL_QQ // w), spec(COL_KK // w), spec(COL_DV // w), tab, tab],
        out_specs=[out, out, out, out],
        out_shape=[jax.ShapeDtypeStruct((b, s, w), BF16)] * 4,
        compiler_params=_cparams("parallel", "parallel"),
        name="rope_prep",
    )(proj, proj, proj, cos_t, sin_t)


def _diff_attn_kernel(qa_ref, qb_ref, k_ref, v_ref, lq1_ref, lk1_ref, lq2_ref, lk2_ref, ng_ref,
                      o_ref, s_ref, *, tq, tk, nk, lam_init):
    q2 = (qa_ref[0], qb_ref[0])
    nt_dims = (((1,), (1,)), ((), ()))

    def fold(x, op, init):
        for j in range(tk // LANES):
            init = op(init, x[:, j * LANES:(j + 1) * LANES])
        return init

    m = [jnp.full((tq, LANES), -jnp.inf, F32)] * 2
    for c in range(nk):
        kc = k_ref[0, c * tk:(c + 1) * tk, :]
        for i in range(2):
            sc = lax.dot_general(q2[i], kc, nt_dims, preferred_element_type=F32)
            s_ref[c, i * tq:(i + 1) * tq, :] = sc
            m[i] = fold(sc, jnp.maximum, m[i])
    m = [jnp.max(x, axis=-1, keepdims=True) for x in m]

    acc = [jnp.zeros((tq, 2 * DF_V), F32)] * 2
    ones = jnp.ones((tk, DF_V), BF16)
    for c in range(nk):
        vc = jnp.concatenate([v_ref[0, c * tk:(c + 1) * tk, :], ones], axis=1)
        for i in range(2):
            e = jnp.exp2((s_ref[c, i * tq:(i + 1) * tq, :] - m[i]).astype(BF16))
            acc[i] = acc[i] + jnp.dot(e, vc, preferred_element_type=F32)
    lam = (jnp.exp(jnp.sum(lq1_ref[...] * lk1_ref[...], axis=-1, keepdims=True))
           - jnp.exp(jnp.sum(lq2_ref[...] * lk2_ref[...], axis=-1, keepdims=True)) + lam_init)
    acc = (acc[0][:, :DF_V] * (1.0 / acc[0][:, DF_V:DF_V + 1])
           - acc[1][:, :DF_V] * (lam / acc[1][:, DF_V:DF_V + 1]))
    ms = jnp.mean(acc * acc, axis=-1, keepdims=True)
    o_ref[0] = acc * lax.rsqrt(ms + RMS_EPS) * ng_ref[...] * (1.0 - lam_init)


def _diff_attn(qa, qb, kk, vv, lq1, lk1, lq2, lk2, norm_g, lam_init):
    b, s, _ = qa.shape
    tq = min(TILE_Q, s)
    tk = min(TILE_K, s)
    nk = s // tk
    kern = functools.partial(_diff_attn_kernel, tq=tq, tk=tk, nk=nk, lam_init=lam_init)
    qspec = pl.BlockSpec((1, tq, LANES), lambda bi, h, i: (bi, i, h))
    kspec = pl.BlockSpec((1, s, LANES), lambda bi, h, i: (bi, 0, h))
    vec64 = pl.BlockSpec((1, DF_QK), lambda bi, h, i: (0, 0))
    return pl.pallas_call(
        kern,
        grid=(b, DF_HEADS, s // tq),
        in_specs=[qspec, qspec, kspec, kspec, vec64, vec64, vec64, vec64,
                  pl.BlockSpec((1, DF_V), lambda bi, h, i: (0, 0))],
        out_specs=pl.BlockSpec((1, tq, DF_V), lambda bi, h, i: (bi, i, h)),
        out_shape=jax.ShapeDtypeStruct((b, s, DF_WIDTH), F32),
        scratch_shapes=[pltpu.VMEM((nk, 2 * tq, tk), F32)],
        compiler_params=_cparams("parallel", "parallel", "arbitrary"),
        name="diff_attn",
    )(qa, qb, kk, vv, lq1, lk1, lq2, lk2, norm_g)


def _layer_norm(x, g, b):
    mu = jnp.mean(x, axis=-1, keepdims=True)
    xc = x - mu
    var = jnp.mean(xc * xc, axis=-1, keepdims=True)
    return xc * lax.rsqrt(var + LN_EPS) * g + b


def _merge_kernel(x_ref, of_ref, ob_ref, z_ref, y5_ref, oc_ref, g0_ref, g1_ref, g2_ref,
                  bgate_ref, dnn_ref, gluw_ref, glub_ref, wbr_ref, wout_ref, lng_ref, lnb_ref,
                  o_ref):
    o = of_ref[0] + ob_ref[0]
    z = z_ref[0]
    parts = []
    for h in range(DN_HEADS):
        oh = o[:, h * DN_DIM:(h + 1) * DN_DIM]
        ms = jnp.mean(oh * oh, axis=-1, keepdims=True)
        parts.append(oh * lax.rsqrt(ms + RMS_EPS) * dnn_ref[...])
    o_a = jnp.concatenate(parts, axis=1) * _silu(z)
    y = jax.nn.gelu(y5_ref[0].astype(F32))
    o_b = y * _sigmoid(_bdot(y, gluw_ref[...]) + glub_ref[...])
    o_c = oc_ref[0]
    bg = bgate_ref[...]
    merged = (_sigmoid(g0_ref[0] + bg[:, 0:D_MODEL]) * _bdot(o_a, wbr_ref[0])
              + _sigmoid(g1_ref[0] + bg[:, D_MODEL:2 * D_MODEL]) * _bdot(o_b, wbr_ref[1])
              + _sigmoid(g2_ref[0] + bg[:, 2 * D_MODEL:]) * _bdot(o_c, wbr_ref[2]))
    hres = DEEPNORM_ALPHA * x_ref[0] + _bdot(merged, wout_ref[...])
    o_ref[0] = _layer_norm(hres, lng_ref[...], lnb_ref[...])


def _merge(x, o_f, o_b, proj, y5, o_c, b_gate, dn_norm, glu_w, glu_b, w_branch, w_out, ln_g, ln_b):
    b, s, _ = x.shape
    tile = min(TILE_MERGE, s)
    row = lambda w, cb=0: pl.BlockSpec((1, tile, w), lambda bi, i: (bi, i, cb))
    full = lambda shape: pl.BlockSpec(shape, lambda bi, i: (0,) * len(shape))
    gcb = COL_GATE // D_MODEL
    return pl.pallas_call(
        _merge_kernel,
        grid=(b, s // tile),
        in_specs=[row(D_MODEL), row(DN_WIDTH), row(DN_WIDTH), row(DN_WIDTH, COL_DN_Z // DN_WIDTH),
                  row(S5_WIDTH), row(DF_WIDTH),
                  row(D_MODEL, gcb), row(D_MODEL, gcb + 1), row(D_MODEL, gcb + 2),
                  full((1, N_BRANCH * D_MODEL)), full((1, DN_DIM)),
                  full((S5_WIDTH, S5_WIDTH)), full((1, S5_WIDTH)),
                  full((N_BRANCH, DN_WIDTH, D_MODEL)), full((D_MODEL, D_MODEL)),
                  full((1, D_MODEL)), full((1, D_MODEL))],
        out_specs=row(D_MODEL),
        out_shape=jax.ShapeDtypeStruct((b, s, D_MODEL), F32),
        compiler_params=_cparams("parallel", "parallel"),
        name="merge",
    )(x, o_f, o_b, proj, y5, o_c, proj, proj, proj, b_gate, dn_norm, glu_w, glu_b,
      w_branch, w_out, ln_g, ln_b)


def _xattn_kernel(x_ref, kv_ref, wq_ref, wo_ref, lng_ref, lnb_ref, o_ref):
    x = x_ref[0]
    q = _bdot(x, wq_ref[...])
    outs = []
    for h in range(XA_HEADS):
        qh = q[:, h * XA_DIM:(h + 1) * XA_DIM]
        kh = kv_ref[0, :, h * XA_DIM:(h + 1) * XA_DIM]
        vh = kv_ref[0, :, XA_WIDTH + h * XA_DIM:XA_WIDTH + (h + 1) * XA_DIM]
        sc = _bdot_nt(qh, kh) * (XA_DIM ** -0.5)
        sc = sc - jnp.max(sc, axis=-1, keepdims=True)
        e = jnp.exp(sc)
        pr = e / jnp.sum(e, axis=-1, keepdims=True)
        outs.append(_bdot(pr, vh))
    o = jnp.concatenate(outs, axis=1)
    hres = DEEPNORM_ALPHA * x + _bdot(o, wo_ref[...])
    o_ref[0] = _layer_norm(hres, lng_ref[...], lnb_ref[...])


def _xattn(x, kv, wq, wo, ln_g, ln_b):
    b, s, _ = x.shape
    tile = min(TILE_XA, s)
    mlen = kv.shape[1]
    row = pl.BlockSpec((1, tile, D_MODEL), lambda bi, i: (bi, i, 0))
    full = lambda shape: pl.BlockSpec(shape, lambda bi, i: (0,) * len(shape))
    return pl.pallas_call(
        _xattn_kernel,
        grid=(b, s // tile),
        in_specs=[row, pl.BlockSpec((1, mlen, 2 * XA_WIDTH), lambda bi, i: (bi, 0, 0)),
                  full((D_MODEL, XA_WIDTH)), full((XA_WIDTH, D_MODEL)),
                  full((1, D_MODEL)), full((1, D_MODEL))],
        out_specs=row,
        out_shape=jax.ShapeDtypeStruct((b, s, D_MODEL), F32),
        compiler_params=_cparams("parallel", "parallel"),
        name="xattn",
    )(x, kv, wq, wo, ln_g, ln_b)


def _router_kernel(x_ref, rw_ref, rb_ref, idx_ref, gate_ref, xb_ref):
    x = x_ref[...]
    xb_ref[...] = x.astype(BF16)
    logits = lax.dot_general(rw_ref[...], x, (((1,), (1,)), ((), ())), precision=HI,
                             preferred_element_type=F32) + rb_ref[...]
    tn = logits.shape[1]
    eid = lax.broadcasted_iota(jnp.int32, (N_EXPERTS, tn), 0)
    vals, ids = [], []
    cur = logits
    for _ in range(TOP_K):
        m = jnp.max(cur, axis=0, keepdims=True)
        sel = jnp.min(jnp.where(cur == m, eid, N_EXPERTS), axis=0, keepdims=True)
        vals.append(m)
        ids.append(sel)
        cur = jnp.where(eid == sel, -jnp.inf, cur)
    top_v = jnp.concatenate(vals, axis=0)
    e = jnp.exp(top_v - top_v[0:1])
    gate_ref[...] = e / jnp.sum(e, axis=0, keepdims=True)
    idx_ref[...] = jnp.concatenate(ids, axis=0)


def _router(x2d, router_w_t, router_b):
    n, _ = x2d.shape
    tile = min(TILE_ROUTER, n)
    return pl.pallas_call(
        _router_kernel,
        grid=(n // tile,),
        in_specs=[pl.BlockSpec((tile, D_MODEL), lambda i: (i, 0)),
                  pl.BlockSpec((N_EXPERTS, D_MODEL), lambda i: (0, 0)),
                  pl.BlockSpec((N_EXPERTS, 1), lambda i: (0, 0))],
        out_specs=[pl.BlockSpec((TOP_K, tile), lambda i: (0, i)),
                   pl.BlockSpec((TOP_K, tile), lambda i: (0, i)),
                   pl.BlockSpec((tile, D_MODEL), lambda i: (i, 0))],
        out_shape=[jax.ShapeDtypeStruct((TOP_K, n), jnp.int32),
                   jax.ShapeDtypeStruct((TOP_K, n), F32),
                   jax.ShapeDtypeStruct((n, D_MODEL), BF16)],
        compiler_params=_cparams("parallel"),
        name="router",
    )(x2d, router_w_t, router_b)


def _rank_kernel(idx_ref, rank_ref, cnt_ref, run_ref, *, tile):
    @pl.when(pl.program_id(0) == 0)
    def _():
        run_ref[...] = jnp.zeros_like(run_ref)

    idx = idx_ref[...]
    eid = lax.broadcasted_iota(jnp.int32, (N_EXPERTS, tile), 0)
    si = lax.broadcasted_iota(jnp.int32, (tile, tile), 0)
    ti = lax.broadcasted_iota(jnp.int32, (tile, tile), 1)
    before = (si < ti).astype(BF16)
    run = run_ref[...]
    ranks = []
    for k in range(TOP_K):
        onehot = eid == idx[k:k + 1]
        oh = onehot.astype(BF16)
        within = jnp.dot(oh, before, preferred_element_type=F32)
        ranks.append(jnp.sum(jnp.where(onehot, within + run, 0.0), axis=0, keepdims=True))
        run = run + jnp.sum(onehot.astype(F32), axis=1, keepdims=True)
    rank_ref[...] = jnp.concatenate(ranks, axis=0).astype(jnp.int32)
    run_ref[...] = run
    cnt_ref[...] = run.astype(jnp.int32)


def _rank(top_i):
    _, n = top_i.shape
    tile = min(TILE_ROUTER, n)
    kern = functools.partial(_rank_kernel, tile=tile)
    return pl.pallas_call(
        kern,
        grid=(n // tile,),
        in_specs=[pl.BlockSpec((TOP_K, tile), lambda i: (0, i))],
        out_specs=[pl.BlockSpec((TOP_K, tile), lambda i: (0, i)),
                   pl.BlockSpec((N_EXPERTS, 1), lambda i: (0, 0))],
        out_shape=[jax.ShapeDtypeStruct((TOP_K, n), jnp.int32),
                   jax.ShapeDtypeStruct((N_EXPERTS, 1), jnp.int32)],
        scratch_shapes=[pltpu.VMEM((N_EXPERTS, 1), F32)],
        compiler_params=_cparams("arbitrary"),
        name="moe_rank",
    )(top_i)


def _expert_kernel(be_ref, on_ref, x_ref, wg_ref, wu_ref, bg_ref, bu_ref, wd_ref, bd_ref, o_ref):
    del be_ref
    live = on_ref[pl.program_id(0)] != 0

    @pl.when(live)
    def _():
        x = x_ref[...]
        hg = jnp.dot(x, wg_ref[0], preferred_element_type=F32) + bg_ref[0]
        hu = jnp.dot(x, wu_ref[0], preferred_element_type=F32) + bu_ref[0]
        gt = jnp.minimum(hg, SWIGLU_LIMIT)
        up = jnp.clip(hu, -SWIGLU_LIMIT, SWIGLU_LIMIT)
        act = (up + 1.0) * gt * _sigmoid(SWIGLU_ALPHA * gt)
        y = jnp.dot(act.astype(BF16), wd_ref[0], preferred_element_type=F32) + bd_ref[0]
        o_ref[...] = y.astype(o_ref.dtype)

    @pl.when(jnp.logical_not(live))
    def _():
        o_ref[...] = jnp.zeros_like(o_ref)


def _experts(x_rows, blk_e, blk_on, w_g, w_u, b_g, b_u, w_d, b_d, *, block):
    rows, _ = x_rows.shape
    wspec = lambda k, m: pl.BlockSpec((1, k, m), lambda i, be, on: (be[i], 0, 0))
    grid_spec = pltpu.PrefetchScalarGridSpec(
        num_scalar_prefetch=2,
        grid=(rows // block,),
        in_specs=[pl.BlockSpec((block, D_MODEL), lambda i, be, on: (i, 0)),
                  wspec(D_MODEL, D_FF), wspec(D_MODEL, D_FF), wspec(1, D_FF), wspec(1, D_FF),
                  wspec(D_FF, D_MODEL), wspec(1, D_MODEL)],
        out_specs=pl.BlockSpec((block, D_MODEL), lambda i, be, on: (i, 0)),
    )
    return pl.pallas_call(
        _expert_kernel,
        grid_spec=grid_spec,
        out_shape=jax.ShapeDtypeStruct((rows, D_MODEL), BF16),
        compiler_params=_cparams("arbitrary"),
        name="moe_experts",
    )(blk_e, blk_on, x_rows, w_g, w_u, b_g, b_u, w_d, b_d)


def _combine_kernel(x_ref, y_ref, gate_ref, lng_ref, lnb_ref, o_ref):
    gate = gate_ref[...]
    h = y_ref[0].astype(F32) * gate[:, 0:1]
    for k in range(1, TOP_K):
        h = h + y_ref[k].astype(F32) * gate[:, k:k + 1]
    o_ref[...] = _layer_norm(DEEPNORM_ALPHA * x_ref[...] + h, lng_ref[...], lnb_ref[...])


def _combine(x2d, y_g, gate_t, ln_g, ln_b):
    n, _ = x2d.shape
    tile = min(TILE_COMBINE, n)
    return pl.pallas_call(
        _combine_kernel,
        grid=(n // tile,),
        in_specs=[pl.BlockSpec((tile, D_MODEL), lambda i: (i, 0)),
                  pl.BlockSpec((TOP_K, tile, D_MODEL), lambda i: (0, i, 0)),
                  pl.BlockSpec((tile, TOP_K), lambda i: (i, 0)),
                  pl.BlockSpec((1, D_MODEL), lambda i: (0, 0)),
                  pl.BlockSpec((1, D_MODEL), lambda i: (0, 0))],
        out_specs=pl.BlockSpec((tile, D_MODEL), lambda i: (i, 0)),
        out_shape=jax.ShapeDtypeStruct((n, D_MODEL), F32),
        compiler_params=_cparams("parallel"),
        name="moe_combine",
    )(x2d, y_g, gate_t, ln_g, ln_b)


def _split_kernel(w_ref, se_ref, so_ref, g_ref, u_ref):
    w = w_ref[0].astype(BF16)
    g_ref[0] = jnp.dot(w, se_ref[...], preferred_element_type=F32).astype(BF16)
    u_ref[0] = jnp.dot(w, so_ref[...], preferred_element_type=F32).astype(BF16)


def _split_gate_up(w_gu):
    e, dm, f2 = w_gu.shape
    f = f2 // 2
    tr = 256
    col = jnp.arange(f2, dtype=jnp.int32)[:, None]
    out = jnp.arange(f, dtype=jnp.int32)[None, :]
    sel_even = (col == 2 * out).astype(BF16)
    sel_odd = (col == 2 * out + 1).astype(BF16)
    sel = pl.BlockSpec((f2, f), lambda ei, i: (0, 0))
    ospec = pl.BlockSpec((1, tr, f), lambda ei, i: (ei, i, 0))
    return pl.pallas_call(
        _split_kernel,
        grid=(e, dm // tr),
        in_specs=[pl.BlockSpec((1, tr, f2), lambda ei, i: (ei, i, 0)), sel, sel],
        out_specs=[ospec, ospec],
        out_shape=[jax.ShapeDtypeStruct((e, dm, f), BF16)] * 2,
        compiler_params=_cparams("parallel", "parallel"),
        name="split_gate_up",
    )(w_gu, sel_even, sel_odd)


def _moe(x, w):
    b, s, dm = x.shape
    n = b * s
    x2d = x.reshape(n, dm)
    top_i, gate, xb = _router(x2d, w['router_w_t'], w['router_b'])
    rank, counts = _rank(top_i)
    block = MOE_BLOCK
    m = n * TOP_K
    nblk = -(-m // block) + N_EXPERTS
    rows = nblk * block
    counts = counts.reshape(N_EXPERTS)
    padded = (counts + block - 1) // block * block
    pad_end = jnp.cumsum(padded)
    pad_start = pad_end - padded
    dest = pad_start[top_i] + rank
    tok = jnp.broadcast_to(jnp.arange(n, dtype=jnp.int32)[None, :], (TOP_K, n))
    row_tok = jnp.full((rows,), n, jnp.int32).at[dest.reshape(-1)].set(
        tok.reshape(-1), unique_indices=True, mode='promise_in_bounds')
    blk_start = jnp.arange(nblk, dtype=pad_end.dtype) * block
    blk_e = jnp.minimum(jnp.sum(pad_end[None, :] <= blk_start[:, None], axis=1),
                        N_EXPERTS - 1).astype(jnp.int32)
    blk_on = (blk_start < pad_end[-1]).astype(jnp.int32)
    x_pad = jnp.concatenate([xb, jnp.zeros((1, dm), BF16)], axis=0)
    x_rows = x_pad.at[row_tok].get(mode='promise_in_bounds')
    y_rows = _experts(x_rows, blk_e, blk_on, w['e_w_g'], w['e_w_u'], w['e_b_g'], w['e_b_u'], w['e_w_d'],
                      w['e_b_d'], block=block)
    y_g = y_rows.at[dest].get(mode='promise_in_bounds')
    out = _combine(x2d, y_g, gate.T, w['ln3_g'], w['ln3_b'])
    return out.reshape(b, s, dm)


def _rope_tables(s):
    half = DF_QK // 2
    inv_freq = jnp.power(ROPE_THETA, -jnp.arange(half, dtype=F32) / half)
    ang = jnp.arange(s, dtype=F32)[:, None] * inv_freq[None, :]
    cos = jnp.cos(ang)
    sin = jnp.sin(ang)
    cos_t = jnp.concatenate([cos, cos, cos, cos], axis=1)
    sin_t = jnp.concatenate([-sin, sin, -sin, sin], axis=1)
    return cos_t, sin_t


def _prepare_layer(l, p, s):
    dm = D_MODEL
    w = {}
    w['w_in'] = _pack_in_proj(p['w_in'][l]).astype(BF16)
    w['w_s5'] = w['w_in'][:, COL_S5_U:COL_S5_U + S5_WIDTH]
    pad8 = jnp.zeros((8,), F32)
    pad112 = jnp.zeros((LANES - 16,), F32)
    w['nega'] = jnp.concatenate([pad8, -jnp.exp(p['dn_A_log'][l].reshape(8)), pad112]).reshape(1, LANES)
    w['dtb'] = jnp.concatenate([pad8, p['dn_dt_bias'][l].reshape(8), pad112]).reshape(1, LANES)
    w['dn_conv'] = p['dn_conv'][l]
    t5 = min(S5_CHUNK, s)
    w['s5'] = _s5_operators(p['s5_lam_re'][l], p['s5_lam_im'][l], p['s5_log_dt'][l],
                            p['s5_B_re'][l], p['s5_B_im'][l], p['s5_C_re'][l],
                            p['s5_C_im'][l], p['s5_D'][l], chunk=t5, nc=s // t5)
    for name in ('df_lq1', 'df_lk1', 'df_lq2', 'df_lk2'):
        w[name] = p[name][l].reshape(1, DF_QK)
    w['df_norm'] = p['df_norm'][l].reshape(1, DF_V)
    w['lam_init'] = 0.8 - 0.6 * math.exp(-0.3 * l)
    w['b_gate'] = p['b_gate'][l].reshape(1, -1)
    w['dn_norm'] = p['dn_norm'][l].reshape(1, DN_DIM)
    w['s5_glu_w'] = p['s5_glu_w'][l].astype(BF16)
    w['s5_glu_b'] = p['s5_glu_b'][l].reshape(1, -1)
    w['w_branch'] = p['w_branch'][l].astype(BF16)
    w['w_out'] = p['w_out'][l].astype(BF16)
    for name in ('ln1_g', 'ln1_b', 'ln2_g', 'ln2_b', 'ln3_g', 'ln3_b'):
        w[name] = p[name][l].reshape(1, dm)
    w['xa_wkv'] = p['xa_wkv'][l].astype(BF16)
    w['xa_wq'] = p['xa_wq'][l].astype(BF16)
    w['xa_wo'] = p['xa_wo'][l].astype(BF16)
    w['router_w_t'] = p['router_w'][l].T
    w['router_b'] = p['router_b'][l].reshape(N_EXPERTS, 1)
    w['e_w_g'], w['e_w_u'] = _split_gate_up(p['e_w_gu'][l])
    w['e_b_g'] = p['e_b_gu'][l][:, None, 0::2]
    w['e_b_u'] = p['e_b_gu'][l][:, None, 1::2]
    w['e_w_d'] = p['e_w_down'][l].astype(BF16)
    w['e_b_d'] = p['e_b_down'][l][:, None, :]
    return w


def _layer(x, mem, w, tabs):
    b, s, dm = x.shape
    n = b * s
    proj = _matmul(x.reshape(n, dm), w['w_in'], tm=TILE_MM, tn=LANES * 19)
    proj = proj.reshape(b, s, PROJ_WIDTH)

    qkv, bg = _dn_prep(proj, w['dn_conv'], w['nega'], w['dtb'], chunk=DN_CHUNK)
    bgr = bg[:, :, :16].reshape(b, s // DN_CHUNK, DN_CHUNK, 16).transpose(0, 1, 3, 2)
    o_f, o_b = _dn_scan(qkv, bg, bgr, chunk=DN_CHUNK)

    t5 = min(S5_CHUNK, s)
    nc = s // t5
    u5 = _matmul(x.reshape(n, dm), w['w_s5'], tm=TILE_MM, tn=S5_WIDTH, out_dtype=BF16)
    u5 = u5.reshape(b, nc, t5, S5_GROUPS, S5_GROUP)
    u5 = u5.transpose(3, 0, 1, 2, 4).reshape(S5_GROUPS, b, nc, t5 * S5_GROUP)
    y5 = _s5_ssm(u5, *w['s5'])
    y5 = y5.reshape(S5_GROUPS, b, nc, t5, S5_GROUP).transpose(1, 2, 3, 0, 4).reshape(b, s, S5_WIDTH)

    qa, qb, kk, vv = _rope_prep(proj, *tabs)
    o_c = _diff_attn(qa, qb, kk, vv, w['df_lq1'], w['df_lk1'], w['df_lq2'], w['df_lk2'],
                     w['df_norm'], w['lam_init'])

    x = _merge(x, o_f, o_b, proj, y5, o_c, w['b_gate'], w['dn_norm'], w['s5_glu_w'], w['s5_glu_b'],
               w['w_branch'], w['w_out'], w['ln1_g'], w['ln1_b'])

    mlen = mem.shape[1]
    kv = _matmul(mem.reshape(b * mlen, dm), w['xa_wkv'], tm=512, tn=1024)
    x = _xattn(x, kv.reshape(b, mlen, 2 * XA_WIDTH), w['xa_wq'], w['xa_wo'], w['ln2_g'], w['ln2_b'])
    return _moe(x, w)


def kernel(x_prompt, x_sample, mem_prompt, mem_sample, w_in, b_gate, dn_conv, dn_A_log, dn_dt_bias,
           dn_norm, s5_lam_re, s5_lam_im, s5_log_dt, s5_B_re, s5_B_im, s5_C_re, s5_C_im, s5_D,
           s5_glu_w, s5_glu_b, df_lq1, df_lk1, df_lq2, df_lk2, df_norm, w_branch, w_out,
           ln1_g, ln1_b, xa_wq, xa_wkv, xa_wo, ln2_g, ln2_b, router_w, router_b,
           e_w_gu, e_b_gu, e_w_down, e_b_down, ln3_g, ln3_b):
    p = dict(w_in=w_in, b_gate=b_gate, dn_conv=dn_conv, dn_A_log=dn_A_log, dn_dt_bias=dn_dt_bias,
             dn_norm=dn_norm, s5_lam_re=s5_lam_re, s5_lam_im=s5_lam_im, s5_log_dt=s5_log_dt,
             s5_B_re=s5_B_re, s5_B_im=s5_B_im, s5_C_re=s5_C_re, s5_C_im=s5_C_im, s5_D=s5_D,
             s5_glu_w=s5_glu_w, s5_glu_b=s5_glu_b, df_lq1=df_lq1, df_lk1=df_lk1, df_lq2=df_lq2,
             df_lk2=df_lk2, df_norm=df_norm, w_branch=w_branch, w_out=w_out, ln1_g=ln1_g,
             ln1_b=ln1_b, xa_wq=xa_wq, xa_wkv=xa_wkv, xa_wo=xa_wo, ln2_g=ln2_g, ln2_b=ln2_b,
             router_w=router_w, router_b=router_b, e_w_gu=e_w_gu, e_b_gu=e_b_gu,
             e_w_down=e_w_down, e_b_down=e_b_down, ln3_g=ln3_g, ln3_b=ln3_b)
    s = x_prompt.shape[1]
    assert x_sample.shape[1] == s
    tabs = _rope_tables(s)
    layers = [_prepare_layer(l, p, s) for l in range(DEPTH)]
    outs = []
    for x, mem in ((x_prompt, mem_prompt), (x_sample, mem_sample)):
        for w in layers:
            x = _layer(x, mem, w, tabs)
        outs.append(x)
    return tuple(outs)
```

```python
import functools
import math

import numpy as np
import jax
import jax.numpy as jnp
from jax import lax
from jax.experimental import pallas as pl
from jax.experimental.pallas import tpu as pltpu

F32 = jnp.float32
BF16 = jnp.bfloat16
HI = lax.Precision.HIGHEST

D_MODEL = 1024
DN_HEADS = 4
DN_DIM = 128
DN_WIDTH = DN_HEADS * DN_DIM
DN_CONV = 4
S5_GROUP = 16
S5_WIDTH = 512
S5_GROUPS = S5_WIDTH // S5_GROUP
S5_STATE = 64
DF_HEADS = 4
DF_QK = 64
DF_V = 128
DF_WIDTH = DF_HEADS * DF_V
ROPE_THETA = 10000.0
XA_HEADS = 4
XA_DIM = 128
XA_WIDTH = XA_HEADS * XA_DIM
N_EXPERTS = 32
TOP_K = 4
D_FF = 1024
SWIGLU_LIMIT = 7.0
SWIGLU_ALPHA = 1.702
N_BRANCH = 3
DEPTH = 2
DEEPNORM_ALPHA = (2 * DEPTH) ** 0.25
LN_EPS = 1e-5
RMS_EPS = 1e-6
LOG2E = 1.4426950408889634

LANES = 128
SUBLANES = 8
VMEM_LIMIT = 56 * 1024 * 1024

COL_DN_QKV = 0
COL_DN_Z = 1536
COL_S5_U = 2048
COL_QQ = 2560
COL_KK = 3072
COL_DV = 3584
COL_GATE = 4096
COL_BA = 7168
PROJ_WIDTH = 7296

TILE_MM = 1024
TILE_DN_PREP = 512
TILE_DN_SCAN = 256
DN_CHUNK = 64
DN_CHUNKS_PER_TRIP = 2
S5_CHUNK = 32
TILE_ROPE = 512
TILE_Q = 256
TILE_K = 512
TILE_MERGE = 256
TILE_XA = 512
TILE_ROUTER = 512
MOE_BLOCK = 512
TILE_COMBINE = 256


def _in_proj_perm():
    o_dn_b, o_dn_a, o_s5 = 2048, 2056, 2064
    o_q1, o_q2, o_k1, o_k2, o_dv, o_gate = 2576, 2832, 3088, 3344, 3600, 4112
    perm = np.full((PROJ_WIDTH,), -1, np.int64)
    perm[0:2048] = np.arange(2048)
    perm[COL_S5_U:COL_S5_U + 512] = o_s5 + np.arange(512)
    for h in range(DF_HEADS):
        j = np.arange(DF_QK)
        perm[COL_QQ + h * 128 + j] = o_q1 + h * DF_QK + j
        perm[COL_QQ + h * 128 + 64 + j] = o_q2 + h * DF_QK + j
        perm[COL_KK + h * 128 + j] = o_k1 + h * DF_QK + j
        perm[COL_KK + h * 128 + 64 + j] = o_k2 + h * DF_QK + j
    perm[COL_DV:COL_DV + 512] = o_dv + np.arange(512)
    perm[COL_GATE:COL_GATE + 3072] = o_gate + np.arange(3072)
    perm[COL_BA:COL_BA + 8] = o_dn_b + np.arange(8)
    perm[COL_BA + 8:COL_BA + 16] = o_dn_a + np.arange(8)
    return perm


_IN_PERM = _in_proj_perm()


def _pack_in_proj(w_in):
    runs, start = [], 0
    for i in range(1, PROJ_WIDTH + 1):
        if i == PROJ_WIDTH or _IN_PERM[i] != _IN_PERM[i - 1] + (1 if _IN_PERM[i - 1] >= 0 else 0):
            runs.append((start, i))
            start = i
    pieces = []
    for a, b in runs:
        src = int(_IN_PERM[a])
        pieces.append(jnp.zeros((w_in.shape[0], b - a), w_in.dtype) if src < 0
                      else w_in[:, src:src + (b - a)])
    return jnp.concatenate(pieces, axis=1)


def _cparams(*sem):
    return pltpu.CompilerParams(dimension_semantics=sem, vmem_limit_bytes=VMEM_LIMIT)


def _bdot(a, b):
    return jnp.dot(a.astype(BF16), b.astype(BF16), preferred_element_type=F32)


def _bdot_nt(a, b):
    return lax.dot_general(a.astype(BF16), b.astype(BF16), (((1,), (1,)), ((), ())),
                           preferred_element_type=F32)


def _bdot_tn(a, b):
    return lax.dot_general(a.astype(BF16), b.astype(BF16), (((0,), (0,)), ((), ())),
                           preferred_element_type=F32)


def _sigmoid(x):
    return 1.0 / (1.0 + jnp.exp(-x))


def _silu(x):
    return x * _sigmoid(x)


def _mm_kernel(x_ref, w_ref, o_ref):
    o_ref[...] = jnp.dot(x_ref[...].astype(BF16), w_ref[...],
                         preferred_element_type=F32).astype(o_ref.dtype)


def _matmul(x, w, *, tm, tn, out_dtype=F32):
    n, k = x.shape
    m = w.shape[1]
    tm = min(tm, n)
    tn = min(tn, m)
    return pl.pallas_call(
        _mm_kernel,
        grid=(n // tm, m // tn),
        in_specs=[pl.BlockSpec((tm, k), lambda i, j: (i, 0)),
                  pl.BlockSpec((k, tn), lambda i, j: (0, j))],
        out_specs=pl.BlockSpec((tm, tn), lambda i, j: (i, j)),
        out_shape=jax.ShapeDtypeStruct((n, m), out_dtype),
        compiler_params=_cparams("parallel", "arbitrary"),
        name="matmul",
    )(x, w)


def _dn_prep_kernel(x_ref, prev_ref, next_ref, ba_ref, cw_ref, nega_ref, dtb_ref,
                    qkv_ref, bg_ref, *, tile, chunk):
    i = pl.program_id(1)
    last = pl.num_programs(1) - 1
    x = x_ref[0]
    prev = jnp.where(i > 0, prev_ref[0], 0.0)
    nxt = jnp.where(i < last, next_ref[0], 0.0)
    ext = jnp.concatenate([prev, x, nxt], axis=0)
    cw = cw_ref[...]
    acc = cw[0:1] * ext[6:6 + tile]
    for j in range(1, DN_CONV):
        acc = acc + cw[j:j + 1] * ext[6 + j:6 + j + tile]
    act = _silu(acc)
    for h in range(DN_HEADS):
        lo = h * DN_DIM
        qh = act[:, lo:lo + DN_DIM]
        qn = qh * lax.rsqrt(jnp.sum(qh * qh, axis=-1, keepdims=True) + 1e-6)
        qkv_ref[0, :, lo:lo + DN_DIM] = qn * (DN_DIM ** -0.5)
        kh = act[:, DN_WIDTH + lo:DN_WIDTH + lo + DN_DIM]
        qkv_ref[0, :, DN_WIDTH + lo:DN_WIDTH + lo + DN_DIM] = kh * lax.rsqrt(
            jnp.sum(kh * kh, axis=-1, keepdims=True) + 1e-6)
    qkv_ref[0, :, 2 * DN_WIDTH:] = act[:, 2 * DN_WIDTH:]

    ba = ba_ref[0]
    beta = _sigmoid(ba)
    z = ba + dtb_ref[...]
    softplus = jnp.maximum(z, 0.0) + jnp.log(1.0 + jnp.exp(-jnp.abs(z)))
    g = nega_ref[...] * softplus
    row = lax.broadcasted_iota(jnp.int32, (chunk, chunk), 0)
    col = lax.broadcasted_iota(jnp.int32, (chunk, chunk), 1)
    lower = (row >= col).astype(F32)
    upper = (row <= col).astype(F32)
    lane = lax.broadcasted_iota(jnp.int32, (chunk, LANES), 1)
    fwd_lane = lane < 8 + DN_HEADS
    for c in range(tile // chunk):
        gc = g[c * chunk:(c + 1) * chunk]
        cf = jnp.dot(lower, gc, precision=HI, preferred_element_type=F32)
        cb = jnp.dot(upper, gc, precision=HI, preferred_element_type=F32)
        gcum = jnp.where(fwd_lane, cf, cb)
        bg_ref[0, c * chunk:(c + 1) * chunk, :] = jnp.where(
            lane < 8, beta[c * chunk:(c + 1) * chunk], gcum)


def _dn_prep(proj, conv_w, nega, dtb, *, chunk):
    b, s, _ = proj.shape
    tile = min(TILE_DN_PREP, s)
    nt = s // tile
    r8 = tile // SUBLANES
    kern = functools.partial(_dn_prep_kernel, tile=tile, chunk=chunk)
    return pl.pallas_call(
        kern,
        grid=(b, nt),
        in_specs=[
            pl.BlockSpec((1, tile, 3 * DN_WIDTH), lambda bi, i: (bi, i, 0)),
            pl.BlockSpec((1, SUBLANES, 3 * DN_WIDTH),
                         lambda bi, i: (bi, jnp.maximum(i * r8 - 1, 0), 0)),
            pl.BlockSpec((1, SUBLANES, 3 * DN_WIDTH),
                         lambda bi, i: (bi, jnp.minimum((i + 1) * r8, s // SUBLANES - 1), 0)),
            pl.BlockSpec((1, tile, LANES), lambda bi, i: (bi, i, COL_BA // LANES)),
            pl.BlockSpec((DN_CONV, 3 * DN_WIDTH), lambda bi, i: (0, 0)),
            pl.BlockSpec((1, LANES), lambda bi, i: (0, 0)),
            pl.BlockSpec((1, LANES), lambda bi, i: (0, 0)),
        ],
        out_specs=[pl.BlockSpec((1, tile, 3 * DN_WIDTH), lambda bi, i: (bi, i, 0)),
                   pl.BlockSpec((1, tile, LANES), lambda bi, i: (bi, i, 0))],
        out_shape=[jax.ShapeDtypeStruct((b, s, 3 * DN_WIDTH), F32),
                   jax.ShapeDtypeStruct((b, s, LANES), F32)],
        compiler_params=_cparams("parallel", "parallel"),
        name="dn_prep",
    )(proj, proj, proj, proj, conv_w, nega, dtb)


def _dn_local(chains, chunk):
    ri = lax.broadcasted_iota(jnp.int32, (chunk, chunk), 0)
    ci = lax.broadcasted_iota(jnp.int32, (chunk, chunk), 1)
    pre = []
    for d, h, q, k, v, bgc, rows in chains:
        li = d * DN_HEADS + h
        beta = bgc[:, li:li + 1]
        gcol = bgc[:, 8 + li:8 + li + 1]
        grow = rows[8 + li:8 + li + 1, :]
        gtot = gcol[chunk - 1:chunk] if d == 0 else gcol[0:1]
        incl = (ri >= ci) if d == 0 else (ri <= ci)
        strict = (ri > ci) if d == 0 else (ri < ci)
        eg = jnp.exp(gcol)
        kb = k * beta
        decay = jnp.where(incl, jnp.exp(jnp.where(incl, gcol - grow, 0.0)), 0.0)
        pre.append(dict(k=k, qg=q * eg, kd=k * jnp.exp(gtot - gcol), decay=decay, strict=strict,
                        x2=jnp.concatenate([v * beta, kb * eg], axis=1), sdec=jnp.exp(gtot),
                        qkb=jnp.concatenate([q, kb], axis=0)))
    qk = [_bdot_nt(c['qkb'], c['k']) for c in pre]
    intra = [x[:chunk] * c['decay'] for x, c in zip(qk, pre)]
    p = [-jnp.where(c['strict'], x[chunk:] * c['decay'], 0.0) for x, c in zip(qk, pre)]
    r = p
    p = [_bdot(x, x) for x in p]
    for _ in range(int(math.log2(chunk)) - 2):
        rp = [_bdot(jnp.concatenate([ri_, pi_], axis=0), pi_) for ri_, pi_ in zip(r, p)]
        r = [ri_ + pi_ + x[:chunk] for ri_, pi_, x in zip(r, p, rp)]
        p = [x[chunk:] for x in rp]
    r = [ri_ + pi_ + _bdot(ri_, pi_) for ri_, pi_ in zip(r, p)]
    uw = [c['x2'] + _bdot(ri_, c['x2']) for ri_, c in zip(r, pre)]
    return [dict(u=x[:, :DN_DIM], wq=jnp.concatenate([x[:, DN_DIM:], c['qg']], axis=0),
                 intra=it, kd=c['kd'], sdec=c['sdec']) for x, it, c in zip(uw, intra, pre)]


def _dn_state(local, states, chunk):
    ws = [_bdot(c['wq'], st) for c, st in zip(local, states)]
    v_new = [c['u'] - y[:chunk] for c, y in zip(local, ws)]
    out = [y[chunk:] + _bdot(c['intra'], vn) for y, c, vn in zip(ws, local, v_new)]
    st_new = [st * c['sdec'] + _bdot_tn(c['kd'], vn) for st, c, vn in zip(states, local, v_new)]
    return out, st_new


def _dn_scan_kernel(xf_ref, xb_ref, gf_ref, gb_ref, rf_ref, rb_ref, of_ref, ob_ref, st_ref,
                    *, tile, chunk):
    @pl.when(pl.program_id(1) == 0)
    def _():
        st_ref[...] = jnp.zeros_like(st_ref)

    nct = tile // chunk
    nchain = 2 * DN_HEADS

    def body(j, carry):
        chains, where = [], []
        for slot in range(DN_CHUNKS_PER_TRIP):
            for d in range(2):
                c = DN_CHUNKS_PER_TRIP * j + slot
                c = c if d == 0 else nct - 1 - c
                r0 = pl.multiple_of(c * chunk, chunk)
                x_ref, g_ref, r_ref = (xf_ref, gf_ref, rf_ref) if d == 0 else (xb_ref, gb_ref, rb_ref)
                bgc = g_ref[0, pl.ds(r0, chunk), :]
                rows = r_ref[0, c]
                for h in range(DN_HEADS):
                    lo = h * DN_DIM
                    q = x_ref[0, pl.ds(r0, chunk), lo:lo + DN_DIM]
                    k = x_ref[0, pl.ds(r0, chunk), DN_WIDTH + lo:DN_WIDTH + lo + DN_DIM]
                    v = x_ref[0, pl.ds(r0, chunk), 2 * DN_WIDTH + lo:2 * DN_WIDTH + lo + DN_DIM]
                    chains.append((d, h, q, k, v, bgc, rows))
                    where.append((d, h, r0))
        states = [st_ref[i] for i in range(nchain)]
        local = _dn_local(chains, chunk)
        outs = []
        for slot in range(DN_CHUNKS_PER_TRIP):
            out, states = _dn_state(local[slot * nchain:(slot + 1) * nchain], states, chunk)
            outs += out
        for (d, h, r0), out in zip(where, outs):
            o_ref = of_ref if d == 0 else ob_ref
            o_ref[0, pl.ds(r0, chunk), h * DN_DIM:(h + 1) * DN_DIM] = out
        for i in range(nchain):
            st_ref[i] = states[i]
        return carry

    lax.fori_loop(0, nct // DN_CHUNKS_PER_TRIP, body, 0)


def _dn_scan(qkv, bg, bgr, *, chunk):
    b, s, _ = qkv.shape
    tile = min(TILE_DN_SCAN, s)
    nt = s // tile
    nct = tile // chunk
    kern = functools.partial(_dn_scan_kernel, tile=tile, chunk=chunk)
    fwd = lambda bi, i: (bi, i, 0)
    bwd = lambda bi, i: (bi, nt - 1 - i, 0)
    fwd4 = lambda bi, i: (bi, i, 0, 0)
    bwd4 = lambda bi, i: (bi, nt - 1 - i, 0, 0)
    return pl.pallas_call(
        kern,
        grid=(b, nt),
        in_specs=[pl.BlockSpec((1, tile, 3 * DN_WIDTH), fwd),
                  pl.BlockSpec((1, tile, 3 * DN_WIDTH), bwd),
                  pl.BlockSpec((1, tile, LANES), fwd),
                  pl.BlockSpec((1, tile, LANES), bwd),
                  pl.BlockSpec((1, nct, 16, chunk), fwd4),
                  pl.BlockSpec((1, nct, 16, chunk), bwd4)],
        out_specs=[pl.BlockSpec((1, tile, DN_WIDTH), fwd),
                   pl.BlockSpec((1, tile, DN_WIDTH), bwd)],
        out_shape=[jax.ShapeDtypeStruct((b, s, DN_WIDTH), F32),
                   jax.ShapeDtypeStruct((b, s, DN_WIDTH), F32)],
        scratch_shapes=[pltpu.VMEM((2 * DN_HEADS, DN_DIM, DN_DIM), F32)],
        compiler_params=_cparams("parallel", "arbitrary"),
        name="dn_scan",
    )(qkv, qkv, bg, bg, bgr, bgr)


def _s5_operators(lam_re, lam_im, log_dt, b_re, b_im, c_re, c_im, d_skip, *, chunk, nc):
    t = chunk
    levels = max(1, int(math.log2(nc)))
    tau = jnp.arange(t + 1, dtype=F32)
    kern = []
    lamp = []
    for d in range(2):
        lam = lax.complex(lam_re[d], lam_im[d])
        dt = jnp.exp(log_dt[d])[:, None]
        lam_dt = lam * dt
        lam_bar = jnp.exp(lam_dt)
        bbar = ((lam_bar - 1.0) / lam)[:, :, None] * lax.complex(b_re[d], b_im[d])
        cc = lax.complex(c_re[d], c_im[d])
        pw = jnp.exp(lam_dt[:, None, :] * tau[None, :, None].astype(jnp.complex64))
        kern.append((pw, bbar, cc))
        steps = (t * 2.0 ** jnp.arange(levels, dtype=F32)).astype(jnp.complex64)
        lamp.append(jnp.exp(lam_dt[:, None, :] * steps[None, :, None]))
    (pwf, bbf, ccf), (pwb, bbb, ccb) = kern
    ef = jnp.einsum('gsp,gpc->gscp', pwf[:, t - 1::-1, :], bbf)
    eb = jnp.einsum('gsp,gpc->gscp', pwb[:, :t, :], bbb)
    w1 = jnp.concatenate([jnp.real(ef), jnp.real(eb), jnp.imag(ef), jnp.imag(eb)], axis=-1)
    w1 = w1.reshape(S5_GROUPS, t * S5_GROUP, 4 * S5_STATE)
    kf = jnp.real(jnp.einsum('gcp,gtp,gpd->gtcd', ccf, pwf[:, :t, :], bbf))
    kb = jnp.real(jnp.einsum('gcp,gtp,gpd->gtcd', ccb, pwb[:, :t, :], bbb))
    si = jnp.arange(t)[:, None]
    ti = jnp.arange(t)[None, :]
    lag_f = jnp.clip(ti - si, 0, t - 1)
    lag_b = jnp.clip(si - ti, 0, t - 1)
    toep = (jnp.where((ti >= si)[None, :, :, None, None], kf[:, lag_f], 0.0)
            + jnp.where((si >= ti)[None, :, :, None, None], kb[:, lag_b], 0.0))
    eye_t = jnp.eye(t, dtype=F32)[None, :, :, None, None]
    eye_c = jnp.eye(S5_GROUP, dtype=F32)[None, None, None, :, :]
    toep = toep + eye_t * eye_c * d_skip.reshape(S5_GROUPS, 1, 1, S5_GROUP, 1)
    w_intra = toep.transpose(0, 1, 4, 2, 3).reshape(S5_GROUPS, t * S5_GROUP, t * S5_GROUP)
    mf = jnp.einsum('gcp,gtp->gptc', ccf, pwf[:, 1:t + 1, :])
    mb = jnp.einsum('gcp,gtp->gptc', ccb, pwb[:, t:0:-1, :])
    w_state = jnp.concatenate([jnp.real(mf), jnp.real(mb), -jnp.imag(mf), -jnp.imag(mb)], axis=1)
    w_state = w_state.reshape(S5_GROUPS, 4 * S5_STATE, t * S5_GROUP)
    w3 = jnp.concatenate([w_intra, w_state], axis=1)
    lam_r = jnp.concatenate([jnp.real(lamp[0]), jnp.real(lamp[1])], axis=-1)
    lam_i = jnp.concatenate([jnp.imag(lamp[0]), jnp.imag(lamp[1])], axis=-1)
    lamp = jnp.stack([lam_r, lam_i], axis=2)
    return w1.astype(BF16), w3.astype(BF16), lamp.astype(F32)


def _s5_kernel(u_ref, w1_ref, w3_ref, lam_ref, y_ref, *, nc, width, levels):
    u = u_ref[0, 0]
    loc = jnp.dot(u, w1_ref[0], preferred_element_type=F32)
    re = loc[:, :2 * S5_STATE]
    im = loc[:, 2 * S5_STATE:]
    row = lax.broadcasted_iota(jnp.int32, (nc, 2 * S5_STATE), 0)
    fwd = lax.broadcasted_iota(jnp.int32, (nc, 2 * S5_STATE), 1) < S5_STATE

    def shifted(x, d):
        down = jnp.where(row >= d, pltpu.roll(x, d, 0), 0.0)
        up = jnp.where(row < nc - d, pltpu.roll(x, nc - d, 0), 0.0)
        return jnp.where(fwd, down, up)

    for k in range(levels):
        d = 2 ** k
        if d >= nc:
            break
        lr = lam_ref[0, k, 0:1, :]
        li = lam_ref[0, k, 1:2, :]
        sr = shifted(re, d)
        si = shifted(im, d)
        re, im = re + sr * lr - si * li, im + sr * li + si * lr
    ent = jnp.concatenate([shifted(re, 1), shifted(im, 1)], axis=1)
    y = jnp.dot(u, w3_ref[0, :width, :], preferred_element_type=F32)
    y = y + jnp.dot(ent.astype(BF16), w3_ref[0, width:, :], preferred_element_type=F32)
    y_ref[0, 0] = y


def _s5_ssm(u5, w1, w3, lamp):
    g, b, nc, width = u5.shape
    levels = lamp.shape[1]
    kern = functools.partial(_s5_kernel, nc=nc, width=width, levels=levels)
    return pl.pallas_call(
        kern,
        grid=(g, b),
        in_specs=[pl.BlockSpec((1, 1, nc, width), lambda gi, bi: (gi, bi, 0, 0)),
                  pl.BlockSpec((1, width, 4 * S5_STATE), lambda gi, bi: (gi, 0, 0)),
                  pl.BlockSpec((1, width + 4 * S5_STATE, width), lambda gi, bi: (gi, 0, 0)),
                  pl.BlockSpec((1, levels, 2, 2 * S5_STATE), lambda gi, bi: (gi, 0, 0, 0))],
        out_specs=pl.BlockSpec((1, 1, nc, width), lambda gi, bi: (gi, bi, 0, 0)),
        out_shape=jax.ShapeDtypeStruct((g, b, nc, width), F32),
        compiler_params=_cparams("parallel", "parallel"),
        name="s5_ssm",
    )(u5, w1, w3, lamp)


def _rope_kernel(qq_ref, kk_ref, v_ref, cos_ref, sin_ref, qa_ref, qb_ref, ko_ref, vo_ref):
    width = DF_HEADS * 2 * DF_QK
    cos = jnp.concatenate([cos_ref[...]] * (width // LANES), axis=1)
    sin = jnp.concatenate([sin_ref[...]] * (width // LANES), axis=1)
    lane = lax.broadcasted_iota(jnp.int32, (1, width), 1)
    first_half = (lane % DF_QK) < (DF_QK // 2)

    def rot(x):
        partner = jnp.where(first_half, pltpu.roll(x, width - DF_QK // 2, 1),
                            pltpu.roll(x, DF_QK // 2, 1))
        return x * cos + partner * sin

    q = rot(qq_ref[0]) * (DF_QK ** -0.5 * LOG2E)
    is_a = (lane % (2 * DF_QK)) < DF_QK
    qa_ref[0] = jnp.where(is_a, q, 0.0).astype(BF16)
    qb_ref[0] = jnp.where(is_a, 0.0, q).astype(BF16)
    ko_ref[0] = rot(kk_ref[0]).astype(BF16)
    vo_ref[0] = v_ref[0].astype(BF16)


def _rope_prep(proj, cos_t, sin_t):
    b, s, _ = proj.shape
    tile = min(TILE_ROPE, s)
    w = DF_WIDTH
    spec = lambda cb: pl.BlockSpec((1, tile, w), lambda bi, i: (bi, i, cb))
    out = pl.BlockSpec((1, tile, w), lambda bi, i: (bi, i, 0))
    tab = pl.BlockSpec((tile, LANES), lambda bi, i: (i, 0))
    return pl.pallas_call(
        _rope_kernel,
        grid=(b, s // tile),
        in_specs=[spec(COL_QQ // w), spec(COL_KK // w), spec(COL_DV // w), tab, tab],
        out_specs=[out, out, out, out],
        out_shape=[jax.ShapeDtypeStruct((b, s, w), BF16)] * 4,
        compiler_params=_cparams("parallel", "parallel"),
        name="rope_prep",
    )(proj, proj, proj, cos_t, sin_t)


def _diff_attn_kernel(qa_ref, qb_ref, k_ref, v_ref, lq1_ref, lk1_ref, lq2_ref, lk2_ref, ng_ref,
                      o_ref, s_ref, *, tq, tk, nk, lam_init):
    q2 = (qa_ref[0], qb_ref[0])
    nt_dims = (((1,), (1,)), ((), ()))

    def fold(x, op, init):
        for j in range(tk // LANES):
            init = op(init, x[:, j * LANES:(j + 1) * LANES])
        return init

    m = [jnp.full((tq, LANES), -jnp.inf, F32)] * 2
    for c in range(nk):
        kc = k_ref[0, c * tk:(c + 1) * tk, :]
        for i in range(2):
            sc = lax.dot_general(q2[i], kc, nt_dims, preferred_element_type=F32)
            s_ref[c, i * tq:(i + 1) * tq, :] = sc
            m[i] = fold(sc, jnp.maximum, m[i])
    m = [jnp.max(x, axis=-1, keepdims=True) for x in m]

    acc = [jnp.zeros((tq, 2 * DF_V), F32)] * 2
    ones = jnp.ones((tk, DF_V), BF16)
    for c in range(nk):
        vc = jnp.concatenate([v_ref[0, c * tk:(c + 1) * tk, :], ones], axis=1)
        for i in range(2):
            e = jnp.exp2((s_ref[c, i * tq:(i + 1) * tq, :] - m[i]).astype(BF16))
            acc[i] = acc[i] + jnp.dot(e, vc, preferred_element_type=F32)
    lam = (jnp.exp(jnp.sum(lq1_ref[...] * lk1_ref[...], axis=-1, keepdims=True))
           - jnp.exp(jnp.sum(lq2_ref[...] * lk2_ref[...], axis=-1, keepdims=True)) + lam_init)
    acc = (acc[0][:, :DF_V] * (1.0 / acc[0][:, DF_V:DF_V + 1])
           - acc[1][:, :DF_V] * (lam / acc[1][:, DF_V:DF_V + 1]))
    ms = jnp.mean(acc * acc, axis=-1, keepdims=True)
    o_ref[0] = acc * lax.rsqrt(ms + RMS_EPS) * ng_ref[...] * (1.0 - lam_init)


def _diff_attn(qa, qb, kk, vv, lq1, lk1, lq2, lk2, norm_g, lam_init):
    b, s, _ = qa.shape
    tq = min(TILE_Q, s)
    tk = min(TILE_K, s)
    nk = s // tk
    kern = functools.partial(_diff_attn_kernel, tq=tq, tk=tk, nk=nk, lam_init=lam_init)
    qspec = pl.BlockSpec((1, tq, LANES), lambda bi, h, i: (bi, i, h))
    kspec = pl.BlockSpec((1, s, LANES), lambda bi, h, i: (bi, 0, h))
    vec64 = pl.BlockSpec((1, DF_QK), lambda bi, h, i: (0, 0))
    return pl.pallas_call(
        kern,
        grid=(b, DF_HEADS, s // tq),
        in_specs=[qspec, qspec, kspec, kspec, vec64, vec64, vec64, vec64,
                  pl.BlockSpec((1, DF_V), lambda bi, h, i: (0, 0))],
        out_specs=pl.BlockSpec((1, tq, DF_V), lambda bi, h, i: (bi, i, h)),
        out_shape=jax.ShapeDtypeStruct((b, s, DF_WIDTH), F32),
        scratch_shapes=[pltpu.VMEM((nk, 2 * tq, tk), F32)],
        compiler_params=_cparams("parallel", "parallel", "arbitrary"),
        name="diff_attn",
    )(qa, qb, kk, vv, lq1, lk1, lq2, lk2, norm_g)


def _layer_norm(x, g, b):
    mu = jnp.mean(x, axis=-1, keepdims=True)
    xc = x - mu
    var = jnp.mean(xc * xc, axis=-1, keepdims=True)
    return xc * lax.rsqrt(var + LN_EPS) * g + b


def _merge_kernel(x_ref, of_ref, ob_ref, z_ref, y5_ref, oc_ref, g0_ref, g1_ref, g2_ref,
                  bgate_ref, dnn_ref, gluw_ref, glub_ref, wbr_ref, wout_ref, lng_ref, lnb_ref,
                  o_ref):
    o = of_ref[0] + ob_ref[0]
    z = z_ref[0]
    parts = []
    for h in range(DN_HEADS):
        oh = o[:, h * DN_DIM:(h + 1) * DN_DIM]
        ms = jnp.mean(oh * oh, axis=-1, keepdims=True)
        parts.append(oh * lax.rsqrt(ms + RMS_EPS) * dnn_ref[...])
    o_a = jnp.concatenate(parts, axis=1) * _silu(z)
    y = jax.nn.gelu(y5_ref[0])
    o_b = y * _sigmoid(_bdot(y, gluw_ref[...]) + glub_ref[...])
    o_c = oc_ref[0]
    bg = bgate_ref[...]
    merged = (_sigmoid(g0_ref[0] + bg[:, 0:D_MODEL]) * _bdot(o_a, wbr_ref[0])
              + _sigmoid(g1_ref[0] + bg[:, D_MODEL:2 * D_MODEL]) * _bdot(o_b, wbr_ref[1])
              + _sigmoid(g2_ref[0] + bg[:, 2 * D_MODEL:]) * _bdot(o_c, wbr_ref[2]))
    hres = DEEPNORM_ALPHA * x_ref[0] + _bdot(merged, wout_ref[...])
    o_ref[0] = _layer_norm(hres, lng_ref[...], lnb_ref[...])


def _merge(x, o_f, o_b, proj, y5, o_c, b_gate, dn_norm, glu_w, glu_b, w_branch, w_out, ln_g, ln_b):
    b, s, _ = x.shape
    tile = min(TILE_MERGE, s)
    row = lambda w, cb=0: pl.BlockSpec((1, tile, w), lambda bi, i: (bi, i, cb))
    full = lambda shape: pl.BlockSpec(shape, lambda bi, i: (0,) * len(shape))
    gcb = COL_GATE // D_MODEL
    return pl.pallas_call(
        _merge_kernel,
        grid=(b, s // tile),
        in_specs=[row(D_MODEL), row(DN_WIDTH), row(DN_WIDTH), row(DN_WIDTH, COL_DN_Z // DN_WIDTH),
                  row(S5_WIDTH), row(DF_WIDTH),
                  row(D_MODEL, gcb), row(D_MODEL, gcb + 1), row(D_MODEL, gcb + 2),
                  full((1, N_BRANCH * D_MODEL)), full((1, DN_DIM)),
                  full((S5_WIDTH, S5_WIDTH)), full((1, S5_WIDTH)),
                  full((N_BRANCH, DN_WIDTH, D_MODEL)), full((D_MODEL, D_MODEL)),
                  full((1, D_MODEL)), full((1, D_MODEL))],
        out_specs=row(D_MODEL),
        out_shape=jax.ShapeDtypeStruct((b, s, D_MODEL), F32),
        compiler_params=_cparams("parallel", "parallel"),
        name="merge",
    )(x, o_f, o_b, proj, y5, o_c, proj, proj, proj, b_gate, dn_norm, glu_w, glu_b,
      w_branch, w_out, ln_g, ln_b)


def _xattn_kernel(x_ref, kv_ref, wq_ref, wo_ref, lng_ref, lnb_ref, o_ref):
    x = x_ref[0]
    q = _bdot(x, wq_ref[...])
    outs = []
    for h in range(XA_HEADS):
        qh = q[:, h * XA_DIM:(h + 1) * XA_DIM]
        kh = kv_ref[0, :, h * XA_DIM:(h + 1) * XA_DIM]
        vh = kv_ref[0, :, XA_WIDTH + h * XA_DIM:XA_WIDTH + (h + 1) * XA_DIM]
        sc = _bdot_nt(qh, kh) * (XA_DIM ** -0.5)
        sc = sc - jnp.max(sc, axis=-1, keepdims=True)
        e = jnp.exp(sc)
        pr = e / jnp.sum(e, axis=-1, keepdims=True)
        outs.append(_bdot(pr, vh))
    o = jnp.concatenate(outs, axis=1)
    hres = DEEPNORM_ALPHA * x + _bdot(o, wo_ref[...])
    o_ref[0] = _layer_norm(hres, lng_ref[...], lnb_ref[...])


def _xattn(x, kv, wq, wo, ln_g, ln_b):
    b, s, _ = x.shape
    tile = min(TILE_XA, s)
    mlen = kv.shape[1]
    row = pl.BlockSpec((1, tile, D_MODEL), lambda bi, i: (bi, i, 0))
    full = lambda shape: pl.BlockSpec(shape, lambda bi, i: (0,) * len(shape))
    return pl.pallas_call(
        _xattn_kernel,
        grid=(b, s // tile),
        in_specs=[row, pl.BlockSpec((1, mlen, 2 * XA_WIDTH), lambda bi, i: (bi, 0, 0)),
                  full((D_MODEL, XA_WIDTH)), full((XA_WIDTH, D_MODEL)),
                  full((1, D_MODEL)), full((1, D_MODEL))],
        out_specs=row,
        out_shape=jax.ShapeDtypeStruct((b, s, D_MODEL), F32),
        compiler_params=_cparams("parallel", "parallel"),
        name="xattn",
    )(x, kv, wq, wo, ln_g, ln_b)


def _router_kernel(x_ref, rw_ref, rb_ref, idx_ref, gate_ref, xb_ref):
    x = x_ref[...]
    xb_ref[...] = x.astype(BF16)
    logits = lax.dot_general(rw_ref[...], x, (((1,), (1,)), ((), ())), precision=HI,
                             preferred_element_type=F32) + rb_ref[...]
    tn = logits.shape[1]
    eid = lax.broadcasted_iota(jnp.int32, (N_EXPERTS, tn), 0)
    vals, ids = [], []
    cur = logits
    for _ in range(TOP_K):
        m = jnp.max(cur, axis=0, keepdims=True)
        sel = jnp.min(jnp.where(cur == m, eid, N_EXPERTS), axis=0, keepdims=True)
        vals.append(m)
        ids.append(sel)
        cur = jnp.where(eid == sel, -jnp.inf, cur)
    top_v = jnp.concatenate(vals, axis=0)
    e = jnp.exp(top_v - top_v[0:1])
    gate_ref[...] = e / jnp.sum(e, axis=0, keepdims=True)
    idx_ref[...] = jnp.concatenate(ids, axis=0)


def _router(x2d, router_w_t, router_b):
    n, _ = x2d.shape
    tile = min(TILE_ROUTER, n)
    return pl.pallas_call(
        _router_kernel,
        grid=(n // tile,),
        in_specs=[pl.BlockSpec((tile, D_MODEL), lambda i: (i, 0)),
                  pl.BlockSpec((N_EXPERTS, D_MODEL), lambda i: (0, 0)),
                  pl.BlockSpec((N_EXPERTS, 1), lambda i: (0, 0))],
        out_specs=[pl.BlockSpec((TOP_K, tile), lambda i: (0, i)),
                   pl.BlockSpec((TOP_K, tile), lambda i: (0, i)),
                   pl.BlockSpec((tile, D_MODEL), lambda i: (i, 0))],
        out_shape=[jax.ShapeDtypeStruct((TOP_K, n), jnp.int32),
                   jax.ShapeDtypeStruct((TOP_K, n), F32),
                   jax.ShapeDtypeStruct((n, D_MODEL), BF16)],
        compiler_params=_cparams("parallel"),
        name="router",
    )(x2d, router_w_t, router_b)


def _rank_kernel(idx_ref, rank_ref, cnt_ref, run_ref, *, tile):
    @pl.when(pl.program_id(0) == 0)
    def _():
        run_ref[...] = jnp.zeros_like(run_ref)

    idx = idx_ref[...]
    eid = lax.broadcasted_iota(jnp.int32, (N_EXPERTS, tile), 0)
    si = lax.broadcasted_iota(jnp.int32, (tile, tile), 0)
    ti = lax.broadcasted_iota(jnp.int32, (tile, tile), 1)
    before = (si < ti).astype(BF16)
    run = run_ref[...]
    ranks = []
    for k in range(TOP_K):
        onehot = eid == idx[k:k + 1]
        oh = onehot.astype(BF16)
        within = jnp.dot(oh, before, preferred_element_type=F32)
        ranks.append(jnp.sum(jnp.where(onehot, within + run, 0.0), axis=0, keepdims=True))
        run = run + jnp.sum(onehot.astype(F32), axis=1, keepdims=True)
    rank_ref[...] = jnp.concatenate(ranks, axis=0).astype(jnp.int32)
    run_ref[...] = run
    cnt_ref[...] = run.astype(jnp.int32)


def _rank(top_i):
    _, n = top_i.shape
    tile = min(TILE_ROUTER, n)
    kern = functools.partial(_rank_kernel, tile=tile)
    return pl.pallas_call(
        kern,
        grid=(n // tile,),
        in_specs=[pl.BlockSpec((TOP_K, tile), lambda i: (0, i))],
        out_specs=[pl.BlockSpec((TOP_K, tile), lambda i: (0, i)),
                   pl.BlockSpec((N_EXPERTS, 1), lambda i: (0, 0))],
        out_shape=[jax.ShapeDtypeStruct((TOP_K, n), jnp.int32),
                   jax.ShapeDtypeStruct((N_EXPERTS, 1), jnp.int32)],
        scratch_shapes=[pltpu.VMEM((N_EXPERTS, 1), F32)],
        compiler_params=_cparams("arbitrary"),
        name="moe_rank",
    )(top_i)


def _expert_kernel(be_ref, on_ref, x_ref, wg_ref, wu_ref, bg_ref, bu_ref, wd_ref, bd_ref, o_ref,
                   wd_bf_ref):
    i = pl.program_id(0)
    live = on_ref[i] != 0

    @pl.when(jnp.logical_or(i == 0, be_ref[i] != be_ref[jnp.maximum(i - 1, 0)]))
    def _():
        wd_bf_ref[...] = wd_ref[0, 0].astype(BF16)

    @pl.when(live)
    def _():
        x = x_ref[...]
        hg = jnp.dot(x, wg_ref[0], preferred_element_type=F32) + bg_ref[0]
        hu = jnp.dot(x, wu_ref[0], preferred_element_type=F32) + bu_ref[0]
        gt = jnp.minimum(hg, SWIGLU_LIMIT)
        up = jnp.clip(hu, -SWIGLU_LIMIT, SWIGLU_LIMIT)
        act = (up + 1.0) * gt * _sigmoid(SWIGLU_ALPHA * gt)
        y = jnp.dot(act.astype(BF16), wd_bf_ref[...], preferred_element_type=F32) + bd_ref[0]
        o_ref[...] = y.astype(o_ref.dtype)

    @pl.when(jnp.logical_not(live))
    def _():
        o_ref[...] = jnp.zeros_like(o_ref)


def _experts(x_rows, blk_e, blk_on, w_g, w_u, b_g, b_u, w_d, l, b_d, *, block):
    rows, _ = x_rows.shape
    wspec = lambda k, m: pl.BlockSpec((1, k, m), lambda i, be, on: (be[i], 0, 0))
    grid_spec = pltpu.PrefetchScalarGridSpec(
        num_scalar_prefetch=2,
        grid=(rows // block,),
        in_specs=[pl.BlockSpec((block, D_MODEL), lambda i, be, on: (i, 0)),
                  wspec(D_MODEL, D_FF), wspec(D_MODEL, D_FF), wspec(1, D_FF), wspec(1, D_FF),
                  pl.BlockSpec((1, 1, D_FF, D_MODEL), lambda i, be, on: (l, be[i], 0, 0)),
                  wspec(1, D_MODEL)],
        out_specs=pl.BlockSpec((block, D_MODEL), lambda i, be, on: (i, 0)),
        scratch_shapes=[pltpu.VMEM((D_FF, D_MODEL), BF16)],
    )
    return pl.pallas_call(
        _expert_kernel,
        grid_spec=grid_spec,
        out_shape=jax.ShapeDtypeStruct((rows, D_MODEL), BF16),
        compiler_params=_cparams("arbitrary"),
        name="moe_experts",
    )(blk_e, blk_on, x_rows, w_g, w_u, b_g, b_u, w_d, b_d)


def _combine_kernel(x_ref, y_ref, gate_ref, lng_ref, lnb_ref, o_ref):
    gate = gate_ref[...]
    h = y_ref[0].astype(F32) * gate[:, 0:1]
    for k in range(1, TOP_K):
        h = h + y_ref[k].astype(F32) * gate[:, k:k + 1]
    o_ref[...] = _layer_norm(DEEPNORM_ALPHA * x_ref[...] + h, lng_ref[...], lnb_ref[...])


def _combine(x2d, y_g, gate_t, ln_g, ln_b):
    n, _ = x2d.shape
    tile = min(TILE_COMBINE, n)
    return pl.pallas_call(
        _combine_kernel,
        grid=(n // tile,),
        in_specs=[pl.BlockSpec((tile, D_MODEL), lambda i: (i, 0)),
                  pl.BlockSpec((TOP_K, tile, D_MODEL), lambda i: (0, i, 0)),
                  pl.BlockSpec((tile, TOP_K), lambda i: (i, 0)),
                  pl.BlockSpec((1, D_MODEL), lambda i: (0, 0)),
                  pl.BlockSpec((1, D_MODEL), lambda i: (0, 0))],
        out_specs=pl.BlockSpec((tile, D_MODEL), lambda i: (i, 0)),
        out_shape=jax.ShapeDtypeStruct((n, D_MODEL), F32),
        compiler_params=_cparams("parallel"),
        name="moe_combine",
    )(x2d, y_g, gate_t, ln_g, ln_b)


def _split_kernel(w_ref, se_ref, so_ref, g_ref, u_ref):
    w = w_ref[0, 0].astype(BF16)
    g_ref[0] = jnp.dot(w, se_ref[...], preferred_element_type=F32).astype(BF16)
    u_ref[0] = jnp.dot(w, so_ref[...], preferred_element_type=F32).astype(BF16)


def _split_gate_up(w_gu, l):
    _, e, dm, f2 = w_gu.shape
    f = f2 // 2
    tr = 256
    col = jnp.arange(f2, dtype=jnp.int32)[:, None]
    out = jnp.arange(f, dtype=jnp.int32)[None, :]
    sel_even = (col == 2 * out).astype(BF16)
    sel_odd = (col == 2 * out + 1).astype(BF16)
    sel = pl.BlockSpec((f2, f), lambda ei, i: (0, 0))
    ospec = pl.BlockSpec((1, tr, f), lambda ei, i: (ei, i, 0))
    return pl.pallas_call(
        _split_kernel,
        grid=(e, dm // tr),
        in_specs=[pl.BlockSpec((1, 1, tr, f2), lambda ei, i: (l, ei, i, 0)), sel, sel],
        out_specs=[ospec, ospec],
        out_shape=[jax.ShapeDtypeStruct((e, dm, f), BF16)] * 2,
        compiler_params=_cparams("parallel", "parallel"),
        name="split_gate_up",
    )(w_gu, sel_even, sel_odd)


def _moe(x, w):
    b, s, dm = x.shape
    n = b * s
    x2d = x.reshape(n, dm)
    top_i, gate, xb = _router(x2d, w['router_w_t'], w['router_b'])
    rank, counts = _rank(top_i)
    block = MOE_BLOCK
    m = n * TOP_K
    nblk = -(-m // block) + N_EXPERTS
    rows = nblk * block
    counts = counts.reshape(N_EXPERTS)
    padded = (counts + block - 1) // block * block
    pad_end = jnp.cumsum(padded)
    pad_start = pad_end - padded
    dest = pad_start[top_i] + rank
    tok = jnp.broadcast_to(jnp.arange(n, dtype=jnp.int32)[None, :], (TOP_K, n))
    row_tok = jnp.full((rows,), n, jnp.int32).at[dest.reshape(-1)].set(
        tok.reshape(-1), unique_indices=True, mode='promise_in_bounds')
    blk_start = jnp.arange(nblk, dtype=pad_end.dtype) * block
    blk_e = jnp.minimum(jnp.sum(pad_end[None, :] <= blk_start[:, None], axis=1),
                        N_EXPERTS - 1).astype(jnp.int32)
    blk_on = (blk_start < pad_end[-1]).astype(jnp.int32)
    x_pad = jnp.concatenate([xb, jnp.zeros((1, dm), BF16)], axis=0)
    x_rows = x_pad.at[row_tok].get(mode='promise_in_bounds')
    y_rows = _experts(x_rows, blk_e, blk_on, w['e_w_g'], w['e_w_u'], w['e_b_g'], w['e_b_u'], w['e_w_d'],
                      w['layer'], w['e_b_d'], block=block)
    y_g = y_rows.at[dest].get(mode='promise_in_bounds')
    out = _combine(x2d, y_g, gate.T, w['ln3_g'], w['ln3_b'])
    return out.reshape(b, s, dm)


def _rope_tables(s):
    half = DF_QK // 2
    inv_freq = jnp.power(ROPE_THETA, -jnp.arange(half, dtype=F32) / half)
    ang = jnp.arange(s, dtype=F32)[:, None] * inv_freq[None, :]
    cos = jnp.cos(ang)
    sin = jnp.sin(ang)
    cos_t = jnp.concatenate([cos, cos, cos, cos], axis=1)
    sin_t = jnp.concatenate([-sin, sin, -sin, sin], axis=1)
    return cos_t, sin_t


def _prepare_layer(l, p, s):
    dm = D_MODEL
    w = {}
    w['w_in'] = _pack_in_proj(p['w_in'][l]).astype(BF16)
    pad8 = jnp.zeros((8,), F32)
    pad112 = jnp.zeros((LANES - 16,), F32)
    w['nega'] = jnp.concatenate([pad8, -jnp.exp(p['dn_A_log'][l].reshape(8)), pad112]).reshape(1, LANES)
    w['dtb'] = jnp.concatenate([pad8, p['dn_dt_bias'][l].reshape(8), pad112]).reshape(1, LANES)
    w['dn_conv'] = p['dn_conv'][l]
    t5 = min(S5_CHUNK, s)
    w['s5'] = _s5_operators(p['s5_lam_re'][l], p['s5_lam_im'][l], p['s5_log_dt'][l],
                            p['s5_B_re'][l], p['s5_B_im'][l], p['s5_C_re'][l],
                            p['s5_C_im'][l], p['s5_D'][l], chunk=t5, nc=s // t5)
    for name in ('df_lq1', 'df_lk1', 'df_lq2', 'df_lk2'):
        w[name] = p[name][l].reshape(1, DF_QK)
    w['df_norm'] = p['df_norm'][l].reshape(1, DF_V)
    w['lam_init'] = 0.8 - 0.6 * math.exp(-0.3 * l)
    w['b_gate'] = p['b_gate'][l].reshape(1, -1)
    w['dn_norm'] = p['dn_norm'][l].reshape(1, DN_DIM)
    w['s5_glu_w'] = p['s5_glu_w'][l].astype(BF16)
    w['s5_glu_b'] = p['s5_glu_b'][l].reshape(1, -1)
    w['w_branch'] = p['w_branch'][l].astype(BF16)
    w['w_out'] = p['w_out'][l].astype(BF16)
    for name in ('ln1_g', 'ln1_b', 'ln2_g', 'ln2_b', 'ln3_g', 'ln3_b'):
        w[name] = p[name][l].reshape(1, dm)
    w['xa_wkv'] = p['xa_wkv'][l].astype(BF16)
    w['xa_wq'] = p['xa_wq'][l].astype(BF16)
    w['xa_wo'] = p['xa_wo'][l].astype(BF16)
    w['router_w_t'] = p['router_w'][l].T
    w['router_b'] = p['router_b'][l].reshape(N_EXPERTS, 1)
    w['e_w_g'], w['e_w_u'] = _split_gate_up(p['e_w_gu'], l)
    w['e_b_g'] = p['e_b_gu'][l][:, None, 0::2]
    w['e_b_u'] = p['e_b_gu'][l][:, None, 1::2]
    w['e_w_d'] = p['e_w_down']
    w['layer'] = l
    w['e_b_d'] = p['e_b_down'][l][:, None, :]
    return w


def _layer(x, mem, w, tabs):
    b, s, dm = x.shape
    n = b * s
    proj = _matmul(x.reshape(n, dm), w['w_in'], tm=TILE_MM, tn=LANES * 19)
    proj = proj.reshape(b, s, PROJ_WIDTH)

    qkv, bg = _dn_prep(proj, w['dn_conv'], w['nega'], w['dtb'], chunk=DN_CHUNK)
    bgr = bg[:, :, :16].reshape(b, s // DN_CHUNK, DN_CHUNK, 16).transpose(0, 1, 3, 2)
    o_f, o_b = _dn_scan(qkv, bg, bgr, chunk=DN_CHUNK)

    t5 = min(S5_CHUNK, s)
    nc = s // t5
    u5 = proj[:, :, COL_S5_U:COL_S5_U + S5_WIDTH].reshape(b, nc, t5, S5_GROUPS, S5_GROUP)
    u5 = u5.transpose(3, 0, 1, 2, 4).reshape(S5_GROUPS, b, nc, t5 * S5_GROUP).astype(BF16)
    y5 = _s5_ssm(u5, *w['s5'])
    y5 = y5.reshape(S5_GROUPS, b, nc, t5, S5_GROUP).transpose(1, 2, 3, 0, 4).reshape(b, s, S5_WIDTH)

    qa, qb, kk, vv = _rope_prep(proj, *tabs)
    o_c = _diff_attn(qa, qb, kk, vv, w['df_lq1'], w['df_lk1'], w['df_lq2'], w['df_lk2'],
                     w['df_norm'], w['lam_init'])

    x = _merge(x, o_f, o_b, proj, y5, o_c, w['b_gate'], w['dn_norm'], w['s5_glu_w'], w['s5_glu_b'],
               w['w_branch'], w['w_out'], w['ln1_g'], w['ln1_b'])

    mlen = mem.shape[1]
    kv = _matmul(mem.reshape(b * mlen, dm), w['xa_wkv'], tm=512, tn=1024)
    x = _xattn(x, kv.reshape(b, mlen, 2 * XA_WIDTH), w['xa_wq'], w['xa_wo'], w['ln2_g'], w['ln2_b'])
    return _moe(x, w)


def kernel(x_prompt, x_sample, mem_prompt, mem_sample, w_in, b_gate, dn_conv, dn_A_log, dn_dt_bias,
           dn_norm, s5_lam_re, s5_lam_im, s5_log_dt, s5_B_re, s5_B_im, s5_C_re, s5_C_im, s5_D,
           s5_glu_w, s5_glu_b, df_lq1, df_lk1, df_lq2, df_lk2, df_norm, w_branch, w_out,
           ln1_g, ln1_b, xa_wq, xa_wkv, xa_wo, ln2_g, ln2_b, router_w, router_b,
           e_w_gu, e_b_gu, e_w_down, e_b_down, ln3_g, ln3_b):
    p = dict(w_in=w_in, b_gate=b_gate, dn_conv=dn_conv, dn_A_log=dn_A_log, dn_dt_bias=dn_dt_bias,
             dn_norm=dn_norm, s5_lam_re=s5_lam_re, s5_lam_im=s5_lam_im, s5_log_dt=s5_log_dt,
             s5_B_re=s5_B_re, s5_B_im=s5_B_im, s5_C_re=s5_C_re, s5_C_im=s5_C_im, s5_D=s5_D,
             s5_glu_w=s5_glu_w, s5_glu_b=s5_glu_b, df_lq1=df_lq1, df_lk1=df_lk1, df_lq2=df_lq2,
             df_lk2=df_lk2, df_norm=df_norm, w_branch=w_branch, w_out=w_out, ln1_g=ln1_g,
             ln1_b=ln1_b, xa_wq=xa_wq, xa_wkv=xa_wkv, xa_wo=xa_wo, ln2_g=ln2_g, ln2_b=ln2_b,
             router_w=router_w, router_b=router_b, e_w_gu=e_w_gu, e_b_gu=e_b_gu,
             e_w_down=e_w_down, e_b_down=e_b_down, ln3_g=ln3_g, ln3_b=ln3_b)
    s = x_prompt.shape[1]
    assert x_sample.shape[1] == s
    tabs = _rope_tables(s)
    layers = [_prepare_layer(l, p, s) for l in range(DEPTH)]
    outs = []
    for x, mem in ((x_prompt, mem_prompt), (x_sample, mem_sample)):
        for w in layers:
            x = _layer(x, mem, w, tabs)
        outs.append(x)
    return tuple(outs)
```

```python
import functools
import math

import numpy as np
import jax
import jax.numpy as jnp
from jax import lax
from jax.experimental import pallas as pl
from jax.experimental.pallas import tpu as pltpu

F32 = jnp.float32
BF16 = jnp.bfloat16
HI = lax.Precision.HIGHEST

D_MODEL = 1024
DN_HEADS = 4
DN_DIM = 128
DN_WIDTH = DN_HEADS * DN_DIM
DN_CONV = 4
S5_GROUP = 16
S5_WIDTH = 512
S5_GROUPS = S5_WIDTH // S5_GROUP
S5_STATE = 64
DF_HEADS = 4
DF_QK = 64
DF_V = 128
DF_WIDTH = DF_HEADS * DF_V
ROPE_THETA = 10000.0
XA_HEADS = 4
XA_DIM = 128
XA_WIDTH = XA_HEADS * XA_DIM
N_EXPERTS = 32
TOP_K = 4
D_FF = 1024
SWIGLU_LIMIT = 7.0
SWIGLU_ALPHA = 1.702
N_BRANCH = 3
DEPTH = 2
DEEPNORM_ALPHA = (2 * DEPTH) ** 0.25
LN_EPS = 1e-5
RMS_EPS = 1e-6
LOG2E = 1.4426950408889634

LANES = 128
SUBLANES = 8
VMEM_LIMIT = 56 * 1024 * 1024

COL_DN_QKV = 0
COL_DN_Z = 1536
COL_S5_U = 2048
COL_QQ = 2560
COL_KK = 3072
COL_DV = 3584
COL_GATE = 4096
COL_BA = 7168
PROJ_WIDTH = 7296

TILE_MM = 1024
TILE_DN_PREP = 512
TILE_DN_SCAN = 256
DN_CHUNK = 64
DN_CHUNKS_PER_TRIP = 2
S5_CHUNK = 32
TILE_ROPE = 512
TILE_Q = 256
TILE_K = 512
TILE_MERGE = 256
TILE_XA = 512
TILE_ROUTER = 512
MOE_BLOCK = 512
TILE_COMBINE = 256


def _in_proj_perm():
    o_dn_b, o_dn_a, o_s5 = 2048, 2056, 2064
    o_q1, o_q2, o_k1, o_k2, o_dv, o_gate = 2576, 2832, 3088, 3344, 3600, 4112
    perm = np.full((PROJ_WIDTH,), -1, np.int64)
    perm[0:2048] = np.arange(2048)
    perm[COL_S5_U:COL_S5_U + 512] = o_s5 + np.arange(512)
    for h in range(DF_HEADS):
        j = np.arange(DF_QK)
        perm[COL_QQ + h * 128 + j] = o_q1 + h * DF_QK + j
        perm[COL_QQ + h * 128 + 64 + j] = o_q2 + h * DF_QK + j
        perm[COL_KK + h * 128 + j] = o_k1 + h * DF_QK + j
        perm[COL_KK + h * 128 + 64 + j] = o_k2 + h * DF_QK + j
    perm[COL_DV:COL_DV + 512] = o_dv + np.arange(512)
    perm[COL_GATE:COL_GATE + 3072] = o_gate + np.arange(3072)
    perm[COL_BA:COL_BA + 8] = o_dn_b + np.arange(8)
    perm[COL_BA + 8:COL_BA + 16] = o_dn_a + np.arange(8)
    return perm


_IN_PERM = _in_proj_perm()


def _pack_in_proj(w_in):
    runs, start = [], 0
    for i in range(1, PROJ_WIDTH + 1):
        if i == PROJ_WIDTH or _IN_PERM[i] != _IN_PERM[i - 1] + (1 if _IN_PERM[i - 1] >= 0 else 0):
            runs.append((start, i))
            start = i
    pieces = []
    for a, b in runs:
        src = int(_IN_PERM[a])
        pieces.append(jnp.zeros((w_in.shape[0], b - a), w_in.dtype) if src < 0
                      else w_in[:, src:src + (b - a)])
    return jnp.concatenate(pieces, axis=1)


def _cparams(*sem):
    return pltpu.CompilerParams(dimension_semantics=sem, vmem_limit_bytes=VMEM_LIMIT)


def _bdot(a, b):
    return jnp.dot(a.astype(BF16), b.astype(BF16), preferred_element_type=F32)


def _bdot_nt(a, b):
    return lax.dot_general(a.astype(BF16), b.astype(BF16), (((1,), (1,)), ((), ())),
                           preferred_element_type=F32)


def _bdot_tn(a, b):
    return lax.dot_general(a.astype(BF16), b.astype(BF16), (((0,), (0,)), ((), ())),
                           preferred_element_type=F32)


def _sigmoid(x):
    return 1.0 / (1.0 + jnp.exp(-x))


def _silu(x):
    return x * _sigmoid(x)


def _mm_kernel(x_ref, w_ref, o_ref):
    o_ref[...] = jnp.dot(x_ref[...].astype(BF16), w_ref[...],
                         preferred_element_type=F32).astype(o_ref.dtype)


def _matmul(x, w, *, tm, tn, out_dtype=F32):
    n, k = x.shape
    m = w.shape[1]
    tm = min(tm, n)
    tn = min(tn, m)
    return pl.pallas_call(
        _mm_kernel,
        grid=(n // tm, m // tn),
        in_specs=[pl.BlockSpec((tm, k), lambda i, j: (i, 0)),
                  pl.BlockSpec((k, tn), lambda i, j: (0, j))],
        out_specs=pl.BlockSpec((tm, tn), lambda i, j: (i, j)),
        out_shape=jax.ShapeDtypeStruct((n, m), out_dtype),
        compiler_params=_cparams("parallel", "arbitrary"),
        name="matmul",
    )(x, w)


def _dn_prep_kernel(x_ref, prev_ref, next_ref, ba_ref, cw_ref, nega_ref, dtb_ref,
                    qkv_ref, bg_ref, *, tile, chunk):
    i = pl.program_id(1)
    last = pl.num_programs(1) - 1
    x = x_ref[0]
    prev = jnp.where(i > 0, prev_ref[0], 0.0)
    nxt = jnp.where(i < last, next_ref[0], 0.0)
    ext = jnp.concatenate([prev, x, nxt], axis=0)
    cw = cw_ref[...]
    acc = cw[0:1] * ext[6:6 + tile]
    for j in range(1, DN_CONV):
        acc = acc + cw[j:j + 1] * ext[6 + j:6 + j + tile]
    act = _silu(acc)
    for h in range(DN_HEADS):
        lo = h * DN_DIM
        qh = act[:, lo:lo + DN_DIM]
        qn = qh * lax.rsqrt(jnp.sum(qh * qh, axis=-1, keepdims=True) + 1e-6)
        qkv_ref[0, :, lo:lo + DN_DIM] = qn * (DN_DIM ** -0.5)
        kh = act[:, DN_WIDTH + lo:DN_WIDTH + lo + DN_DIM]
        qkv_ref[0, :, DN_WIDTH + lo:DN_WIDTH + lo + DN_DIM] = kh * lax.rsqrt(
            jnp.sum(kh * kh, axis=-1, keepdims=True) + 1e-6)
    qkv_ref[0, :, 2 * DN_WIDTH:] = act[:, 2 * DN_WIDTH:]

    ba = ba_ref[0]
    beta = _sigmoid(ba)
    z = ba + dtb_ref[...]
    softplus = jnp.maximum(z, 0.0) + jnp.log(1.0 + jnp.exp(-jnp.abs(z)))
    g = nega_ref[...] * softplus
    row = lax.broadcasted_iota(jnp.int32, (chunk, chunk), 0)
    col = lax.broadcasted_iota(jnp.int32, (chunk, chunk), 1)
    lower = (row >= col).astype(F32)
    upper = (row <= col).astype(F32)
    lane = lax.broadcasted_iota(jnp.int32, (chunk, LANES), 1)
    fwd_lane = lane < 8 + DN_HEADS
    for c in range(tile // chunk):
        gc = g[c * chunk:(c + 1) * chunk]
        cf = jnp.dot(lower, gc, precision=HI, preferred_element_type=F32)
        cb = jnp.dot(upper, gc, precision=HI, preferred_element_type=F32)
        gcum = jnp.where(fwd_lane, cf, cb)
        bg_ref[0, c * chunk:(c + 1) * chunk, :] = jnp.where(
            lane < 8, beta[c * chunk:(c + 1) * chunk], gcum)


def _dn_prep(proj, conv_w, nega, dtb, *, chunk):
    b, s, _ = proj.shape
    tile = min(TILE_DN_PREP, s)
    nt = s // tile
    r8 = tile // SUBLANES
    kern = functools.partial(_dn_prep_kernel, tile=tile, chunk=chunk)
    return pl.pallas_call(
        kern,
        grid=(b, nt),
        in_specs=[
            pl.BlockSpec((1, tile, 3 * DN_WIDTH), lambda bi, i: (bi, i, 0)),
            pl.BlockSpec((1, SUBLANES, 3 * DN_WIDTH),
                         lambda bi, i: (bi, jnp.maximum(i * r8 - 1, 0), 0)),
            pl.BlockSpec((1, SUBLANES, 3 * DN_WIDTH),
                         lambda bi, i: (bi, jnp.minimum((i + 1) * r8, s // SUBLANES - 1), 0)),
            pl.BlockSpec((1, tile, LANES), lambda bi, i: (bi, i, COL_BA // LANES)),
            pl.BlockSpec((DN_CONV, 3 * DN_WIDTH), lambda bi, i: (0, 0)),
            pl.BlockSpec((1, LANES), lambda bi, i: (0, 0)),
            pl.BlockSpec((1, LANES), lambda bi, i: (0, 0)),
        ],
        out_specs=[pl.BlockSpec((1, tile, 3 * DN_WIDTH), lambda bi, i: (bi, i, 0)),
                   pl.BlockSpec((1, tile, LANES), lambda bi, i: (bi, i, 0))],
        out_shape=[jax.ShapeDtypeStruct((b, s, 3 * DN_WIDTH), F32),
                   jax.ShapeDtypeStruct((b, s, LANES), F32)],
        compiler_params=_cparams("parallel", "parallel"),
        name="dn_prep",
    )(proj, proj, proj, proj, conv_w, nega, dtb)


def _dn_local(chains, chunk):
    ri = lax.broadcasted_iota(jnp.int32, (chunk, chunk), 0)
    ci = lax.broadcasted_iota(jnp.int32, (chunk, chunk), 1)
    pre = []
    for d, h, q, k, v, bgc, rows in chains:
        li = d * DN_HEADS + h
        beta = bgc[:, li:li + 1]
        gcol = bgc[:, 8 + li:8 + li + 1]
        grow = rows[8 + li:8 + li + 1, :]
        gtot = gcol[chunk - 1:chunk] if d == 0 else gcol[0:1]
        incl = (ri >= ci) if d == 0 else (ri <= ci)
        strict = (ri > ci) if d == 0 else (ri < ci)
        eg = jnp.exp(gcol)
        kb = k * beta
        decay = jnp.where(incl, jnp.exp(jnp.where(incl, gcol - grow, 0.0)), 0.0)
        pre.append(dict(k=k, qg=q * eg, kd=k * jnp.exp(gtot - gcol), decay=decay, strict=strict,
                        x2=jnp.concatenate([v * beta, kb * eg], axis=1), sdec=jnp.exp(gtot),
                        qkb=jnp.concatenate([q, kb], axis=0)))
    qk = [_bdot_nt(c['qkb'], c['k']) for c in pre]
    intra = [x[:chunk] * c['decay'] for x, c in zip(qk, pre)]
    p = [-jnp.where(c['strict'], x[chunk:] * c['decay'], 0.0) for x, c in zip(qk, pre)]
    r = p
    p = [_bdot(x, x) for x in p]
    for _ in range(int(math.log2(chunk)) - 2):
        rp = [_bdot(jnp.concatenate([ri_, pi_], axis=0), pi_) for ri_, pi_ in zip(r, p)]
        r = [ri_ + pi_ + x[:chunk] for ri_, pi_, x in zip(r, p, rp)]
        p = [x[chunk:] for x in rp]
    r = [ri_ + pi_ + _bdot(ri_, pi_) for ri_, pi_ in zip(r, p)]
    uw = [c['x2'] + _bdot(ri_, c['x2']) for ri_, c in zip(r, pre)]
    return [dict(u=x[:, :DN_DIM], wq=jnp.concatenate([x[:, DN_DIM:], c['qg']], axis=0),
                 intra=it, kd=c['kd'], sdec=c['sdec']) for x, it, c in zip(uw, intra, pre)]


def _dn_state(local, states, chunk):
    ws = [_bdot(c['wq'], st) for c, st in zip(local, states)]
    v_new = [c['u'] - y[:chunk] for c, y in zip(local, ws)]
    out = [y[chunk:] + _bdot(c['intra'], vn) for y, c, vn in zip(ws, local, v_new)]
    st_new = [st * c['sdec'] + _bdot_tn(c['kd'], vn) for st, c, vn in zip(states, local, v_new)]
    return out, st_new


def _dn_scan_kernel(xf_ref, xb_ref, gf_ref, gb_ref, rf_ref, rb_ref, of_ref, ob_ref, st_ref,
                    *, tile, chunk):
    @pl.when(pl.program_id(1) == 0)
    def _():
        st_ref[...] = jnp.zeros_like(st_ref)

    nct = tile // chunk
    nchain = 2 * DN_HEADS

    def body(j, carry):
        chains, where = [], []
        for slot in range(DN_CHUNKS_PER_TRIP):
            for d in range(2):
                c = DN_CHUNKS_PER_TRIP * j + slot
                c = c if d == 0 else nct - 1 - c
                r0 = pl.multiple_of(c * chunk, chunk)
                x_ref, g_ref, r_ref = (xf_ref, gf_ref, rf_ref) if d == 0 else (xb_ref, gb_ref, rb_ref)
                bgc = g_ref[0, pl.ds(r0, chunk), :]
                rows = r_ref[0, c]
                for h in range(DN_HEADS):
                    lo = h * DN_DIM
                    q = x_ref[0, pl.ds(r0, chunk), lo:lo + DN_DIM]
                    k = x_ref[0, pl.ds(r0, chunk), DN_WIDTH + lo:DN_WIDTH + lo + DN_DIM]
                    v = x_ref[0, pl.ds(r0, chunk), 2 * DN_WIDTH + lo:2 * DN_WIDTH + lo + DN_DIM]
                    chains.append((d, h, q, k, v, bgc, rows))
                    where.append((d, h, r0))
        states = [st_ref[i] for i in range(nchain)]
        local = _dn_local(chains, chunk)
        outs = []
        for slot in range(DN_CHUNKS_PER_TRIP):
            out, states = _dn_state(local[slot * nchain:(slot + 1) * nchain], states, chunk)
            outs += out
        for (d, h, r0), out in zip(where, outs):
            o_ref = of_ref if d == 0 else ob_ref
            o_ref[0, pl.ds(r0, chunk), h * DN_DIM:(h + 1) * DN_DIM] = out
        for i in range(nchain):
            st_ref[i] = states[i]
        return carry

    lax.fori_loop(0, nct // DN_CHUNKS_PER_TRIP, body, 0)


def _dn_scan(qkv, bg, bgr, *, chunk):
    b, s, _ = qkv.shape
    tile = min(TILE_DN_SCAN, s)
    nt = s // tile
    nct = tile // chunk
    kern = functools.partial(_dn_scan_kernel, tile=tile, chunk=chunk)
    fwd = lambda bi, i: (bi, i, 0)
    bwd = lambda bi, i: (bi, nt - 1 - i, 0)
    fwd4 = lambda bi, i: (bi, i, 0, 0)
    bwd4 = lambda bi, i: (bi, nt - 1 - i, 0, 0)
    return pl.pallas_call(
        kern,
        grid=(b, nt),
        in_specs=[pl.BlockSpec((1, tile, 3 * DN_WIDTH), fwd),
                  pl.BlockSpec((1, tile, 3 * DN_WIDTH), bwd),
                  pl.BlockSpec((1, tile, LANES), fwd),
                  pl.BlockSpec((1, tile, LANES), bwd),
                  pl.BlockSpec((1, nct, 16, chunk), fwd4),
                  pl.BlockSpec((1, nct, 16, chunk), bwd4)],
        out_specs=[pl.BlockSpec((1, tile, DN_WIDTH), fwd),
                   pl.BlockSpec((1, tile, DN_WIDTH), bwd)],
        out_shape=[jax.ShapeDtypeStruct((b, s, DN_WIDTH), F32),
                   jax.ShapeDtypeStruct((b, s, DN_WIDTH), F32)],
        scratch_shapes=[pltpu.VMEM((2 * DN_HEADS, DN_DIM, DN_DIM), F32)],
        compiler_params=_cparams("parallel", "arbitrary"),
        name="dn_scan",
    )(qkv, qkv, bg, bg, bgr, bgr)


def _s5_operators(lam_re, lam_im, log_dt, b_re, b_im, c_re, c_im, d_skip, *, chunk, nc):
    t = chunk
    levels = max(1, int(math.log2(nc)))
    tau = jnp.arange(t + 1, dtype=F32)
    kern = []
    lamp = []
    for d in range(2):
        lam = lax.complex(lam_re[d], lam_im[d])
        dt = jnp.exp(log_dt[d])[:, None]
        lam_dt = lam * dt
        lam_bar = jnp.exp(lam_dt)
        bbar = ((lam_bar - 1.0) / lam)[:, :, None] * lax.complex(b_re[d], b_im[d])
        cc = lax.complex(c_re[d], c_im[d])
        pw = jnp.exp(lam_dt[:, None, :] * tau[None, :, None].astype(jnp.complex64))
        kern.append((pw, bbar, cc))
        steps = (t * 2.0 ** jnp.arange(levels, dtype=F32)).astype(jnp.complex64)
        lamp.append(jnp.exp(lam_dt[:, None, :] * steps[None, :, None]))
    (pwf, bbf, ccf), (pwb, bbb, ccb) = kern
    ef = jnp.einsum('gsp,gpc->gscp', pwf[:, t - 1::-1, :], bbf)
    eb = jnp.einsum('gsp,gpc->gscp', pwb[:, :t, :], bbb)
    w1 = jnp.concatenate([jnp.real(ef), jnp.real(eb), jnp.imag(ef), jnp.imag(eb)], axis=-1)
    w1 = w1.reshape(S5_GROUPS, t * S5_GROUP, 4 * S5_STATE)
    kf = jnp.real(jnp.einsum('gcp,gtp,gpd->gtcd', ccf, pwf[:, :t, :], bbf))
    kb = jnp.real(jnp.einsum('gcp,gtp,gpd->gtcd', ccb, pwb[:, :t, :], bbb))
    si = jnp.arange(t)[:, None]
    ti = jnp.arange(t)[None, :]
    lag_f = jnp.clip(ti - si, 0, t - 1)
    lag_b = jnp.clip(si - ti, 0, t - 1)
    toep = (jnp.where((ti >= si)[None, :, :, None, None], kf[:, lag_f], 0.0)
            + jnp.where((si >= ti)[None, :, :, None, None], kb[:, lag_b], 0.0))
    eye_t = jnp.eye(t, dtype=F32)[None, :, :, None, None]
    eye_c = jnp.eye(S5_GROUP, dtype=F32)[None, None, None, :, :]
    toep = toep + eye_t * eye_c * d_skip.reshape(S5_GROUPS, 1, 1, S5_GROUP, 1)
    w_intra = toep.transpose(0, 1, 4, 2, 3).reshape(S5_GROUPS, t * S5_GROUP, t * S5_GROUP)
    mf = jnp.einsum('gcp,gtp->gptc', ccf, pwf[:, 1:t + 1, :])
    mb = jnp.einsum('gcp,gtp->gptc', ccb, pwb[:, t:0:-1, :])
    w_state = jnp.concatenate([jnp.real(mf), jnp.real(mb), -jnp.imag(mf), -jnp.imag(mb)], axis=1)
    w_state = w_state.reshape(S5_GROUPS, 4 * S5_STATE, t * S5_GROUP)
    w3 = jnp.concatenate([w_intra, w_state], axis=1)
    lam_r = jnp.concatenate([jnp.real(lamp[0]), jnp.real(lamp[1])], axis=-1)
    lam_i = jnp.concatenate([jnp.imag(lamp[0]), jnp.imag(lamp[1])], axis=-1)
    lamp = jnp.stack([lam_r, lam_i], axis=2)
    return w1.astype(BF16), w3.astype(BF16), lamp.astype(F32)


def _s5_kernel(u_ref, w1_ref, w3_ref, lam_ref, y_ref, *, nc, width, levels):
    u = u_ref[0, 0]
    loc = jnp.dot(u, w1_ref[0], preferred_element_type=F32)
    re = loc[:, :2 * S5_STATE]
    im = loc[:, 2 * S5_STATE:]
    row = lax.broadcasted_iota(jnp.int32, (nc, 2 * S5_STATE), 0)
    fwd = lax.broadcasted_iota(jnp.int32, (nc, 2 * S5_STATE), 1) < S5_STATE

    def shifted(x, d):
        down = jnp.where(row >= d, pltpu.roll(x, d, 0), 0.0)
        up = jnp.where(row < nc - d, pltpu.roll(x, nc - d, 0), 0.0)
        return jnp.where(fwd, down, up)

    for k in range(levels):
        d = 2 ** k
        if d >= nc:
            break
        lr = lam_ref[0, k, 0:1, :]
        li = lam_ref[0, k, 1:2, :]
        sr = shifted(re, d)
        si = shifted(im, d)
        re, im = re + sr * lr - si * li, im + sr * li + si * lr
    ent = jnp.concatenate([shifted(re, 1), shifted(im, 1)], axis=1)
    y = jnp.dot(u, w3_ref[0, :width, :], preferred_element_type=F32)
    y = y + jnp.dot(ent.astype(BF16), w3_ref[0, width:, :], preferred_element_type=F32)
    y_ref[0, 0] = y


def _s5_ssm(u5, w1, w3, lamp):
    g, b, nc, width = u5.shape
    levels = lamp.shape[1]
    kern = functools.partial(_s5_kernel, nc=nc, width=width, levels=levels)
    return pl.pallas_call(
        kern,
        grid=(g, b),
        in_specs=[pl.BlockSpec((1, 1, nc, width), lambda gi, bi: (gi, bi, 0, 0)),
                  pl.BlockSpec((1, width, 4 * S5_STATE), lambda gi, bi: (gi, 0, 0)),
                  pl.BlockSpec((1, width + 4 * S5_STATE, width), lambda gi, bi: (gi, 0, 0)),
                  pl.BlockSpec((1, levels, 2, 2 * S5_STATE), lambda gi, bi: (gi, 0, 0, 0))],
        out_specs=pl.BlockSpec((1, 1, nc, width), lambda gi, bi: (gi, bi, 0, 0)),
        out_shape=jax.ShapeDtypeStruct((g, b, nc, width), F32),
        compiler_params=_cparams("parallel", "parallel"),
        name="s5_ssm",
    )(u5, w1, w3, lamp)


def _rope_kernel(qq_ref, kk_ref, v_ref, cos_ref, sin_ref, qa_ref, qb_ref, ko_ref, vo_ref):
    width = DF_HEADS * 2 * DF_QK
    cos = jnp.concatenate([cos_ref[...]] * (width // LANES), axis=1)
    sin = jnp.concatenate([sin_ref[...]] * (width // LANES), axis=1)
    lane = lax.broadcasted_iota(jnp.int32, (1, width), 1)
    first_half = (lane % DF_QK) < (DF_QK // 2)

    def rot(x):
        partner = jnp.where(first_half, pltpu.roll(x, width - DF_QK // 2, 1),
                            pltpu.roll(x, DF_QK // 2, 1))
        return x * cos + partner * sin

    q = rot(qq_ref[0]) * (DF_QK ** -0.5 * LOG2E)
    is_a = (lane % (2 * DF_QK)) < DF_QK
    qa_ref[0] = jnp.where(is_a, q, 0.0).astype(BF16)
    qb_ref[0] = jnp.where(is_a, 0.0, q).astype(BF16)
    ko_ref[0] = rot(kk_ref[0]).astype(BF16)
    vo_ref[0] = v_ref[0].astype(BF16)


def _rope_prep(proj, cos_t, sin_t):
    b, s, _ = proj.shape
    tile = min(TILE_ROPE, s)
    w = DF_WIDTH
    spec = lambda cb: pl.BlockSpec((1, tile, w), lambda bi, i: (bi, i, cb))
    out = pl.BlockSpec((1, tile, w), lambda bi, i: (bi, i, 0))
    tab = pl.BlockSpec((tile, LANES), lambda bi, i: (i, 0))
    return pl.pallas_call(
        _rope_kernel,
        grid=(b, s // tile),
        in_specs=[spec(COL_QQ // w), spec(COL_KK // w), spec(COL_DV // w), tab, tab],
        out_specs=[out, out, out, out],
        out_shape=[jax.ShapeDtypeStruct((b, s, w), BF16)] * 4,
        compiler_params=_cparams("parallel", "parallel"),
        name="rope_prep",
    )(proj, proj, proj, cos_t, sin_t)


def _diff_attn_kernel(qa_ref, qb_ref, k_ref, v_ref, lq1_ref, lk1_ref, lq2_ref, lk2_ref, ng_ref,
                      o_ref, s_ref, *, tq, tk, nk, lam_init):
    q2 = (qa_ref[0], qb_ref[0])
    nt_dims = (((1,), (1,)), ((), ()))

    def fold(x, op, init):
        for j in range(tk // LANES):
            init = op(init, x[:, j * LANES:(j + 1) * LANES])
        return init

    m = [jnp.full((tq, LANES), -jnp.inf, F32)] * 2
    for c in range(nk):
        kc = k_ref[0, c * tk:(c + 1) * tk, :]
        for i in range(2):
            sc = lax.dot_general(q2[i], kc, nt_dims, preferred_element_type=F32)
            s_ref[c, i * tq:(i + 1) * tq, :] = sc
            m[i] = fold(sc, jnp.maximum, m[i])
    m = [jnp.max(x, axis=-1, keepdims=True) for x in m]

    acc = [jnp.zeros((tq, 2 * DF_V), F32)] * 2
    ones = jnp.ones((tk, DF_V), BF16)
    for c in range(nk):
        vc = jnp.concatenate([v_ref[0, c * tk:(c + 1) * tk, :], ones], axis=1)
        for i in range(2):
            e = jnp.exp2((s_ref[c, i * tq:(i + 1) * tq, :] - m[i]).astype(BF16))
            acc[i] = acc[i] + jnp.dot(e, vc, preferred_element_type=F32)
    lam = (jnp.exp(jnp.sum(lq1_ref[...] * lk1_ref[...], axis=-1, keepdims=True))
           - jnp.exp(jnp.sum(lq2_ref[...] * lk2_ref[...], axis=-1, keepdims=True)) + lam_init)
    acc = (acc[0][:, :DF_V] * (1.0 / acc[0][:, DF_V:DF_V + 1])
           - acc[1][:, :DF_V] * (lam / acc[1][:, DF_V:DF_V + 1]))
    ms = jnp.mean(acc * acc, axis=-1, keepdims=True)
    o_ref[0] = acc * lax.rsqrt(ms + RMS_EPS) * ng_ref[...] * (1.0 - lam_init)


def _diff_attn(qa, qb, kk, vv, lq1, lk1, lq2, lk2, norm_g, lam_init):
    b, s, _ = qa.shape
    tq = min(TILE_Q, s)
    tk = min(TILE_K, s)
    nk = s // tk
    kern = functools.partial(_diff_attn_kernel, tq=tq, tk=tk, nk=nk, lam_init=lam_init)
    qspec = pl.BlockSpec((1, tq, LANES), lambda bi, h, i: (bi, i, h))
    kspec = pl.BlockSpec((1, s, LANES), lambda bi, h, i: (bi, 0, h))
    vec64 = pl.BlockSpec((1, DF_QK), lambda bi, h, i: (0, 0))
    return pl.pallas_call(
        kern,
        grid=(b, DF_HEADS, s // tq),
        in_specs=[qspec, qspec, kspec, kspec, vec64, vec64, vec64, vec64,
                  pl.BlockSpec((1, DF_V), lambda bi, h, i: (0, 0))],
        out_specs=pl.BlockSpec((1, tq, DF_V), lambda bi, h, i: (bi, i, h)),
        out_shape=jax.ShapeDtypeStruct((b, s, DF_WIDTH), F32),
        scratch_shapes=[pltpu.VMEM((nk, 2 * tq, tk), F32)],
        compiler_params=_cparams("parallel", "parallel", "arbitrary"),
        name="diff_attn",
    )(qa, qb, kk, vv, lq1, lk1, lq2, lk2, norm_g)


def _layer_norm(x, g, b):
    mu = jnp.mean(x, axis=-1, keepdims=True)
    xc = x - mu
    var = jnp.mean(xc * xc, axis=-1, keepdims=True)
    return xc * lax.rsqrt(var + LN_EPS) * g + b


def _merge_kernel(x_ref, of_ref, ob_ref, z_ref, y5_ref, oc_ref, g0_ref, g1_ref, g2_ref,
                  bgate_ref, dnn_ref, gluw_ref, glub_ref, wbr_ref, wout_ref, lng_ref, lnb_ref,
                  o_ref):
    o = of_ref[0] + ob_ref[0]
    z = z_ref[0]
    parts = []
    for h in range(DN_HEADS):
        oh = o[:, h * DN_DIM:(h + 1) * DN_DIM]
        ms = jnp.mean(oh * oh, axis=-1, keepdims=True)
        parts.append(oh * lax.rsqrt(ms + RMS_EPS) * dnn_ref[...])
    o_a = jnp.concatenate(parts, axis=1) * _silu(z)
    y = jax.nn.gelu(y5_ref[0])
    o_b = y * _sigmoid(_bdot(y, gluw_ref[...]) + glub_ref[...])
    o_c = oc_ref[0]
    bg = bgate_ref[...]
    merged = (_sigmoid(g0_ref[0] + bg[:, 0:D_MODEL]) * _bdot(o_a, wbr_ref[0])
              + _sigmoid(g1_ref[0] + bg[:, D_MODEL:2 * D_MODEL]) * _bdot(o_b, wbr_ref[1])
              + _sigmoid(g2_ref[0] + bg[:, 2 * D_MODEL:]) * _bdot(o_c, wbr_ref[2]))
    hres = DEEPNORM_ALPHA * x_ref[0] + _bdot(merged, wout_ref[...])
    o_ref[0] = _layer_norm(hres, lng_ref[...], lnb_ref[...])


def _merge(x, o_f, o_b, proj, y5, o_c, b_gate, dn_norm, glu_w, glu_b, w_branch, w_out, ln_g, ln_b):
    b, s, _ = x.shape
    tile = min(TILE_MERGE, s)
    row = lambda w, cb=0: pl.BlockSpec((1, tile, w), lambda bi, i: (bi, i, cb))
    full = lambda shape: pl.BlockSpec(shape, lambda bi, i: (0,) * len(shape))
    gcb = COL_GATE // D_MODEL
    return pl.pallas_call(
        _merge_kernel,
        grid=(b, s // tile),
        in_specs=[row(D_MODEL), row(DN_WIDTH), row(DN_WIDTH), row(DN_WIDTH, COL_DN_Z // DN_WIDTH),
                  row(S5_WIDTH), row(DF_WIDTH),
                  row(D_MODEL, gcb), row(D_MODEL, gcb + 1), row(D_MODEL, gcb + 2),
                  full((1, N_BRANCH * D_MODEL)), full((1, DN_DIM)),
                  full((S5_WIDTH, S5_WIDTH)), full((1, S5_WIDTH)),
                  full((N_BRANCH, DN_WIDTH, D_MODEL)), full((D_MODEL, D_MODEL)),
                  full((1, D_MODEL)), full((1, D_MODEL))],
        out_specs=row(D_MODEL),
        out_shape=jax.ShapeDtypeStruct((b, s, D_MODEL), F32),
        compiler_params=_cparams("parallel", "parallel"),
        name="merge",
    )(x, o_f, o_b, proj, y5, o_c, proj, proj, proj, b_gate, dn_norm, glu_w, glu_b,
      w_branch, w_out, ln_g, ln_b)


def _xattn_kernel(x_ref, kv_ref, wq_ref, wo_ref, lng_ref, lnb_ref, o_ref):
    x = x_ref[0]
    q = _bdot(x, wq_ref[...])
    outs = []
    for h in range(XA_HEADS):
        qh = q[:, h * XA_DIM:(h + 1) * XA_DIM]
        kh = kv_ref[0, :, h * XA_DIM:(h + 1) * XA_DIM]
        vh = kv_ref[0, :, XA_WIDTH + h * XA_DIM:XA_WIDTH + (h + 1) * XA_DIM]
        sc = _bdot_nt(qh, kh) * (XA_DIM ** -0.5)
        sc = sc - jnp.max(sc, axis=-1, keepdims=True)
        e = jnp.exp(sc)
        pr = e / jnp.sum(e, axis=-1, keepdims=True)
        outs.append(_bdot(pr, vh))
    o = jnp.concatenate(outs, axis=1)
    hres = DEEPNORM_ALPHA * x + _bdot(o, wo_ref[...])
    o_ref[0] = _layer_norm(hres, lng_ref[...], lnb_ref[...])


def _xattn(x, kv, wq, wo, ln_g, ln_b):
    b, s, _ = x.shape
    tile = min(TILE_XA, s)
    mlen = kv.shape[1]
    row = pl.BlockSpec((1, tile, D_MODEL), lambda bi, i: (bi, i, 0))
    full = lambda shape: pl.BlockSpec(shape, lambda bi, i: (0,) * len(shape))
    return pl.pallas_call(
        _xattn_kernel,
        grid=(b, s // tile),
        in_specs=[row, pl.BlockSpec((1, mlen, 2 * XA_WIDTH), lambda bi, i: (bi, 0, 0)),
                  full((D_MODEL, XA_WIDTH)), full((XA_WIDTH, D_MODEL)),
                  full((1, D_MODEL)), full((1, D_MODEL))],
        out_specs=row,
        out_shape=jax.ShapeDtypeStruct((b, s, D_MODEL), F32),
        compiler_params=_cparams("parallel", "parallel"),
        name="xattn",
    )(x, kv, wq, wo, ln_g, ln_b)


def _router_kernel(x_ref, rw_ref, rb_ref, idx_ref, gate_ref, xb_ref):
    x = x_ref[...]
    xb_ref[...] = x.astype(BF16)
    logits = lax.dot_general(rw_ref[...], x, (((1,), (1,)), ((), ())), precision=HI,
                             preferred_element_type=F32) + rb_ref[...]
    tn = logits.shape[1]
    eid = lax.broadcasted_iota(jnp.int32, (N_EXPERTS, tn), 0)
    vals, ids = [], []
    cur = logits
    for _ in range(TOP_K):
        m = jnp.max(cur, axis=0, keepdims=True)
        sel = jnp.min(jnp.where(cur == m, eid, N_EXPERTS), axis=0, keepdims=True)
        vals.append(m)
        ids.append(sel)
        cur = jnp.where(eid == sel, -jnp.inf, cur)
    top_v = jnp.concatenate(vals, axis=0)
    e = jnp.exp(top_v - top_v[0:1])
    gate_ref[...] = e / jnp.sum(e, axis=0, keepdims=True)
    idx_ref[...] = jnp.concatenate(ids, axis=0)


def _router(x2d, router_w_t, router_b):
    n, _ = x2d.shape
    tile = min(TILE_ROUTER, n)
    return pl.pallas_call(
        _router_kernel,
        grid=(n // tile,),
        in_specs=[pl.BlockSpec((tile, D_MODEL), lambda i: (i, 0)),
                  pl.BlockSpec((N_EXPERTS, D_MODEL), lambda i: (0, 0)),
                  pl.BlockSpec((N_EXPERTS, 1), lambda i: (0, 0))],
        out_specs=[pl.BlockSpec((TOP_K, tile), lambda i: (0, i)),
                   pl.BlockSpec((TOP_K, tile), lambda i: (0, i)),
                   pl.BlockSpec((tile, D_MODEL), lambda i: (i, 0))],
        out_shape=[jax.ShapeDtypeStruct((TOP_K, n), jnp.int32),
                   jax.ShapeDtypeStruct((TOP_K, n), F32),
                   jax.ShapeDtypeStruct((n, D_MODEL), BF16)],
        compiler_params=_cparams("parallel"),
        name="router",
    )(x2d, router_w_t, router_b)


def _rank_kernel(idx_ref, rank_ref, cnt_ref, run_ref, *, tile):
    @pl.when(pl.program_id(0) == 0)
    def _():
        run_ref[...] = jnp.zeros_like(run_ref)

    idx = idx_ref[...]
    eid = lax.broadcasted_iota(jnp.int32, (N_EXPERTS, tile), 0)
    si = lax.broadcasted_iota(jnp.int32, (tile, tile), 0)
    ti = lax.broadcasted_iota(jnp.int32, (tile, tile), 1)
    before = (si < ti).astype(BF16)
    run = run_ref[...]
    ranks = []
    for k in range(TOP_K):
        onehot = eid == idx[k:k + 1]
        oh = onehot.astype(BF16)
        within = jnp.dot(oh, before, preferred_element_type=F32)
        ranks.append(jnp.sum(jnp.where(onehot, within + run, 0.0), axis=0, keepdims=True))
        run = run + jnp.sum(onehot.astype(F32), axis=1, keepdims=True)
    rank_ref[...] = jnp.concatenate(ranks, axis=0).astype(jnp.int32)
    run_ref[...] = run
    cnt_ref[...] = run.astype(jnp.int32)


def _rank(top_i):
    _, n = top_i.shape
    tile = min(TILE_ROUTER, n)
    kern = functools.partial(_rank_kernel, tile=tile)
    return pl.pallas_call(
        kern,
        grid=(n // tile,),
        in_specs=[pl.BlockSpec((TOP_K, tile), lambda i: (0, i))],
        out_specs=[pl.BlockSpec((TOP_K, tile), lambda i: (0, i)),
                   pl.BlockSpec((N_EXPERTS, 1), lambda i: (0, 0))],
        out_shape=[jax.ShapeDtypeStruct((TOP_K, n), jnp.int32),
                   jax.ShapeDtypeStruct((N_EXPERTS, 1), jnp.int32)],
        scratch_shapes=[pltpu.VMEM((N_EXPERTS, 1), F32)],
        compiler_params=_cparams("arbitrary"),
        name="moe_rank",
    )(top_i)


def _expert_kernel(be_ref, on_ref, x_ref, wg_ref, wu_ref, bg_ref, bu_ref, wd_ref, bd_ref, o_ref,
                   wd_bf_ref):
    i = pl.program_id(0)
    live = on_ref[i] != 0

    @pl.when(jnp.logical_or(i == 0, be_ref[i] != be_ref[jnp.maximum(i - 1, 0)]))
    def _():
        wd_bf_ref[...] = wd_ref[0, 0].astype(BF16)

    @pl.when(live)
    def _():
        x = x_ref[...]
        hg = jnp.dot(x, wg_ref[0], preferred_element_type=F32) + bg_ref[0]
        hu = jnp.dot(x, wu_ref[0], preferred_element_type=F32) + bu_ref[0]
        gt = jnp.minimum(hg, SWIGLU_LIMIT)
        up = jnp.clip(hu, -SWIGLU_LIMIT, SWIGLU_LIMIT)
        act = (up + 1.0) * gt * _sigmoid(SWIGLU_ALPHA * gt)
        y = jnp.dot(act.astype(BF16), wd_bf_ref[...], preferred_element_type=F32) + bd_ref[0]
        o_ref[...] = y.astype(o_ref.dtype)

    @pl.when(jnp.logical_not(live))
    def _():
        o_ref[...] = jnp.zeros_like(o_ref)


def _experts(x_rows, blk_e, blk_on, w_g, w_u, b_g, b_u, w_d, l, b_d, *, block):
    rows, _ = x_rows.shape
    wspec = lambda k, m: pl.BlockSpec((1, k, m), lambda i, be, on: (be[i], 0, 0))
    grid_spec = pltpu.PrefetchScalarGridSpec(
        num_scalar_prefetch=2,
        grid=(rows // block,),
        in_specs=[pl.BlockSpec((block, D_MODEL), lambda i, be, on: (i, 0)),
                  wspec(D_MODEL, D_FF), wspec(D_MODEL, D_FF), wspec(1, D_FF), wspec(1, D_FF),
                  pl.BlockSpec((1, 1, D_FF, D_MODEL), lambda i, be, on: (l, be[i], 0, 0)),
                  wspec(1, D_MODEL)],
        out_specs=pl.BlockSpec((block, D_MODEL), lambda i, be, on: (i, 0)),
        scratch_shapes=[pltpu.VMEM((D_FF, D_MODEL), BF16)],
    )
    return pl.pallas_call(
        _expert_kernel,
        grid_spec=grid_spec,
        out_shape=jax.ShapeDtypeStruct((rows, D_MODEL), BF16),
        compiler_params=_cparams("arbitrary"),
        name="moe_experts",
    )(blk_e, blk_on, x_rows, w_g, w_u, b_g, b_u, w_d, b_d)


def _combine_kernel(x_ref, y_ref, gate_ref, lng_ref, lnb_ref, o_ref):
    gate = gate_ref[...]
    h = y_ref[0].astype(F32) * gate[:, 0:1]
    for k in range(1, TOP_K):
        h = h + y_ref[k].astype(F32) * gate[:, k:k + 1]
    o_ref[...] = _layer_norm(DEEPNORM_ALPHA * x_ref[...] + h, lng_ref[...], lnb_ref[...])


def _combine(x2d, y_g, gate_t, ln_g, ln_b):
    n, _ = x2d.shape
    tile = min(TILE_COMBINE, n)
    return pl.pallas_call(
        _combine_kernel,
        grid=(n // tile,),
        in_specs=[pl.BlockSpec((tile, D_MODEL), lambda i: (i, 0)),
                  pl.BlockSpec((TOP_K, tile, D_MODEL), lambda i: (0, i, 0)),
                  pl.BlockSpec((tile, TOP_K), lambda i: (i, 0)),
                  pl.BlockSpec((1, D_MODEL), lambda i: (0, 0)),
                  pl.BlockSpec((1, D_MODEL), lambda i: (0, 0))],
        out_specs=pl.BlockSpec((tile, D_MODEL), lambda i: (i, 0)),
        out_shape=jax.ShapeDtypeStruct((n, D_MODEL), F32),
        compiler_params=_cparams("parallel"),
        name="moe_combine",
    )(x2d, y_g, gate_t, ln_g, ln_b)


def _split_kernel(w_ref, se_ref, so_ref, g_ref, u_ref):
    w = w_ref[0, 0].astype(BF16)
    g_ref[0] = jnp.dot(w, se_ref[...], preferred_element_type=F32).astype(BF16)
    u_ref[0] = jnp.dot(w, so_ref[...], preferred_element_type=F32).astype(BF16)


def _split_gate_up(w_gu, l):
    _, e, dm, f2 = w_gu.shape
    f = f2 // 2
    tr = 256
    col = jnp.arange(f2, dtype=jnp.int32)[:, None]
    out = jnp.arange(f, dtype=jnp.int32)[None, :]
    sel_even = (col == 2 * out).astype(BF16)
    sel_odd = (col == 2 * out + 1).astype(BF16)
    sel = pl.BlockSpec((f2, f), lambda ei, i: (0, 0))
    ospec = pl.BlockSpec((1, tr, f), lambda ei, i: (ei, i, 0))
    return pl.pallas_call(
        _split_kernel,
        grid=(e, dm // tr),
        in_specs=[pl.BlockSpec((1, 1, tr, f2), lambda ei, i: (l, ei, i, 0)), sel, sel],
        out_specs=[ospec, ospec],
        out_shape=[jax.ShapeDtypeStruct((e, dm, f), BF16)] * 2,
        compiler_params=_cparams("parallel", "parallel"),
        name="split_gate_up",
    )(w_gu, sel_even, sel_odd)


def _moe(x, w):
    b, s, dm = x.shape
    n = b * s
    x2d = x.reshape(n, dm)
    top_i, gate, xb = _router(x2d, w['router_w_t'], w['router_b'])
    rank, counts = _rank(top_i)
    block = MOE_BLOCK
    m = n * TOP_K
    nblk = -(-m // block) + N_EXPERTS
    rows = nblk * block
    counts = counts.reshape(N_EXPERTS)
    padded = (counts + block - 1) // block * block
    pad_end = jnp.cumsum(padded)
    pad_start = pad_end - padded
    dest = pad_start[top_i] + rank
    tok = jnp.broadcast_to(jnp.arange(n, dtype=jnp.int32)[None, :], (TOP_K, n))
    row_tok = jnp.full((rows,), n, jnp.int32).at[dest.reshape(-1)].set(
        tok.reshape(-1), unique_indices=True, mode='promise_in_bounds')
    blk_start = jnp.arange(nblk, dtype=pad_end.dtype) * block
    blk_e = jnp.minimum(jnp.sum(pad_end[None, :] <= blk_start[:, None], axis=1),
                        N_EXPERTS - 1).astype(jnp.int32)
    blk_on = (blk_start < pad_end[-1]).astype(jnp.int32)
    x_pad = jnp.concatenate([xb, jnp.zeros((1, dm), BF16)], axis=0)
    x_rows = x_pad.at[row_tok].get(mode='promise_in_bounds')
    y_rows = _experts(x_rows, blk_e, blk_on, w['e_w_g'], w['e_w_u'], w['e_b_g'], w['e_b_u'], w['e_w_d'],
                      w['layer'], w['e_b_d'], block=block)
    y_g = y_rows.at[dest].get(mode='promise_in_bounds')
    out = _combine(x2d, y_g, gate.T, w['ln3_g'], w['ln3_b'])
    return out.reshape(b, s, dm)


def _rope_tables(s):
    half = DF_QK // 2
    inv_freq = jnp.power(ROPE_THETA, -jnp.arange(half, dtype=F32) / half)
    ang = jnp.arange(s, dtype=F32)[:, None] * inv_freq[None, :]
    cos = jnp.cos(ang)
    sin = jnp.sin(ang)
    cos_t = jnp.concatenate([cos, cos, cos, cos], axis=1)
    sin_t = jnp.concatenate([-sin, sin, -sin, sin], axis=1)
    return cos_t, sin_t


def _prepare_layer(l, p, s):
    dm = D_MODEL
    w = {}
    w['w_in'] = _pack_in_proj(p['w_in'][l]).astype(BF16)
    pad8 = jnp.zeros((8,), F32)
    pad112 = jnp.zeros((LANES - 16,), F32)
    w['nega'] = jnp.concatenate([pad8, -jnp.exp(p['dn_A_log'][l].reshape(8)), pad112]).reshape(1, LANES)
    w['dtb'] = jnp.concatenate([pad8, p['dn_dt_bias'][l].reshape(8), pad112]).reshape(1, LANES)
    w['dn_conv'] = p['dn_conv'][l]
    t5 = min(S5_CHUNK, s)
    w['s5'] = _s5_operators(p['s5_lam_re'][l], p['s5_lam_im'][l], p['s5_log_dt'][l],
                            p['s5_B_re'][l], p['s5_B_im'][l], p['s5_C_re'][l],
                            p['s5_C_im'][l], p['s5_D'][l], chunk=t5, nc=s // t5)
    for name in ('df_lq1', 'df_lk1', 'df_lq2', 'df_lk2'):
        w[name] = p[name][l].reshape(1, DF_QK)
    w['df_norm'] = p['df_norm'][l].reshape(1, DF_V)
    w['lam_init'] = 0.8 - 0.6 * math.exp(-0.3 * l)
    w['b_gate'] = p['b_gate'][l].reshape(1, -1)
    w['dn_norm'] = p['dn_norm'][l].reshape(1, DN_DIM)
    w['s5_glu_w'] = p['s5_glu_w'][l].astype(BF16)
    w['s5_glu_b'] = p['s5_glu_b'][l].reshape(1, -1)
    w['w_branch'] = p['w_branch'][l].astype(BF16)
    w['w_out'] = p['w_out'][l].astype(BF16)
    for name in ('ln1_g', 'ln1_b', 'ln2_g', 'ln2_b', 'ln3_g', 'ln3_b'):
        w[name] = p[name][l].reshape(1, dm)
    w['xa_wkv'] = p['xa_wkv'][l].astype(BF16)
    w['xa_wq'] = p['xa_wq'][l].astype(BF16)
    w['xa_wo'] = p['xa_wo'][l].astype(BF16)
    w['router_w_t'] = p['router_w'][l].T
    w['router_b'] = p['router_b'][l].reshape(N_EXPERTS, 1)
    w['e_w_g'], w['e_w_u'] = _split_gate_up(p['e_w_gu'], l)
    w['e_b_g'] = p['e_b_gu'][l][:, None, 0::2]
    w['e_b_u'] = p['e_b_gu'][l][:, None, 1::2]
    w['e_w_d'] = p['e_w_down']
    w['layer'] = l
    w['e_b_d'] = p['e_b_down'][l][:, None, :]
    return w


def _layer(x, mem, w, tabs):
    b, s, dm = x.shape
    n = b * s
    proj = _matmul(x.reshape(n, dm), w['w_in'], tm=TILE_MM, tn=LANES * 19)
    proj = proj.reshape(b, s, PROJ_WIDTH)

    qkv, bg = _dn_prep(proj, w['dn_conv'], w['nega'], w['dtb'], chunk=DN_CHUNK)
    bgr = bg[:, :, :16].reshape(b, s // DN_CHUNK, DN_CHUNK, 16).transpose(0, 1, 3, 2)
    o_f, o_b = _dn_scan(qkv, bg, bgr, chunk=DN_CHUNK)

    t5 = min(S5_CHUNK, s)
    nc = s // t5
    u5 = proj[:, :, COL_S5_U:COL_S5_U + S5_WIDTH].reshape(b, nc, t5, S5_GROUPS, S5_GROUP)
    u5 = u5.transpose(3, 0, 1, 2, 4).reshape(S5_GROUPS, b, nc, t5 * S5_GROUP).astype(BF16)
    y5 = _s5_ssm(u5, *w['s5'])
    y5 = y5.reshape(S5_GROUPS, b, nc, t5, S5_GROUP).transpose(1, 2, 3, 0, 4).reshape(b, s, S5_WIDTH)

    qa, qb, kk, vv = _rope_prep(proj, *tabs)
    o_c = _diff_attn(qa, qb, kk, vv, w['df_lq1'], w['df_lk1'], w['df_lq2'], w['df_lk2'],
                     w['df_norm'], w['lam_init'])

    x = _merge(x, o_f, o_b, proj, y5, o_c, w['b_gate'], w['dn_norm'], w['s5_glu_w'], w['s5_glu_b'],
               w['w_branch'], w['w_out'], w['ln1_g'], w['ln1_b'])

    mlen = mem.shape[1]
    kv = _matmul(mem.reshape(b * mlen, dm), w['xa_wkv'], tm=512, tn=1024)
    x = _xattn(x, kv.reshape(b, mlen, 2 * XA_WIDTH), w['xa_wq'], w['xa_wo'], w['ln2_g'], w['ln2_b'])
    return _moe(x, w)


def kernel(x_prompt, x_sample, mem_prompt, mem_sample, w_in, b_gate, dn_conv, dn_A_log, dn_dt_bias,
           dn_norm, s5_lam_re, s5_lam_im, s5_log_dt, s5_B_re, s5_B_im, s5_C_re, s5_C_im, s5_D,
           s5_glu_w, s5_glu_b, df_lq1, df_lk1, df_lq2, df_lk2, df_norm, w_branch, w_out,
           ln1_g, ln1_b, xa_wq, xa_wkv, xa_wo, ln2_g, ln2_b, router_w, router_b,
           e_w_gu, e_b_gu, e_w_down, e_b_down, ln3_g, ln3_b):
    p = dict(w_in=w_in, b_gate=b_gate, dn_conv=dn_conv, dn_A_log=dn_A_log, dn_dt_bias=dn_dt_bias,
             dn_norm=dn_norm, s5_lam_re=s5_lam_re, s5_lam_im=s5_lam_im, s5_log_dt=s5_log_dt,
             s5_B_re=s5_B_re, s5_B_im=s5_B_im, s5_C_re=s5_C_re, s5_C_im=s5_C_im, s5_D=s5_D,
             s5_glu_w=s5_glu_w, s5_glu_b=s5_glu_b, df_lq1=df_lq1, df_lk1=df_lk1, df_lq2=df_lq2,
             df_lk2=df_lk2, df_norm=df_norm, w_branch=w_branch, w_out=w_out, ln1_g=ln1_g,
             ln1_b=ln1_b, xa_wq=xa_wq, xa_wkv=xa_wkv, xa_wo=xa_wo, ln2_g=ln2_g, ln2_b=ln2_b,
             router_w=router_w, router_b=router_b, e_w_gu=e_w_gu, e_b_gu=e_b_gu,
             e_w_down=e_w_down, e_b_down=e_b_down, ln3_g=ln3_g, ln3_b=ln3_b)
    s = x_prompt.shape[1]
    assert x_sample.shape[1] == s
    tabs = _rope_tables(s)
    layers = [_prepare_layer(l, p, s) for l in range(DEPTH)]
    nb = x_prompt.shape[0]
    x = jnp.concatenate([x_prompt, x_sample], axis=0)
    mem = jnp.concatenate([mem_prompt, mem_sample], axis=0)
    for w in layers:
        x = _layer(x, mem, w, tabs)
    return (x[:nb], x[nb:])
```

```python
import functools
import math

import numpy as np
import jax
import jax.numpy as jnp
from jax import lax
from jax.experimental import pallas as pl
from jax.experimental.pallas import tpu as pltpu

F32 = jnp.float32
BF16 = jnp.bfloat16
HI = lax.Precision.HIGHEST

D_MODEL = 1024
DN_HEADS = 4
DN_DIM = 128
DN_WIDTH = DN_HEADS * DN_DIM
DN_CONV = 4
S5_GROUP = 16
S5_WIDTH = 512
S5_GROUPS = S5_WIDTH // S5_GROUP
S5_STATE = 64
DF_HEADS = 4
DF_QK = 64
DF_V = 128
DF_WIDTH = DF_HEADS * DF_V
ROPE_THETA = 10000.0
XA_HEADS = 4
XA_DIM = 128
XA_WIDTH = XA_HEADS * XA_DIM
N_EXPERTS = 32
TOP_K = 4
D_FF = 1024
SWIGLU_LIMIT = 7.0
SWIGLU_ALPHA = 1.702
N_BRANCH = 3
DEPTH = 2
DEEPNORM_ALPHA = (2 * DEPTH) ** 0.25
LN_EPS = 1e-5
RMS_EPS = 1e-6
LOG2E = 1.4426950408889634

LANES = 128
SUBLANES = 8
VMEM_LIMIT = 56 * 1024 * 1024

COL_DN_QKV = 0
COL_DN_Z = 1536
COL_S5_U = 2048
COL_QQ = 2560
COL_KK = 3072
COL_DV = 3584
COL_GATE = 4096
COL_BA = 7168
PROJ_WIDTH = 7296

TILE_MM = 1024
TILE_DN_PREP = 512
TILE_DN_SCAN = 256
DN_CHUNK = 64
DN_CHUNKS_PER_TRIP = 2
S5_CHUNK = 32
TILE_ROPE = 512
TILE_Q = 256
TILE_K = 512
TILE_MERGE = 256
TILE_XA = 512
TILE_ROUTER = 512
MOE_BLOCK = 512
TILE_COMBINE = 256


def _in_proj_perm():
    o_dn_b, o_dn_a, o_s5 = 2048, 2056, 2064
    o_q1, o_q2, o_k1, o_k2, o_dv, o_gate = 2576, 2832, 3088, 3344, 3600, 4112
    perm = np.full((PROJ_WIDTH,), -1, np.int64)
    perm[0:2048] = np.arange(2048)
    perm[COL_S5_U:COL_S5_U + 512] = o_s5 + np.arange(512)
    for h in range(DF_HEADS):
        j = np.arange(DF_QK)
        perm[COL_QQ + h * 128 + j] = o_q1 + h * DF_QK + j
        perm[COL_QQ + h * 128 + 64 + j] = o_q2 + h * DF_QK + j
        perm[COL_KK + h * 128 + j] = o_k1 + h * DF_QK + j
        perm[COL_KK + h * 128 + 64 + j] = o_k2 + h * DF_QK + j
    perm[COL_DV:COL_DV + 512] = o_dv + np.arange(512)
    perm[COL_GATE:COL_GATE + 3072] = o_gate + np.arange(3072)
    perm[COL_BA:COL_BA + 8] = o_dn_b + np.arange(8)
    perm[COL_BA + 8:COL_BA + 16] = o_dn_a + np.arange(8)
    return perm


_IN_PERM = _in_proj_perm()


def _pack_in_proj(w_in):
    runs, start = [], 0
    for i in range(1, PROJ_WIDTH + 1):
        if i == PROJ_WIDTH or _IN_PERM[i] != _IN_PERM[i - 1] + (1 if _IN_PERM[i - 1] >= 0 else 0):
            runs.append((start, i))
            start = i
    pieces = []
    for a, b in runs:
        src = int(_IN_PERM[a])
        pieces.append(jnp.zeros((w_in.shape[0], b - a), w_in.dtype) if src < 0
                      else w_in[:, src:src + (b - a)])
    return jnp.concatenate(pieces, axis=1)


def _cparams(*sem):
    return pltpu.CompilerParams(dimension_semantics=sem, vmem_limit_bytes=VMEM_LIMIT)


def _bdot(a, b):
    return jnp.dot(a.astype(BF16), b.astype(BF16), preferred_element_type=F32)


def _bdot_nt(a, b):
    return lax.dot_general(a.astype(BF16), b.astype(BF16), (((1,), (1,)), ((), ())),
                           preferred_element_type=F32)


def _bdot_tn(a, b):
    return lax.dot_general(a.astype(BF16), b.astype(BF16), (((0,), (0,)), ((), ())),
                           preferred_element_type=F32)


def _sigmoid(x):
    return 1.0 / (1.0 + jnp.exp(-x))


def _silu(x):
    return x * _sigmoid(x)


def _mm_kernel(x_ref, w_ref, o_ref):
    o_ref[...] = jnp.dot(x_ref[...].astype(BF16), w_ref[...],
                         preferred_element_type=F32).astype(o_ref.dtype)


def _matmul(x, w, *, tm, tn, out_dtype=F32):
    n, k = x.shape
    m = w.shape[1]
    tm = min(tm, n)
    tn = min(tn, m)
    return pl.pallas_call(
        _mm_kernel,
        grid=(n // tm, m // tn),
        in_specs=[pl.BlockSpec((tm, k), lambda i, j: (i, 0)),
                  pl.BlockSpec((k, tn), lambda i, j: (0, j))],
        out_specs=pl.BlockSpec((tm, tn), lambda i, j: (i, j)),
        out_shape=jax.ShapeDtypeStruct((n, m), out_dtype),
        compiler_params=_cparams("parallel", "arbitrary"),
        name="matmul",
    )(x, w)


def _dn_prep_kernel(x_ref, prev_ref, next_ref, ba_ref, cw_ref, nega_ref, dtb_ref,
                    qkv_ref, bg_ref, *, tile, chunk):
    i = pl.program_id(1)
    last = pl.num_programs(1) - 1
    x = x_ref[0]
    prev = jnp.where(i > 0, prev_ref[0], 0.0)
    nxt = jnp.where(i < last, next_ref[0], 0.0)
    ext = jnp.concatenate([prev, x, nxt], axis=0)
    cw = cw_ref[...]
    acc = cw[0:1] * ext[6:6 + tile]
    for j in range(1, DN_CONV):
        acc = acc + cw[j:j + 1] * ext[6 + j:6 + j + tile]
    act = _silu(acc)
    for h in range(DN_HEADS):
        lo = h * DN_DIM
        qh = act[:, lo:lo + DN_DIM]
        qn = qh * lax.rsqrt(jnp.sum(qh * qh, axis=-1, keepdims=True) + 1e-6)
        qkv_ref[0, :, lo:lo + DN_DIM] = qn * (DN_DIM ** -0.5)
        kh = act[:, DN_WIDTH + lo:DN_WIDTH + lo + DN_DIM]
        qkv_ref[0, :, DN_WIDTH + lo:DN_WIDTH + lo + DN_DIM] = kh * lax.rsqrt(
            jnp.sum(kh * kh, axis=-1, keepdims=True) + 1e-6)
    qkv_ref[0, :, 2 * DN_WIDTH:] = act[:, 2 * DN_WIDTH:]

    ba = ba_ref[0]
    beta = _sigmoid(ba)
    z = ba + dtb_ref[...]
    softplus = jnp.maximum(z, 0.0) + jnp.log(1.0 + jnp.exp(-jnp.abs(z)))
    g = nega_ref[...] * softplus
    row = lax.broadcasted_iota(jnp.int32, (chunk, chunk), 0)
    col = lax.broadcasted_iota(jnp.int32, (chunk, chunk), 1)
    lower = (row >= col).astype(F32)
    upper = (row <= col).astype(F32)
    lane = lax.broadcasted_iota(jnp.int32, (chunk, LANES), 1)
    fwd_lane = lane < 8 + DN_HEADS
    for c in range(tile // chunk):
        gc = g[c * chunk:(c + 1) * chunk]
        cf = jnp.dot(lower, gc, precision=HI, preferred_element_type=F32)
        cb = jnp.dot(upper, gc, precision=HI, preferred_element_type=F32)
        gcum = jnp.where(fwd_lane, cf, cb)
        bg_ref[0, c * chunk:(c + 1) * chunk, :] = jnp.where(
            lane < 8, beta[c * chunk:(c + 1) * chunk], gcum)


def _dn_prep(proj, conv_w, nega, dtb, *, chunk):
    b, s, _ = proj.shape
    tile = min(TILE_DN_PREP, s)
    nt = s // tile
    r8 = tile // SUBLANES
    kern = functools.partial(_dn_prep_kernel, tile=tile, chunk=chunk)
    return pl.pallas_call(
        kern,
        grid=(b, nt),
        in_specs=[
            pl.BlockSpec((1, tile, 3 * DN_WIDTH), lambda bi, i: (bi, i, 0)),
            pl.BlockSpec((1, SUBLANES, 3 * DN_WIDTH),
                         lambda bi, i: (bi, jnp.maximum(i * r8 - 1, 0), 0)),
            pl.BlockSpec((1, SUBLANES, 3 * DN_WIDTH),
                         lambda bi, i: (bi, jnp.minimum((i + 1) * r8, s // SUBLANES - 1), 0)),
            pl.BlockSpec((1, tile, LANES), lambda bi, i: (bi, i, COL_BA // LANES)),
            pl.BlockSpec((DN_CONV, 3 * DN_WIDTH), lambda bi, i: (0, 0)),
            pl.BlockSpec((1, LANES), lambda bi, i: (0, 0)),
            pl.BlockSpec((1, LANES), lambda bi, i: (0, 0)),
        ],
        out_specs=[pl.BlockSpec((1, tile, 3 * DN_WIDTH), lambda bi, i: (bi, i, 0)),
                   pl.BlockSpec((1, tile, LANES), lambda bi, i: (bi, i, 0))],
        out_shape=[jax.ShapeDtypeStruct((b, s, 3 * DN_WIDTH), F32),
                   jax.ShapeDtypeStruct((b, s, LANES), F32)],
        compiler_params=_cparams("parallel", "parallel"),
        name="dn_prep",
    )(proj, proj, proj, proj, conv_w, nega, dtb)


def _dn_local(chains, chunk):
    ri = lax.broadcasted_iota(jnp.int32, (chunk, chunk), 0)
    ci = lax.broadcasted_iota(jnp.int32, (chunk, chunk), 1)
    pre = []
    for d, h, q, k, v, bgc, rows in chains:
        li = d * DN_HEADS + h
        beta = bgc[:, li:li + 1]
        gcol = bgc[:, 8 + li:8 + li + 1]
        grow = rows[8 + li:8 + li + 1, :]
        gtot = gcol[chunk - 1:chunk] if d == 0 else gcol[0:1]
        incl = (ri >= ci) if d == 0 else (ri <= ci)
        strict = (ri > ci) if d == 0 else (ri < ci)
        eg = jnp.exp(gcol)
        kb = k * beta
        decay = jnp.where(incl, jnp.exp(jnp.where(incl, gcol - grow, 0.0)), 0.0)
        pre.append(dict(k=k, qg=q * eg, kd=k * jnp.exp(gtot - gcol), decay=decay, strict=strict,
                        x2=jnp.concatenate([v * beta, kb * eg], axis=1), sdec=jnp.exp(gtot),
                        qkb=jnp.concatenate([q, kb], axis=0)))
    qk = [_bdot_nt(c['qkb'], c['k']) for c in pre]
    intra = [x[:chunk] * c['decay'] for x, c in zip(qk, pre)]
    p = [-jnp.where(c['strict'], x[chunk:] * c['decay'], 0.0) for x, c in zip(qk, pre)]
    r = p
    p = [_bdot(x, x) for x in p]
    for _ in range(int(math.log2(chunk)) - 2):
        rp = [_bdot(jnp.concatenate([ri_, pi_], axis=0), pi_) for ri_, pi_ in zip(r, p)]
        r = [ri_ + pi_ + x[:chunk] for ri_, pi_, x in zip(r, p, rp)]
        p = [x[chunk:] for x in rp]
    r = [ri_ + pi_ + _bdot(ri_, pi_) for ri_, pi_ in zip(r, p)]
    uw = [c['x2'] + _bdot(ri_, c['x2']) for ri_, c in zip(r, pre)]
    return [dict(u=x[:, :DN_DIM], wq=jnp.concatenate([x[:, DN_DIM:], c['qg']], axis=0),
                 intra=it, kd=c['kd'], sdec=c['sdec']) for x, it, c in zip(uw, intra, pre)]


def _dn_state(local, states, chunk):
    ws = [_bdot(c['wq'], st) for c, st in zip(local, states)]
    v_new = [c['u'] - y[:chunk] for c, y in zip(local, ws)]
    out = [y[chunk:] + _bdot(c['intra'], vn) for y, c, vn in zip(ws, local, v_new)]
    st_new = [st * c['sdec'] + _bdot_tn(c['kd'], vn) for st, c, vn in zip(states, local, v_new)]
    return out, st_new


def _dn_scan_kernel(xf_ref, xb_ref, gf_ref, gb_ref, rf_ref, rb_ref, of_ref, ob_ref, st_ref,
                    *, tile, chunk):
    @pl.when(pl.program_id(1) == 0)
    def _():
        st_ref[...] = jnp.zeros_like(st_ref)

    nct = tile // chunk
    nchain = 2 * DN_HEADS

    def body(j, carry):
        chains, where = [], []
        for slot in range(DN_CHUNKS_PER_TRIP):
            for d in range(2):
                c = DN_CHUNKS_PER_TRIP * j + slot
                c = c if d == 0 else nct - 1 - c
                r0 = pl.multiple_of(c * chunk, chunk)
                x_ref, g_ref, r_ref = (xf_ref, gf_ref, rf_ref) if d == 0 else (xb_ref, gb_ref, rb_ref)
                bgc = g_ref[0, pl.ds(r0, chunk), :]
                rows = r_ref[0, c]
                for h in range(DN_HEADS):
                    lo = h * DN_DIM
                    q = x_ref[0, pl.ds(r0, chunk), lo:lo + DN_DIM]
                    k = x_ref[0, pl.ds(r0, chunk), DN_WIDTH + lo:DN_WIDTH + lo + DN_DIM]
                    v = x_ref[0, pl.ds(r0, chunk), 2 * DN_WIDTH + lo:2 * DN_WIDTH + lo + DN_DIM]
                    chains.append((d, h, q, k, v, bgc, rows))
                    where.append((d, h, r0))
        states = [st_ref[i] for i in range(nchain)]
        local = _dn_local(chains, chunk)
        outs = []
        for slot in range(DN_CHUNKS_PER_TRIP):
            out, states = _dn_state(local[slot * nchain:(slot + 1) * nchain], states, chunk)
            outs += out
        for (d, h, r0), out in zip(where, outs):
            o_ref = of_ref if d == 0 else ob_ref
            o_ref[0, pl.ds(r0, chunk), h * DN_DIM:(h + 1) * DN_DIM] = out
        for i in range(nchain):
            st_ref[i] = states[i]
        return carry

    lax.fori_loop(0, nct // DN_CHUNKS_PER_TRIP, body, 0)


def _dn_scan(qkv, bg, bgr, *, chunk):
    b, s, _ = qkv.shape
    tile = min(TILE_DN_SCAN, s)
    nt = s // tile
    nct = tile // chunk
    kern = functools.partial(_dn_scan_kernel, tile=tile, chunk=chunk)
    fwd = lambda bi, i: (bi, i, 0)
    bwd = lambda bi, i: (bi, nt - 1 - i, 0)
    fwd4 = lambda bi, i: (bi, i, 0, 0)
    bwd4 = lambda bi, i: (bi, nt - 1 - i, 0, 0)
    return pl.pallas_call(
        kern,
        grid=(b, nt),
        in_specs=[pl.BlockSpec((1, tile, 3 * DN_WIDTH), fwd),
                  pl.BlockSpec((1, tile, 3 * DN_WIDTH), bwd),
                  pl.BlockSpec((1, tile, LANES), fwd),
                  pl.BlockSpec((1, tile, LANES), bwd),
                  pl.BlockSpec((1, nct, 16, chunk), fwd4),
                  pl.BlockSpec((1, nct, 16, chunk), bwd4)],
        out_specs=[pl.BlockSpec((1, tile, DN_WIDTH), fwd),
                   pl.BlockSpec((1, tile, DN_WIDTH), bwd)],
        out_shape=[jax.ShapeDtypeStruct((b, s, DN_WIDTH), F32),
                   jax.ShapeDtypeStruct((b, s, DN_WIDTH), F32)],
        scratch_shapes=[pltpu.VMEM((2 * DN_HEADS, DN_DIM, DN_DIM), F32)],
        compiler_params=_cparams("parallel", "arbitrary"),
        name="dn_scan",
    )(qkv, qkv, bg, bg, bgr, bgr)


def _s5_operators(lam_re, lam_im, log_dt, b_re, b_im, c_re, c_im, d_skip, *, chunk, nc):
    t = chunk
    levels = max(1, int(math.log2(nc)))
    tau = jnp.arange(t + 1, dtype=F32)
    kern = []
    lamp = []
    for d in range(2):
        lam = lax.complex(lam_re[d], lam_im[d])
        dt = jnp.exp(log_dt[d])[:, None]
        lam_dt = lam * dt
        lam_bar = jnp.exp(lam_dt)
        bbar = ((lam_bar - 1.0) / lam)[:, :, None] * lax.complex(b_re[d], b_im[d])
        cc = lax.complex(c_re[d], c_im[d])
        pw = jnp.exp(lam_dt[:, None, :] * tau[None, :, None].astype(jnp.complex64))
        kern.append((pw, bbar, cc))
        steps = (t * 2.0 ** jnp.arange(levels, dtype=F32)).astype(jnp.complex64)
        lamp.append(jnp.exp(lam_dt[:, None, :] * steps[None, :, None]))
    (pwf, bbf, ccf), (pwb, bbb, ccb) = kern
    ef = jnp.einsum('gsp,gpc->gscp', pwf[:, t - 1::-1, :], bbf)
    eb = jnp.einsum('gsp,gpc->gscp', pwb[:, :t, :], bbb)
    w1 = jnp.concatenate([jnp.real(ef), jnp.real(eb), jnp.imag(ef), jnp.imag(eb)], axis=-1)
    w1 = w1.reshape(S5_GROUPS, t * S5_GROUP, 4 * S5_STATE)
    kf = jnp.real(jnp.einsum('gcp,gtp,gpd->gtcd', ccf, pwf[:, :t, :], bbf))
    kb = jnp.real(jnp.einsum('gcp,gtp,gpd->gtcd', ccb, pwb[:, :t, :], bbb))
    si = jnp.arange(t)[:, None]
    ti = jnp.arange(t)[None, :]
    lag_f = jnp.clip(ti - si, 0, t - 1)
    lag_b = jnp.clip(si - ti, 0, t - 1)
    toep = (jnp.where((ti >= si)[None, :, :, None, None], kf[:, lag_f], 0.0)
            + jnp.where((si >= ti)[None, :, :, None, None], kb[:, lag_b], 0.0))
    eye_t = jnp.eye(t, dtype=F32)[None, :, :, None, None]
    eye_c = jnp.eye(S5_GROUP, dtype=F32)[None, None, None, :, :]
    toep = toep + eye_t * eye_c * d_skip.reshape(S5_GROUPS, 1, 1, S5_GROUP, 1)
    w_intra = toep.transpose(0, 1, 4, 2, 3).reshape(S5_GROUPS, t * S5_GROUP, t * S5_GROUP)
    mf = jnp.einsum('gcp,gtp->gptc', ccf, pwf[:, 1:t + 1, :])
    mb = jnp.einsum('gcp,gtp->gptc', ccb, pwb[:, t:0:-1, :])
    w_state = jnp.concatenate([jnp.real(mf), jnp.real(mb), -jnp.imag(mf), -jnp.imag(mb)], axis=1)
    w_state = w_state.reshape(S5_GROUPS, 4 * S5_STATE, t * S5_GROUP)
    w3 = jnp.concatenate([w_intra, w_state], axis=1)
    lam_r = jnp.concatenate([jnp.real(lamp[0]), jnp.real(lamp[1])], axis=-1)
    lam_i = jnp.concatenate([jnp.imag(lamp[0]), jnp.imag(lamp[1])], axis=-1)
    lamp = jnp.stack([lam_r, lam_i], axis=2)
    return w1.astype(BF16), w3.astype(BF16), lamp.astype(F32)


def _s5_kernel(u_ref, w1_ref, w3_ref, lam_ref, y_ref, *, nc, width, levels):
    u = u_ref[0, 0]
    loc = jnp.dot(u, w1_ref[0], preferred_element_type=F32)
    re = loc[:, :2 * S5_STATE]
    im = loc[:, 2 * S5_STATE:]
    row = lax.broadcasted_iota(jnp.int32, (nc, 2 * S5_STATE), 0)
    fwd = lax.broadcasted_iota(jnp.int32, (nc, 2 * S5_STATE), 1) < S5_STATE

    def shifted(x, d):
        down = jnp.where(row >= d, pltpu.roll(x, d, 0), 0.0)
        up = jnp.where(row < nc - d, pltpu.roll(x, nc - d, 0), 0.0)
        return jnp.where(fwd, down, up)

    for k in range(levels):
        d = 2 ** k
        if d >= nc:
            break
        lr = lam_ref[0, k, 0:1, :]
        li = lam_ref[0, k, 1:2, :]
        sr = shifted(re, d)
        si = shifted(im, d)
        re, im = re + sr * lr - si * li, im + sr * li + si * lr
    ent = jnp.concatenate([shifted(re, 1), shifted(im, 1)], axis=1)
    y = jnp.dot(u, w3_ref[0, :width, :], preferred_element_type=F32)
    y = y + jnp.dot(ent.astype(BF16), w3_ref[0, width:, :], preferred_element_type=F32)
    y_ref[0, 0] = y


def _s5_ssm(u5, w1, w3, lamp):
    g, b, nc, width = u5.shape
    levels = lamp.shape[1]
    kern = functools.partial(_s5_kernel, nc=nc, width=width, levels=levels)
    return pl.pallas_call(
        kern,
        grid=(g, b),
        in_specs=[pl.BlockSpec((1, 1, nc, width), lambda gi, bi: (gi, bi, 0, 0)),
                  pl.BlockSpec((1, width, 4 * S5_STATE), lambda gi, bi: (gi, 0, 0)),
                  pl.BlockSpec((1, width + 4 * S5_STATE, width), lambda gi, bi: (gi, 0, 0)),
                  pl.BlockSpec((1, levels, 2, 2 * S5_STATE), lambda gi, bi: (gi, 0, 0, 0))],
        out_specs=pl.BlockSpec((1, 1, nc, width), lambda gi, bi: (gi, bi, 0, 0)),
        out_shape=jax.ShapeDtypeStruct((g, b, nc, width), F32),
        compiler_params=_cparams("parallel", "parallel"),
        name="s5_ssm",
    )(u5, w1, w3, lamp)


def _rope_kernel(qq_ref, kk_ref, v_ref, cos_ref, sin_ref, qa_ref, qb_ref, ko_ref, vo_ref):
    width = DF_HEADS * 2 * DF_QK
    cos = jnp.concatenate([cos_ref[...]] * (width // LANES), axis=1)
    sin = jnp.concatenate([sin_ref[...]] * (width // LANES), axis=1)
    lane = lax.broadcasted_iota(jnp.int32, (1, width), 1)
    first_half = (lane % DF_QK) < (DF_QK // 2)

    def rot(x):
        partner = jnp.where(first_half, pltpu.roll(x, width - DF_QK // 2, 1),
                            pltpu.roll(x, DF_QK // 2, 1))
        return x * cos + partner * sin

    q = rot(qq_ref[0]) * (DF_QK ** -0.5 * LOG2E)
    is_a = (lane % (2 * DF_QK)) < DF_QK
    qa_ref[0] = jnp.where(is_a, q, 0.0).astype(BF16)
    qb_ref[0] = jnp.where(is_a, 0.0, q).astype(BF16)
    ko_ref[0] = rot(kk_ref[0]).astype(BF16)
    vo_ref[0] = v_ref[0].astype(BF16)


def _rope_prep(proj, cos_t, sin_t):
    b, s, _ = proj.shape
    tile = min(TILE_ROPE, s)
    w = DF_WIDTH
    spec = lambda cb: pl.BlockSpec((1, tile, w), lambda bi, i: (bi, i, cb))
    out = pl.BlockSpec((1, tile, w), lambda bi, i: (bi, i, 0))
    tab = pl.BlockSpec((tile, LANES), lambda bi, i: (i, 0))
    return pl.pallas_call(
        _rope_kernel,
        grid=(b, s // tile),
        in_specs=[spec(COL_QQ // w), spec(COL_KK // w), spec(COL_DV // w), tab, tab],
        out_specs=[out, out, out, out],
        out_shape=[jax.ShapeDtypeStruct((b, s, w), BF16)] * 4,
        compiler_params=_cparams("parallel", "parallel"),
        name="rope_prep",
    )(proj, proj, proj, cos_t, sin_t)


def _diff_attn_kernel(qa_ref, qb_ref, k_ref, v_ref, lq1_ref, lk1_ref, lq2_ref, lk2_ref, ng_ref,
                      o_ref, s_ref, *, tq, tk, nk, lam_init):
    q2 = (qa_ref[0], qb_ref[0])
    nt_dims = (((1,), (1,)), ((), ()))

    def fold(x, op, init):
        for j in range(tk // LANES):
            init = op(init, x[:, j * LANES:(j + 1) * LANES])
        return init

    m = [jnp.full((tq, LANES), -jnp.inf, F32)] * 2
    for c in range(nk):
        kc = k_ref[0, c * tk:(c + 1) * tk, :]
        for i in range(2):
            sc = lax.dot_general(q2[i], kc, nt_dims, preferred_element_type=F32)
            s_ref[c, i * tq:(i + 1) * tq, :] = sc
            m[i] = fold(sc, jnp.maximum, m[i])
    m = [jnp.max(x, axis=-1, keepdims=True) for x in m]

    acc = [jnp.zeros((tq, 2 * DF_V), F32)] * 2
    ones = jnp.ones((tk, DF_V), BF16)
    for c in range(nk):
        vc = jnp.concatenate([v_ref[0, c * tk:(c + 1) * tk, :], ones], axis=1)
        for i in range(2):
            e = jnp.exp2((s_ref[c, i * tq:(i + 1) * tq, :] - m[i]).astype(BF16))
            acc[i] = acc[i] + jnp.dot(e, vc, preferred_element_type=F32)
    lam = (jnp.exp(jnp.sum(lq1_ref[...] * lk1_ref[...], axis=-1, keepdims=True))
           - jnp.exp(jnp.sum(lq2_ref[...] * lk2_ref[...], axis=-1, keepdims=True)) + lam_init)
    acc = (acc[0][:, :DF_V] * (1.0 / acc[0][:, DF_V:DF_V + 1])
           - acc[1][:, :DF_V] * (lam / acc[1][:, DF_V:DF_V + 1]))
    ms = jnp.mean(acc * acc, axis=-1, keepdims=True)
    o_ref[0] = acc * lax.rsqrt(ms + RMS_EPS) * ng_ref[...] * (1.0 - lam_init)


def _diff_attn(qa, qb, kk, vv, lq1, lk1, lq2, lk2, norm_g, lam_init):
    b, s, _ = qa.shape
    tq = min(TILE_Q, s)
    tk = min(TILE_K, s)
    nk = s // tk
    kern = functools.partial(_diff_attn_kernel, tq=tq, tk=tk, nk=nk, lam_init=lam_init)
    qspec = pl.BlockSpec((1, tq, LANES), lambda bi, h, i: (bi, i, h))
    kspec = pl.BlockSpec((1, s, LANES), lambda bi, h, i: (bi, 0, h))
    vec64 = pl.BlockSpec((1, DF_QK), lambda bi, h, i: (0, 0))
    return pl.pallas_call(
        kern,
        grid=(b, DF_HEADS, s // tq),
        in_specs=[qspec, qspec, kspec, kspec, vec64, vec64, vec64, vec64,
                  pl.BlockSpec((1, DF_V), lambda bi, h, i: (0, 0))],
        out_specs=pl.BlockSpec((1, tq, DF_V), lambda bi, h, i: (bi, i, h)),
        out_shape=jax.ShapeDtypeStruct((b, s, DF_WIDTH), F32),
        scratch_shapes=[pltpu.VMEM((nk, 2 * tq, tk), F32)],
        compiler_params=_cparams("parallel", "parallel", "arbitrary"),
        name="diff_attn",
    )(qa, qb, kk, vv, lq1, lk1, lq2, lk2, norm_g)


def _layer_norm(x, g, b):
    mu = jnp.mean(x, axis=-1, keepdims=True)
    xc = x - mu
    var = jnp.mean(xc * xc, axis=-1, keepdims=True)
    return xc * lax.rsqrt(var + LN_EPS) * g + b


def _merge_kernel(x_ref, of_ref, ob_ref, z_ref, y5_ref, oc_ref, g0_ref, g1_ref, g2_ref,
                  bgate_ref, dnn_ref, gluw_ref, glub_ref, wbr_ref, wout_ref, lng_ref, lnb_ref,
                  o_ref):
    o = of_ref[0] + ob_ref[0]
    z = z_ref[0]
    parts = []
    for h in range(DN_HEADS):
        oh = o[:, h * DN_DIM:(h + 1) * DN_DIM]
        ms = jnp.mean(oh * oh, axis=-1, keepdims=True)
        parts.append(oh * lax.rsqrt(ms + RMS_EPS) * dnn_ref[...])
    o_a = jnp.concatenate(parts, axis=1) * _silu(z)
    y = jax.nn.gelu(y5_ref[0])
    o_b = y * _sigmoid(_bdot(y, gluw_ref[...]) + glub_ref[...])
    o_c = oc_ref[0]
    bg = bgate_ref[...]
    merged = (_sigmoid(g0_ref[0] + bg[:, 0:D_MODEL]) * _bdot(o_a, wbr_ref[0])
              + _sigmoid(g1_ref[0] + bg[:, D_MODEL:2 * D_MODEL]) * _bdot(o_b, wbr_ref[1])
              + _sigmoid(g2_ref[0] + bg[:, 2 * D_MODEL:]) * _bdot(o_c, wbr_ref[2]))
    hres = DEEPNORM_ALPHA * x_ref[0] + _bdot(merged, wout_ref[...])
    o_ref[0] = _layer_norm(hres, lng_ref[...], lnb_ref[...])


def _merge(x, o_f, o_b, proj, y5, o_c, b_gate, dn_norm, glu_w, glu_b, w_branch, w_out, ln_g, ln_b):
    b, s, _ = x.shape
    tile = min(TILE_MERGE, s)
    row = lambda w, cb=0: pl.BlockSpec((1, tile, w), lambda bi, i: (bi, i, cb))
    full = lambda shape: pl.BlockSpec(shape, lambda bi, i: (0,) * len(shape))
    gcb = COL_GATE // D_MODEL
    return pl.pallas_call(
        _merge_kernel,
        grid=(b, s // tile),
        in_specs=[row(D_MODEL), row(DN_WIDTH), row(DN_WIDTH), row(DN_WIDTH, COL_DN_Z // DN_WIDTH),
                  row(S5_WIDTH), row(DF_WIDTH),
                  row(D_MODEL, gcb), row(D_MODEL, gcb + 1), row(D_MODEL, gcb + 2),
                  full((1, N_BRANCH * D_MODEL)), full((1, DN_DIM)),
                  full((S5_WIDTH, S5_WIDTH)), full((1, S5_WIDTH)),
                  full((N_BRANCH, DN_WIDTH, D_MODEL)), full((D_MODEL, D_MODEL)),
                  full((1, D_MODEL)), full((1, D_MODEL))],
        out_specs=row(D_MODEL),
        out_shape=jax.ShapeDtypeStruct((b, s, D_MODEL), F32),
        compiler_params=_cparams("parallel", "parallel"),
        name="merge",
    )(x, o_f, o_b, proj, y5, o_c, proj, proj, proj, b_gate, dn_norm, glu_w, glu_b,
      w_branch, w_out, ln_g, ln_b)


def _xattn_kernel(x_ref, kv_ref, wq_ref, wo_ref, lng_ref, lnb_ref, o_ref):
    x = x_ref[0]
    q = _bdot(x, wq_ref[...])
    outs = []
    for h in range(XA_HEADS):
        qh = q[:, h * XA_DIM:(h + 1) * XA_DIM]
        kh = kv_ref[0, :, h * XA_DIM:(h + 1) * XA_DIM]
        vh = kv_ref[0, :, XA_WIDTH + h * XA_DIM:XA_WIDTH + (h + 1) * XA_DIM]
        sc = _bdot_nt(qh, kh) * (XA_DIM ** -0.5)
        sc = sc - jnp.max(sc, axis=-1, keepdims=True)
        e = jnp.exp(sc)
        pr = e / jnp.sum(e, axis=-1, keepdims=True)
        outs.append(_bdot(pr, vh))
    o = jnp.concatenate(outs, axis=1)
    hres = DEEPNORM_ALPHA * x + _bdot(o, wo_ref[...])
    o_ref[0] = _layer_norm(hres, lng_ref[...], lnb_ref[...])


def _xattn(x, kv, wq, wo, ln_g, ln_b):
    b, s, _ = x.shape
    tile = min(TILE_XA, s)
    mlen = kv.shape[1]
    row = pl.BlockSpec((1, tile, D_MODEL), lambda bi, i: (bi, i, 0))
    full = lambda shape: pl.BlockSpec(shape, lambda bi, i: (0,) * len(shape))
    return pl.pallas_call(
        _xattn_kernel,
        grid=(b, s // tile),
        in_specs=[row, pl.BlockSpec((1, mlen, 2 * XA_WIDTH), lambda bi, i: (bi, 0, 0)),
                  full((D_MODEL, XA_WIDTH)), full((XA_WIDTH, D_MODEL)),
                  full((1, D_MODEL)), full((1, D_MODEL))],
        out_specs=row,
        out_shape=jax.ShapeDtypeStruct((b, s, D_MODEL), F32),
        compiler_params=_cparams("parallel", "parallel"),
        name="xattn",
    )(x, kv, wq, wo, ln_g, ln_b)


def _router_kernel(x_ref, rw_ref, rb_ref, idx_ref, gate_ref, xb_ref):
    x = x_ref[...]
    xb_ref[...] = x.astype(BF16)
    logits = lax.dot_general(rw_ref[...], x, (((1,), (1,)), ((), ())), precision=HI,
                             preferred_element_type=F32) + rb_ref[...]
    tn = logits.shape[1]
    eid = lax.broadcasted_iota(jnp.int32, (N_EXPERTS, tn), 0)
    vals, ids = [], []
    cur = logits
    for _ in range(TOP_K):
        m = jnp.max(cur, axis=0, keepdims=True)
        sel = jnp.min(jnp.where(cur == m, eid, N_EXPERTS), axis=0, keepdims=True)
        vals.append(m)
        ids.append(sel)
        cur = jnp.where(eid == sel, -jnp.inf, cur)
    top_v = jnp.concatenate(vals, axis=0)
    e = jnp.exp(top_v - top_v[0:1])
    gate_ref[...] = e / jnp.sum(e, axis=0, keepdims=True)
    idx_ref[...] = jnp.concatenate(ids, axis=0)


def _router(x2d, router_w_t, router_b):
    n, _ = x2d.shape
    tile = min(TILE_ROUTER, n)
    return pl.pallas_call(
        _router_kernel,
        grid=(n // tile,),
        in_specs=[pl.BlockSpec((tile, D_MODEL), lambda i: (i, 0)),
                  pl.BlockSpec((N_EXPERTS, D_MODEL), lambda i: (0, 0)),
                  pl.BlockSpec((N_EXPERTS, 1), lambda i: (0, 0))],
        out_specs=[pl.BlockSpec((TOP_K, tile), lambda i: (0, i)),
                   pl.BlockSpec((TOP_K, tile), lambda i: (0, i)),
                   pl.BlockSpec((tile, D_MODEL), lambda i: (i, 0))],
        out_shape=[jax.ShapeDtypeStruct((TOP_K, n), jnp.int32),
                   jax.ShapeDtypeStruct((TOP_K, n), F32),
                   jax.ShapeDtypeStruct((n, D_MODEL), BF16)],
        compiler_params=_cparams("parallel"),
        name="router",
    )(x2d, router_w_t, router_b)


def _rank_kernel(idx_ref, rank_ref, cnt_ref, run_ref, *, tile):
    @pl.when(pl.program_id(0) == 0)
    def _():
        run_ref[...] = jnp.zeros_like(run_ref)

    idx = idx_ref[...]
    eid = lax.broadcasted_iota(jnp.int32, (N_EXPERTS, tile), 0)
    si = lax.broadcasted_iota(jnp.int32, (tile, tile), 0)
    ti = lax.broadcasted_iota(jnp.int32, (tile, tile), 1)
    before = (si < ti).astype(BF16)
    run = run_ref[...]
    ranks = []
    for k in range(TOP_K):
        onehot = eid == idx[k:k + 1]
        oh = onehot.astype(BF16)
        within = jnp.dot(oh, before, preferred_element_type=F32)
        ranks.append(jnp.sum(jnp.where(onehot, within + run, 0.0), axis=0, keepdims=True))
        run = run + jnp.sum(onehot.astype(F32), axis=1, keepdims=True)
    rank_ref[...] = jnp.concatenate(ranks, axis=0).astype(jnp.int32)
    run_ref[...] = run
    cnt_ref[...] = run.astype(jnp.int32)


def _rank(top_i):
    _, n = top_i.shape
    tile = min(TILE_ROUTER, n)
    kern = functools.partial(_rank_kernel, tile=tile)
    return pl.pallas_call(
        kern,
        grid=(n // tile,),
        in_specs=[pl.BlockSpec((TOP_K, tile), lambda i: (0, i))],
        out_specs=[pl.BlockSpec((TOP_K, tile), lambda i: (0, i)),
                   pl.BlockSpec((N_EXPERTS, 1), lambda i: (0, 0))],
        out_shape=[jax.ShapeDtypeStruct((TOP_K, n), jnp.int32),
                   jax.ShapeDtypeStruct((N_EXPERTS, 1), jnp.int32)],
        scratch_shapes=[pltpu.VMEM((N_EXPERTS, 1), F32)],
        compiler_params=_cparams("arbitrary"),
        name="moe_rank",
    )(top_i)


def _expert_kernel(be_ref, on_ref, x_ref, wg_ref, wu_ref, bg_ref, bu_ref, wd_ref, bd_ref, o_ref,
                   wd_bf_ref):
    i = pl.program_id(0)
    live = on_ref[i] != 0

    @pl.when(jnp.logical_or(i == 0, be_ref[i] != be_ref[jnp.maximum(i - 1, 0)]))
    def _():
        wd_bf_ref[...] = wd_ref[0, 0].astype(BF16)

    @pl.when(live)
    def _():
        x = x_ref[...]
        hg = jnp.dot(x, wg_ref[0], preferred_element_type=F32) + bg_ref[0]
        hu = jnp.dot(x, wu_ref[0], preferred_element_type=F32) + bu_ref[0]
        gt = jnp.minimum(hg, SWIGLU_LIMIT)
        up = jnp.clip(hu, -SWIGLU_LIMIT, SWIGLU_LIMIT)
        act = (up + 1.0) * gt * _sigmoid(SWIGLU_ALPHA * gt)
        y = jnp.dot(act.astype(BF16), wd_bf_ref[...], preferred_element_type=F32) + bd_ref[0]
        o_ref[...] = y.astype(o_ref.dtype)

    @pl.when(jnp.logical_not(live))
    def _():
        o_ref[...] = jnp.zeros_like(o_ref)


def _experts(x_rows, blk_e, blk_on, w_g, w_u, b_g, b_u, w_d, l, b_d, *, block):
    rows, _ = x_rows.shape
    wspec = lambda k, m: pl.BlockSpec((1, k, m), lambda i, be, on: (be[i], 0, 0))
    grid_spec = pltpu.PrefetchScalarGridSpec(
        num_scalar_prefetch=2,
        grid=(rows // block,),
        in_specs=[pl.BlockSpec((block, D_MODEL), lambda i, be, on: (i, 0)),
                  wspec(D_MODEL, D_FF), wspec(D_MODEL, D_FF), wspec(1, D_FF), wspec(1, D_FF),
                  pl.BlockSpec((1, 1, D_FF, D_MODEL), lambda i, be, on: (l, be[i], 0, 0)),
                  wspec(1, D_MODEL)],
        out_specs=pl.BlockSpec((block, D_MODEL), lambda i, be, on: (i, 0)),
        scratch_shapes=[pltpu.VMEM((D_FF, D_MODEL), BF16)],
    )
    return pl.pallas_call(
        _expert_kernel,
        grid_spec=grid_spec,
        out_shape=jax.ShapeDtypeStruct((rows, D_MODEL), BF16),
        compiler_params=_cparams("arbitrary"),
        name="moe_experts",
    )(blk_e, blk_on, x_rows, w_g, w_u, b_g, b_u, w_d, b_d)


def _combine_kernel(x_ref, y_ref, gate_ref, lng_ref, lnb_ref, o_ref):
    gate = gate_ref[...]
    h = y_ref[0].astype(F32) * gate[:, 0:1]
    for k in range(1, TOP_K):
        h = h + y_ref[k].astype(F32) * gate[:, k:k + 1]
    o_ref[...] = _layer_norm(DEEPNORM_ALPHA * x_ref[...] + h, lng_ref[...], lnb_ref[...])


def _combine(x2d, y_g, gate_t, ln_g, ln_b):
    n, _ = x2d.shape
    tile = min(TILE_COMBINE, n)
    return pl.pallas_call(
        _combine_kernel,
        grid=(n // tile,),
        in_specs=[pl.BlockSpec((tile, D_MODEL), lambda i: (i, 0)),
                  pl.BlockSpec((TOP_K, tile, D_MODEL), lambda i: (0, i, 0)),
                  pl.BlockSpec((tile, TOP_K), lambda i: (i, 0)),
                  pl.BlockSpec((1, D_MODEL), lambda i: (0, 0)),
                  pl.BlockSpec((1, D_MODEL), lambda i: (0, 0))],
        out_specs=pl.BlockSpec((tile, D_MODEL), lambda i: (i, 0)),
        out_shape=jax.ShapeDtypeStruct((n, D_MODEL), F32),
        compiler_params=_cparams("parallel"),
        name="moe_combine",
    )(x2d, y_g, gate_t, ln_g, ln_b)


def _split_kernel(w_ref, se_ref, so_ref, g_ref, u_ref):
    w = w_ref[0, 0].astype(BF16)
    g_ref[0] = jnp.dot(w, se_ref[...], preferred_element_type=F32).astype(BF16)
    u_ref[0] = jnp.dot(w, so_ref[...], preferred_element_type=F32).astype(BF16)


def _split_gate_up(w_gu, l):
    _, e, dm, f2 = w_gu.shape
    f = f2 // 2
    tr = 256
    col = jnp.arange(f2, dtype=jnp.int32)[:, None]
    out = jnp.arange(f, dtype=jnp.int32)[None, :]
    sel_even = (col == 2 * out).astype(BF16)
    sel_odd = (col == 2 * out + 1).astype(BF16)
    sel = pl.BlockSpec((f2, f), lambda ei, i: (0, 0))
    ospec = pl.BlockSpec((1, tr, f), lambda ei, i: (ei, i, 0))
    return pl.pallas_call(
        _split_kernel,
        grid=(e, dm // tr),
        in_specs=[pl.BlockSpec((1, 1, tr, f2), lambda ei, i: (l, ei, i, 0)), sel, sel],
        out_specs=[ospec, ospec],
        out_shape=[jax.ShapeDtypeStruct((e, dm, f), BF16)] * 2,
        compiler_params=_cparams("parallel", "parallel"),
        name="split_gate_up",
    )(w_gu, sel_even, sel_odd)


def _moe(x, w):
    b, s, dm = x.shape
    n = b * s
    x2d = x.reshape(n, dm)
    top_i, gate, xb = _router(x2d, w['router_w_t'], w['router_b'])
    rank, counts = _rank(top_i)
    block = MOE_BLOCK
    m = n * TOP_K
    nblk = -(-m // block) + N_EXPERTS
    rows = nblk * block
    counts = counts.reshape(N_EXPERTS)
    padded = (counts + block - 1) // block * block
    pad_end = jnp.cumsum(padded)
    pad_start = pad_end - padded
    experts = jnp.arange(N_EXPERTS, dtype=top_i.dtype)
    dest = rank + jnp.sum(jnp.where(top_i[:, :, None] == experts, pad_start.astype(jnp.int32), 0),
                          axis=-1)
    tok = jnp.broadcast_to(jnp.arange(n, dtype=jnp.int32)[None, :], (TOP_K, n))
    row_tok = jnp.zeros((rows,), jnp.int32).at[dest.reshape(-1)].set(
        tok.reshape(-1), unique_indices=True, mode='promise_in_bounds')
    blk_start = jnp.arange(nblk, dtype=pad_end.dtype) * block
    blk_e = jnp.minimum(jnp.sum(pad_end[None, :] <= blk_start[:, None], axis=1),
                        N_EXPERTS - 1).astype(jnp.int32)
    blk_on = (blk_start < pad_end[-1]).astype(jnp.int32)
    x_rows = xb.at[row_tok].get(mode='promise_in_bounds')
    y_rows = _experts(x_rows, blk_e, blk_on, w['e_w_g'], w['e_w_u'], w['e_b_g'], w['e_b_u'], w['e_w_d'],
                      w['layer'], w['e_b_d'], block=block)
    y_g = y_rows.at[dest].get(mode='promise_in_bounds')
    out = _combine(x2d, y_g, gate.T, w['ln3_g'], w['ln3_b'])
    return out.reshape(b, s, dm)


def _rope_tables(s):
    half = DF_QK // 2
    inv_freq = jnp.power(ROPE_THETA, -jnp.arange(half, dtype=F32) / half)
    ang = jnp.arange(s, dtype=F32)[:, None] * inv_freq[None, :]
    cos = jnp.cos(ang)
    sin = jnp.sin(ang)
    cos_t = jnp.concatenate([cos, cos, cos, cos], axis=1)
    sin_t = jnp.concatenate([-sin, sin, -sin, sin], axis=1)
    return cos_t, sin_t


def _prepare_layer(l, p, s):
    dm = D_MODEL
    w = {}
    w['w_in'] = _pack_in_proj(p['w_in'][l]).astype(BF16)
    pad8 = jnp.zeros((8,), F32)
    pad112 = jnp.zeros((LANES - 16,), F32)
    w['nega'] = jnp.concatenate([pad8, -jnp.exp(p['dn_A_log'][l].reshape(8)), pad112]).reshape(1, LANES)
    w['dtb'] = jnp.concatenate([pad8, p['dn_dt_bias'][l].reshape(8), pad112]).reshape(1, LANES)
    w['dn_conv'] = p['dn_conv'][l]
    t5 = min(S5_CHUNK, s)
    w['s5'] = _s5_operators(p['s5_lam_re'][l], p['s5_lam_im'][l], p['s5_log_dt'][l],
                            p['s5_B_re'][l], p['s5_B_im'][l], p['s5_C_re'][l],
                            p['s5_C_im'][l], p['s5_D'][l], chunk=t5, nc=s // t5)
    for name in ('df_lq1', 'df_lk1', 'df_lq2', 'df_lk2'):
        w[name] = p[name][l].reshape(1, DF_QK)
    w['df_norm'] = p['df_norm'][l].reshape(1, DF_V)
    w['lam_init'] = 0.8 - 0.6 * math.exp(-0.3 * l)
    w['b_gate'] = p['b_gate'][l].reshape(1, -1)
    w['dn_norm'] = p['dn_norm'][l].reshape(1, DN_DIM)
    w['s5_glu_w'] = p['s5_glu_w'][l].astype(BF16)
    w['s5_glu_b'] = p['s5_glu_b'][l].reshape(1, -1)
    w['w_branch'] = p['w_branch'][l].astype(BF16)
    w['w_out'] = p['w_out'][l].astype(BF16)
    for name in ('ln1_g', 'ln1_b', 'ln2_g', 'ln2_b', 'ln3_g', 'ln3_b'):
        w[name] = p[name][l].reshape(1, dm)
    w['xa_wkv'] = p['xa_wkv'][l].astype(BF16)
    w['xa_wq'] = p['xa_wq'][l].astype(BF16)
    w['xa_wo'] = p['xa_wo'][l].astype(BF16)
    w['router_w_t'] = p['router_w'][l].T
    w['router_b'] = p['router_b'][l].reshape(N_EXPERTS, 1)
    w['e_w_g'], w['e_w_u'] = _split_gate_up(p['e_w_gu'], l)
    w['e_b_g'] = p['e_b_gu'][l][:, None, 0::2]
    w['e_b_u'] = p['e_b_gu'][l][:, None, 1::2]
    w['e_w_d'] = p['e_w_down']
    w['layer'] = l
    w['e_b_d'] = p['e_b_down'][l][:, None, :]
    return w


def _layer(x, mem, w, tabs):
    b, s, dm = x.shape
    n = b * s
    proj = _matmul(x.reshape(n, dm), w['w_in'], tm=TILE_MM, tn=LANES * 19)
    proj = proj.reshape(b, s, PROJ_WIDTH)

    qkv, bg = _dn_prep(proj, w['dn_conv'], w['nega'], w['dtb'], chunk=DN_CHUNK)
    bgr = bg[:, :, :16].reshape(b, s // DN_CHUNK, DN_CHUNK, 16).transpose(0, 1, 3, 2)
    o_f, o_b = _dn_scan(qkv, bg, bgr, chunk=DN_CHUNK)

    t5 = min(S5_CHUNK, s)
    nc = s // t5
    u5 = proj[:, :, COL_S5_U:COL_S5_U + S5_WIDTH].reshape(b, nc, t5, S5_GROUPS, S5_GROUP)
    u5 = u5.transpose(3, 0, 1, 2, 4).reshape(S5_GROUPS, b, nc, t5 * S5_GROUP).astype(BF16)
    y5 = _s5_ssm(u5, *w['s5'])
    y5 = y5.reshape(S5_GROUPS, b, nc, t5, S5_GROUP).transpose(1, 2, 3, 0, 4).reshape(b, s, S5_WIDTH)

    qa, qb, kk, vv = _rope_prep(proj, *tabs)
    o_c = _diff_attn(qa, qb, kk, vv, w['df_lq1'], w['df_lk1'], w['df_lq2'], w['df_lk2'],
                     w['df_norm'], w['lam_init'])

    x = _merge(x, o_f, o_b, proj, y5, o_c, w['b_gate'], w['dn_norm'], w['s5_glu_w'], w['s5_glu_b'],
               w['w_branch'], w['w_out'], w['ln1_g'], w['ln1_b'])

    mlen = mem.shape[1]
    kv = _matmul(mem.reshape(b * mlen, dm), w['xa_wkv'], tm=512, tn=1024)
    x = _xattn(x, kv.reshape(b, mlen, 2 * XA_WIDTH), w['xa_wq'], w['xa_wo'], w['ln2_g'], w['ln2_b'])
    return _moe(x, w)


def kernel(x_prompt, x_sample, mem_prompt, mem_sample, w_in, b_gate, dn_conv, dn_A_log, dn_dt_bias,
           dn_norm, s5_lam_re, s5_lam_im, s5_log_dt, s5_B_re, s5_B_im, s5_C_re, s5_C_im, s5_D,
           s5_glu_w, s5_glu_b, df_lq1, df_lk1, df_lq2, df_lk2, df_norm, w_branch, w_out,
           ln1_g, ln1_b, xa_wq, xa_wkv, xa_wo, ln2_g, ln2_b, router_w, router_b,
           e_w_gu, e_b_gu, e_w_down, e_b_down, ln3_g, ln3_b):
    p = dict(w_in=w_in, b_gate=b_gate, dn_conv=dn_conv, dn_A_log=dn_A_log, dn_dt_bias=dn_dt_bias,
             dn_norm=dn_norm, s5_lam_re=s5_lam_re, s5_lam_im=s5_lam_im, s5_log_dt=s5_log_dt,
             s5_B_re=s5_B_re, s5_B_im=s5_B_im, s5_C_re=s5_C_re, s5_C_im=s5_C_im, s5_D=s5_D,
             s5_glu_w=s5_glu_w, s5_glu_b=s5_glu_b, df_lq1=df_lq1, df_lk1=df_lk1, df_lq2=df_lq2,
             df_lk2=df_lk2, df_norm=df_norm, w_branch=w_branch, w_out=w_out, ln1_g=ln1_g,
             ln1_b=ln1_b, xa_wq=xa_wq, xa_wkv=xa_wkv, xa_wo=xa_wo, ln2_g=ln2_g, ln2_b=ln2_b,
             router_w=router_w, router_b=router_b, e_w_gu=e_w_gu, e_b_gu=e_b_gu,
             e_w_down=e_w_down, e_b_down=e_b_down, ln3_g=ln3_g, ln3_b=ln3_b)
    s = x_prompt.shape[1]
    assert x_sample.shape[1] == s
    tabs = _rope_tables(s)
    layers = [_prepare_layer(l, p, s) for l in range(DEPTH)]
    nb = x_prompt.shape[0]
    x = jnp.concatenate([x_prompt, x_sample], axis=0)
    mem = jnp.concatenate([mem_prompt, mem_sample], axis=0)
    for w in layers:
        x = _layer(x, mem, w, tabs)
    return (x[:nb], x[nb:])
```

```python
import functools
import math

import numpy as np
import jax
import jax.numpy as jnp
from jax import lax
from jax.experimental import pallas as pl
from jax.experimental.pallas import tpu as pltpu

F32 = jnp.float32
BF16 = jnp.bfloat16
HI = lax.Precision.HIGHEST

D_MODEL = 1024
DN_HEADS = 4
DN_DIM = 128
DN_WIDTH = DN_HEADS * DN_DIM
DN_CONV = 4
S5_GROUP = 16
S5_WIDTH = 512
S5_GROUPS = S5_WIDTH // S5_GROUP
S5_STATE = 64
DF_HEADS = 4
DF_QK = 64
DF_V = 128
DF_WIDTH = DF_HEADS * DF_V
ROPE_THETA = 10000.0
XA_HEADS = 4
XA_DIM = 128
XA_WIDTH = XA_HEADS * XA_DIM
N_EXPERTS = 32
TOP_K = 4
D_FF = 1024
SWIGLU_LIMIT = 7.0
SWIGLU_ALPHA = 1.702
N_BRANCH = 3
DEPTH = 2
DEEPNORM_ALPHA = (2 * DEPTH) ** 0.25
LN_EPS = 1e-5
RMS_EPS = 1e-6
LOG2E = 1.4426950408889634

LANES = 128
SUBLANES = 8
VMEM_LIMIT = 56 * 1024 * 1024

COL_DN_QKV = 0
COL_DN_Z = 1536
COL_S5_U = 2048
COL_QQ = 2560
COL_KK = 3072
COL_DV = 3584
COL_GATE = 4096
COL_BA = 7168
PROJ_WIDTH = 7296

TILE_MM = 1024
TILE_DN_PREP = 512
TILE_DN_SCAN = 256
DN_CHUNK = 64
DN_CHUNKS_PER_TRIP = 2
S5_CHUNK = 32
TILE_ROPE = 512
TILE_Q = 256
TILE_K = 512
TILE_MERGE = 256
TILE_XA = 512
TILE_ROUTER = 512
MOE_BLOCK = 512
TILE_COMBINE = 256


def _in_proj_perm():
    o_dn_b, o_dn_a, o_s5 = 2048, 2056, 2064
    o_q1, o_q2, o_k1, o_k2, o_dv, o_gate = 2576, 2832, 3088, 3344, 3600, 4112
    perm = np.full((PROJ_WIDTH,), -1, np.int64)
    perm[0:2048] = np.arange(2048)
    perm[COL_S5_U:COL_S5_U + 512] = o_s5 + np.arange(512)
    for h in range(DF_HEADS):
        j = np.arange(DF_QK)
        perm[COL_QQ + h * 128 + j] = o_q1 + h * DF_QK + j
        perm[COL_QQ + h * 128 + 64 + j] = o_q2 + h * DF_QK + j
        perm[COL_KK + h * 128 + j] = o_k1 + h * DF_QK + j
        perm[COL_KK + h * 128 + 64 + j] = o_k2 + h * DF_QK + j
    perm[COL_DV:COL_DV + 512] = o_dv + np.arange(512)
    perm[COL_GATE:COL_GATE + 3072] = o_gate + np.arange(3072)
    perm[COL_BA:COL_BA + 8] = o_dn_b + np.arange(8)
    perm[COL_BA + 8:COL_BA + 16] = o_dn_a + np.arange(8)
    return perm


_IN_PERM = _in_proj_perm()


def _pack_in_proj(w_in):
    runs, start = [], 0
    for i in range(1, PROJ_WIDTH + 1):
        if i == PROJ_WIDTH or _IN_PERM[i] != _IN_PERM[i - 1] + (1 if _IN_PERM[i - 1] >= 0 else 0):
            runs.append((start, i))
            start = i
    pieces = []
    for a, b in runs:
        src = int(_IN_PERM[a])
        pieces.append(jnp.zeros((w_in.shape[0], b - a), w_in.dtype) if src < 0
                      else w_in[:, src:src + (b - a)])
    return jnp.concatenate(pieces, axis=1)


def _cparams(*sem):
    return pltpu.CompilerParams(dimension_semantics=sem, vmem_limit_bytes=VMEM_LIMIT)


def _bdot(a, b):
    return jnp.dot(a.astype(BF16), b.astype(BF16), preferred_element_type=F32)


def _bdot_nt(a, b):
    return lax.dot_general(a.astype(BF16), b.astype(BF16), (((1,), (1,)), ((), ())),
                           preferred_element_type=F32)


def _bdot_tn(a, b):
    return lax.dot_general(a.astype(BF16), b.astype(BF16), (((0,), (0,)), ((), ())),
                           preferred_element_type=F32)


def _sigmoid(x):
    return 0.5 * jnp.tanh(0.5 * x) + 0.5


def _silu(x):
    return x * _sigmoid(x)


def _mm_kernel(x_ref, w_ref, o_ref):
    o_ref[...] = jnp.dot(x_ref[...].astype(BF16), w_ref[...],
                         preferred_element_type=F32).astype(o_ref.dtype)


def _matmul(x, w, *, tm, tn, out_dtype=F32):
    n, k = x.shape
    m = w.shape[1]
    tm = min(tm, n)
    tn = min(tn, m)
    return pl.pallas_call(
        _mm_kernel,
        grid=(n // tm, m // tn),
        in_specs=[pl.BlockSpec((tm, k), lambda i, j: (i, 0)),
                  pl.BlockSpec((k, tn), lambda i, j: (0, j))],
        out_specs=pl.BlockSpec((tm, tn), lambda i, j: (i, j)),
        out_shape=jax.ShapeDtypeStruct((n, m), out_dtype),
        compiler_params=_cparams("parallel", "arbitrary"),
        name="matmul",
    )(x, w)


def _dn_prep_kernel(x_ref, prev_ref, next_ref, ba_ref, cw_ref, nega_ref, dtb_ref,
                    qkv_ref, bg_ref, *, tile, chunk):
    i = pl.program_id(1)
    last = pl.num_programs(1) - 1
    x = x_ref[0]
    prev = jnp.where(i > 0, prev_ref[0], 0.0)
    nxt = jnp.where(i < last, next_ref[0], 0.0)
    ext = jnp.concatenate([prev, x, nxt], axis=0)
    cw = cw_ref[...]
    acc = cw[0:1] * ext[6:6 + tile]
    for j in range(1, DN_CONV):
        acc = acc + cw[j:j + 1] * ext[6 + j:6 + j + tile]
    act = _silu(acc)
    for h in range(DN_HEADS):
        lo = h * DN_DIM
        qh = act[:, lo:lo + DN_DIM]
        qn = qh * lax.rsqrt(jnp.sum(qh * qh, axis=-1, keepdims=True) + 1e-6)
        qkv_ref[0, :, lo:lo + DN_DIM] = qn * (DN_DIM ** -0.5)
        kh = act[:, DN_WIDTH + lo:DN_WIDTH + lo + DN_DIM]
        qkv_ref[0, :, DN_WIDTH + lo:DN_WIDTH + lo + DN_DIM] = kh * lax.rsqrt(
            jnp.sum(kh * kh, axis=-1, keepdims=True) + 1e-6)
    qkv_ref[0, :, 2 * DN_WIDTH:] = act[:, 2 * DN_WIDTH:]

    ba = ba_ref[0]
    beta = _sigmoid(ba)
    z = ba + dtb_ref[...]
    softplus = jnp.maximum(z, 0.0) + jnp.log(1.0 + jnp.exp(-jnp.abs(z)))
    g = nega_ref[...] * softplus
    row = lax.broadcasted_iota(jnp.int32, (chunk, chunk), 0)
    col = lax.broadcasted_iota(jnp.int32, (chunk, chunk), 1)
    lower = (row >= col).astype(F32)
    upper = (row <= col).astype(F32)
    lane = lax.broadcasted_iota(jnp.int32, (chunk, LANES), 1)
    fwd_lane = lane < 8 + DN_HEADS
    for c in range(tile // chunk):
        gc = g[c * chunk:(c + 1) * chunk]
        cf = jnp.dot(lower, gc, precision=HI, preferred_element_type=F32)
        cb = jnp.dot(upper, gc, precision=HI, preferred_element_type=F32)
        gcum = jnp.where(fwd_lane, cf, cb)
        bg_ref[0, c * chunk:(c + 1) * chunk, :] = jnp.where(
            lane < 8, beta[c * chunk:(c + 1) * chunk], gcum)


def _dn_prep(proj, conv_w, nega, dtb, *, chunk):
    b, s, _ = proj.shape
    tile = min(TILE_DN_PREP, s)
    nt = s // tile
    r8 = tile // SUBLANES
    kern = functools.partial(_dn_prep_kernel, tile=tile, chunk=chunk)
    return pl.pallas_call(
        kern,
        grid=(b, nt),
        in_specs=[
            pl.BlockSpec((1, tile, 3 * DN_WIDTH), lambda bi, i: (bi, i, 0)),
            pl.BlockSpec((1, SUBLANES, 3 * DN_WIDTH),
                         lambda bi, i: (bi, jnp.maximum(i * r8 - 1, 0), 0)),
            pl.BlockSpec((1, SUBLANES, 3 * DN_WIDTH),
                         lambda bi, i: (bi, jnp.minimum((i + 1) * r8, s // SUBLANES - 1), 0)),
            pl.BlockSpec((1, tile, LANES), lambda bi, i: (bi, i, COL_BA // LANES)),
            pl.BlockSpec((DN_CONV, 3 * DN_WIDTH), lambda bi, i: (0, 0)),
            pl.BlockSpec((1, LANES), lambda bi, i: (0, 0)),
            pl.BlockSpec((1, LANES), lambda bi, i: (0, 0)),
        ],
        out_specs=[pl.BlockSpec((1, tile, 3 * DN_WIDTH), lambda bi, i: (bi, i, 0)),
                   pl.BlockSpec((1, tile, LANES), lambda bi, i: (bi, i, 0))],
        out_shape=[jax.ShapeDtypeStruct((b, s, 3 * DN_WIDTH), F32),
                   jax.ShapeDtypeStruct((b, s, LANES), F32)],
        compiler_params=_cparams("parallel", "parallel"),
        name="dn_prep",
    )(proj, proj, proj, proj, conv_w, nega, dtb)


def _dn_local(chains, chunk):
    ri = lax.broadcasted_iota(jnp.int32, (chunk, chunk), 0)
    ci = lax.broadcasted_iota(jnp.int32, (chunk, chunk), 1)
    pre = []
    for d, h, q, k, v, bgc, rows in chains:
        li = d * DN_HEADS + h
        beta = bgc[:, li:li + 1]
        gcol = bgc[:, 8 + li:8 + li + 1]
        grow = rows[8 + li:8 + li + 1, :]
        gtot = gcol[chunk - 1:chunk] if d == 0 else gcol[0:1]
        incl = (ri >= ci) if d == 0 else (ri <= ci)
        strict = (ri > ci) if d == 0 else (ri < ci)
        eg = jnp.exp(gcol)
        kb = k * beta
        decay = jnp.where(incl, jnp.exp(jnp.where(incl, gcol - grow, 0.0)), 0.0)
        pre.append(dict(k=k, qg=q * eg, kd=k * jnp.exp(gtot - gcol), decay=decay, strict=strict,
                        x2=jnp.concatenate([v * beta, kb * eg], axis=1), sdec=jnp.exp(gtot),
                        qkb=jnp.concatenate([q, kb], axis=0)))
    qk = [_bdot_nt(c['qkb'], c['k']) for c in pre]
    intra = [x[:chunk] * c['decay'] for x, c in zip(qk, pre)]
    p = [-jnp.where(c['strict'], x[chunk:] * c['decay'], 0.0) for x, c in zip(qk, pre)]
    r = p
    p = [_bdot(x, x) for x in p]
    for _ in range(int(math.log2(chunk)) - 2):
        rp = [_bdot(jnp.concatenate([ri_, pi_], axis=0), pi_) for ri_, pi_ in zip(r, p)]
        r = [ri_ + pi_ + x[:chunk] for ri_, pi_, x in zip(r, p, rp)]
        p = [x[chunk:] for x in rp]
    r = [ri_ + pi_ + _bdot(ri_, pi_) for ri_, pi_ in zip(r, p)]
    uw = [c['x2'] + _bdot(ri_, c['x2']) for ri_, c in zip(r, pre)]
    return [dict(u=x[:, :DN_DIM], wq=jnp.concatenate([x[:, DN_DIM:], c['qg']], axis=0),
                 intra=it, kd=c['kd'], sdec=c['sdec']) for x, it, c in zip(uw, intra, pre)]


def _dn_state(local, states, chunk):
    ws = [_bdot(c['wq'], st) for c, st in zip(local, states)]
    v_new = [c['u'] - y[:chunk] for c, y in zip(local, ws)]
    out = [y[chunk:] + _bdot(c['intra'], vn) for y, c, vn in zip(ws, local, v_new)]
    st_new = [st * c['sdec'] + _bdot_tn(c['kd'], vn) for st, c, vn in zip(states, local, v_new)]
    return out, st_new


def _dn_scan_kernel(xf_ref, xb_ref, gf_ref, gb_ref, rf_ref, rb_ref, of_ref, ob_ref, st_ref,
                    *, tile, chunk):
    @pl.when(pl.program_id(1) == 0)
    def _():
        st_ref[...] = jnp.zeros_like(st_ref)

    nct = tile // chunk
    nchain = 2 * DN_HEADS

    def body(j, carry):
        chains, where = [], []
        for slot in range(DN_CHUNKS_PER_TRIP):
            for d in range(2):
                c = DN_CHUNKS_PER_TRIP * j + slot
                c = c if d == 0 else nct - 1 - c
                r0 = pl.multiple_of(c * chunk, chunk)
                x_ref, g_ref, r_ref = (xf_ref, gf_ref, rf_ref) if d == 0 else (xb_ref, gb_ref, rb_ref)
                bgc = g_ref[0, pl.ds(r0, chunk), :]
                rows = r_ref[0, c]
                for h in range(DN_HEADS):
                    lo = h * DN_DIM
                    q = x_ref[0, pl.ds(r0, chunk), lo:lo + DN_DIM]
                    k = x_ref[0, pl.ds(r0, chunk), DN_WIDTH + lo:DN_WIDTH + lo + DN_DIM]
                    v = x_ref[0, pl.ds(r0, chunk), 2 * DN_WIDTH + lo:2 * DN_WIDTH + lo + DN_DIM]
                    chains.append((d, h, q, k, v, bgc, rows))
                    where.append((d, h, r0))
        states = [st_ref[i] for i in range(nchain)]
        local = _dn_local(chains, chunk)
        outs = []
        for slot in range(DN_CHUNKS_PER_TRIP):
            out, states = _dn_state(local[slot * nchain:(slot + 1) * nchain], states, chunk)
            outs += out
        for (d, h, r0), out in zip(where, outs):
            o_ref = of_ref if d == 0 else ob_ref
            o_ref[0, pl.ds(r0, chunk), h * DN_DIM:(h + 1) * DN_DIM] = out
        for i in range(nchain):
            st_ref[i] = states[i]
        return carry

    lax.fori_loop(0, nct // DN_CHUNKS_PER_TRIP, body, 0)


def _dn_scan(qkv, bg, bgr, *, chunk):
    b, s, _ = qkv.shape
    tile = min(TILE_DN_SCAN, s)
    nt = s // tile
    nct = tile // chunk
    kern = functools.partial(_dn_scan_kernel, tile=tile, chunk=chunk)
    fwd = lambda bi, i: (bi, i, 0)
    bwd = lambda bi, i: (bi, nt - 1 - i, 0)
    fwd4 = lambda bi, i: (bi, i, 0, 0)
    bwd4 = lambda bi, i: (bi, nt - 1 - i, 0, 0)
    return pl.pallas_call(
        kern,
        grid=(b, nt),
        in_specs=[pl.BlockSpec((1, tile, 3 * DN_WIDTH), fwd),
                  pl.BlockSpec((1, tile, 3 * DN_WIDTH), bwd),
                  pl.BlockSpec((1, tile, LANES), fwd),
                  pl.BlockSpec((1, tile, LANES), bwd),
                  pl.BlockSpec((1, nct, 16, chunk), fwd4),
                  pl.BlockSpec((1, nct, 16, chunk), bwd4)],
        out_specs=[pl.BlockSpec((1, tile, DN_WIDTH), fwd),
                   pl.BlockSpec((1, tile, DN_WIDTH), bwd)],
        out_shape=[jax.ShapeDtypeStruct((b, s, DN_WIDTH), F32),
                   jax.ShapeDtypeStruct((b, s, DN_WIDTH), F32)],
        scratch_shapes=[pltpu.VMEM((2 * DN_HEADS, DN_DIM, DN_DIM), F32)],
        compiler_params=_cparams("parallel", "arbitrary"),
        name="dn_scan",
    )(qkv, qkv, bg, bg, bgr, bgr)


def _s5_operators(lam_re, lam_im, log_dt, b_re, b_im, c_re, c_im, d_skip, *, chunk, nc):
    t = chunk
    levels = max(1, int(math.log2(nc)))
    tau = jnp.arange(t + 1, dtype=F32)
    kern = []
    lamp = []
    for d in range(2):
        lam = lax.complex(lam_re[d], lam_im[d])
        dt = jnp.exp(log_dt[d])[:, None]
        lam_dt = lam * dt
        lam_bar = jnp.exp(lam_dt)
        bbar = ((lam_bar - 1.0) / lam)[:, :, None] * lax.complex(b_re[d], b_im[d])
        cc = lax.complex(c_re[d], c_im[d])
        pw = jnp.exp(lam_dt[:, None, :] * tau[None, :, None].astype(jnp.complex64))
        kern.append((pw, bbar, cc))
        steps = (t * 2.0 ** jnp.arange(levels, dtype=F32)).astype(jnp.complex64)
        lamp.append(jnp.exp(lam_dt[:, None, :] * steps[None, :, None]))
    (pwf, bbf, ccf), (pwb, bbb, ccb) = kern
    ef = jnp.einsum('gsp,gpc->gscp', pwf[:, t - 1::-1, :], bbf)
    eb = jnp.einsum('gsp,gpc->gscp', pwb[:, :t, :], bbb)
    w1 = jnp.concatenate([jnp.real(ef), jnp.real(eb), jnp.imag(ef), jnp.imag(eb)], axis=-1)
    w1 = w1.reshape(S5_GROUPS, t * S5_GROUP, 4 * S5_STATE)
    kf = jnp.real(jnp.einsum('gcp,gtp,gpd->gtcd', ccf, pwf[:, :t, :], bbf))
    kb = jnp.real(jnp.einsum('gcp,gtp,gpd->gtcd', ccb, pwb[:, :t, :], bbb))
    si = jnp.arange(t)[:, None]
    ti = jnp.arange(t)[None, :]
    lag_f = jnp.clip(ti - si, 0, t - 1)
    lag_b = jnp.clip(si - ti, 0, t - 1)
    toep = (jnp.where((ti >= si)[None, :, :, None, None], kf[:, lag_f], 0.0)
            + jnp.where((si >= ti)[None, :, :, None, None], kb[:, lag_b], 0.0))
    eye_t = jnp.eye(t, dtype=F32)[None, :, :, None, None]
    eye_c = jnp.eye(S5_GROUP, dtype=F32)[None, None, None, :, :]
    toep = toep + eye_t * eye_c * d_skip.reshape(S5_GROUPS, 1, 1, S5_GROUP, 1)
    w_intra = toep.transpose(0, 1, 4, 2, 3).reshape(S5_GROUPS, t * S5_GROUP, t * S5_GROUP)
    mf = jnp.einsum('gcp,gtp->gptc', ccf, pwf[:, 1:t + 1, :])
    mb = jnp.einsum('gcp,gtp->gptc', ccb, pwb[:, t:0:-1, :])
    w_state = jnp.concatenate([jnp.real(mf), jnp.real(mb), -jnp.imag(mf), -jnp.imag(mb)], axis=1)
    w_state = w_state.reshape(S5_GROUPS, 4 * S5_STATE, t * S5_GROUP)
    w3 = jnp.concatenate([w_intra, w_state], axis=1)
    lam_r = jnp.concatenate([jnp.real(lamp[0]), jnp.real(lamp[1])], axis=-1)
    lam_i = jnp.concatenate([jnp.imag(lamp[0]), jnp.imag(lamp[1])], axis=-1)
    lamp = jnp.stack([lam_r, lam_i], axis=2)
    return w1.astype(BF16), w3.astype(BF16), lamp.astype(F32)


def _s5_kernel(u_ref, w1_ref, w3_ref, lam_ref, y_ref, *, nc, width, levels):
    u = u_ref[0, 0]
    loc = jnp.dot(u, w1_ref[0], preferred_element_type=F32)
    re = loc[:, :2 * S5_STATE]
    im = loc[:, 2 * S5_STATE:]
    row = lax.broadcasted_iota(jnp.int32, (nc, 2 * S5_STATE), 0)
    fwd = lax.broadcasted_iota(jnp.int32, (nc, 2 * S5_STATE), 1) < S5_STATE

    def shifted(x, d):
        down = jnp.where(row >= d, pltpu.roll(x, d, 0), 0.0)
        up = jnp.where(row < nc - d, pltpu.roll(x, nc - d, 0), 0.0)
        return jnp.where(fwd, down, up)

    for k in range(levels):
        d = 2 ** k
        if d >= nc:
            break
        lr = lam_ref[0, k, 0:1, :]
        li = lam_ref[0, k, 1:2, :]
        sr = shifted(re, d)
        si = shifted(im, d)
        re, im = re + sr * lr - si * li, im + sr * li + si * lr
    ent = jnp.concatenate([shifted(re, 1), shifted(im, 1)], axis=1)
    y = jnp.dot(u, w3_ref[0, :width, :], preferred_element_type=F32)
    y = y + jnp.dot(ent.astype(BF16), w3_ref[0, width:, :], preferred_element_type=F32)
    y_ref[0, 0] = y


def _s5_ssm(u5, w1, w3, lamp):
    g, b, nc, width = u5.shape
    levels = lamp.shape[1]
    kern = functools.partial(_s5_kernel, nc=nc, width=width, levels=levels)
    return pl.pallas_call(
        kern,
        grid=(g, b),
        in_specs=[pl.BlockSpec((1, 1, nc, width), lambda gi, bi: (gi, bi, 0, 0)),
                  pl.BlockSpec((1, width, 4 * S5_STATE), lambda gi, bi: (gi, 0, 0)),
                  pl.BlockSpec((1, width + 4 * S5_STATE, width), lambda gi, bi: (gi, 0, 0)),
                  pl.BlockSpec((1, levels, 2, 2 * S5_STATE), lambda gi, bi: (gi, 0, 0, 0))],
        out_specs=pl.BlockSpec((1, 1, nc, width), lambda gi, bi: (gi, bi, 0, 0)),
        out_shape=jax.ShapeDtypeStruct((g, b, nc, width), F32),
        compiler_params=_cparams("parallel", "parallel"),
        name="s5_ssm",
    )(u5, w1, w3, lamp)


def _rope_kernel(qq_ref, kk_ref, v_ref, cos_ref, sin_ref, qa_ref, qb_ref, ko_ref, vo_ref):
    width = DF_HEADS * 2 * DF_QK
    cos = jnp.concatenate([cos_ref[...]] * (width // LANES), axis=1)
    sin = jnp.concatenate([sin_ref[...]] * (width // LANES), axis=1)
    lane = lax.broadcasted_iota(jnp.int32, (1, width), 1)
    first_half = (lane % DF_QK) < (DF_QK // 2)

    def rot(x):
        partner = jnp.where(first_half, pltpu.roll(x, width - DF_QK // 2, 1),
                            pltpu.roll(x, DF_QK // 2, 1))
        return x * cos + partner * sin

    q = rot(qq_ref[0]) * (DF_QK ** -0.5 * LOG2E)
    is_a = (lane % (2 * DF_QK)) < DF_QK
    qa_ref[0] = jnp.where(is_a, q, 0.0).astype(BF16)
    qb_ref[0] = jnp.where(is_a, 0.0, q).astype(BF16)
    ko_ref[0] = rot(kk_ref[0]).astype(BF16)
    vo_ref[0] = v_ref[0].astype(BF16)


def _rope_prep(proj, cos_t, sin_t):
    b, s, _ = proj.shape
    tile = min(TILE_ROPE, s)
    w = DF_WIDTH
    spec = lambda cb: pl.BlockSpec((1, tile, w), lambda bi, i: (bi, i, cb))
    out = pl.BlockSpec((1, tile, w), lambda bi, i: (bi, i, 0))
    tab = pl.BlockSpec((tile, LANES), lambda bi, i: (i, 0))
    return pl.pallas_call(
        _rope_kernel,
        grid=(b, s // tile),
        in_specs=[spec(COL_QQ // w), spec(COL_KK // w), spec(COL_DV // w), tab, tab],
        out_specs=[out, out, out, out],
        out_shape=[jax.ShapeDtypeStruct((b, s, w), BF16)] * 4,
        compiler_params=_cparams("parallel", "parallel"),
        name="rope_prep",
    )(proj, proj, proj, cos_t, sin_t)


def _diff_attn_kernel(qa_ref, qb_ref, k_ref, v_ref, lq1_ref, lk1_ref, lq2_ref, lk2_ref, ng_ref,
                      o_ref, s_ref, *, tq, tk, nk, lam_init):
    q2 = (qa_ref[0], qb_ref[0])
    nt_dims = (((1,), (1,)), ((), ()))

    def fold(x, op, init):
        for j in range(tk // LANES):
            init = op(init, x[:, j * LANES:(j + 1) * LANES])
        return init

    m = [jnp.full((tq, LANES), -jnp.inf, F32)] * 2
    for c in range(nk):
        kc = k_ref[0, c * tk:(c + 1) * tk, :]
        for i in range(2):
            sc = lax.dot_general(q2[i], kc, nt_dims, preferred_element_type=F32)
            s_ref[c, i * tq:(i + 1) * tq, :] = sc
            m[i] = fold(sc, jnp.maximum, m[i])
    m = [jnp.max(x, axis=-1, keepdims=True) for x in m]

    acc = [jnp.zeros((tq, 2 * DF_V), F32)] * 2
    ones = jnp.ones((tk, DF_V), BF16)
    for c in range(nk):
        vc = jnp.concatenate([v_ref[0, c * tk:(c + 1) * tk, :], ones], axis=1)
        for i in range(2):
            e = jnp.exp2((s_ref[c, i * tq:(i + 1) * tq, :] - m[i]).astype(BF16))
            acc[i] = acc[i] + jnp.dot(e, vc, preferred_element_type=F32)
    lam = (jnp.exp(jnp.sum(lq1_ref[...] * lk1_ref[...], axis=-1, keepdims=True))
           - jnp.exp(jnp.sum(lq2_ref[...] * lk2_ref[...], axis=-1, keepdims=True)) + lam_init)
    acc = (acc[0][:, :DF_V] * (1.0 / acc[0][:, DF_V:DF_V + 1])
           - acc[1][:, :DF_V] * (lam / acc[1][:, DF_V:DF_V + 1]))
    ms = jnp.mean(acc * acc, axis=-1, keepdims=True)
    o_ref[0] = acc * lax.rsqrt(ms + RMS_EPS) * ng_ref[...] * (1.0 - lam_init)


def _diff_attn(qa, qb, kk, vv, lq1, lk1, lq2, lk2, norm_g, lam_init):
    b, s, _ = qa.shape
    tq = min(TILE_Q, s)
    tk = min(TILE_K, s)
    nk = s // tk
    kern = functools.partial(_diff_attn_kernel, tq=tq, tk=tk, nk=nk, lam_init=lam_init)
    qspec = pl.BlockSpec((1, tq, LANES), lambda bi, h, i: (bi, i, h))
    kspec = pl.BlockSpec((1, s, LANES), lambda bi, h, i: (bi, 0, h))
    vec64 = pl.BlockSpec((1, DF_QK), lambda bi, h, i: (0, 0))
    return pl.pallas_call(
        kern,
        grid=(b, DF_HEADS, s // tq),
        in_specs=[qspec, qspec, kspec, kspec, vec64, vec64, vec64, vec64,
                  pl.BlockSpec((1, DF_V), lambda bi, h, i: (0, 0))],
        out_specs=pl.BlockSpec((1, tq, DF_V), lambda bi, h, i: (bi, i, h)),
        out_shape=jax.ShapeDtypeStruct((b, s, DF_WIDTH), F32),
        scratch_shapes=[pltpu.VMEM((nk, 2 * tq, tk), F32)],
        compiler_params=_cparams("parallel", "parallel", "arbitrary"),
        name="diff_attn",
    )(qa, qb, kk, vv, lq1, lk1, lq2, lk2, norm_g)


def _layer_norm(x, g, b):
    mu = jnp.mean(x, axis=-1, keepdims=True)
    xc = x - mu
    var = jnp.mean(xc * xc, axis=-1, keepdims=True)
    return xc * lax.rsqrt(var + LN_EPS) * g + b


def _merge_kernel(x_ref, of_ref, ob_ref, z_ref, y5_ref, oc_ref, g0_ref, g1_ref, g2_ref,
                  bgate_ref, dnn_ref, gluw_ref, glub_ref, wbr_ref, wout_ref, lng_ref, lnb_ref,
                  o_ref):
    o = of_ref[0] + ob_ref[0]
    z = z_ref[0]
    parts = []
    for h in range(DN_HEADS):
        oh = o[:, h * DN_DIM:(h + 1) * DN_DIM]
        ms = jnp.mean(oh * oh, axis=-1, keepdims=True)
        parts.append(oh * lax.rsqrt(ms + RMS_EPS) * dnn_ref[...])
    o_a = jnp.concatenate(parts, axis=1) * _silu(z)
    y = jax.nn.gelu(y5_ref[0])
    o_b = y * _sigmoid(_bdot(y, gluw_ref[...]) + glub_ref[...])
    o_c = oc_ref[0]
    bg = bgate_ref[...]
    merged = (_sigmoid(g0_ref[0] + bg[:, 0:D_MODEL]) * _bdot(o_a, wbr_ref[0])
              + _sigmoid(g1_ref[0] + bg[:, D_MODEL:2 * D_MODEL]) * _bdot(o_b, wbr_ref[1])
              + _sigmoid(g2_ref[0] + bg[:, 2 * D_MODEL:]) * _bdot(o_c, wbr_ref[2]))
    hres = DEEPNORM_ALPHA * x_ref[0] + _bdot(merged, wout_ref[...])
    o_ref[0] = _layer_norm(hres, lng_ref[...], lnb_ref[...])


def _merge(x, o_f, o_b, proj, y5, o_c, b_gate, dn_norm, glu_w, glu_b, w_branch, w_out, ln_g, ln_b):
    b, s, _ = x.shape
    tile = min(TILE_MERGE, s)
    row = lambda w, cb=0: pl.BlockSpec((1, tile, w), lambda bi, i: (bi, i, cb))
    full = lambda shape: pl.BlockSpec(shape, lambda bi, i: (0,) * len(shape))
    gcb = COL_GATE // D_MODEL
    return pl.pallas_call(
        _merge_kernel,
        grid=(b, s // tile),
        in_specs=[row(D_MODEL), row(DN_WIDTH), row(DN_WIDTH), row(DN_WIDTH, COL_DN_Z // DN_WIDTH),
                  row(S5_WIDTH), row(DF_WIDTH),
                  row(D_MODEL, gcb), row(D_MODEL, gcb + 1), row(D_MODEL, gcb + 2),
                  full((1, N_BRANCH * D_MODEL)), full((1, DN_DIM)),
                  full((S5_WIDTH, S5_WIDTH)), full((1, S5_WIDTH)),
                  full((N_BRANCH, DN_WIDTH, D_MODEL)), full((D_MODEL, D_MODEL)),
                  full((1, D_MODEL)), full((1, D_MODEL))],
        out_specs=row(D_MODEL),
        out_shape=jax.ShapeDtypeStruct((b, s, D_MODEL), F32),
        compiler_params=_cparams("parallel", "parallel"),
        name="merge",
    )(x, o_f, o_b, proj, y5, o_c, proj, proj, proj, b_gate, dn_norm, glu_w, glu_b,
      w_branch, w_out, ln_g, ln_b)


def _xattn_kernel(x_ref, kv_ref, wq_ref, wo_ref, lng_ref, lnb_ref, o_ref):
    x = x_ref[0]
    q = _bdot(x, wq_ref[...])
    outs = []
    for h in range(XA_HEADS):
        qh = q[:, h * XA_DIM:(h + 1) * XA_DIM]
        kh = kv_ref[0, :, h * XA_DIM:(h + 1) * XA_DIM]
        vh = kv_ref[0, :, XA_WIDTH + h * XA_DIM:XA_WIDTH + (h + 1) * XA_DIM]
        sc = _bdot_nt(qh, kh) * (XA_DIM ** -0.5)
        sc = sc - jnp.max(sc, axis=-1, keepdims=True)
        e = jnp.exp(sc)
        pr = e / jnp.sum(e, axis=-1, keepdims=True)
        outs.append(_bdot(pr, vh))
    o = jnp.concatenate(outs, axis=1)
    hres = DEEPNORM_ALPHA * x + _bdot(o, wo_ref[...])
    o_ref[0] = _layer_norm(hres, lng_ref[...], lnb_ref[...])


def _xattn(x, kv, wq, wo, ln_g, ln_b):
    b, s, _ = x.shape
    tile = min(TILE_XA, s)
    mlen = kv.shape[1]
    row = pl.BlockSpec((1, tile, D_MODEL), lambda bi, i: (bi, i, 0))
    full = lambda shape: pl.BlockSpec(shape, lambda bi, i: (0,) * len(shape))
    return pl.pallas_call(
        _xattn_kernel,
        grid=(b, s // tile),
        in_specs=[row, pl.BlockSpec((1, mlen, 2 * XA_WIDTH), lambda bi, i: (bi, 0, 0)),
                  full((D_MODEL, XA_WIDTH)), full((XA_WIDTH, D_MODEL)),
                  full((1, D_MODEL)), full((1, D_MODEL))],
        out_specs=row,
        out_shape=jax.ShapeDtypeStruct((b, s, D_MODEL), F32),
        compiler_params=_cparams("parallel", "parallel"),
        name="xattn",
    )(x, kv, wq, wo, ln_g, ln_b)


def _router_kernel(x_ref, rw_ref, rb_ref, idx_ref, gate_ref, xb_ref):
    x = x_ref[...]
    xb_ref[...] = x.astype(BF16)
    logits = lax.dot_general(rw_ref[...], x, (((1,), (1,)), ((), ())), precision=HI,
                             preferred_element_type=F32) + rb_ref[...]
    tn = logits.shape[1]
    eid = lax.broadcasted_iota(jnp.int32, (N_EXPERTS, tn), 0)
    vals, ids = [], []
    cur = logits
    for _ in range(TOP_K):
        m = jnp.max(cur, axis=0, keepdims=True)
        sel = jnp.min(jnp.where(cur == m, eid, N_EXPERTS), axis=0, keepdims=True)
        vals.append(m)
        ids.append(sel)
        cur = jnp.where(eid == sel, -jnp.inf, cur)
    top_v = jnp.concatenate(vals, axis=0)
    e = jnp.exp(top_v - top_v[0:1])
    gate_ref[...] = e / jnp.sum(e, axis=0, keepdims=True)
    idx_ref[...] = jnp.concatenate(ids, axis=0)


def _router(x2d, router_w_t, router_b):
    n, _ = x2d.shape
    tile = min(TILE_ROUTER, n)
    return pl.pallas_call(
        _router_kernel,
        grid=(n // tile,),
        in_specs=[pl.BlockSpec((tile, D_MODEL), lambda i: (i, 0)),
                  pl.BlockSpec((N_EXPERTS, D_MODEL), lambda i: (0, 0)),
                  pl.BlockSpec((N_EXPERTS, 1), lambda i: (0, 0))],
        out_specs=[pl.BlockSpec((TOP_K, tile), lambda i: (0, i)),
                   pl.BlockSpec((TOP_K, tile), lambda i: (0, i)),
                   pl.BlockSpec((tile, D_MODEL), lambda i: (i, 0))],
        out_shape=[jax.ShapeDtypeStruct((TOP_K, n), jnp.int32),
                   jax.ShapeDtypeStruct((TOP_K, n), F32),
                   jax.ShapeDtypeStruct((n, D_MODEL), BF16)],
        compiler_params=_cparams("parallel"),
        name="router",
    )(x2d, router_w_t, router_b)


def _rank_kernel(idx_ref, rank_ref, cnt_ref, run_ref, *, tile):
    @pl.when(pl.program_id(0) == 0)
    def _():
        run_ref[...] = jnp.zeros_like(run_ref)

    idx = idx_ref[...]
    eid = lax.broadcasted_iota(jnp.int32, (N_EXPERTS, tile), 0)
    si = lax.broadcasted_iota(jnp.int32, (tile, tile), 0)
    ti = lax.broadcasted_iota(jnp.int32, (tile, tile), 1)
    before = (si < ti).astype(BF16)
    run = run_ref[...]
    ranks = []
    for k in range(TOP_K):
        onehot = eid == idx[k:k + 1]
        oh = onehot.astype(BF16)
        within = jnp.dot(oh, before, preferred_element_type=F32)
        ranks.append(jnp.sum(jnp.where(onehot, within + run, 0.0), axis=0, keepdims=True))
        run = run + jnp.sum(onehot.astype(F32), axis=1, keepdims=True)
    rank_ref[...] = jnp.concatenate(ranks, axis=0).astype(jnp.int32)
    run_ref[...] = run
    cnt_ref[...] = run.astype(jnp.int32)


def _rank(top_i):
    _, n = top_i.shape
    tile = min(TILE_ROUTER, n)
    kern = functools.partial(_rank_kernel, tile=tile)
    return pl.pallas_call(
        kern,
        grid=(n // tile,),
        in_specs=[pl.BlockSpec((TOP_K, tile), lambda i: (0, i))],
        out_specs=[pl.BlockSpec((TOP_K, tile), lambda i: (0, i)),
                   pl.BlockSpec((N_EXPERTS, 1), lambda i: (0, 0))],
        out_shape=[jax.ShapeDtypeStruct((TOP_K, n), jnp.int32),
                   jax.ShapeDtypeStruct((N_EXPERTS, 1), jnp.int32)],
        scratch_shapes=[pltpu.VMEM((N_EXPERTS, 1), F32)],
        compiler_params=_cparams("arbitrary"),
        name="moe_rank",
    )(top_i)


def _expert_kernel(be_ref, on_ref, x_ref, wg_ref, wu_ref, bg_ref, bu_ref, wd_ref, bd_ref, o_ref,
                   wd_bf_ref):
    i = pl.program_id(0)
    live = on_ref[i] != 0

    @pl.when(jnp.logical_or(i == 0, be_ref[i] != be_ref[jnp.maximum(i - 1, 0)]))
    def _():
        wd_bf_ref[...] = wd_ref[0, 0].astype(BF16)

    @pl.when(live)
    def _():
        x = x_ref[...]
        hg = jnp.dot(x, wg_ref[0], preferred_element_type=F32) + bg_ref[0]
        hu = jnp.dot(x, wu_ref[0], preferred_element_type=F32) + bu_ref[0]
        gt = jnp.minimum(hg, SWIGLU_LIMIT)
        up = jnp.clip(hu, -SWIGLU_LIMIT, SWIGLU_LIMIT)
        act = (up + 1.0) * gt * _sigmoid(SWIGLU_ALPHA * gt)
        y = jnp.dot(act.astype(BF16), wd_bf_ref[...], preferred_element_type=F32) + bd_ref[0]
        o_ref[...] = y.astype(o_ref.dtype)

    @pl.when(jnp.logical_not(live))
    def _():
        o_ref[...] = jnp.zeros_like(o_ref)


def _experts(x_rows, blk_e, blk_on, w_g, w_u, b_g, b_u, w_d, l, b_d, *, block):
    rows, _ = x_rows.shape
    wspec = lambda k, m: pl.BlockSpec((1, k, m), lambda i, be, on: (be[i], 0, 0))
    grid_spec = pltpu.PrefetchScalarGridSpec(
        num_scalar_prefetch=2,
        grid=(rows // block,),
        in_specs=[pl.BlockSpec((block, D_MODEL), lambda i, be, on: (i, 0)),
                  wspec(D_MODEL, D_FF), wspec(D_MODEL, D_FF), wspec(1, D_FF), wspec(1, D_FF),
                  pl.BlockSpec((1, 1, D_FF, D_MODEL), lambda i, be, on: (l, be[i], 0, 0)),
                  wspec(1, D_MODEL)],
        out_specs=pl.BlockSpec((block, D_MODEL), lambda i, be, on: (i, 0)),
        scratch_shapes=[pltpu.VMEM((D_FF, D_MODEL), BF16)],
    )
    return pl.pallas_call(
        _expert_kernel,
        grid_spec=grid_spec,
        out_shape=jax.ShapeDtypeStruct((rows, D_MODEL), BF16),
        compiler_params=_cparams("arbitrary"),
        name="moe_experts",
    )(blk_e, blk_on, x_rows, w_g, w_u, b_g, b_u, w_d, b_d)


def _combine_kernel(x_ref, y_ref, gate_ref, lng_ref, lnb_ref, o_ref):
    gate = gate_ref[...]
    h = y_ref[0].astype(F32) * gate[:, 0:1]
    for k in range(1, TOP_K):
        h = h + y_ref[k].astype(F32) * gate[:, k:k + 1]
    o_ref[...] = _layer_norm(DEEPNORM_ALPHA * x_ref[...] + h, lng_ref[...], lnb_ref[...])


def _combine(x2d, y_g, gate_t, ln_g, ln_b):
    n, _ = x2d.shape
    tile = min(TILE_COMBINE, n)
    return pl.pallas_call(
        _combine_kernel,
        grid=(n // tile,),
        in_specs=[pl.BlockSpec((tile, D_MODEL), lambda i: (i, 0)),
                  pl.BlockSpec((TOP_K, tile, D_MODEL), lambda i: (0, i, 0)),
                  pl.BlockSpec((tile, TOP_K), lambda i: (i, 0)),
                  pl.BlockSpec((1, D_MODEL), lambda i: (0, 0)),
                  pl.BlockSpec((1, D_MODEL), lambda i: (0, 0))],
        out_specs=pl.BlockSpec((tile, D_MODEL), lambda i: (i, 0)),
        out_shape=jax.ShapeDtypeStruct((n, D_MODEL), F32),
        compiler_params=_cparams("parallel"),
        name="moe_combine",
    )(x2d, y_g, gate_t, ln_g, ln_b)


def _split_kernel(w_ref, sel_ref, g_ref, u_ref):
    w = w_ref[0, 0].astype(BF16)
    for j in range(w.shape[1] // (2 * LANES)):
        r = jnp.dot(w[:, 2 * LANES * j:2 * LANES * (j + 1)], sel_ref[...],
                    preferred_element_type=F32).astype(BF16)
        g_ref[0, :, LANES * j:LANES * (j + 1)] = r[:, :LANES]
        u_ref[0, :, LANES * j:LANES * (j + 1)] = r[:, LANES:]


def _split_gate_up(w_gu, l):
    _, e, dm, f2 = w_gu.shape
    f = f2 // 2
    tr = 256
    col = jnp.arange(2 * LANES, dtype=jnp.int32)[:, None]
    out = jnp.arange(2 * LANES, dtype=jnp.int32)[None, :]
    sel = jnp.where(out < LANES, col == 2 * out, col == 2 * (out - LANES) + 1).astype(BF16)
    ospec = pl.BlockSpec((1, tr, f), lambda ei, i: (ei, i, 0))
    return pl.pallas_call(
        _split_kernel,
        grid=(e, dm // tr),
        in_specs=[pl.BlockSpec((1, 1, tr, f2), lambda ei, i: (l, ei, i, 0)),
                  pl.BlockSpec((2 * LANES, 2 * LANES), lambda ei, i: (0, 0))],
        out_specs=[ospec, ospec],
        out_shape=[jax.ShapeDtypeStruct((e, dm, f), BF16)] * 2,
        compiler_params=_cparams("parallel", "parallel"),
        name="split_gate_up",
    )(w_gu, sel)


def _moe(x, w):
    b, s, dm = x.shape
    n = b * s
    x2d = x.reshape(n, dm)
    top_i, gate, xb = _router(x2d, w['router_w_t'], w['router_b'])
    rank, counts = _rank(top_i)
    block = MOE_BLOCK
    m = n * TOP_K
    nblk = -(-m // block) + N_EXPERTS
    rows = nblk * block
    counts = counts.reshape(N_EXPERTS)
    padded = (counts + block - 1) // block * block
    pad_end = jnp.cumsum(padded)
    pad_start = pad_end - padded
    experts = jnp.arange(N_EXPERTS, dtype=top_i.dtype)
    dest = rank + jnp.sum(jnp.where(top_i[:, :, None] == experts, pad_start.astype(jnp.int32), 0),
                          axis=-1)
    tok = jnp.broadcast_to(jnp.arange(n, dtype=jnp.int32)[None, :], (TOP_K, n))
    row_tok = jnp.zeros((rows,), jnp.int32).at[dest.reshape(-1)].set(
        tok.reshape(-1), unique_indices=True, mode='promise_in_bounds')
    blk_start = jnp.arange(nblk, dtype=pad_end.dtype) * block
    blk_e = jnp.minimum(jnp.sum(pad_end[None, :] <= blk_start[:, None], axis=1),
                        N_EXPERTS - 1).astype(jnp.int32)
    blk_on = (blk_start < pad_end[-1]).astype(jnp.int32)
    x_rows = xb.at[row_tok].get(mode='promise_in_bounds')
    y_rows = _experts(x_rows, blk_e, blk_on, w['e_w_g'], w['e_w_u'], w['e_b_g'], w['e_b_u'], w['e_w_d'],
                      w['layer'], w['e_b_d'], block=block)
    y_g = y_rows.at[dest].get(mode='promise_in_bounds')
    out = _combine(x2d, y_g, gate.T, w['ln3_g'], w['ln3_b'])
    return out.reshape(b, s, dm)


def _rope_tables(s):
    half = DF_QK // 2
    inv_freq = jnp.power(ROPE_THETA, -jnp.arange(half, dtype=F32) / half)
    ang = jnp.arange(s, dtype=F32)[:, None] * inv_freq[None, :]
    cos = jnp.cos(ang)
    sin = jnp.sin(ang)
    cos_t = jnp.concatenate([cos, cos, cos, cos], axis=1)
    sin_t = jnp.concatenate([-sin, sin, -sin, sin], axis=1)
    return cos_t, sin_t


def _prepare_layer(l, p, s):
    dm = D_MODEL
    w = {}
    w['w_in'] = _pack_in_proj(p['w_in'][l]).astype(BF16)
    pad8 = jnp.zeros((8,), F32)
    pad112 = jnp.zeros((LANES - 16,), F32)
    w['nega'] = jnp.concatenate([pad8, -jnp.exp(p['dn_A_log'][l].reshape(8)), pad112]).reshape(1, LANES)
    w['dtb'] = jnp.concatenate([pad8, p['dn_dt_bias'][l].reshape(8), pad112]).reshape(1, LANES)
    w['dn_conv'] = p['dn_conv'][l]
    t5 = min(S5_CHUNK, s)
    w['s5'] = _s5_operators(p['s5_lam_re'][l], p['s5_lam_im'][l], p['s5_log_dt'][l],
                            p['s5_B_re'][l], p['s5_B_im'][l], p['s5_C_re'][l],
                            p['s5_C_im'][l], p['s5_D'][l], chunk=t5, nc=s // t5)
    for name in ('df_lq1', 'df_lk1', 'df_lq2', 'df_lk2'):
        w[name] = p[name][l].reshape(1, DF_QK)
    w['df_norm'] = p['df_norm'][l].reshape(1, DF_V)
    w['lam_init'] = 0.8 - 0.6 * math.exp(-0.3 * l)
    w['b_gate'] = p['b_gate'][l].reshape(1, -1)
    w['dn_norm'] = p['dn_norm'][l].reshape(1, DN_DIM)
    w['s5_glu_w'] = p['s5_glu_w'][l].astype(BF16)
    w['s5_glu_b'] = p['s5_glu_b'][l].reshape(1, -1)
    w['w_branch'] = p['w_branch'][l].astype(BF16)
    w['w_out'] = p['w_out'][l].astype(BF16)
    for name in ('ln1_g', 'ln1_b', 'ln2_g', 'ln2_b', 'ln3_g', 'ln3_b'):
        w[name] = p[name][l].reshape(1, dm)
    w['xa_wkv'] = p['xa_wkv'][l].astype(BF16)
    w['xa_wq'] = p['xa_wq'][l].astype(BF16)
    w['xa_wo'] = p['xa_wo'][l].astype(BF16)
    w['router_w_t'] = p['router_w'][l].T
    w['router_b'] = p['router_b'][l].reshape(N_EXPERTS, 1)
    w['e_w_g'], w['e_w_u'] = _split_gate_up(p['e_w_gu'], l)
    w['e_b_g'] = p['e_b_gu'][l][:, None, 0::2]
    w['e_b_u'] = p['e_b_gu'][l][:, None, 1::2]
    w['e_w_d'] = p['e_w_down']
    w['layer'] = l
    w['e_b_d'] = p['e_b_down'][l][:, None, :]
    return w


def _layer(x, mem, w, tabs):
    b, s, dm = x.shape
    n = b * s
    proj = _matmul(x.reshape(n, dm), w['w_in'], tm=TILE_MM, tn=LANES * 19)
    proj = proj.reshape(b, s, PROJ_WIDTH)

    qkv, bg = _dn_prep(proj, w['dn_conv'], w['nega'], w['dtb'], chunk=DN_CHUNK)
    bgr = bg[:, :, :16].reshape(b, s // DN_CHUNK, DN_CHUNK, 16).transpose(0, 1, 3, 2)
    o_f, o_b = _dn_scan(qkv, bg, bgr, chunk=DN_CHUNK)

    t5 = min(S5_CHUNK, s)
    nc = s // t5
    u5 = proj[:, :, COL_S5_U:COL_S5_U + S5_WIDTH].reshape(b, nc, t5, S5_GROUPS, S5_GROUP)
    u5 = u5.transpose(3, 0, 1, 2, 4).reshape(S5_GROUPS, b, nc, t5 * S5_GROUP).astype(BF16)
    y5 = _s5_ssm(u5, *w['s5'])
    y5 = y5.reshape(S5_GROUPS, b, nc, t5, S5_GROUP).transpose(1, 2, 3, 0, 4).reshape(b, s, S5_WIDTH)

    qa, qb, kk, vv = _rope_prep(proj, *tabs)
    o_c = _diff_attn(qa, qb, kk, vv, w['df_lq1'], w['df_lk1'], w['df_lq2'], w['df_lk2'],
                     w['df_norm'], w['lam_init'])

    x = _merge(x, o_f, o_b, proj, y5, o_c, w['b_gate'], w['dn_norm'], w['s5_glu_w'], w['s5_glu_b'],
               w['w_branch'], w['w_out'], w['ln1_g'], w['ln1_b'])

    mlen = mem.shape[1]
    kv = _matmul(mem.reshape(b * mlen, dm), w['xa_wkv'], tm=512, tn=1024)
    x = _xattn(x, kv.reshape(b, mlen, 2 * XA_WIDTH), w['xa_wq'], w['xa_wo'], w['ln2_g'], w['ln2_b'])
    return _moe(x, w)


def kernel(x_prompt, x_sample, mem_prompt, mem_sample, w_in, b_gate, dn_conv, dn_A_log, dn_dt_bias,
           dn_norm, s5_lam_re, s5_lam_im, s5_log_dt, s5_B_re, s5_B_im, s5_C_re, s5_C_im, s5_D,
           s5_glu_w, s5_glu_b, df_lq1, df_lk1, df_lq2, df_lk2, df_norm, w_branch, w_out,
           ln1_g, ln1_b, xa_wq, xa_wkv, xa_wo, ln2_g, ln2_b, router_w, router_b,
           e_w_gu, e_b_gu, e_w_down, e_b_down, ln3_g, ln3_b):
    p = dict(w_in=w_in, b_gate=b_gate, dn_conv=dn_conv, dn_A_log=dn_A_log, dn_dt_bias=dn_dt_bias,
             dn_norm=dn_norm, s5_lam_re=s5_lam_re, s5_lam_im=s5_lam_im, s5_log_dt=s5_log_dt,
             s5_B_re=s5_B_re, s5_B_im=s5_B_im, s5_C_re=s5_C_re, s5_C_im=s5_C_im, s5_D=s5_D,
             s5_glu_w=s5_glu_w, s5_glu_b=s5_glu_b, df_lq1=df_lq1, df_lk1=df_lk1, df_lq2=df_lq2,
             df_lk2=df_lk2, df_norm=df_norm, w_branch=w_branch, w_out=w_out, ln1_g=ln1_g,
             ln1_b=ln1_b, xa_wq=xa_wq, xa_wkv=xa_wkv, xa_wo=xa_wo, ln2_g=ln2_g, ln2_b=ln2_b,
             router_w=router_w, router_b=router_b, e_w_gu=e_w_gu, e_b_gu=e_b_gu,
             e_w_down=e_w_down, e_b_down=e_b_down, ln3_g=ln3_g, ln3_b=ln3_b)
    s = x_prompt.shape[1]
    assert x_sample.shape[1] == s
    tabs = _rope_tables(s)
    layers = [_prepare_layer(l, p, s) for l in range(DEPTH)]
    nb = x_prompt.shape[0]
    x = jnp.concatenate([x_prompt, x_sample], axis=0)
    mem = jnp.concatenate([mem_prompt, mem_sample], axis=0)
    for w in layers:
        x = _layer(x, mem, w, tabs)
    return (x[:nb], x[nb:])
```

```python
import functools
import math

import numpy as np
import jax
import jax.numpy as jnp
from jax import lax
from jax.experimental import pallas as pl
from jax.experimental.pallas import tpu as pltpu

F32 = jnp.float32
BF16 = jnp.bfloat16
HI = lax.Precision.HIGHEST

D_MODEL = 1024
DN_HEADS = 4
DN_DIM = 128
DN_WIDTH = DN_HEADS * DN_DIM
DN_CONV = 4
S5_GROUP = 16
S5_WIDTH = 512
S5_GROUPS = S5_WIDTH // S5_GROUP
S5_STATE = 64
DF_HEADS = 4
DF_QK = 64
DF_V = 128
DF_WIDTH = DF_HEADS * DF_V
ROPE_THETA = 10000.0
XA_HEADS = 4
XA_DIM = 128
XA_WIDTH = XA_HEADS * XA_DIM
N_EXPERTS = 32
TOP_K = 4
D_FF = 1024
SWIGLU_LIMIT = 7.0
SWIGLU_ALPHA = 1.702
N_BRANCH = 3
DEPTH = 2
DEEPNORM_ALPHA = (2 * DEPTH) ** 0.25
LN_EPS = 1e-5
RMS_EPS = 1e-6
LOG2E = 1.4426950408889634

LANES = 128
SUBLANES = 8
VMEM_LIMIT = 56 * 1024 * 1024

COL_DN_QKV = 0
COL_DN_Z = 1536
COL_S5_U = 2048
COL_QQ = 2560
COL_KK = 3072
COL_DV = 3584
COL_GATE = 4096
COL_BA = 7168
PROJ_WIDTH = 7296

TILE_MM = 1024
TILE_DN_PREP = 512
TILE_DN_SCAN = 256
DN_CHUNK = 64
DN_CHUNKS_PER_TRIP = 2
S5_CHUNK = 32
TILE_ROPE = 512
TILE_Q = 256
TILE_K = 512
TILE_MERGE = 256
TILE_XA = 512
TILE_ROUTER = 512
MOE_BLOCK = 512
MOE_TOKEN_GROUPS = 2
TILE_COMBINE = 256


def _in_proj_perm():
    o_dn_b, o_dn_a, o_s5 = 2048, 2056, 2064
    o_q1, o_q2, o_k1, o_k2, o_dv, o_gate = 2576, 2832, 3088, 3344, 3600, 4112
    perm = np.full((PROJ_WIDTH,), -1, np.int64)
    perm[0:2048] = np.arange(2048)
    perm[COL_S5_U:COL_S5_U + 512] = o_s5 + np.arange(512)
    for h in range(DF_HEADS):
        j = np.arange(DF_QK)
        perm[COL_QQ + h * 128 + j] = o_q1 + h * DF_QK + j
        perm[COL_QQ + h * 128 + 64 + j] = o_q2 + h * DF_QK + j
        perm[COL_KK + h * 128 + j] = o_k1 + h * DF_QK + j
        perm[COL_KK + h * 128 + 64 + j] = o_k2 + h * DF_QK + j
    perm[COL_DV:COL_DV + 512] = o_dv + np.arange(512)
    perm[COL_GATE:COL_GATE + 3072] = o_gate + np.arange(3072)
    perm[COL_BA:COL_BA + 8] = o_dn_b + np.arange(8)
    perm[COL_BA + 8:COL_BA + 16] = o_dn_a + np.arange(8)
    return perm


_IN_PERM = _in_proj_perm()


def _pack_in_proj(w_in):
    runs, start = [], 0
    for i in range(1, PROJ_WIDTH + 1):
        if i == PROJ_WIDTH or _IN_PERM[i] != _IN_PERM[i - 1] + (1 if _IN_PERM[i - 1] >= 0 else 0):
            runs.append((start, i))
            start = i
    pieces = []
    for a, b in runs:
        src = int(_IN_PERM[a])
        pieces.append(jnp.zeros((w_in.shape[0], b - a), w_in.dtype) if src < 0
                      else w_in[:, src:src + (b - a)])
    return jnp.concatenate(pieces, axis=1)


def _cparams(*sem):
    return pltpu.CompilerParams(dimension_semantics=sem, vmem_limit_bytes=VMEM_LIMIT)


def _bdot(a, b):
    return jnp.dot(a.astype(BF16), b.astype(BF16), preferred_element_type=F32)


def _bdot_nt(a, b):
    return lax.dot_general(a.astype(BF16), b.astype(BF16), (((1,), (1,)), ((), ())),
                           preferred_element_type=F32)


def _bdot_tn(a, b):
    return lax.dot_general(a.astype(BF16), b.astype(BF16), (((0,), (0,)), ((), ())),
                           preferred_element_type=F32)


def _sigmoid(x):
    return 0.5 * jnp.tanh(0.5 * x) + 0.5


def _silu(x):
    return x * _sigmoid(x)


def _mm_kernel(x_ref, w_ref, o_ref):
    o_ref[...] = jnp.dot(x_ref[...].astype(BF16), w_ref[...],
                         preferred_element_type=F32).astype(o_ref.dtype)


def _matmul(x, w, *, tm, tn, out_dtype=F32):
    n, k = x.shape
    m = w.shape[1]
    tm = min(tm, n)
    tn = min(tn, m)
    return pl.pallas_call(
        _mm_kernel,
        grid=(n // tm, m // tn),
        in_specs=[pl.BlockSpec((tm, k), lambda i, j: (i, 0)),
                  pl.BlockSpec((k, tn), lambda i, j: (0, j))],
        out_specs=pl.BlockSpec((tm, tn), lambda i, j: (i, j)),
        out_shape=jax.ShapeDtypeStruct((n, m), out_dtype),
        compiler_params=_cparams("parallel", "arbitrary"),
        name="matmul",
    )(x, w)


def _dn_prep_kernel(x_ref, prev_ref, next_ref, ba_ref, cw_ref, nega_ref, dtb_ref,
                    qkv_ref, bg_ref, *, tile, chunk):
    i = pl.program_id(1)
    last = pl.num_programs(1) - 1
    x = x_ref[0]
    prev = jnp.where(i > 0, prev_ref[0], 0.0)
    nxt = jnp.where(i < last, next_ref[0], 0.0)
    ext = jnp.concatenate([prev, x, nxt], axis=0)
    cw = cw_ref[...]
    acc = cw[0:1] * ext[6:6 + tile]
    for j in range(1, DN_CONV):
        acc = acc + cw[j:j + 1] * ext[6 + j:6 + j + tile]
    act = _silu(acc)
    for h in range(DN_HEADS):
        lo = h * DN_DIM
        qh = act[:, lo:lo + DN_DIM]
        qn = qh * lax.rsqrt(jnp.sum(qh * qh, axis=-1, keepdims=True) + 1e-6)
        qkv_ref[0, :, lo:lo + DN_DIM] = qn * (DN_DIM ** -0.5)
        kh = act[:, DN_WIDTH + lo:DN_WIDTH + lo + DN_DIM]
        qkv_ref[0, :, DN_WIDTH + lo:DN_WIDTH + lo + DN_DIM] = kh * lax.rsqrt(
            jnp.sum(kh * kh, axis=-1, keepdims=True) + 1e-6)
    qkv_ref[0, :, 2 * DN_WIDTH:] = act[:, 2 * DN_WIDTH:]

    ba = ba_ref[0]
    beta = _sigmoid(ba)
    z = ba + dtb_ref[...]
    softplus = jnp.maximum(z, 0.0) + jnp.log(1.0 + jnp.exp(-jnp.abs(z)))
    g = nega_ref[...] * softplus
    row = lax.broadcasted_iota(jnp.int32, (chunk, chunk), 0)
    col = lax.broadcasted_iota(jnp.int32, (chunk, chunk), 1)
    lower = (row >= col).astype(F32)
    upper = (row <= col).astype(F32)
    lane = lax.broadcasted_iota(jnp.int32, (chunk, LANES), 1)
    fwd_lane = lane < 8 + DN_HEADS
    for c in range(tile // chunk):
        gc = g[c * chunk:(c + 1) * chunk]
        cf = jnp.dot(lower, gc, precision=HI, preferred_element_type=F32)
        cb = jnp.dot(upper, gc, precision=HI, preferred_element_type=F32)
        gcum = jnp.where(fwd_lane, cf, cb)
        bg_ref[0, c * chunk:(c + 1) * chunk, :] = jnp.where(
            lane < 8, beta[c * chunk:(c + 1) * chunk], gcum)


def _dn_prep(proj, conv_w, nega, dtb, *, chunk):
    b, s, _ = proj.shape
    tile = min(TILE_DN_PREP, s)
    nt = s // tile
    r8 = tile // SUBLANES
    kern = functools.partial(_dn_prep_kernel, tile=tile, chunk=chunk)
    return pl.pallas_call(
        kern,
        grid=(b, nt),
        in_specs=[
            pl.BlockSpec((1, tile, 3 * DN_WIDTH), lambda bi, i: (bi, i, 0)),
            pl.BlockSpec((1, SUBLANES, 3 * DN_WIDTH),
                         lambda bi, i: (bi, jnp.maximum(i * r8 - 1, 0), 0)),
            pl.BlockSpec((1, SUBLANES, 3 * DN_WIDTH),
                         lambda bi, i: (bi, jnp.minimum((i + 1) * r8, s // SUBLANES - 1), 0)),
            pl.BlockSpec((1, tile, LANES), lambda bi, i: (bi, i, COL_BA // LANES)),
            pl.BlockSpec((DN_CONV, 3 * DN_WIDTH), lambda bi, i: (0, 0)),
            pl.BlockSpec((1, LANES), lambda bi, i: (0, 0)),
            pl.BlockSpec((1, LANES), lambda bi, i: (0, 0)),
        ],
        out_specs=[pl.BlockSpec((1, tile, 3 * DN_WIDTH), lambda bi, i: (bi, i, 0)),
                   pl.BlockSpec((1, tile, LANES), lambda bi, i: (bi, i, 0))],
        out_shape=[jax.ShapeDtypeStruct((b, s, 3 * DN_WIDTH), F32),
                   jax.ShapeDtypeStruct((b, s, LANES), F32)],
        compiler_params=_cparams("parallel", "parallel"),
        name="dn_prep",
    )(proj, proj, proj, proj, conv_w, nega, dtb)


def _dn_local(chains, chunk):
    ri = lax.broadcasted_iota(jnp.int32, (chunk, chunk), 0)
    ci = lax.broadcasted_iota(jnp.int32, (chunk, chunk), 1)
    pre = []
    for d, h, q, k, v, bgc, rows in chains:
        li = d * DN_HEADS + h
        beta = bgc[:, li:li + 1]
        gcol = bgc[:, 8 + li:8 + li + 1]
        grow = rows[8 + li:8 + li + 1, :]
        gtot = gcol[chunk - 1:chunk] if d == 0 else gcol[0:1]
        incl = (ri >= ci) if d == 0 else (ri <= ci)
        strict = (ri > ci) if d == 0 else (ri < ci)
        eg = jnp.exp(gcol)
        kb = k * beta
        decay = jnp.where(incl, jnp.exp(jnp.where(incl, gcol - grow, 0.0)), 0.0)
        pre.append(dict(k=k, qg=q * eg, kd=k * jnp.exp(gtot - gcol), decay=decay, strict=strict,
                        x2=jnp.concatenate([v * beta, kb * eg], axis=1), sdec=jnp.exp(gtot),
                        qkb=jnp.concatenate([q, kb], axis=0)))
    qk = [_bdot_nt(c['qkb'], c['k']) for c in pre]
    intra = [x[:chunk] * c['decay'] for x, c in zip(qk, pre)]
    p = [-jnp.where(c['strict'], x[chunk:] * c['decay'], 0.0) for x, c in zip(qk, pre)]
    r = p
    p = [_bdot(x, x) for x in p]
    for _ in range(int(math.log2(chunk)) - 2):
        rp = [_bdot(jnp.concatenate([ri_, pi_], axis=0), pi_) for ri_, pi_ in zip(r, p)]
        r = [ri_ + pi_ + x[:chunk] for ri_, pi_, x in zip(r, p, rp)]
        p = [x[chunk:] for x in rp]
    r = [ri_ + pi_ + _bdot(ri_, pi_) for ri_, pi_ in zip(r, p)]
    uw = [c['x2'] + _bdot(ri_, c['x2']) for ri_, c in zip(r, pre)]
    return [dict(u=x[:, :DN_DIM], wq=jnp.concatenate([x[:, DN_DIM:], c['qg']], axis=0),
                 intra=it, kd=c['kd'], sdec=c['sdec']) for x, it, c in zip(uw, intra, pre)]


def _dn_state(local, states, chunk):
    ws = [_bdot(c['wq'], st) for c, st in zip(local, states)]
    v_new = [c['u'] - y[:chunk] for c, y in zip(local, ws)]
    out = [y[chunk:] + _bdot(c['intra'], vn) for y, c, vn in zip(ws, local, v_new)]
    st_new = [st * c['sdec'] + _bdot_tn(c['kd'], vn) for st, c, vn in zip(states, local, v_new)]
    return out, st_new


def _dn_scan_kernel(xf_ref, xb_ref, gf_ref, gb_ref, rf_ref, rb_ref, of_ref, ob_ref, st_ref,
                    *, tile, chunk):
    @pl.when(pl.program_id(1) == 0)
    def _():
        st_ref[...] = jnp.zeros_like(st_ref)

    nct = tile // chunk
    nchain = 2 * DN_HEADS

    def body(j, carry):
        chains, where = [], []
        for slot in range(DN_CHUNKS_PER_TRIP):
            for d in range(2):
                c = DN_CHUNKS_PER_TRIP * j + slot
                c = c if d == 0 else nct - 1 - c
                r0 = pl.multiple_of(c * chunk, chunk)
                x_ref, g_ref, r_ref = (xf_ref, gf_ref, rf_ref) if d == 0 else (xb_ref, gb_ref, rb_ref)
                bgc = g_ref[0, pl.ds(r0, chunk), :]
                rows = r_ref[0, c]
                for h in range(DN_HEADS):
                    lo = h * DN_DIM
                    q = x_ref[0, pl.ds(r0, chunk), lo:lo + DN_DIM]
                    k = x_ref[0, pl.ds(r0, chunk), DN_WIDTH + lo:DN_WIDTH + lo + DN_DIM]
                    v = x_ref[0, pl.ds(r0, chunk), 2 * DN_WIDTH + lo:2 * DN_WIDTH + lo + DN_DIM]
                    chains.append((d, h, q, k, v, bgc, rows))
                    where.append((d, h, r0))
        states = [st_ref[i] for i in range(nchain)]
        local = _dn_local(chains, chunk)
        outs = []
        for slot in range(DN_CHUNKS_PER_TRIP):
            out, states = _dn_state(local[slot * nchain:(slot + 1) * nchain], states, chunk)
            outs += out
        for (d, h, r0), out in zip(where, outs):
            o_ref = of_ref if d == 0 else ob_ref
            o_ref[0, pl.ds(r0, chunk), h * DN_DIM:(h + 1) * DN_DIM] = out
        for i in range(nchain):
            st_ref[i] = states[i]
        return carry

    lax.fori_loop(0, nct // DN_CHUNKS_PER_TRIP, body, 0)


def _dn_scan(qkv, bg, bgr, *, chunk):
    b, s, _ = qkv.shape
    tile = min(TILE_DN_SCAN, s)
    nt = s // tile
    nct = tile // chunk
    kern = functools.partial(_dn_scan_kernel, tile=tile, chunk=chunk)
    fwd = lambda bi, i: (bi, i, 0)
    bwd = lambda bi, i: (bi, nt - 1 - i, 0)
    fwd4 = lambda bi, i: (bi, i, 0, 0)
    bwd4 = lambda bi, i: (bi, nt - 1 - i, 0, 0)
    return pl.pallas_call(
        kern,
        grid=(b, nt),
        in_specs=[pl.BlockSpec((1, tile, 3 * DN_WIDTH), fwd),
                  pl.BlockSpec((1, tile, 3 * DN_WIDTH), bwd),
                  pl.BlockSpec((1, tile, LANES), fwd),
                  pl.BlockSpec((1, tile, LANES), bwd),
                  pl.BlockSpec((1, nct, 16, chunk), fwd4),
                  pl.BlockSpec((1, nct, 16, chunk), bwd4)],
        out_specs=[pl.BlockSpec((1, tile, DN_WIDTH), fwd),
                   pl.BlockSpec((1, tile, DN_WIDTH), bwd)],
        out_shape=[jax.ShapeDtypeStruct((b, s, DN_WIDTH), F32),
                   jax.ShapeDtypeStruct((b, s, DN_WIDTH), F32)],
        scratch_shapes=[pltpu.VMEM((2 * DN_HEADS, DN_DIM, DN_DIM), F32)],
        compiler_params=_cparams("parallel", "arbitrary"),
        name="dn_scan",
    )(qkv, qkv, bg, bg, bgr, bgr)


def _s5_operators(lam_re, lam_im, log_dt, b_re, b_im, c_re, c_im, d_skip, *, chunk, nc):
    t = chunk
    levels = max(1, int(math.log2(nc)))
    tau = jnp.arange(t + 1, dtype=F32)
    kern = []
    lamp = []
    for d in range(2):
        lam = lax.complex(lam_re[d], lam_im[d])
        dt = jnp.exp(log_dt[d])[:, None]
        lam_dt = lam * dt
        lam_bar = jnp.exp(lam_dt)
        bbar = ((lam_bar - 1.0) / lam)[:, :, None] * lax.complex(b_re[d], b_im[d])
        cc = lax.complex(c_re[d], c_im[d])
        pw = jnp.exp(lam_dt[:, None, :] * tau[None, :, None].astype(jnp.complex64))
        kern.append((pw, bbar, cc))
        steps = (t * 2.0 ** jnp.arange(levels, dtype=F32)).astype(jnp.complex64)
        lamp.append(jnp.exp(lam_dt[:, None, :] * steps[None, :, None]))
    (pwf, bbf, ccf), (pwb, bbb, ccb) = kern
    ef = jnp.einsum('gsp,gpc->gscp', pwf[:, t - 1::-1, :], bbf)
    eb = jnp.einsum('gsp,gpc->gscp', pwb[:, :t, :], bbb)
    w1 = jnp.concatenate([jnp.real(ef), jnp.real(eb), jnp.imag(ef), jnp.imag(eb)], axis=-1)
    w1 = w1.reshape(S5_GROUPS, t * S5_GROUP, 4 * S5_STATE)
    kf = jnp.real(jnp.einsum('gcp,gtp,gpd->gtcd', ccf, pwf[:, :t, :], bbf))
    kb = jnp.real(jnp.einsum('gcp,gtp,gpd->gtcd', ccb, pwb[:, :t, :], bbb))
    si = jnp.arange(t)[:, None]
    ti = jnp.arange(t)[None, :]
    lag_f = jnp.clip(ti - si, 0, t - 1)
    lag_b = jnp.clip(si - ti, 0, t - 1)
    toep = (jnp.where((ti >= si)[None, :, :, None, None], kf[:, lag_f], 0.0)
            + jnp.where((si >= ti)[None, :, :, None, None], kb[:, lag_b], 0.0))
    eye_t = jnp.eye(t, dtype=F32)[None, :, :, None, None]
    eye_c = jnp.eye(S5_GROUP, dtype=F32)[None, None, None, :, :]
    toep = toep + eye_t * eye_c * d_skip.reshape(S5_GROUPS, 1, 1, S5_GROUP, 1)
    w_intra = toep.transpose(0, 1, 4, 2, 3).reshape(S5_GROUPS, t * S5_GROUP, t * S5_GROUP)
    mf = jnp.einsum('gcp,gtp->gptc', ccf, pwf[:, 1:t + 1, :])
    mb = jnp.einsum('gcp,gtp->gptc', ccb, pwb[:, t:0:-1, :])
    w_state = jnp.concatenate([jnp.real(mf), jnp.real(mb), -jnp.imag(mf), -jnp.imag(mb)], axis=1)
    w_state = w_state.reshape(S5_GROUPS, 4 * S5_STATE, t * S5_GROUP)
    w3 = jnp.concatenate([w_intra, w_state], axis=1)
    lam_r = jnp.concatenate([jnp.real(lamp[0]), jnp.real(lamp[1])], axis=-1)
    lam_i = jnp.concatenate([jnp.imag(lamp[0]), jnp.imag(lamp[1])], axis=-1)
    lamp = jnp.stack([lam_r, lam_i], axis=2)
    return w1.astype(BF16), w3.astype(BF16), lamp.astype(F32)


def _s5_kernel(u_ref, w1_ref, w3_ref, lam_ref, y_ref, *, nc, width, levels):
    u = u_ref[0, 0]
    loc = jnp.dot(u, w1_ref[0], preferred_element_type=F32)
    re = loc[:, :2 * S5_STATE]
    im = loc[:, 2 * S5_STATE:]
    row = lax.broadcasted_iota(jnp.int32, (nc, 2 * S5_STATE), 0)
    fwd = lax.broadcasted_iota(jnp.int32, (nc, 2 * S5_STATE), 1) < S5_STATE

    def shifted(x, d):
        down = jnp.where(row >= d, pltpu.roll(x, d, 0), 0.0)
        up = jnp.where(row < nc - d, pltpu.roll(x, nc - d, 0), 0.0)
        return jnp.where(fwd, down, up)

    for k in range(levels):
        d = 2 ** k
        if d >= nc:
            break
        lr = lam_ref[0, k, 0:1, :]
        li = lam_ref[0, k, 1:2, :]
        sr = shifted(re, d)
        si = shifted(im, d)
        re, im = re + sr * lr - si * li, im + sr * li + si * lr
    ent = jnp.concatenate([shifted(re, 1), shifted(im, 1)], axis=1)
    y = jnp.dot(u, w3_ref[0, :width, :], preferred_element_type=F32)
    y = y + jnp.dot(ent.astype(BF16), w3_ref[0, width:, :], preferred_element_type=F32)
    y_ref[0, 0] = y


def _s5_ssm(u5, w1, w3, lamp):
    g, b, nc, width = u5.shape
    levels = lamp.shape[1]
    kern = functools.partial(_s5_kernel, nc=nc, width=width, levels=levels)
    return pl.pallas_call(
        kern,
        grid=(g, b),
        in_specs=[pl.BlockSpec((1, 1, nc, width), lambda gi, bi: (gi, bi, 0, 0)),
                  pl.BlockSpec((1, width, 4 * S5_STATE), lambda gi, bi: (gi, 0, 0)),
                  pl.BlockSpec((1, width + 4 * S5_STATE, width), lambda gi, bi: (gi, 0, 0)),
                  pl.BlockSpec((1, levels, 2, 2 * S5_STATE), lambda gi, bi: (gi, 0, 0, 0))],
        out_specs=pl.BlockSpec((1, 1, nc, width), lambda gi, bi: (gi, bi, 0, 0)),
        out_shape=jax.ShapeDtypeStruct((g, b, nc, width), F32),
        compiler_params=_cparams("parallel", "parallel"),
        name="s5_ssm",
    )(u5, w1, w3, lamp)


def _rope_kernel(qq_ref, kk_ref, v_ref, cos_ref, sin_ref, qa_ref, qb_ref, ko_ref, vo_ref):
    width = DF_HEADS * 2 * DF_QK
    cos = jnp.concatenate([cos_ref[...]] * (width // LANES), axis=1)
    sin = jnp.concatenate([sin_ref[...]] * (width // LANES), axis=1)
    lane = lax.broadcasted_iota(jnp.int32, (1, width), 1)
    first_half = (lane % DF_QK) < (DF_QK // 2)

    def rot(x):
        partner = jnp.where(first_half, pltpu.roll(x, width - DF_QK // 2, 1),
                            pltpu.roll(x, DF_QK // 2, 1))
        return x * cos + partner * sin

    q = rot(qq_ref[0]) * (DF_QK ** -0.5 * LOG2E)
    is_a = (lane % (2 * DF_QK)) < DF_QK
    qa_ref[0] = jnp.where(is_a, q, 0.0).astype(BF16)
    qb_ref[0] = jnp.where(is_a, 0.0, q).astype(BF16)
    ko_ref[0] = rot(kk_ref[0]).astype(BF16)
    vo_ref[0] = v_ref[0].astype(BF16)


def _rope_prep(proj, cos_t, sin_t):
    b, s, _ = proj.shape
    tile = min(TILE_ROPE, s)
    w = DF_WIDTH
    spec = lambda cb: pl.BlockSpec((1, tile, w), lambda bi, i: (bi, i, cb))
    out = pl.BlockSpec((1, tile, w), lambda bi, i: (bi, i, 0))
    tab = pl.BlockSpec((tile, LANES), lambda bi, i: (i, 0))
    return pl.pallas_call(
        _rope_kernel,
        grid=(b, s // tile),
        in_specs=[spec(COL_QQ // w), spec(COL_KK // w), spec(COL_DV // w), tab, tab],
        out_specs=[out, out, out, out],
        out_shape=[jax.ShapeDtypeStruct((b, s, w), BF16)] * 4,
        compiler_params=_cparams("parallel", "parallel"),
        name="rope_prep",
    )(proj, proj, proj, cos_t, sin_t)


def _diff_attn_kernel(qa_ref, qb_ref, k_ref, v_ref, lq1_ref, lk1_ref, lq2_ref, lk2_ref, ng_ref,
                      o_ref, s_ref, *, tq, tk, nk, lam_init):
    q2 = (qa_ref[0], qb_ref[0])
    nt_dims = (((1,), (1,)), ((), ()))

    def fold(x, op, init):
        for j in range(tk // LANES):
            init = op(init, x[:, j * LANES:(j + 1) * LANES])
        return init

    m = [jnp.full((tq, LANES), -jnp.inf, F32)] * 2
    for c in range(nk):
        kc = k_ref[0, c * tk:(c + 1) * tk, :]
        for i in range(2):
            sc = lax.dot_general(q2[i], kc, nt_dims, preferred_element_type=F32)
            s_ref[c, i * tq:(i + 1) * tq, :] = sc
            m[i] = fold(sc, jnp.maximum, m[i])
    m = [jnp.max(x, axis=-1, keepdims=True) for x in m]

    acc = [jnp.zeros((tq, 2 * DF_V), F32)] * 2
    ones = jnp.ones((tk, DF_V), BF16)
    for c in range(nk):
        vc = jnp.concatenate([v_ref[0, c * tk:(c + 1) * tk, :], ones], axis=1)
        for i in range(2):
            e = jnp.exp2((s_ref[c, i * tq:(i + 1) * tq, :] - m[i]).astype(BF16))
            acc[i] = acc[i] + jnp.dot(e, vc, preferred_element_type=F32)
    lam = (jnp.exp(jnp.sum(lq1_ref[...] * lk1_ref[...], axis=-1, keepdims=True))
           - jnp.exp(jnp.sum(lq2_ref[...] * lk2_ref[...], axis=-1, keepdims=True)) + lam_init)
    acc = (acc[0][:, :DF_V] * (1.0 / acc[0][:, DF_V:DF_V + 1])
           - acc[1][:, :DF_V] * (lam / acc[1][:, DF_V:DF_V + 1]))
    ms = jnp.mean(acc * acc, axis=-1, keepdims=True)
    o_ref[0] = acc * lax.rsqrt(ms + RMS_EPS) * ng_ref[...] * (1.0 - lam_init)


def _diff_attn(qa, qb, kk, vv, lq1, lk1, lq2, lk2, norm_g, lam_init):
    b, s, _ = qa.shape
    tq = min(TILE_Q, s)
    tk = min(TILE_K, s)
    nk = s // tk
    kern = functools.partial(_diff_attn_kernel, tq=tq, tk=tk, nk=nk, lam_init=lam_init)
    qspec = pl.BlockSpec((1, tq, LANES), lambda bi, h, i: (bi, i, h))
    kspec = pl.BlockSpec((1, s, LANES), lambda bi, h, i: (bi, 0, h))
    vec64 = pl.BlockSpec((1, DF_QK), lambda bi, h, i: (0, 0))
    return pl.pallas_call(
        kern,
        grid=(b, DF_HEADS, s // tq),
        in_specs=[qspec, qspec, kspec, kspec, vec64, vec64, vec64, vec64,
                  pl.BlockSpec((1, DF_V), lambda bi, h, i: (0, 0))],
        out_specs=pl.BlockSpec((1, tq, DF_V), lambda bi, h, i: (bi, i, h)),
        out_shape=jax.ShapeDtypeStruct((b, s, DF_WIDTH), F32),
        scratch_shapes=[pltpu.VMEM((nk, 2 * tq, tk), F32)],
        compiler_params=_cparams("parallel", "parallel", "arbitrary"),
        name="diff_attn",
    )(qa, qb, kk, vv, lq1, lk1, lq2, lk2, norm_g)


def _layer_norm(x, g, b):
    mu = jnp.mean(x, axis=-1, keepdims=True)
    xc = x - mu
    var = jnp.mean(xc * xc, axis=-1, keepdims=True)
    return xc * lax.rsqrt(var + LN_EPS) * g + b


def _merge_kernel(x_ref, of_ref, ob_ref, z_ref, y5_ref, oc_ref, g0_ref, g1_ref, g2_ref,
                  bgate_ref, dnn_ref, gluw_ref, glub_ref, wbr_ref, wout_ref, lng_ref, lnb_ref,
                  o_ref):
    o = of_ref[0] + ob_ref[0]
    z = z_ref[0]
    parts = []
    for h in range(DN_HEADS):
        oh = o[:, h * DN_DIM:(h + 1) * DN_DIM]
        ms = jnp.mean(oh * oh, axis=-1, keepdims=True)
        parts.append(oh * lax.rsqrt(ms + RMS_EPS) * dnn_ref[...])
    o_a = jnp.concatenate(parts, axis=1) * _silu(z)
    y = jax.nn.gelu(y5_ref[0])
    o_b = y * _sigmoid(_bdot(y, gluw_ref[...]) + glub_ref[...])
    o_c = oc_ref[0]
    bg = bgate_ref[...]
    merged = (_sigmoid(g0_ref[0] + bg[:, 0:D_MODEL]) * _bdot(o_a, wbr_ref[0])
              + _sigmoid(g1_ref[0] + bg[:, D_MODEL:2 * D_MODEL]) * _bdot(o_b, wbr_ref[1])
              + _sigmoid(g2_ref[0] + bg[:, 2 * D_MODEL:]) * _bdot(o_c, wbr_ref[2]))
    hres = DEEPNORM_ALPHA * x_ref[0] + _bdot(merged, wout_ref[...])
    o_ref[0] = _layer_norm(hres, lng_ref[...], lnb_ref[...])


def _merge(x, o_f, o_b, proj, y5, o_c, b_gate, dn_norm, glu_w, glu_b, w_branch, w_out, ln_g, ln_b):
    b, s, _ = x.shape
    tile = min(TILE_MERGE, s)
    row = lambda w, cb=0: pl.BlockSpec((1, tile, w), lambda bi, i: (bi, i, cb))
    full = lambda shape: pl.BlockSpec(shape, lambda bi, i: (0,) * len(shape))
    gcb = COL_GATE // D_MODEL
    return pl.pallas_call(
        _merge_kernel,
        grid=(b, s // tile),
        in_specs=[row(D_MODEL), row(DN_WIDTH), row(DN_WIDTH), row(DN_WIDTH, COL_DN_Z // DN_WIDTH),
                  row(S5_WIDTH), row(DF_WIDTH),
                  row(D_MODEL, gcb), row(D_MODEL, gcb + 1), row(D_MODEL, gcb + 2),
                  full((1, N_BRANCH * D_MODEL)), full((1, DN_DIM)),
                  full((S5_WIDTH, S5_WIDTH)), full((1, S5_WIDTH)),
                  full((N_BRANCH, DN_WIDTH, D_MODEL)), full((D_MODEL, D_MODEL)),
                  full((1, D_MODEL)), full((1, D_MODEL))],
        out_specs=row(D_MODEL),
        out_shape=jax.ShapeDtypeStruct((b, s, D_MODEL), F32),
        compiler_params=_cparams("parallel", "parallel"),
        name="merge",
    )(x, o_f, o_b, proj, y5, o_c, proj, proj, proj, b_gate, dn_norm, glu_w, glu_b,
      w_branch, w_out, ln_g, ln_b)


def _xattn_kernel(x_ref, kv_ref, wq_ref, wo_ref, lng_ref, lnb_ref, o_ref):
    x = x_ref[0]
    q = _bdot(x, wq_ref[...])
    outs = []
    for h in range(XA_HEADS):
        qh = q[:, h * XA_DIM:(h + 1) * XA_DIM]
        kh = kv_ref[0, :, h * XA_DIM:(h + 1) * XA_DIM]
        vh = kv_ref[0, :, XA_WIDTH + h * XA_DIM:XA_WIDTH + (h + 1) * XA_DIM]
        sc = _bdot_nt(qh, kh) * (XA_DIM ** -0.5)
        sc = sc - jnp.max(sc, axis=-1, keepdims=True)
        e = jnp.exp(sc)
        pr = e / jnp.sum(e, axis=-1, keepdims=True)
        outs.append(_bdot(pr, vh))
    o = jnp.concatenate(outs, axis=1)
    hres = DEEPNORM_ALPHA * x + _bdot(o, wo_ref[...])
    o_ref[0] = _layer_norm(hres, lng_ref[...], lnb_ref[...])


def _xattn(x, kv, wq, wo, ln_g, ln_b):
    b, s, _ = x.shape
    tile = min(TILE_XA, s)
    mlen = kv.shape[1]
    row = pl.BlockSpec((1, tile, D_MODEL), lambda bi, i: (bi, i, 0))
    full = lambda shape: pl.BlockSpec(shape, lambda bi, i: (0,) * len(shape))
    return pl.pallas_call(
        _xattn_kernel,
        grid=(b, s // tile),
        in_specs=[row, pl.BlockSpec((1, mlen, 2 * XA_WIDTH), lambda bi, i: (bi, 0, 0)),
                  full((D_MODEL, XA_WIDTH)), full((XA_WIDTH, D_MODEL)),
                  full((1, D_MODEL)), full((1, D_MODEL))],
        out_specs=row,
        out_shape=jax.ShapeDtypeStruct((b, s, D_MODEL), F32),
        compiler_params=_cparams("parallel", "parallel"),
        name="xattn",
    )(x, kv, wq, wo, ln_g, ln_b)


def _router_kernel(x_ref, rw_ref, rb_ref, idx_ref, gate_ref, xb_ref):
    x = x_ref[...]
    xb_ref[...] = x.astype(BF16)
    logits = lax.dot_general(rw_ref[...], x, (((1,), (1,)), ((), ())), precision=HI,
                             preferred_element_type=F32) + rb_ref[...]
    tn = logits.shape[1]
    eid = lax.broadcasted_iota(jnp.int32, (N_EXPERTS, tn), 0)
    vals, ids = [], []
    cur = logits
    for _ in range(TOP_K):
        m = jnp.max(cur, axis=0, keepdims=True)
        sel = jnp.min(jnp.where(cur == m, eid, N_EXPERTS), axis=0, keepdims=True)
        vals.append(m)
        ids.append(sel)
        cur = jnp.where(eid == sel, -jnp.inf, cur)
    top_v = jnp.concatenate(vals, axis=0)
    e = jnp.exp(top_v - top_v[0:1])
    gate_ref[...] = e / jnp.sum(e, axis=0, keepdims=True)
    idx_ref[...] = jnp.concatenate(ids, axis=0)


def _router(x2d, router_w_t, router_b, row0, n):
    tile = min(TILE_ROUTER, n)
    t0 = row0 // tile
    return pl.pallas_call(
        _router_kernel,
        grid=(n // tile,),
        in_specs=[pl.BlockSpec((tile, D_MODEL), lambda i: (i + t0, 0)),
                  pl.BlockSpec((N_EXPERTS, D_MODEL), lambda i: (0, 0)),
                  pl.BlockSpec((N_EXPERTS, 1), lambda i: (0, 0))],
        out_specs=[pl.BlockSpec((TOP_K, tile), lambda i: (0, i)),
                   pl.BlockSpec((TOP_K, tile), lambda i: (0, i)),
                   pl.BlockSpec((tile, D_MODEL), lambda i: (i, 0))],
        out_shape=[jax.ShapeDtypeStruct((TOP_K, n), jnp.int32),
                   jax.ShapeDtypeStruct((TOP_K, n), F32),
                   jax.ShapeDtypeStruct((n, D_MODEL), BF16)],
        compiler_params=_cparams("parallel"),
        name="router",
    )(x2d, router_w_t, router_b)


def _rank_kernel(idx_ref, rank_ref, cnt_ref, run_ref, *, tile):
    @pl.when(pl.program_id(0) == 0)
    def _():
        run_ref[...] = jnp.zeros_like(run_ref)

    idx = idx_ref[...]
    eid = lax.broadcasted_iota(jnp.int32, (N_EXPERTS, tile), 0)
    si = lax.broadcasted_iota(jnp.int32, (tile, tile), 0)
    ti = lax.broadcasted_iota(jnp.int32, (tile, tile), 1)
    before = (si < ti).astype(BF16)
    run = run_ref[...]
    ranks = []
    for k in range(TOP_K):
        onehot = eid == idx[k:k + 1]
        oh = onehot.astype(BF16)
        within = jnp.dot(oh, before, preferred_element_type=F32)
        ranks.append(jnp.sum(jnp.where(onehot, within + run, 0.0), axis=0, keepdims=True))
        run = run + jnp.sum(onehot.astype(F32), axis=1, keepdims=True)
    rank_ref[...] = jnp.concatenate(ranks, axis=0).astype(jnp.int32)
    run_ref[...] = run
    cnt_ref[...] = run.astype(jnp.int32)


def _rank(top_i):
    _, n = top_i.shape
    tile = min(TILE_ROUTER, n)
    kern = functools.partial(_rank_kernel, tile=tile)
    return pl.pallas_call(
        kern,
        grid=(n // tile,),
        in_specs=[pl.BlockSpec((TOP_K, tile), lambda i: (0, i))],
        out_specs=[pl.BlockSpec((TOP_K, tile), lambda i: (0, i)),
                   pl.BlockSpec((N_EXPERTS, 1), lambda i: (0, 0))],
        out_shape=[jax.ShapeDtypeStruct((TOP_K, n), jnp.int32),
                   jax.ShapeDtypeStruct((N_EXPERTS, 1), jnp.int32)],
        scratch_shapes=[pltpu.VMEM((N_EXPERTS, 1), F32)],
        compiler_params=_cparams("arbitrary"),
        name="moe_rank",
    )(top_i)


def _expert_kernel(be_ref, on_ref, x_ref, wg_ref, wu_ref, bg_ref, bu_ref, wd_ref, bd_ref, o_ref,
                   wd_bf_ref):
    i = pl.program_id(0)
    live = on_ref[i] != 0

    @pl.when(jnp.logical_or(i == 0, be_ref[i] != be_ref[jnp.maximum(i - 1, 0)]))
    def _():
        wd_bf_ref[...] = wd_ref[0, 0].astype(BF16)

    @pl.when(live)
    def _():
        x = x_ref[...]
        hg = jnp.dot(x, wg_ref[0], preferred_element_type=F32) + bg_ref[0]
        hu = jnp.dot(x, wu_ref[0], preferred_element_type=F32) + bu_ref[0]
        gt = jnp.minimum(hg, SWIGLU_LIMIT)
        up = jnp.clip(hu, -SWIGLU_LIMIT, SWIGLU_LIMIT)
        act = (up + 1.0) * gt * _sigmoid(SWIGLU_ALPHA * gt)
        y = jnp.dot(act.astype(BF16), wd_bf_ref[...], preferred_element_type=F32) + bd_ref[0]
        o_ref[...] = y.astype(o_ref.dtype)

    @pl.when(jnp.logical_not(live))
    def _():
        o_ref[...] = jnp.zeros_like(o_ref)


def _experts(x_rows, blk_e, blk_on, w_g, w_u, b_g, b_u, w_d, l, b_d, *, block):
    rows, _ = x_rows.shape
    wspec = lambda k, m: pl.BlockSpec((1, k, m), lambda i, be, on: (be[i], 0, 0))
    grid_spec = pltpu.PrefetchScalarGridSpec(
        num_scalar_prefetch=2,
        grid=(rows // block,),
        in_specs=[pl.BlockSpec((block, D_MODEL), lambda i, be, on: (i, 0)),
                  wspec(D_MODEL, D_FF), wspec(D_MODEL, D_FF), wspec(1, D_FF), wspec(1, D_FF),
                  pl.BlockSpec((1, 1, D_FF, D_MODEL), lambda i, be, on: (l, be[i], 0, 0)),
                  wspec(1, D_MODEL)],
        out_specs=pl.BlockSpec((block, D_MODEL), lambda i, be, on: (i, 0)),
        scratch_shapes=[pltpu.VMEM((D_FF, D_MODEL), BF16)],
    )
    return pl.pallas_call(
        _expert_kernel,
        grid_spec=grid_spec,
        out_shape=jax.ShapeDtypeStruct((rows, D_MODEL), BF16),
        compiler_params=_cparams("arbitrary"),
        name="moe_experts",
    )(blk_e, blk_on, x_rows, w_g, w_u, b_g, b_u, w_d, b_d)


def _combine_kernel(x_ref, y_ref, gate_ref, lng_ref, lnb_ref, o_ref):
    gate = gate_ref[...]
    h = y_ref[0].astype(F32) * gate[:, 0:1]
    for k in range(1, TOP_K):
        h = h + y_ref[k].astype(F32) * gate[:, k:k + 1]
    o_ref[...] = _layer_norm(DEEPNORM_ALPHA * x_ref[...] + h, lng_ref[...], lnb_ref[...])


def _combine_kernel_into(prev_ref, *refs):
    del prev_ref
    _combine_kernel(*refs)


def _combine(x2d, y_g, gate_t, ln_g, ln_b, row0, prev):
    n, _ = x2d.shape
    ng = gate_t.shape[0]
    tile = min(TILE_COMBINE, ng)
    t0 = row0 // tile
    in_specs = [pl.BlockSpec((tile, D_MODEL), lambda i: (i + t0, 0)),
                pl.BlockSpec((TOP_K, tile, D_MODEL), lambda i: (0, i, 0)),
                pl.BlockSpec((tile, TOP_K), lambda i: (i, 0)),
                pl.BlockSpec((1, D_MODEL), lambda i: (0, 0)),
                pl.BlockSpec((1, D_MODEL), lambda i: (0, 0))]
    args = (x2d, y_g, gate_t, ln_g, ln_b)
    kern, aliases = _combine_kernel, {}
    if prev is not None:
        kern, aliases = _combine_kernel_into, {0: 0}
        in_specs = [pl.BlockSpec(memory_space=pl.ANY)] + in_specs
        args = (prev,) + args
    return pl.pallas_call(
        kern,
        grid=(ng // tile,),
        in_specs=in_specs,
        out_specs=pl.BlockSpec((tile, D_MODEL), lambda i: (i + t0, 0)),
        out_shape=jax.ShapeDtypeStruct((n, D_MODEL), F32),
        input_output_aliases=aliases,
        compiler_params=_cparams("parallel"),
        name="moe_combine",
    )(*args)


def _split_kernel(w_ref, sel_ref, g_ref, u_ref):
    w = w_ref[0, 0].astype(BF16)
    for j in range(w.shape[1] // (2 * LANES)):
        r = jnp.dot(w[:, 2 * LANES * j:2 * LANES * (j + 1)], sel_ref[...],
                    preferred_element_type=F32).astype(BF16)
        g_ref[0, :, LANES * j:LANES * (j + 1)] = r[:, :LANES]
        u_ref[0, :, LANES * j:LANES * (j + 1)] = r[:, LANES:]


def _split_gate_up(w_gu, l):
    _, e, dm, f2 = w_gu.shape
    f = f2 // 2
    tr = 256
    col = jnp.arange(2 * LANES, dtype=jnp.int32)[:, None]
    out = jnp.arange(2 * LANES, dtype=jnp.int32)[None, :]
    sel = jnp.where(out < LANES, col == 2 * out, col == 2 * (out - LANES) + 1).astype(BF16)
    ospec = pl.BlockSpec((1, tr, f), lambda ei, i: (ei, i, 0))
    return pl.pallas_call(
        _split_kernel,
        grid=(e, dm // tr),
        in_specs=[pl.BlockSpec((1, 1, tr, f2), lambda ei, i: (l, ei, i, 0)),
                  pl.BlockSpec((2 * LANES, 2 * LANES), lambda ei, i: (0, 0))],
        out_specs=[ospec, ospec],
        out_shape=[jax.ShapeDtypeStruct((e, dm, f), BF16)] * 2,
        compiler_params=_cparams("parallel", "parallel"),
        name="split_gate_up",
    )(w_gu, sel)


def _moe_group(x2d, row0, n, w, prev):
    dm = x2d.shape[1]
    top_i, gate, xb = _router(x2d, w['router_w_t'], w['router_b'], row0, n)
    rank, counts = _rank(top_i)
    block = MOE_BLOCK
    m = n * TOP_K
    nblk = -(-m // block) + N_EXPERTS
    rows = nblk * block
    counts = counts.reshape(N_EXPERTS)
    padded = (counts + block - 1) // block * block
    pad_end = jnp.cumsum(padded)
    pad_start = pad_end - padded
    experts = jnp.arange(N_EXPERTS, dtype=top_i.dtype)
    dest = rank + jnp.sum(jnp.where(top_i[:, :, None] == experts, pad_start.astype(jnp.int32), 0),
                          axis=-1)
    tok = jnp.broadcast_to(jnp.arange(n, dtype=jnp.int32)[None, :], (TOP_K, n))
    row_tok = jnp.zeros((rows,), jnp.int32).at[dest.reshape(-1)].set(
        tok.reshape(-1), unique_indices=True, mode='promise_in_bounds')
    blk_start = jnp.arange(nblk, dtype=pad_end.dtype) * block
    blk_e = jnp.minimum(jnp.sum(pad_end[None, :] <= blk_start[:, None], axis=1),
                        N_EXPERTS - 1).astype(jnp.int32)
    blk_on = (blk_start < pad_end[-1]).astype(jnp.int32)
    x_rows = xb.at[row_tok].get(mode='promise_in_bounds')
    y_rows = _experts(x_rows, blk_e, blk_on, w['e_w_g'], w['e_w_u'], w['e_b_g'], w['e_b_u'], w['e_w_d'],
                      w['layer'], w['e_b_d'], block=block)
    y_g = y_rows.at[dest].get(mode='promise_in_bounds')
    return _combine(x2d, y_g, gate.T, w['ln3_g'], w['ln3_b'], row0, prev)


def _moe(x, w):
    b, s, dm = x.shape
    n = b * s
    x2d = x.reshape(n, dm)
    groups = MOE_TOKEN_GROUPS if n % (MOE_TOKEN_GROUPS * TILE_ROUTER) == 0 else 1
    out = None
    for gi in range(groups):
        out = _moe_group(x2d, gi * (n // groups), n // groups, w, out)
    return out.reshape(b, s, dm)


def _rope_tables(s):
    half = DF_QK // 2
    inv_freq = jnp.power(ROPE_THETA, -jnp.arange(half, dtype=F32) / half)
    ang = jnp.arange(s, dtype=F32)[:, None] * inv_freq[None, :]
    cos = jnp.cos(ang)
    sin = jnp.sin(ang)
    cos_t = jnp.concatenate([cos, cos, cos, cos], axis=1)
    sin_t = jnp.concatenate([-sin, sin, -sin, sin], axis=1)
    return cos_t, sin_t


def _prepare_layer(l, p, s):
    dm = D_MODEL
    w = {}
    w['w_in'] = _pack_in_proj(p['w_in'][l]).astype(BF16)
    pad8 = jnp.zeros((8,), F32)
    pad112 = jnp.zeros((LANES - 16,), F32)
    w['nega'] = jnp.concatenate([pad8, -jnp.exp(p['dn_A_log'][l].reshape(8)), pad112]).reshape(1, LANES)
    w['dtb'] = jnp.concatenate([pad8, p['dn_dt_bias'][l].reshape(8), pad112]).reshape(1, LANES)
    w['dn_conv'] = p['dn_conv'][l]
    t5 = min(S5_CHUNK, s)
    w['s5'] = _s5_operators(p['s5_lam_re'][l], p['s5_lam_im'][l], p['s5_log_dt'][l],
                            p['s5_B_re'][l], p['s5_B_im'][l], p['s5_C_re'][l],
                            p['s5_C_im'][l], p['s5_D'][l], chunk=t5, nc=s // t5)
    for name in ('df_lq1', 'df_lk1', 'df_lq2', 'df_lk2'):
        w[name] = p[name][l].reshape(1, DF_QK)
    w['df_norm'] = p['df_norm'][l].reshape(1, DF_V)
    w['lam_init'] = 0.8 - 0.6 * math.exp(-0.3 * l)
    w['b_gate'] = p['b_gate'][l].reshape(1, -1)
    w['dn_norm'] = p['dn_norm'][l].reshape(1, DN_DIM)
    w['s5_glu_w'] = p['s5_glu_w'][l].astype(BF16)
    w['s5_glu_b'] = p['s5_glu_b'][l].reshape(1, -1)
    w['w_branch'] = p['w_branch'][l].astype(BF16)
    w['w_out'] = p['w_out'][l].astype(BF16)
    for name in ('ln1_g', 'ln1_b', 'ln2_g', 'ln2_b', 'ln3_g', 'ln3_b'):
        w[name] = p[name][l].reshape(1, dm)
    w['xa_wkv'] = p['xa_wkv'][l].astype(BF16)
    w['xa_wq'] = p['xa_wq'][l].astype(BF16)
    w['xa_wo'] = p['xa_wo'][l].astype(BF16)
    w['router_w_t'] = p['router_w'][l].T
    w['router_b'] = p['router_b'][l].reshape(N_EXPERTS, 1)
    w['e_w_g'], w['e_w_u'] = _split_gate_up(p['e_w_gu'], l)
    w['e_b_g'] = p['e_b_gu'][l][:, None, 0::2]
    w['e_b_u'] = p['e_b_gu'][l][:, None, 1::2]
    w['e_w_d'] = p['e_w_down']
    w['layer'] = l
    w['e_b_d'] = p['e_b_down'][l][:, None, :]
    return w


def _layer(x, mem, w, tabs):
    b, s, dm = x.shape
    n = b * s
    proj = _matmul(x.reshape(n, dm), w['w_in'], tm=TILE_MM, tn=LANES * 19)
    proj = proj.reshape(b, s, PROJ_WIDTH)

    qkv, bg = _dn_prep(proj, w['dn_conv'], w['nega'], w['dtb'], chunk=DN_CHUNK)
    bgr = bg[:, :, :16].reshape(b, s // DN_CHUNK, DN_CHUNK, 16).transpose(0, 1, 3, 2)
    o_f, o_b = _dn_scan(qkv, bg, bgr, chunk=DN_CHUNK)

    t5 = min(S5_CHUNK, s)
    nc = s // t5
    u5 = proj[:, :, COL_S5_U:COL_S5_U + S5_WIDTH].reshape(b, nc, t5, S5_GROUPS, S5_GROUP)
    u5 = u5.transpose(3, 0, 1, 2, 4).reshape(S5_GROUPS, b, nc, t5 * S5_GROUP).astype(BF16)
    y5 = _s5_ssm(u5, *w['s5'])
    y5 = y5.reshape(S5_GROUPS, b, nc, t5, S5_GROUP).transpose(1, 2, 3, 0, 4).reshape(b, s, S5_WIDTH)

    qa, qb, kk, vv = _rope_prep(proj, *tabs)
    o_c = _diff_attn(qa, qb, kk, vv, w['df_lq1'], w['df_lk1'], w['df_lq2'], w['df_lk2'],
                     w['df_norm'], w['lam_init'])

    x = _merge(x, o_f, o_b, proj, y5, o_c, w['b_gate'], w['dn_norm'], w['s5_glu_w'], w['s5_glu_b'],
               w['w_branch'], w['w_out'], w['ln1_g'], w['ln1_b'])

    mlen = mem.shape[1]
    kv = _matmul(mem.reshape(b * mlen, dm), w['xa_wkv'], tm=512, tn=1024)
    x = _xattn(x, kv.reshape(b, mlen, 2 * XA_WIDTH), w['xa_wq'], w['xa_wo'], w['ln2_g'], w['ln2_b'])
    return _moe(x, w)


def kernel(x_prompt, x_sample, mem_prompt, mem_sample, w_in, b_gate, dn_conv, dn_A_log, dn_dt_bias,
           dn_norm, s5_lam_re, s5_lam_im, s5_log_dt, s5_B_re, s5_B_im, s5_C_re, s5_C_im, s5_D,
           s5_glu_w, s5_glu_b, df_lq1, df_lk1, df_lq2, df_lk2, df_norm, w_branch, w_out,
           ln1_g, ln1_b, xa_wq, xa_wkv, xa_wo, ln2_g, ln2_b, router_w, router_b,
           e_w_gu, e_b_gu, e_w_down, e_b_down, ln3_g, ln3_b):
    p = dict(w_in=w_in, b_gate=b_gate, dn_conv=dn_conv, dn_A_log=dn_A_log, dn_dt_bias=dn_dt_bias,
             dn_norm=dn_norm, s5_lam_re=s5_lam_re, s5_lam_im=s5_lam_im, s5_log_dt=s5_log_dt,
             s5_B_re=s5_B_re, s5_B_im=s5_B_im, s5_C_re=s5_C_re, s5_C_im=s5_C_im, s5_D=s5_D,
             s5_glu_w=s5_glu_w, s5_glu_b=s5_glu_b, df_lq1=df_lq1, df_lk1=df_lk1, df_lq2=df_lq2,
             df_lk2=df_lk2, df_norm=df_norm, w_branch=w_branch, w_out=w_out, ln1_g=ln1_g,
             ln1_b=ln1_b, xa_wq=xa_wq, xa_wkv=xa_wkv, xa_wo=xa_wo, ln2_g=ln2_g, ln2_b=ln2_b,
             router_w=router_w, router_b=router_b, e_w_gu=e_w_gu, e_b_gu=e_b_gu,
             e_w_down=e_w_down, e_b_down=e_b_down, ln3_g=ln3_g, ln3_b=ln3_b)
    s = x_prompt.shape[1]
    assert x_sample.shape[1] == s
    tabs = _rope_tables(s)
    layers = [_prepare_layer(l, p, s) for l in range(DEPTH)]
    nb = x_prompt.shape[0]
    x = jnp.concatenate([x_prompt, x_sample], axis=0)
    mem = jnp.concatenate([mem_prompt, mem_sample], axis=0)
    for w in layers:
        x = _layer(x, mem, w, tabs)
    return (x[:nb], x[nb:])
```

```python
import functools
import math

import numpy as np
import jax
import jax.numpy as jnp
from jax import lax
from jax.experimental import pallas as pl
from jax.experimental.pallas import tpu as pltpu

F32 = jnp.float32
BF16 = jnp.bfloat16
HI = lax.Precision.HIGHEST

D_MODEL = 1024
DN_HEADS = 4
DN_DIM = 128
DN_WIDTH = DN_HEADS * DN_DIM
DN_CONV = 4
S5_GROUP = 16
S5_WIDTH = 512
S5_GROUPS = S5_WIDTH // S5_GROUP
S5_STATE = 64
DF_HEADS = 4
DF_QK = 64
DF_V = 128
DF_WIDTH = DF_HEADS * DF_V
ROPE_THETA = 10000.0
XA_HEADS = 4
XA_DIM = 128
XA_WIDTH = XA_HEADS * XA_DIM
N_EXPERTS = 32
TOP_K = 4
D_FF = 1024
SWIGLU_LIMIT = 7.0
SWIGLU_ALPHA = 1.702
N_BRANCH = 3
DEPTH = 2
DEEPNORM_ALPHA = (2 * DEPTH) ** 0.25
LN_EPS = 1e-5
RMS_EPS = 1e-6
LOG2E = 1.4426950408889634

LANES = 128
SUBLANES = 8
VMEM_LIMIT = 56 * 1024 * 1024

COL_DN_QKV = 0
COL_DN_Z = 1536
COL_S5_U = 2048
COL_QQ = 2560
COL_KK = 3072
COL_DV = 3584
COL_GATE = 4096
COL_BA = 7168
PROJ_WIDTH = 7296

TILE_MM = 1024
TILE_DN_PREP = 512
TILE_DN_SCAN = 256
DN_CHUNK = 64
DN_CHUNKS_PER_TRIP = 2
S5_CHUNK = 32
TILE_ROPE = 512
TILE_Q = 256
TILE_K = 512
TILE_MERGE = 256
TILE_XA = 512
TILE_ROUTER = 512
MOE_BLOCK = 512
MOE_TOKEN_GROUPS = 2
TILE_SLOTS = 2048
TILE_COMBINE = 256


def _in_proj_perm():
    o_dn_b, o_dn_a, o_s5 = 2048, 2056, 2064
    o_q1, o_q2, o_k1, o_k2, o_dv, o_gate = 2576, 2832, 3088, 3344, 3600, 4112
    perm = np.full((PROJ_WIDTH,), -1, np.int64)
    perm[0:2048] = np.arange(2048)
    perm[COL_S5_U:COL_S5_U + 512] = o_s5 + np.arange(512)
    for h in range(DF_HEADS):
        j = np.arange(DF_QK)
        perm[COL_QQ + h * 128 + j] = o_q1 + h * DF_QK + j
        perm[COL_QQ + h * 128 + 64 + j] = o_q2 + h * DF_QK + j
        perm[COL_KK + h * 128 + j] = o_k1 + h * DF_QK + j
        perm[COL_KK + h * 128 + 64 + j] = o_k2 + h * DF_QK + j
    perm[COL_DV:COL_DV + 512] = o_dv + np.arange(512)
    perm[COL_GATE:COL_GATE + 3072] = o_gate + np.arange(3072)
    perm[COL_BA:COL_BA + 8] = o_dn_b + np.arange(8)
    perm[COL_BA + 8:COL_BA + 16] = o_dn_a + np.arange(8)
    return perm


_IN_PERM = _in_proj_perm()


def _pack_in_proj(w_in):
    runs, start = [], 0
    for i in range(1, PROJ_WIDTH + 1):
        if i == PROJ_WIDTH or _IN_PERM[i] != _IN_PERM[i - 1] + (1 if _IN_PERM[i - 1] >= 0 else 0):
            runs.append((start, i))
            start = i
    pieces = []
    for a, b in runs:
        src = int(_IN_PERM[a])
        pieces.append(jnp.zeros((w_in.shape[0], b - a), w_in.dtype) if src < 0
                      else w_in[:, src:src + (b - a)])
    return jnp.concatenate(pieces, axis=1)


def _cparams(*sem):
    return pltpu.CompilerParams(dimension_semantics=sem, vmem_limit_bytes=VMEM_LIMIT)


def _bdot(a, b):
    return jnp.dot(a.astype(BF16), b.astype(BF16), preferred_element_type=F32)


def _bdot_nt(a, b):
    return lax.dot_general(a.astype(BF16), b.astype(BF16), (((1,), (1,)), ((), ())),
                           preferred_element_type=F32)


def _bdot_tn(a, b):
    return lax.dot_general(a.astype(BF16), b.astype(BF16), (((0,), (0,)), ((), ())),
                           preferred_element_type=F32)


def _sigmoid(x):
    return 0.5 * jnp.tanh(0.5 * x) + 0.5


def _silu(x):
    return x * _sigmoid(x)


def _mm_kernel(x_ref, w_ref, o_ref):
    o_ref[...] = jnp.dot(x_ref[...].astype(BF16), w_ref[...],
                         preferred_element_type=F32).astype(o_ref.dtype)


def _matmul(x, w, *, tm, tn, out_dtype=F32):
    n, k = x.shape
    m = w.shape[1]
    tm = min(tm, n)
    tn = min(tn, m)
    return pl.pallas_call(
        _mm_kernel,
        grid=(n // tm, m // tn),
        in_specs=[pl.BlockSpec((tm, k), lambda i, j: (i, 0)),
                  pl.BlockSpec((k, tn), lambda i, j: (0, j))],
        out_specs=pl.BlockSpec((tm, tn), lambda i, j: (i, j)),
        out_shape=jax.ShapeDtypeStruct((n, m), out_dtype),
        compiler_params=_cparams("parallel", "arbitrary"),
        name="matmul",
    )(x, w)


def _dn_prep_kernel(x_ref, prev_ref, next_ref, ba_ref, cw_ref, nega_ref, dtb_ref,
                    qkv_ref, bg_ref, *, tile, chunk):
    i = pl.program_id(1)
    last = pl.num_programs(1) - 1
    x = x_ref[0]
    prev = jnp.where(i > 0, prev_ref[0], 0.0)
    nxt = jnp.where(i < last, next_ref[0], 0.0)
    ext = jnp.concatenate([prev, x, nxt], axis=0)
    cw = cw_ref[...]
    acc = cw[0:1] * ext[6:6 + tile]
    for j in range(1, DN_CONV):
        acc = acc + cw[j:j + 1] * ext[6 + j:6 + j + tile]
    act = _silu(acc)
    for h in range(DN_HEADS):
        lo = h * DN_DIM
        qh = act[:, lo:lo + DN_DIM]
        qn = qh * lax.rsqrt(jnp.sum(qh * qh, axis=-1, keepdims=True) + 1e-6)
        qkv_ref[0, :, lo:lo + DN_DIM] = qn * (DN_DIM ** -0.5)
        kh = act[:, DN_WIDTH + lo:DN_WIDTH + lo + DN_DIM]
        qkv_ref[0, :, DN_WIDTH + lo:DN_WIDTH + lo + DN_DIM] = kh * lax.rsqrt(
            jnp.sum(kh * kh, axis=-1, keepdims=True) + 1e-6)
    qkv_ref[0, :, 2 * DN_WIDTH:] = act[:, 2 * DN_WIDTH:]

    ba = ba_ref[0]
    beta = _sigmoid(ba)
    z = ba + dtb_ref[...]
    softplus = jnp.maximum(z, 0.0) + jnp.log(1.0 + jnp.exp(-jnp.abs(z)))
    g = nega_ref[...] * softplus
    row = lax.broadcasted_iota(jnp.int32, (chunk, chunk), 0)
    col = lax.broadcasted_iota(jnp.int32, (chunk, chunk), 1)
    lower = (row >= col).astype(F32)
    upper = (row <= col).astype(F32)
    lane = lax.broadcasted_iota(jnp.int32, (chunk, LANES), 1)
    fwd_lane = lane < 8 + DN_HEADS
    for c in range(tile // chunk):
        gc = g[c * chunk:(c + 1) * chunk]
        cf = jnp.dot(lower, gc, precision=HI, preferred_element_type=F32)
        cb = jnp.dot(upper, gc, precision=HI, preferred_element_type=F32)
        gcum = jnp.where(fwd_lane, cf, cb)
        bg_ref[0, c * chunk:(c + 1) * chunk, :] = jnp.where(
            lane < 8, beta[c * chunk:(c + 1) * chunk], gcum)


def _dn_prep(proj, conv_w, nega, dtb, *, chunk):
    b, s, _ = proj.shape
    tile = min(TILE_DN_PREP, s)
    nt = s // tile
    r8 = tile // SUBLANES
    kern = functools.partial(_dn_prep_kernel, tile=tile, chunk=chunk)
    return pl.pallas_call(
        kern,
        grid=(b, nt),
        in_specs=[
            pl.BlockSpec((1, tile, 3 * DN_WIDTH), lambda bi, i: (bi, i, 0)),
            pl.BlockSpec((1, SUBLANES, 3 * DN_WIDTH),
                         lambda bi, i: (bi, jnp.maximum(i * r8 - 1, 0), 0)),
            pl.BlockSpec((1, SUBLANES, 3 * DN_WIDTH),
                         lambda bi, i: (bi, jnp.minimum((i + 1) * r8, s // SUBLANES - 1), 0)),
            pl.BlockSpec((1, tile, LANES), lambda bi, i: (bi, i, COL_BA // LANES)),
            pl.BlockSpec((DN_CONV, 3 * DN_WIDTH), lambda bi, i: (0, 0)),
            pl.BlockSpec((1, LANES), lambda bi, i: (0, 0)),
            pl.BlockSpec((1, LANES), lambda bi, i: (0, 0)),
        ],
        out_specs=[pl.BlockSpec((1, tile, 3 * DN_WIDTH), lambda bi, i: (bi, i, 0)),
                   pl.BlockSpec((1, tile, LANES), lambda bi, i: (bi, i, 0))],
        out_shape=[jax.ShapeDtypeStruct((b, s, 3 * DN_WIDTH), F32),
                   jax.ShapeDtypeStruct((b, s, LANES), F32)],
        compiler_params=_cparams("parallel", "parallel"),
        name="dn_prep",
    )(proj, proj, proj, proj, conv_w, nega, dtb)


def _dn_local(chains, chunk):
    ri = lax.broadcasted_iota(jnp.int32, (chunk, chunk), 0)
    ci = lax.broadcasted_iota(jnp.int32, (chunk, chunk), 1)
    pre = []
    for d, h, q, k, v, bgc, rows in chains:
        li = d * DN_HEADS + h
        beta = bgc[:, li:li + 1]
        gcol = bgc[:, 8 + li:8 + li + 1]
        grow = rows[8 + li:8 + li + 1, :]
        gtot = gcol[chunk - 1:chunk] if d == 0 else gcol[0:1]
        incl = (ri >= ci) if d == 0 else (ri <= ci)
        strict = (ri > ci) if d == 0 else (ri < ci)
        eg = jnp.exp(gcol)
        kb = k * beta
        decay = jnp.where(incl, jnp.exp(jnp.where(incl, gcol - grow, 0.0)), 0.0)
        pre.append(dict(k=k, qg=q * eg, kd=k * jnp.exp(gtot - gcol), decay=decay, strict=strict,
                        x2=jnp.concatenate([v * beta, kb * eg], axis=1), sdec=jnp.exp(gtot),
                        qkb=jnp.concatenate([q, kb], axis=0)))
    qk = [_bdot_nt(c['qkb'], c['k']) for c in pre]
    intra = [x[:chunk] * c['decay'] for x, c in zip(qk, pre)]
    p = [-jnp.where(c['strict'], x[chunk:] * c['decay'], 0.0) for x, c in zip(qk, pre)]
    r = p
    p = [_bdot(x, x) for x in p]
    for _ in range(int(math.log2(chunk)) - 2):
        rp = [_bdot(jnp.concatenate([ri_, pi_], axis=0), pi_) for ri_, pi_ in zip(r, p)]
        r = [ri_ + pi_ + x[:chunk] for ri_, pi_, x in zip(r, p, rp)]
        p = [x[chunk:] for x in rp]
    r = [ri_ + pi_ + _bdot(ri_, pi_) for ri_, pi_ in zip(r, p)]
    uw = [c['x2'] + _bdot(ri_, c['x2']) for ri_, c in zip(r, pre)]
    return [dict(u=x[:, :DN_DIM], wq=jnp.concatenate([x[:, DN_DIM:], c['qg']], axis=0),
                 intra=it, kd=c['kd'], sdec=c['sdec']) for x, it, c in zip(uw, intra, pre)]


def _dn_state(local, states, chunk):
    ws = [_bdot(c['wq'], st) for c, st in zip(local, states)]
    v_new = [c['u'] - y[:chunk] for c, y in zip(local, ws)]
    out = [y[chunk:] + _bdot(c['intra'], vn) for y, c, vn in zip(ws, local, v_new)]
    st_new = [st * c['sdec'] + _bdot_tn(c['kd'], vn) for st, c, vn in zip(states, local, v_new)]
    return out, st_new


def _dn_scan_kernel(xf_ref, xb_ref, gf_ref, gb_ref, rf_ref, rb_ref, of_ref, ob_ref, st_ref,
                    *, tile, chunk):
    @pl.when(pl.program_id(1) == 0)
    def _():
        st_ref[...] = jnp.zeros_like(st_ref)

    nct = tile // chunk
    nchain = 2 * DN_HEADS

    def body(j, carry):
        chains, where = [], []
        for slot in range(DN_CHUNKS_PER_TRIP):
            for d in range(2):
                c = DN_CHUNKS_PER_TRIP * j + slot
                c = c if d == 0 else nct - 1 - c
                r0 = pl.multiple_of(c * chunk, chunk)
                x_ref, g_ref, r_ref = (xf_ref, gf_ref, rf_ref) if d == 0 else (xb_ref, gb_ref, rb_ref)
                bgc = g_ref[0, pl.ds(r0, chunk), :]
                rows = r_ref[0, c]
                for h in range(DN_HEADS):
                    lo = h * DN_DIM
                    q = x_ref[0, pl.ds(r0, chunk), lo:lo + DN_DIM]
                    k = x_ref[0, pl.ds(r0, chunk), DN_WIDTH + lo:DN_WIDTH + lo + DN_DIM]
                    v = x_ref[0, pl.ds(r0, chunk), 2 * DN_WIDTH + lo:2 * DN_WIDTH + lo + DN_DIM]
                    chains.append((d, h, q, k, v, bgc, rows))
                    where.append((d, h, r0))
        states = [st_ref[i] for i in range(nchain)]
        local = _dn_local(chains, chunk)
        outs = []
        for slot in range(DN_CHUNKS_PER_TRIP):
            out, states = _dn_state(local[slot * nchain:(slot + 1) * nchain], states, chunk)
            outs += out
        for (d, h, r0), out in zip(where, outs):
            o_ref = of_ref if d == 0 else ob_ref
            o_ref[0, pl.ds(r0, chunk), h * DN_DIM:(h + 1) * DN_DIM] = out
        for i in range(nchain):
            st_ref[i] = states[i]
        return carry

    lax.fori_loop(0, nct // DN_CHUNKS_PER_TRIP, body, 0)


def _dn_scan(qkv, bg, bgr, *, chunk):
    b, s, _ = qkv.shape
    tile = min(TILE_DN_SCAN, s)
    nt = s // tile
    nct = tile // chunk
    kern = functools.partial(_dn_scan_kernel, tile=tile, chunk=chunk)
    fwd = lambda bi, i: (bi, i, 0)
    bwd = lambda bi, i: (bi, nt - 1 - i, 0)
    fwd4 = lambda bi, i: (bi, i, 0, 0)
    bwd4 = lambda bi, i: (bi, nt - 1 - i, 0, 0)
    return pl.pallas_call(
        kern,
        grid=(b, nt),
        in_specs=[pl.BlockSpec((1, tile, 3 * DN_WIDTH), fwd),
                  pl.BlockSpec((1, tile, 3 * DN_WIDTH), bwd),
                  pl.BlockSpec((1, tile, LANES), fwd),
                  pl.BlockSpec((1, tile, LANES), bwd),
                  pl.BlockSpec((1, nct, 16, chunk), fwd4),
                  pl.BlockSpec((1, nct, 16, chunk), bwd4)],
        out_specs=[pl.BlockSpec((1, tile, DN_WIDTH), fwd),
                   pl.BlockSpec((1, tile, DN_WIDTH), bwd)],
        out_shape=[jax.ShapeDtypeStruct((b, s, DN_WIDTH), F32),
                   jax.ShapeDtypeStruct((b, s, DN_WIDTH), F32)],
        scratch_shapes=[pltpu.VMEM((2 * DN_HEADS, DN_DIM, DN_DIM), F32)],
        compiler_params=_cparams("parallel", "arbitrary"),
        name="dn_scan",
    )(qkv, qkv, bg, bg, bgr, bgr)


def _s5_operators(lam_re, lam_im, log_dt, b_re, b_im, c_re, c_im, d_skip, *, chunk, nc):
    t = chunk
    levels = max(1, int(math.log2(nc)))
    tau = jnp.arange(t + 1, dtype=F32)
    kern = []
    lamp = []
    for d in range(2):
        lam = lax.complex(lam_re[d], lam_im[d])
        dt = jnp.exp(log_dt[d])[:, None]
        lam_dt = lam * dt
        lam_bar = jnp.exp(lam_dt)
        bbar = ((lam_bar - 1.0) / lam)[:, :, None] * lax.complex(b_re[d], b_im[d])
        cc = lax.complex(c_re[d], c_im[d])
        pw = jnp.exp(lam_dt[:, None, :] * tau[None, :, None].astype(jnp.complex64))
        kern.append((pw, bbar, cc))
        steps = (t * 2.0 ** jnp.arange(levels, dtype=F32)).astype(jnp.complex64)
        lamp.append(jnp.exp(lam_dt[:, None, :] * steps[None, :, None]))
    (pwf, bbf, ccf), (pwb, bbb, ccb) = kern
    ef = jnp.einsum('gsp,gpc->gscp', pwf[:, t - 1::-1, :], bbf)
    eb = jnp.einsum('gsp,gpc->gscp', pwb[:, :t, :], bbb)
    w1 = jnp.concatenate([jnp.real(ef), jnp.real(eb), jnp.imag(ef), jnp.imag(eb)], axis=-1)
    w1 = w1.reshape(S5_GROUPS, t * S5_GROUP, 4 * S5_STATE)
    kf = jnp.real(jnp.einsum('gcp,gtp,gpd->gtcd', ccf, pwf[:, :t, :], bbf))
    kb = jnp.real(jnp.einsum('gcp,gtp,gpd->gtcd', ccb, pwb[:, :t, :], bbb))
    si = jnp.arange(t)[:, None]
    ti = jnp.arange(t)[None, :]
    lag_f = jnp.clip(ti - si, 0, t - 1)
    lag_b = jnp.clip(si - ti, 0, t - 1)
    toep = (jnp.where((ti >= si)[None, :, :, None, None], kf[:, lag_f], 0.0)
            + jnp.where((si >= ti)[None, :, :, None, None], kb[:, lag_b], 0.0))
    eye_t = jnp.eye(t, dtype=F32)[None, :, :, None, None]
    eye_c = jnp.eye(S5_GROUP, dtype=F32)[None, None, None, :, :]
    toep = toep + eye_t * eye_c * d_skip.reshape(S5_GROUPS, 1, 1, S5_GROUP, 1)
    w_intra = toep.transpose(0, 1, 4, 2, 3).reshape(S5_GROUPS, t * S5_GROUP, t * S5_GROUP)
    mf = jnp.einsum('gcp,gtp->gptc', ccf, pwf[:, 1:t + 1, :])
    mb = jnp.einsum('gcp,gtp->gptc', ccb, pwb[:, t:0:-1, :])
    w_state = jnp.concatenate([jnp.real(mf), jnp.real(mb), -jnp.imag(mf), -jnp.imag(mb)], axis=1)
    w_state = w_state.reshape(S5_GROUPS, 4 * S5_STATE, t * S5_GROUP)
    w3 = jnp.concatenate([w_intra, w_state], axis=1)
    lam_r = jnp.concatenate([jnp.real(lamp[0]), jnp.real(lamp[1])], axis=-1)
    lam_i = jnp.concatenate([jnp.imag(lamp[0]), jnp.imag(lamp[1])], axis=-1)
    lamp = jnp.stack([lam_r, lam_i], axis=2)
    return w1.astype(BF16), w3.astype(BF16), lamp.astype(F32)


def _s5_kernel(u_ref, w1_ref, w3_ref, lam_ref, y_ref, *, nc, width, levels):
    u = u_ref[0, 0]
    loc = jnp.dot(u, w1_ref[0], preferred_element_type=F32)
    re = loc[:, :2 * S5_STATE]
    im = loc[:, 2 * S5_STATE:]
    row = lax.broadcasted_iota(jnp.int32, (nc, 2 * S5_STATE), 0)
    fwd = lax.broadcasted_iota(jnp.int32, (nc, 2 * S5_STATE), 1) < S5_STATE

    def shifted(x, d):
        down = jnp.where(row >= d, pltpu.roll(x, d, 0), 0.0)
        up = jnp.where(row < nc - d, pltpu.roll(x, nc - d, 0), 0.0)
        return jnp.where(fwd, down, up)

    for k in range(levels):
        d = 2 ** k
        if d >= nc:
            break
        lr = lam_ref[0, k, 0:1, :]
        li = lam_ref[0, k, 1:2, :]
        sr = shifted(re, d)
        si = shifted(im, d)
        re, im = re + sr * lr - si * li, im + sr * li + si * lr
    ent = jnp.concatenate([shifted(re, 1), shifted(im, 1)], axis=1)
    y = jnp.dot(u, w3_ref[0, :width, :], preferred_element_type=F32)
    y = y + jnp.dot(ent.astype(BF16), w3_ref[0, width:, :], preferred_element_type=F32)
    y_ref[0, 0] = y


def _s5_ssm(u5, w1, w3, lamp):
    g, b, nc, width = u5.shape
    levels = lamp.shape[1]
    kern = functools.partial(_s5_kernel, nc=nc, width=width, levels=levels)
    return pl.pallas_call(
        kern,
        grid=(g, b),
        in_specs=[pl.BlockSpec((1, 1, nc, width), lambda gi, bi: (gi, bi, 0, 0)),
                  pl.BlockSpec((1, width, 4 * S5_STATE), lambda gi, bi: (gi, 0, 0)),
                  pl.BlockSpec((1, width + 4 * S5_STATE, width), lambda gi, bi: (gi, 0, 0)),
                  pl.BlockSpec((1, levels, 2, 2 * S5_STATE), lambda gi, bi: (gi, 0, 0, 0))],
        out_specs=pl.BlockSpec((1, 1, nc, width), lambda gi, bi: (gi, bi, 0, 0)),
        out_shape=jax.ShapeDtypeStruct((g, b, nc, width), F32),
        compiler_params=_cparams("parallel", "parallel"),
        name="s5_ssm",
    )(u5, w1, w3, lamp)


def _rope_kernel(qq_ref, kk_ref, v_ref, cos_ref, sin_ref, qa_ref, qb_ref, ko_ref, vo_ref):
    width = DF_HEADS * 2 * DF_QK
    cos = jnp.concatenate([cos_ref[...]] * (width // LANES), axis=1)
    sin = jnp.concatenate([sin_ref[...]] * (width // LANES), axis=1)
    lane = lax.broadcasted_iota(jnp.int32, (1, width), 1)
    first_half = (lane % DF_QK) < (DF_QK // 2)

    def rot(x):
        partner = jnp.where(first_half, pltpu.roll(x, width - DF_QK // 2, 1),
                            pltpu.roll(x, DF_QK // 2, 1))
        return x * cos + partner * sin

    q = rot(qq_ref[0]) * (DF_QK ** -0.5 * LOG2E)
    is_a = (lane % (2 * DF_QK)) < DF_QK
    qa_ref[0] = jnp.where(is_a, q, 0.0).astype(BF16)
    qb_ref[0] = jnp.where(is_a, 0.0, q).astype(BF16)
    ko_ref[0] = rot(kk_ref[0]).astype(BF16)
    vo_ref[0] = v_ref[0].astype(BF16)


def _rope_prep(proj, cos_t, sin_t):
    b, s, _ = proj.shape
    tile = min(TILE_ROPE, s)
    w = DF_WIDTH
    spec = lambda cb: pl.BlockSpec((1, tile, w), lambda bi, i: (bi, i, cb))
    out = pl.BlockSpec((1, tile, w), lambda bi, i: (bi, i, 0))
    tab = pl.BlockSpec((tile, LANES), lambda bi, i: (i, 0))
    return pl.pallas_call(
        _rope_kernel,
        grid=(b, s // tile),
        in_specs=[spec(COL_QQ // w), spec(COL_KK // w), spec(COL_DV // w), tab, tab],
        out_specs=[out, out, out, out],
        out_shape=[jax.ShapeDtypeStruct((b, s, w), BF16)] * 4,
        compiler_params=_cparams("parallel", "parallel"),
        name="rope_prep",
    )(proj, proj, proj, cos_t, sin_t)


def _diff_attn_kernel(qa_ref, qb_ref, k_ref, v_ref, lq1_ref, lk1_ref, lq2_ref, lk2_ref, ng_ref,
                      o_ref, s_ref, *, tq, tk, nk, lam_init):
    q2 = (qa_ref[0], qb_ref[0])
    nt_dims = (((1,), (1,)), ((), ()))

    def fold(x, op, init):
        for j in range(tk // LANES):
            init = op(init, x[:, j * LANES:(j + 1) * LANES])
        return init

    m = [jnp.full((tq, LANES), -jnp.inf, F32)] * 2
    for c in range(nk):
        kc = k_ref[0, c * tk:(c + 1) * tk, :]
        for i in range(2):
            sc = lax.dot_general(q2[i], kc, nt_dims, preferred_element_type=F32)
            s_ref[c, i * tq:(i + 1) * tq, :] = sc
            m[i] = fold(sc, jnp.maximum, m[i])
    m = [jnp.max(x, axis=-1, keepdims=True) for x in m]

    acc = [jnp.zeros((tq, 2 * DF_V), F32)] * 2
    ones = jnp.ones((tk, DF_V), BF16)
    for c in range(nk):
        vc = jnp.concatenate([v_ref[0, c * tk:(c + 1) * tk, :], ones], axis=1)
        for i in range(2):
            e = jnp.exp2((s_ref[c, i * tq:(i + 1) * tq, :] - m[i]).astype(BF16))
            acc[i] = acc[i] + jnp.dot(e, vc, preferred_element_type=F32)
    lam = (jnp.exp(jnp.sum(lq1_ref[...] * lk1_ref[...], axis=-1, keepdims=True))
           - jnp.exp(jnp.sum(lq2_ref[...] * lk2_ref[...], axis=-1, keepdims=True)) + lam_init)
    acc = (acc[0][:, :DF_V] * (1.0 / acc[0][:, DF_V:DF_V + 1])
           - acc[1][:, :DF_V] * (lam / acc[1][:, DF_V:DF_V + 1]))
    ms = jnp.mean(acc * acc, axis=-1, keepdims=True)
    o_ref[0] = acc * lax.rsqrt(ms + RMS_EPS) * ng_ref[...] * (1.0 - lam_init)


def _diff_attn(qa, qb, kk, vv, lq1, lk1, lq2, lk2, norm_g, lam_init):
    b, s, _ = qa.shape
    tq = min(TILE_Q, s)
    tk = min(TILE_K, s)
    nk = s // tk
    kern = functools.partial(_diff_attn_kernel, tq=tq, tk=tk, nk=nk, lam_init=lam_init)
    qspec = pl.BlockSpec((1, tq, LANES), lambda bi, h, i: (bi, i, h))
    kspec = pl.BlockSpec((1, s, LANES), lambda bi, h, i: (bi, 0, h))
    vec64 = pl.BlockSpec((1, DF_QK), lambda bi, h, i: (0, 0))
    return pl.pallas_call(
        kern,
        grid=(b, DF_HEADS, s // tq),
        in_specs=[qspec, qspec, kspec, kspec, vec64, vec64, vec64, vec64,
                  pl.BlockSpec((1, DF_V), lambda bi, h, i: (0, 0))],
        out_specs=pl.BlockSpec((1, tq, DF_V), lambda bi, h, i: (bi, i, h)),
        out_shape=jax.ShapeDtypeStruct((b, s, DF_WIDTH), F32),
        scratch_shapes=[pltpu.VMEM((nk, 2 * tq, tk), F32)],
        compiler_params=_cparams("parallel", "parallel", "arbitrary"),
        name="diff_attn",
    )(qa, qb, kk, vv, lq1, lk1, lq2, lk2, norm_g)


def _layer_norm(x, g, b):
    mu = jnp.mean(x, axis=-1, keepdims=True)
    xc = x - mu
    var = jnp.mean(xc * xc, axis=-1, keepdims=True)
    return xc * lax.rsqrt(var + LN_EPS) * g + b


def _merge_kernel(x_ref, of_ref, ob_ref, z_ref, y5_ref, oc_ref, g0_ref, g1_ref, g2_ref,
                  bgate_ref, dnn_ref, gluw_ref, glub_ref, wbr_ref, wout_ref, lng_ref, lnb_ref,
                  o_ref):
    o = of_ref[0] + ob_ref[0]
    z = z_ref[0]
    parts = []
    for h in range(DN_HEADS):
        oh = o[:, h * DN_DIM:(h + 1) * DN_DIM]
        ms = jnp.mean(oh * oh, axis=-1, keepdims=True)
        parts.append(oh * lax.rsqrt(ms + RMS_EPS) * dnn_ref[...])
    o_a = jnp.concatenate(parts, axis=1) * _silu(z)
    y = jax.nn.gelu(y5_ref[0])
    o_b = y * _sigmoid(_bdot(y, gluw_ref[...]) + glub_ref[...])
    o_c = oc_ref[0]
    bg = bgate_ref[...]
    merged = (_sigmoid(g0_ref[0] + bg[:, 0:D_MODEL]) * _bdot(o_a, wbr_ref[0])
              + _sigmoid(g1_ref[0] + bg[:, D_MODEL:2 * D_MODEL]) * _bdot(o_b, wbr_ref[1])
              + _sigmoid(g2_ref[0] + bg[:, 2 * D_MODEL:]) * _bdot(o_c, wbr_ref[2]))
    hres = DEEPNORM_ALPHA * x_ref[0] + _bdot(merged, wout_ref[...])
    o_ref[0] = _layer_norm(hres, lng_ref[...], lnb_ref[...])


def _merge(x, o_f, o_b, proj, y5, o_c, b_gate, dn_norm, glu_w, glu_b, w_branch, w_out, ln_g, ln_b):
    b, s, _ = x.shape
    tile = min(TILE_MERGE, s)
    row = lambda w, cb=0: pl.BlockSpec((1, tile, w), lambda bi, i: (bi, i, cb))
    full = lambda shape: pl.BlockSpec(shape, lambda bi, i: (0,) * len(shape))
    gcb = COL_GATE // D_MODEL
    return pl.pallas_call(
        _merge_kernel,
        grid=(b, s // tile),
        in_specs=[row(D_MODEL), row(DN_WIDTH), row(DN_WIDTH), row(DN_WIDTH, COL_DN_Z // DN_WIDTH),
                  row(S5_WIDTH), row(DF_WIDTH),
                  row(D_MODEL, gcb), row(D_MODEL, gcb + 1), row(D_MODEL, gcb + 2),
                  full((1, N_BRANCH * D_MODEL)), full((1, DN_DIM)),
                  full((S5_WIDTH, S5_WIDTH)), full((1, S5_WIDTH)),
                  full((N_BRANCH, DN_WIDTH, D_MODEL)), full((D_MODEL, D_MODEL)),
                  full((1, D_MODEL)), full((1, D_MODEL))],
        out_specs=row(D_MODEL),
        out_shape=jax.ShapeDtypeStruct((b, s, D_MODEL), F32),
        compiler_params=_cparams("parallel", "parallel"),
        name="merge",
    )(x, o_f, o_b, proj, y5, o_c, proj, proj, proj, b_gate, dn_norm, glu_w, glu_b,
      w_branch, w_out, ln_g, ln_b)


def _xattn_kernel(x_ref, kv_ref, wq_ref, wo_ref, lng_ref, lnb_ref, o_ref):
    x = x_ref[0]
    q = _bdot(x, wq_ref[...])
    outs = []
    for h in range(XA_HEADS):
        qh = q[:, h * XA_DIM:(h + 1) * XA_DIM]
        kh = kv_ref[0, :, h * XA_DIM:(h + 1) * XA_DIM]
        vh = kv_ref[0, :, XA_WIDTH + h * XA_DIM:XA_WIDTH + (h + 1) * XA_DIM]
        sc = _bdot_nt(qh, kh) * (XA_DIM ** -0.5)
        sc = sc - jnp.max(sc, axis=-1, keepdims=True)
        e = jnp.exp(sc)
        pr = e / jnp.sum(e, axis=-1, keepdims=True)
        outs.append(_bdot(pr, vh))
    o = jnp.concatenate(outs, axis=1)
    hres = DEEPNORM_ALPHA * x + _bdot(o, wo_ref[...])
    o_ref[0] = _layer_norm(hres, lng_ref[...], lnb_ref[...])


def _xattn(x, kv, wq, wo, ln_g, ln_b):
    b, s, _ = x.shape
    tile = min(TILE_XA, s)
    mlen = kv.shape[1]
    row = pl.BlockSpec((1, tile, D_MODEL), lambda bi, i: (bi, i, 0))
    full = lambda shape: pl.BlockSpec(shape, lambda bi, i: (0,) * len(shape))
    return pl.pallas_call(
        _xattn_kernel,
        grid=(b, s // tile),
        in_specs=[row, pl.BlockSpec((1, mlen, 2 * XA_WIDTH), lambda bi, i: (bi, 0, 0)),
                  full((D_MODEL, XA_WIDTH)), full((XA_WIDTH, D_MODEL)),
                  full((1, D_MODEL)), full((1, D_MODEL))],
        out_specs=row,
        out_shape=jax.ShapeDtypeStruct((b, s, D_MODEL), F32),
        compiler_params=_cparams("parallel", "parallel"),
        name="xattn",
    )(x, kv, wq, wo, ln_g, ln_b)


def _router_kernel(x_ref, rw_ref, rb_ref, idx_ref, gate_ref, xb_ref):
    x = x_ref[...]
    xb_ref[...] = x.astype(BF16)
    logits = lax.dot_general(rw_ref[...], x, (((1,), (1,)), ((), ())), precision=HI,
                             preferred_element_type=F32) + rb_ref[...]
    tn = logits.shape[1]
    eid = lax.broadcasted_iota(jnp.int32, (N_EXPERTS, tn), 0)
    vals, ids = [], []
    cur = logits
    for _ in range(TOP_K):
        m = jnp.max(cur, axis=0, keepdims=True)
        sel = jnp.min(jnp.where(cur == m, eid, N_EXPERTS), axis=0, keepdims=True)
        vals.append(m)
        ids.append(sel)
        cur = jnp.where(eid == sel, -jnp.inf, cur)
    top_v = jnp.concatenate(vals, axis=0)
    e = jnp.exp(top_v - top_v[0:1])
    gate_ref[...] = e / jnp.sum(e, axis=0, keepdims=True)
    idx_ref[...] = jnp.concatenate(ids, axis=0)


def _router(x2d, router_w_t, router_b, row0, n):
    tile = min(TILE_ROUTER, n)
    t0 = row0 // tile
    return pl.pallas_call(
        _router_kernel,
        grid=(n // tile,),
        in_specs=[pl.BlockSpec((tile, D_MODEL), lambda i: (i + t0, 0)),
                  pl.BlockSpec((N_EXPERTS, D_MODEL), lambda i: (0, 0)),
                  pl.BlockSpec((N_EXPERTS, 1), lambda i: (0, 0))],
        out_specs=[pl.BlockSpec((TOP_K, tile), lambda i: (0, i)),
                   pl.BlockSpec((TOP_K, tile), lambda i: (0, i)),
                   pl.BlockSpec((tile, D_MODEL), lambda i: (i, 0))],
        out_shape=[jax.ShapeDtypeStruct((TOP_K, n), jnp.int32),
                   jax.ShapeDtypeStruct((TOP_K, n), F32),
                   jax.ShapeDtypeStruct((n, D_MODEL), BF16)],
        compiler_params=_cparams("parallel"),
        name="router",
    )(x2d, router_w_t, router_b)


def _rank_kernel(idx_ref, rank_ref, cnt_ref, run_ref, *, tile):
    @pl.when(pl.program_id(0) == 0)
    def _():
        run_ref[...] = jnp.zeros_like(run_ref)

    idx = idx_ref[...]
    eid = lax.broadcasted_iota(jnp.int32, (N_EXPERTS, tile), 0)
    si = lax.broadcasted_iota(jnp.int32, (tile, tile), 0)
    ti = lax.broadcasted_iota(jnp.int32, (tile, tile), 1)
    before = (si < ti).astype(BF16)
    run = run_ref[...]
    ranks = []
    for k in range(TOP_K):
        onehot = eid == idx[k:k + 1]
        oh = onehot.astype(BF16)
        within = jnp.dot(oh, before, preferred_element_type=F32)
        ranks.append(jnp.sum(jnp.where(onehot, within + run, 0.0), axis=0, keepdims=True))
        run = run + jnp.sum(onehot.astype(F32), axis=1, keepdims=True)
    rank_ref[...] = jnp.concatenate(ranks, axis=0).astype(jnp.int32)
    run_ref[...] = run
    cnt_ref[...] = run.astype(jnp.int32)


def _rank(top_i):
    _, n = top_i.shape
    tile = min(TILE_ROUTER, n)
    kern = functools.partial(_rank_kernel, tile=tile)
    return pl.pallas_call(
        kern,
        grid=(n // tile,),
        in_specs=[pl.BlockSpec((TOP_K, tile), lambda i: (0, i))],
        out_specs=[pl.BlockSpec((TOP_K, tile), lambda i: (0, i)),
                   pl.BlockSpec((N_EXPERTS, 1), lambda i: (0, 0))],
        out_shape=[jax.ShapeDtypeStruct((TOP_K, n), jnp.int32),
                   jax.ShapeDtypeStruct((N_EXPERTS, 1), jnp.int32)],
        scratch_shapes=[pltpu.VMEM((N_EXPERTS, 1), F32)],
        compiler_params=_cparams("arbitrary"),
        name="moe_rank",
    )(top_i)


def _expert_kernel(be_ref, on_ref, x_ref, wg_ref, wu_ref, bg_ref, bu_ref, wd_ref, bd_ref, o_ref,
                   wd_bf_ref):
    i = pl.program_id(0)
    live = on_ref[i] != 0

    @pl.when(jnp.logical_or(i == 0, be_ref[i] != be_ref[jnp.maximum(i - 1, 0)]))
    def _():
        wd_bf_ref[...] = wd_ref[0, 0].astype(BF16)

    @pl.when(live)
    def _():
        x = x_ref[...]
        hg = jnp.dot(x, wg_ref[0], preferred_element_type=F32) + bg_ref[0]
        hu = jnp.dot(x, wu_ref[0], preferred_element_type=F32) + bu_ref[0]
        gt = jnp.minimum(hg, SWIGLU_LIMIT)
        up = jnp.clip(hu, -SWIGLU_LIMIT, SWIGLU_LIMIT)
        act = (up + 1.0) * gt * _sigmoid(SWIGLU_ALPHA * gt)
        y = jnp.dot(act.astype(BF16), wd_bf_ref[...], preferred_element_type=F32) + bd_ref[0]
        o_ref[...] = y.astype(o_ref.dtype)

    @pl.when(jnp.logical_not(live))
    def _():
        o_ref[...] = jnp.zeros_like(o_ref)


def _experts(x_rows, blk_e, blk_on, w_g, w_u, b_g, b_u, w_d, l, b_d, *, block):
    rows, _ = x_rows.shape
    wspec = lambda k, m: pl.BlockSpec((1, k, m), lambda i, be, on: (be[i], 0, 0))
    grid_spec = pltpu.PrefetchScalarGridSpec(
        num_scalar_prefetch=2,
        grid=(rows // block,),
        in_specs=[pl.BlockSpec((block, D_MODEL), lambda i, be, on: (i, 0)),
                  wspec(D_MODEL, D_FF), wspec(D_MODEL, D_FF), wspec(1, D_FF), wspec(1, D_FF),
                  pl.BlockSpec((1, 1, D_FF, D_MODEL), lambda i, be, on: (l, be[i], 0, 0)),
                  wspec(1, D_MODEL)],
        out_specs=pl.BlockSpec((block, D_MODEL), lambda i, be, on: (i, 0)),
        scratch_shapes=[pltpu.VMEM((D_FF, D_MODEL), BF16)],
    )
    return pl.pallas_call(
        _expert_kernel,
        grid_spec=grid_spec,
        out_shape=jax.ShapeDtypeStruct((rows, D_MODEL), BF16),
        compiler_params=_cparams("arbitrary"),
        name="moe_experts",
    )(blk_e, blk_on, x_rows, w_g, w_u, b_g, b_u, w_d, b_d)


def _combine_kernel(x_ref, y_ref, gate_ref, lng_ref, lnb_ref, o_ref):
    gate = gate_ref[...]
    h = y_ref[0].astype(F32) * gate[:, 0:1]
    for k in range(1, TOP_K):
        h = h + y_ref[k].astype(F32) * gate[:, k:k + 1]
    o_ref[...] = _layer_norm(DEEPNORM_ALPHA * x_ref[...] + h, lng_ref[...], lnb_ref[...])


def _combine_kernel_into(prev_ref, *refs):
    del prev_ref
    _combine_kernel(*refs)


def _combine(x2d, y_g, gate_t, ln_g, ln_b, row0, prev):
    n, _ = x2d.shape
    ng = gate_t.shape[0]
    tile = min(TILE_COMBINE, ng)
    t0 = row0 // tile
    in_specs = [pl.BlockSpec((tile, D_MODEL), lambda i: (i + t0, 0)),
                pl.BlockSpec((TOP_K, tile, D_MODEL), lambda i: (0, i, 0)),
                pl.BlockSpec((tile, TOP_K), lambda i: (i, 0)),
                pl.BlockSpec((1, D_MODEL), lambda i: (0, 0)),
                pl.BlockSpec((1, D_MODEL), lambda i: (0, 0))]
    args = (x2d, y_g, gate_t, ln_g, ln_b)
    kern, aliases = _combine_kernel, {}
    if prev is not None:
        kern, aliases = _combine_kernel_into, {0: 0}
        in_specs = [pl.BlockSpec(memory_space=pl.ANY)] + in_specs
        args = (prev,) + args
    return pl.pallas_call(
        kern,
        grid=(ng // tile,),
        in_specs=in_specs,
        out_specs=pl.BlockSpec((tile, D_MODEL), lambda i: (i + t0, 0)),
        out_shape=jax.ShapeDtypeStruct((n, D_MODEL), F32),
        input_output_aliases=aliases,
        compiler_params=_cparams("parallel"),
        name="moe_combine",
    )(*args)


def _slot_kernel(dest_ref, out_ref, *, tile, rows):
    base = pl.program_id(0) * tile

    @pl.when(pl.program_id(0) == 0)
    def _():
        def zero(r, carry):
            out_ref[r] = 0
            return carry
        lax.fori_loop(0, rows, zero, 0, unroll=16)

    def body(t, carry):
        for k in range(TOP_K):
            out_ref[dest_ref[k, t]] = base + t
        return carry

    lax.fori_loop(0, tile, body, 0, unroll=8)


def _slot_tokens(dest, rows):
    _, n = dest.shape
    tile = min(TILE_SLOTS, n)
    kern = functools.partial(_slot_kernel, tile=tile, rows=rows)
    return pl.pallas_call(
        kern,
        grid=(n // tile,),
        in_specs=[pl.BlockSpec((TOP_K, tile), lambda i: (0, i), memory_space=pltpu.SMEM)],
        out_specs=pl.BlockSpec((rows,), lambda i: (0,), memory_space=pltpu.SMEM),
        out_shape=jax.ShapeDtypeStruct((rows,), jnp.int32),
        compiler_params=_cparams("arbitrary"),
        name="moe_slot_tokens",
    )(dest)


def _split_kernel(w_ref, sel_ref, g_ref, u_ref):
    w = w_ref[0, 0].astype(BF16)
    for j in range(w.shape[1] // (2 * LANES)):
        r = jnp.dot(w[:, 2 * LANES * j:2 * LANES * (j + 1)], sel_ref[...],
                    preferred_element_type=F32).astype(BF16)
        g_ref[0, :, LANES * j:LANES * (j + 1)] = r[:, :LANES]
        u_ref[0, :, LANES * j:LANES * (j + 1)] = r[:, LANES:]


def _split_gate_up(w_gu, l):
    _, e, dm, f2 = w_gu.shape
    f = f2 // 2
    tr = 256
    col = jnp.arange(2 * LANES, dtype=jnp.int32)[:, None]
    out = jnp.arange(2 * LANES, dtype=jnp.int32)[None, :]
    sel = jnp.where(out < LANES, col == 2 * out, col == 2 * (out - LANES) + 1).astype(BF16)
    ospec = pl.BlockSpec((1, tr, f), lambda ei, i: (ei, i, 0))
    return pl.pallas_call(
        _split_kernel,
        grid=(e, dm // tr),
        in_specs=[pl.BlockSpec((1, 1, tr, f2), lambda ei, i: (l, ei, i, 0)),
                  pl.BlockSpec((2 * LANES, 2 * LANES), lambda ei, i: (0, 0))],
        out_specs=[ospec, ospec],
        out_shape=[jax.ShapeDtypeStruct((e, dm, f), BF16)] * 2,
        compiler_params=_cparams("parallel", "parallel"),
        name="split_gate_up",
    )(w_gu, sel)


def _moe_group(x2d, row0, n, w, prev):
    dm = x2d.shape[1]
    top_i, gate, xb = _router(x2d, w['router_w_t'], w['router_b'], row0, n)
    rank, counts = _rank(top_i)
    block = MOE_BLOCK
    m = n * TOP_K
    nblk = -(-m // block) + N_EXPERTS
    rows = nblk * block
    counts = counts.reshape(N_EXPERTS)
    padded = (counts + block - 1) // block * block
    pad_end = jnp.cumsum(padded)
    pad_start = pad_end - padded
    experts = jnp.arange(N_EXPERTS, dtype=top_i.dtype)
    dest = rank + jnp.sum(jnp.where(top_i[:, :, None] == experts, pad_start.astype(jnp.int32), 0),
                          axis=-1)
    row_tok = _slot_tokens(dest, rows)
    blk_start = jnp.arange(nblk, dtype=pad_end.dtype) * block
    blk_e = jnp.minimum(jnp.sum(pad_end[None, :] <= blk_start[:, None], axis=1),
                        N_EXPERTS - 1).astype(jnp.int32)
    blk_on = (blk_start < pad_end[-1]).astype(jnp.int32)
    x_rows = xb.at[row_tok].get(mode='promise_in_bounds')
    y_rows = _experts(x_rows, blk_e, blk_on, w['e_w_g'], w['e_w_u'], w['e_b_g'], w['e_b_u'], w['e_w_d'],
                      w['layer'], w['e_b_d'], block=block)
    y_g = y_rows.at[dest].get(mode='promise_in_bounds')
    return _combine(x2d, y_g, gate.T, w['ln3_g'], w['ln3_b'], row0, prev)


def _moe(x, w):
    b, s, dm = x.shape
    n = b * s
    x2d = x.reshape(n, dm)
    groups = MOE_TOKEN_GROUPS if n % (MOE_TOKEN_GROUPS * TILE_ROUTER) == 0 else 1
    out = None
    for gi in range(groups):
        out = _moe_group(x2d, gi * (n // groups), n // groups, w, out)
    return out.reshape(b, s, dm)


def _rope_tables(s):
    half = DF_QK // 2
    inv_freq = jnp.power(ROPE_THETA, -jnp.arange(half, dtype=F32) / half)
    ang = jnp.arange(s, dtype=F32)[:, None] * inv_freq[None, :]
    cos = jnp.cos(ang)
    sin = jnp.sin(ang)
    cos_t = jnp.concatenate([cos, cos, cos, cos], axis=1)
    sin_t = jnp.concatenate([-sin, sin, -sin, sin], axis=1)
    return cos_t, sin_t


def _prepare_layer(l, p, s):
    dm = D_MODEL
    w = {}
    w['w_in'] = _pack_in_proj(p['w_in'][l]).astype(BF16)
    pad8 = jnp.zeros((8,), F32)
    pad112 = jnp.zeros((LANES - 16,), F32)
    w['nega'] = jnp.concatenate([pad8, -jnp.exp(p['dn_A_log'][l].reshape(8)), pad112]).reshape(1, LANES)
    w['dtb'] = jnp.concatenate([pad8, p['dn_dt_bias'][l].reshape(8), pad112]).reshape(1, LANES)
    w['dn_conv'] = p['dn_conv'][l]
    t5 = min(S5_CHUNK, s)
    w['s5'] = _s5_operators(p['s5_lam_re'][l], p['s5_lam_im'][l], p['s5_log_dt'][l],
                            p['s5_B_re'][l], p['s5_B_im'][l], p['s5_C_re'][l],
                            p['s5_C_im'][l], p['s5_D'][l], chunk=t5, nc=s // t5)
    for name in ('df_lq1', 'df_lk1', 'df_lq2', 'df_lk2'):
        w[name] = p[name][l].reshape(1, DF_QK)
    w['df_norm'] = p['df_norm'][l].reshape(1, DF_V)
    w['lam_init'] = 0.8 - 0.6 * math.exp(-0.3 * l)
    w['b_gate'] = p['b_gate'][l].reshape(1, -1)
    w['dn_norm'] = p['dn_norm'][l].reshape(1, DN_DIM)
    w['s5_glu_w'] = p['s5_glu_w'][l].astype(BF16)
    w['s5_glu_b'] = p['s5_glu_b'][l].reshape(1, -1)
    w['w_branch'] = p['w_branch'][l].astype(BF16)
    w['w_out'] = p['w_out'][l].astype(BF16)
    for name in ('ln1_g', 'ln1_b', 'ln2_g', 'ln2_b', 'ln3_g', 'ln3_b'):
        w[name] = p[name][l].reshape(1, dm)
    w['xa_wkv'] = p['xa_wkv'][l].astype(BF16)
    w['xa_wq'] = p['xa_wq'][l].astype(BF16)
    w['xa_wo'] = p['xa_wo'][l].astype(BF16)
    w['router_w_t'] = p['router_w'][l].T
    w['router_b'] = p['router_b'][l].reshape(N_EXPERTS, 1)
    w['e_w_g'], w['e_w_u'] = _split_gate_up(p['e_w_gu'], l)
    w['e_b_g'] = p['e_b_gu'][l][:, None, 0::2]
    w['e_b_u'] = p['e_b_gu'][l][:, None, 1::2]
    w['e_w_d'] = p['e_w_down']
    w['layer'] = l
    w['e_b_d'] = p['e_b_down'][l][:, None, :]
    return w


def _layer(x, mem, w, tabs):
    b, s, dm = x.shape
    n = b * s
    proj = _matmul(x.reshape(n, dm), w['w_in'], tm=TILE_MM, tn=LANES * 19)
    proj = proj.reshape(b, s, PROJ_WIDTH)

    qkv, bg = _dn_prep(proj, w['dn_conv'], w['nega'], w['dtb'], chunk=DN_CHUNK)
    bgr = bg[:, :, :16].reshape(b, s // DN_CHUNK, DN_CHUNK, 16).transpose(0, 1, 3, 2)
    o_f, o_b = _dn_scan(qkv, bg, bgr, chunk=DN_CHUNK)

    t5 = min(S5_CHUNK, s)
    nc = s // t5
    u5 = proj[:, :, COL_S5_U:COL_S5_U + S5_WIDTH].reshape(b, nc, t5, S5_GROUPS, S5_GROUP)
    u5 = u5.transpose(3, 0, 1, 2, 4).reshape(S5_GROUPS, b, nc, t5 * S5_GROUP).astype(BF16)
    y5 = _s5_ssm(u5, *w['s5'])
    y5 = y5.reshape(S5_GROUPS, b, nc, t5, S5_GROUP).transpose(1, 2, 3, 0, 4).reshape(b, s, S5_WIDTH)

    qa, qb, kk, vv = _rope_prep(proj, *tabs)
    o_c = _diff_attn(qa, qb, kk, vv, w['df_lq1'], w['df_lk1'], w['df_lq2'], w['df_lk2'],
                     w['df_norm'], w['lam_init'])

    x = _merge(x, o_f, o_b, proj, y5, o_c, w['b_gate'], w['dn_norm'], w['s5_glu_w'], w['s5_glu_b'],
               w['w_branch'], w['w_out'], w['ln1_g'], w['ln1_b'])

    mlen = mem.shape[1]
    kv = _matmul(mem.reshape(b * mlen, dm), w['xa_wkv'], tm=512, tn=1024)
    x = _xattn(x, kv.reshape(b, mlen, 2 * XA_WIDTH), w['xa_wq'], w['xa_wo'], w['ln2_g'], w['ln2_b'])
    return _moe(x, w)


def kernel(x_prompt, x_sample, mem_prompt, mem_sample, w_in, b_gate, dn_conv, dn_A_log, dn_dt_bias,
           dn_norm, s5_lam_re, s5_lam_im, s5_log_dt, s5_B_re, s5_B_im, s5_C_re, s5_C_im, s5_D,
           s5_glu_w, s5_glu_b, df_lq1, df_lk1, df_lq2, df_lk2, df_norm, w_branch, w_out,
           ln1_g, ln1_b, xa_wq, xa_wkv, xa_wo, ln2_g, ln2_b, router_w, router_b,
           e_w_gu, e_b_gu, e_w_down, e_b_down, ln3_g, ln3_b):
    p = dict(w_in=w_in, b_gate=b_gate, dn_conv=dn_conv, dn_A_log=dn_A_log, dn_dt_bias=dn_dt_bias,
             dn_norm=dn_norm, s5_lam_re=s5_lam_re, s5_lam_im=s5_lam_im, s5_log_dt=s5_log_dt,
             s5_B_re=s5_B_re, s5_B_im=s5_B_im, s5_C_re=s5_C_re, s5_C_im=s5_C_im, s5_D=s5_D,
             s5_glu_w=s5_glu_w, s5_glu_b=s5_glu_b, df_lq1=df_lq1, df_lk1=df_lk1, df_lq2=df_lq2,
             df_lk2=df_lk2, df_norm=df_norm, w_branch=w_branch, w_out=w_out, ln1_g=ln1_g,
             ln1_b=ln1_b, xa_wq=xa_wq, xa_wkv=xa_wkv, xa_wo=xa_wo, ln2_g=ln2_g, ln2_b=ln2_b,
             router_w=router_w, router_b=router_b, e_w_gu=e_w_gu, e_b_gu=e_b_gu,
             e_w_down=e_w_down, e_b_down=e_b_down, ln3_g=ln3_g, ln3_b=ln3_b)
    s = x_prompt.shape[1]
    assert x_sample.shape[1] == s
    tabs = _rope_tables(s)
    layers = [_prepare_layer(l, p, s) for l in range(DEPTH)]
    nb = x_prompt.shape[0]
    x = jnp.concatenate([x_prompt, x_sample], axis=0)
    mem = jnp.concatenate([mem_prompt, mem_sample], axis=0)
    for w in layers:
        x = _layer(x, mem, w, tabs)
    return (x[:nb], x[nb:])
```

```python
import functools
import math

import numpy as np
import jax
import jax.numpy as jnp
from jax import lax
from jax.experimental import pallas as pl
from jax.experimental.pallas import tpu as pltpu

F32 = jnp.float32
BF16 = jnp.bfloat16
HI = lax.Precision.HIGHEST

D_MODEL = 1024
DN_HEADS = 4
DN_DIM = 128
DN_WIDTH = DN_HEADS * DN_DIM
DN_CONV = 4
S5_GROUP = 16
S5_WIDTH = 512
S5_GROUPS = S5_WIDTH // S5_GROUP
S5_STATE = 64
DF_HEADS = 4
DF_QK = 64
DF_V = 128
DF_WIDTH = DF_HEADS * DF_V
ROPE_THETA = 10000.0
XA_HEADS = 4
XA_DIM = 128
XA_WIDTH = XA_HEADS * XA_DIM
N_EXPERTS = 32
TOP_K = 4
D_FF = 1024
SWIGLU_LIMIT = 7.0
SWIGLU_ALPHA = 1.702
N_BRANCH = 3
DEPTH = 2
DEEPNORM_ALPHA = (2 * DEPTH) ** 0.25
LN_EPS = 1e-5
RMS_EPS = 1e-6
LOG2E = 1.4426950408889634

LANES = 128
SUBLANES = 8
VMEM_LIMIT = 56 * 1024 * 1024

COL_DN_QKV = 0
COL_DN_Z = 1536
COL_S5_U = 2048
COL_QQ = 2560
COL_KK = 3072
COL_DV = 3584
COL_GATE = 4096
COL_BA = 7168
PROJ_WIDTH = 7296

TILE_MM = 1024
TILE_DN_PREP = 512
TILE_DN_SCAN = 256
DN_CHUNK = 64
DN_CHUNKS_PER_TRIP = 2
S5_CHUNK = 32
TILE_ROPE = 512
TILE_Q = 256
TILE_K = 512
TILE_MERGE = 256
TILE_XA = 512
TILE_ROUTER = 512
MOE_BLOCK = 512
MOE_TOKEN_GROUPS = 4
TILE_SLOTS = 2048
TILE_COMBINE = 256


def _in_proj_perm():
    o_dn_b, o_dn_a, o_s5 = 2048, 2056, 2064
    o_q1, o_q2, o_k1, o_k2, o_dv, o_gate = 2576, 2832, 3088, 3344, 3600, 4112
    perm = np.full((PROJ_WIDTH,), -1, np.int64)
    perm[0:2048] = np.arange(2048)
    perm[COL_S5_U:COL_S5_U + 512] = o_s5 + np.arange(512)
    for h in range(DF_HEADS):
        j = np.arange(DF_QK)
        perm[COL_QQ + h * 128 + j] = o_q1 + h * DF_QK + j
        perm[COL_QQ + h * 128 + 64 + j] = o_q2 + h * DF_QK + j
        perm[COL_KK + h * 128 + j] = o_k1 + h * DF_QK + j
        perm[COL_KK + h * 128 + 64 + j] = o_k2 + h * DF_QK + j
    perm[COL_DV:COL_DV + 512] = o_dv + np.arange(512)
    perm[COL_GATE:COL_GATE + 3072] = o_gate + np.arange(3072)
    perm[COL_BA:COL_BA + 8] = o_dn_b + np.arange(8)
    perm[COL_BA + 8:COL_BA + 16] = o_dn_a + np.arange(8)
    return perm


_IN_PERM = _in_proj_perm()


def _pack_in_proj(w_in):
    runs, start = [], 0
    for i in range(1, PROJ_WIDTH + 1):
        if i == PROJ_WIDTH or _IN_PERM[i] != _IN_PERM[i - 1] + (1 if _IN_PERM[i - 1] >= 0 else 0):
            runs.append((start, i))
            start = i
    pieces = []
    for a, b in runs:
        src = int(_IN_PERM[a])
        pieces.append(jnp.zeros((w_in.shape[0], b - a), w_in.dtype) if src < 0
                      else w_in[:, src:src + (b - a)])
    return jnp.concatenate(pieces, axis=1)


def _cparams(*sem):
    return pltpu.CompilerParams(dimension_semantics=sem, vmem_limit_bytes=VMEM_LIMIT)


def _bdot(a, b):
    return jnp.dot(a.astype(BF16), b.astype(BF16), preferred_element_type=F32)


def _bdot_nt(a, b):
    return lax.dot_general(a.astype(BF16), b.astype(BF16), (((1,), (1,)), ((), ())),
                           preferred_element_type=F32)


def _bdot_tn(a, b):
    return lax.dot_general(a.astype(BF16), b.astype(BF16), (((0,), (0,)), ((), ())),
                           preferred_element_type=F32)


def _sigmoid(x):
    return 0.5 * jnp.tanh(0.5 * x) + 0.5


def _silu(x):
    return x * _sigmoid(x)


def _mm_kernel(x_ref, w_ref, o_ref):
    o_ref[...] = jnp.dot(x_ref[...].astype(BF16), w_ref[...],
                         preferred_element_type=F32).astype(o_ref.dtype)


def _matmul(x, w, *, tm, tn, out_dtype=F32):
    n, k = x.shape
    m = w.shape[1]
    tm = min(tm, n)
    tn = min(tn, m)
    return pl.pallas_call(
        _mm_kernel,
        grid=(n // tm, m // tn),
        in_specs=[pl.BlockSpec((tm, k), lambda i, j: (i, 0)),
                  pl.BlockSpec((k, tn), lambda i, j: (0, j))],
        out_specs=pl.BlockSpec((tm, tn), lambda i, j: (i, j)),
        out_shape=jax.ShapeDtypeStruct((n, m), out_dtype),
        compiler_params=_cparams("parallel", "arbitrary"),
        name="matmul",
    )(x, w)


def _dn_prep_kernel(x_ref, prev_ref, next_ref, ba_ref, cw_ref, nega_ref, dtb_ref,
                    qkv_ref, bg_ref, *, tile, chunk):
    i = pl.program_id(1)
    last = pl.num_programs(1) - 1
    x = x_ref[0]
    prev = jnp.where(i > 0, prev_ref[0], 0.0)
    nxt = jnp.where(i < last, next_ref[0], 0.0)
    ext = jnp.concatenate([prev, x, nxt], axis=0)
    cw = cw_ref[...]
    acc = cw[0:1] * ext[6:6 + tile]
    for j in range(1, DN_CONV):
        acc = acc + cw[j:j + 1] * ext[6 + j:6 + j + tile]
    act = _silu(acc)
    for h in range(DN_HEADS):
        lo = h * DN_DIM
        qh = act[:, lo:lo + DN_DIM]
        qn = qh * lax.rsqrt(jnp.sum(qh * qh, axis=-1, keepdims=True) + 1e-6)
        qkv_ref[0, :, lo:lo + DN_DIM] = qn * (DN_DIM ** -0.5)
        kh = act[:, DN_WIDTH + lo:DN_WIDTH + lo + DN_DIM]
        qkv_ref[0, :, DN_WIDTH + lo:DN_WIDTH + lo + DN_DIM] = kh * lax.rsqrt(
            jnp.sum(kh * kh, axis=-1, keepdims=True) + 1e-6)
    qkv_ref[0, :, 2 * DN_WIDTH:] = act[:, 2 * DN_WIDTH:]

    ba = ba_ref[0]
    beta = _sigmoid(ba)
    z = ba + dtb_ref[...]
    softplus = jnp.maximum(z, 0.0) + jnp.log(1.0 + jnp.exp(-jnp.abs(z)))
    g = nega_ref[...] * softplus
    row = lax.broadcasted_iota(jnp.int32, (chunk, chunk), 0)
    col = lax.broadcasted_iota(jnp.int32, (chunk, chunk), 1)
    lower = (row >= col).astype(F32)
    upper = (row <= col).astype(F32)
    lane = lax.broadcasted_iota(jnp.int32, (chunk, LANES), 1)
    fwd_lane = lane < 8 + DN_HEADS
    for c in range(tile // chunk):
        gc = g[c * chunk:(c + 1) * chunk]
        cf = jnp.dot(lower, gc, precision=HI, preferred_element_type=F32)
        cb = jnp.dot(upper, gc, precision=HI, preferred_element_type=F32)
        gcum = jnp.where(fwd_lane, cf, cb)
        bg_ref[0, c * chunk:(c + 1) * chunk, :] = jnp.where(
            lane < 8, beta[c * chunk:(c + 1) * chunk], gcum)


def _dn_prep(proj, conv_w, nega, dtb, *, chunk):
    b, s, _ = proj.shape
    tile = min(TILE_DN_PREP, s)
    nt = s // tile
    r8 = tile // SUBLANES
    kern = functools.partial(_dn_prep_kernel, tile=tile, chunk=chunk)
    return pl.pallas_call(
        kern,
        grid=(b, nt),
        in_specs=[
            pl.BlockSpec((1, tile, 3 * DN_WIDTH), lambda bi, i: (bi, i, 0)),
            pl.BlockSpec((1, SUBLANES, 3 * DN_WIDTH),
                         lambda bi, i: (bi, jnp.maximum(i * r8 - 1, 0), 0)),
            pl.BlockSpec((1, SUBLANES, 3 * DN_WIDTH),
                         lambda bi, i: (bi, jnp.minimum((i + 1) * r8, s // SUBLANES - 1), 0)),
            pl.BlockSpec((1, tile, LANES), lambda bi, i: (bi, i, COL_BA // LANES)),
            pl.BlockSpec((DN_CONV, 3 * DN_WIDTH), lambda bi, i: (0, 0)),
            pl.BlockSpec((1, LANES), lambda bi, i: (0, 0)),
            pl.BlockSpec((1, LANES), lambda bi, i: (0, 0)),
        ],
        out_specs=[pl.BlockSpec((1, tile, 3 * DN_WIDTH), lambda bi, i: (bi, i, 0)),
                   pl.BlockSpec((1, tile, LANES), lambda bi, i: (bi, i, 0))],
        out_shape=[jax.ShapeDtypeStruct((b, s, 3 * DN_WIDTH), F32),
                   jax.ShapeDtypeStruct((b, s, LANES), F32)],
        compiler_params=_cparams("parallel", "parallel"),
        name="dn_prep",
    )(proj, proj, proj, proj, conv_w, nega, dtb)


def _dn_local(chains, chunk):
    ri = lax.broadcasted_iota(jnp.int32, (chunk, chunk), 0)
    ci = lax.broadcasted_iota(jnp.int32, (chunk, chunk), 1)
    pre = []
    for d, h, q, k, v, bgc, rows in chains:
        li = d * DN_HEADS + h
        beta = bgc[:, li:li + 1]
        gcol = bgc[:, 8 + li:8 + li + 1]
        grow = rows[8 + li:8 + li + 1, :]
        gtot = gcol[chunk - 1:chunk] if d == 0 else gcol[0:1]
        incl = (ri >= ci) if d == 0 else (ri <= ci)
        strict = (ri > ci) if d == 0 else (ri < ci)
        eg = jnp.exp(gcol)
        kb = k * beta
        decay = jnp.where(incl, jnp.exp(jnp.where(incl, gcol - grow, 0.0)), 0.0)
        pre.append(dict(k=k, qg=q * eg, kd=k * jnp.exp(gtot - gcol), decay=decay, strict=strict,
                        x2=jnp.concatenate([v * beta, kb * eg], axis=1), sdec=jnp.exp(gtot),
                        qkb=jnp.concatenate([q, kb], axis=0)))
    qk = [_bdot_nt(c['qkb'], c['k']) for c in pre]
    intra = [x[:chunk] * c['decay'] for x, c in zip(qk, pre)]
    p = [-jnp.where(c['strict'], x[chunk:] * c['decay'], 0.0) for x, c in zip(qk, pre)]
    r = p
    p = [_bdot(x, x) for x in p]
    for _ in range(int(math.log2(chunk)) - 2):
        rp = [_bdot(jnp.concatenate([ri_, pi_], axis=0), pi_) for ri_, pi_ in zip(r, p)]
        r = [ri_ + pi_ + x[:chunk] for ri_, pi_, x in zip(r, p, rp)]
        p = [x[chunk:] for x in rp]
    r = [ri_ + pi_ + _bdot(ri_, pi_) for ri_, pi_ in zip(r, p)]
    uw = [c['x2'] + _bdot(ri_, c['x2']) for ri_, c in zip(r, pre)]
    return [dict(u=x[:, :DN_DIM], wq=jnp.concatenate([x[:, DN_DIM:], c['qg']], axis=0),
                 intra=it, kd=c['kd'], sdec=c['sdec']) for x, it, c in zip(uw, intra, pre)]


def _dn_state(local, states, chunk):
    ws = [_bdot(c['wq'], st) for c, st in zip(local, states)]
    v_new = [c['u'] - y[:chunk] for c, y in zip(local, ws)]
    out = [y[chunk:] + _bdot(c['intra'], vn) for y, c, vn in zip(ws, local, v_new)]
    st_new = [st * c['sdec'] + _bdot_tn(c['kd'], vn) for st, c, vn in zip(states, local, v_new)]
    return out, st_new


def _dn_scan_kernel(xf_ref, xb_ref, gf_ref, gb_ref, rf_ref, rb_ref, of_ref, ob_ref, st_ref,
                    *, tile, chunk):
    @pl.when(pl.program_id(1) == 0)
    def _():
        st_ref[...] = jnp.zeros_like(st_ref)

    nct = tile // chunk
    nchain = 2 * DN_HEADS

    def body(j, carry):
        chains, where = [], []
        for slot in range(DN_CHUNKS_PER_TRIP):
            for d in range(2):
                c = DN_CHUNKS_PER_TRIP * j + slot
                c = c if d == 0 else nct - 1 - c
                r0 = pl.multiple_of(c * chunk, chunk)
                x_ref, g_ref, r_ref = (xf_ref, gf_ref, rf_ref) if d == 0 else (xb_ref, gb_ref, rb_ref)
                bgc = g_ref[0, pl.ds(r0, chunk), :]
                rows = r_ref[0, c]
                for h in range(DN_HEADS):
                    lo = h * DN_DIM
                    q = x_ref[0, pl.ds(r0, chunk), lo:lo + DN_DIM]
                    k = x_ref[0, pl.ds(r0, chunk), DN_WIDTH + lo:DN_WIDTH + lo + DN_DIM]
                    v = x_ref[0, pl.ds(r0, chunk), 2 * DN_WIDTH + lo:2 * DN_WIDTH + lo + DN_DIM]
                    chains.append((d, h, q, k, v, bgc, rows))
                    where.append((d, h, r0))
        states = [st_ref[i] for i in range(nchain)]
        local = _dn_local(chains, chunk)
        outs = []
        for slot in range(DN_CHUNKS_PER_TRIP):
            out, states = _dn_state(local[slot * nchain:(slot + 1) * nchain], states, chunk)
            outs += out
        for (d, h, r0), out in zip(where, outs):
            o_ref = of_ref if d == 0 else ob_ref
            o_ref[0, pl.ds(r0, chunk), h * DN_DIM:(h + 1) * DN_DIM] = out
        for i in range(nchain):
            st_ref[i] = states[i]
        return carry

    lax.fori_loop(0, nct // DN_CHUNKS_PER_TRIP, body, 0)


def _dn_scan(qkv, bg, bgr, *, chunk):
    b, s, _ = qkv.shape
    tile = min(TILE_DN_SCAN, s)
    nt = s // tile
    nct = tile // chunk
    kern = functools.partial(_dn_scan_kernel, tile=tile, chunk=chunk)
    fwd = lambda bi, i: (bi, i, 0)
    bwd = lambda bi, i: (bi, nt - 1 - i, 0)
    fwd4 = lambda bi, i: (bi, i, 0, 0)
    bwd4 = lambda bi, i: (bi, nt - 1 - i, 0, 0)
    return pl.pallas_call(
        kern,
        grid=(b, nt),
        in_specs=[pl.BlockSpec((1, tile, 3 * DN_WIDTH), fwd),
                  pl.BlockSpec((1, tile, 3 * DN_WIDTH), bwd),
                  pl.BlockSpec((1, tile, LANES), fwd),
                  pl.BlockSpec((1, tile, LANES), bwd),
                  pl.BlockSpec((1, nct, 16, chunk), fwd4),
                  pl.BlockSpec((1, nct, 16, chunk), bwd4)],
        out_specs=[pl.BlockSpec((1, tile, DN_WIDTH), fwd),
                   pl.BlockSpec((1, tile, DN_WIDTH), bwd)],
        out_shape=[jax.ShapeDtypeStruct((b, s, DN_WIDTH), F32),
                   jax.ShapeDtypeStruct((b, s, DN_WIDTH), F32)],
        scratch_shapes=[pltpu.VMEM((2 * DN_HEADS, DN_DIM, DN_DIM), F32)],
        compiler_params=_cparams("parallel", "arbitrary"),
        name="dn_scan",
    )(qkv, qkv, bg, bg, bgr, bgr)


def _s5_operators(lam_re, lam_im, log_dt, b_re, b_im, c_re, c_im, d_skip, *, chunk, nc):
    t = chunk
    levels = max(1, int(math.log2(nc)))
    tau = jnp.arange(t + 1, dtype=F32)
    kern = []
    lamp = []
    for d in range(2):
        lam = lax.complex(lam_re[d], lam_im[d])
        dt = jnp.exp(log_dt[d])[:, None]
        lam_dt = lam * dt
        lam_bar = jnp.exp(lam_dt)
        bbar = ((lam_bar - 1.0) / lam)[:, :, None] * lax.complex(b_re[d], b_im[d])
        cc = lax.complex(c_re[d], c_im[d])
        pw = jnp.exp(lam_dt[:, None, :] * tau[None, :, None].astype(jnp.complex64))
        kern.append((pw, bbar, cc))
        steps = (t * 2.0 ** jnp.arange(levels, dtype=F32)).astype(jnp.complex64)
        lamp.append(jnp.exp(lam_dt[:, None, :] * steps[None, :, None]))
    (pwf, bbf, ccf), (pwb, bbb, ccb) = kern
    ef = jnp.einsum('gsp,gpc->gscp', pwf[:, t - 1::-1, :], bbf)
    eb = jnp.einsum('gsp,gpc->gscp', pwb[:, :t, :], bbb)
    w1 = jnp.concatenate([jnp.real(ef), jnp.real(eb), jnp.imag(ef), jnp.imag(eb)], axis=-1)
    w1 = w1.reshape(S5_GROUPS, t * S5_GROUP, 4 * S5_STATE)
    kf = jnp.real(jnp.einsum('gcp,gtp,gpd->gtcd', ccf, pwf[:, :t, :], bbf))
    kb = jnp.real(jnp.einsum('gcp,gtp,gpd->gtcd', ccb, pwb[:, :t, :], bbb))
    si = jnp.arange(t)[:, None]
    ti = jnp.arange(t)[None, :]
    lag_f = jnp.clip(ti - si, 0, t - 1)
    lag_b = jnp.clip(si - ti, 0, t - 1)
    toep = (jnp.where((ti >= si)[None, :, :, None, None], kf[:, lag_f], 0.0)
            + jnp.where((si >= ti)[None, :, :, None, None], kb[:, lag_b], 0.0))
    eye_t = jnp.eye(t, dtype=F32)[None, :, :, None, None]
    eye_c = jnp.eye(S5_GROUP, dtype=F32)[None, None, None, :, :]
    toep = toep + eye_t * eye_c * d_skip.reshape(S5_GROUPS, 1, 1, S5_GROUP, 1)
    w_intra = toep.transpose(0, 1, 4, 2, 3).reshape(S5_GROUPS, t * S5_GROUP, t * S5_GROUP)
    mf = jnp.einsum('gcp,gtp->gptc', ccf, pwf[:, 1:t + 1, :])
    mb = jnp.einsum('gcp,gtp->gptc', ccb, pwb[:, t:0:-1, :])
    w_state = jnp.concatenate([jnp.real(mf), jnp.real(mb), -jnp.imag(mf), -jnp.imag(mb)], axis=1)
    w_state = w_state.reshape(S5_GROUPS, 4 * S5_STATE, t * S5_GROUP)
    w3 = jnp.concatenate([w_intra, w_state], axis=1)
    lam_r = jnp.concatenate([jnp.real(lamp[0]), jnp.real(lamp[1])], axis=-1)
    lam_i = jnp.concatenate([jnp.imag(lamp[0]), jnp.imag(lamp[1])], axis=-1)
    lamp = jnp.stack([lam_r, lam_i], axis=2)
    return w1.astype(BF16), w3.astype(BF16), lamp.astype(F32)


def _s5_kernel(u_ref, w1_ref, w3_ref, lam_ref, y_ref, *, nc, width, levels):
    u = u_ref[0, 0]
    loc = jnp.dot(u, w1_ref[0], preferred_element_type=F32)
    re = loc[:, :2 * S5_STATE]
    im = loc[:, 2 * S5_STATE:]
    row = lax.broadcasted_iota(jnp.int32, (nc, 2 * S5_STATE), 0)
    fwd = lax.broadcasted_iota(jnp.int32, (nc, 2 * S5_STATE), 1) < S5_STATE

    def shifted(x, d):
        down = jnp.where(row >= d, pltpu.roll(x, d, 0), 0.0)
        up = jnp.where(row < nc - d, pltpu.roll(x, nc - d, 0), 0.0)
        return jnp.where(fwd, down, up)

    for k in range(levels):
        d = 2 ** k
        if d >= nc:
            break
        lr = lam_ref[0, k, 0:1, :]
        li = lam_ref[0, k, 1:2, :]
        sr = shifted(re, d)
        si = shifted(im, d)
        re, im = re + sr * lr - si * li, im + sr * li + si * lr
    ent = jnp.concatenate([shifted(re, 1), shifted(im, 1)], axis=1)
    y = jnp.dot(u, w3_ref[0, :width, :], preferred_element_type=F32)
    y = y + jnp.dot(ent.astype(BF16), w3_ref[0, width:, :], preferred_element_type=F32)
    y_ref[0, 0] = y


def _s5_ssm(u5, w1, w3, lamp):
    g, b, nc, width = u5.shape
    levels = lamp.shape[1]
    kern = functools.partial(_s5_kernel, nc=nc, width=width, levels=levels)
    return pl.pallas_call(
        kern,
        grid=(g, b),
        in_specs=[pl.BlockSpec((1, 1, nc, width), lambda gi, bi: (gi, bi, 0, 0)),
                  pl.BlockSpec((1, width, 4 * S5_STATE), lambda gi, bi: (gi, 0, 0)),
                  pl.BlockSpec((1, width + 4 * S5_STATE, width), lambda gi, bi: (gi, 0, 0)),
                  pl.BlockSpec((1, levels, 2, 2 * S5_STATE), lambda gi, bi: (gi, 0, 0, 0))],
        out_specs=pl.BlockSpec((1, 1, nc, width), lambda gi, bi: (gi, bi, 0, 0)),
        out_shape=jax.ShapeDtypeStruct((g, b, nc, width), F32),
        compiler_params=_cparams("parallel", "parallel"),
        name="s5_ssm",
    )(u5, w1, w3, lamp)


def _rope_kernel(qq_ref, kk_ref, v_ref, cos_ref, sin_ref, qa_ref, qb_ref, ko_ref, vo_ref):
    width = DF_HEADS * 2 * DF_QK
    cos = jnp.concatenate([cos_ref[...]] * (width // LANES), axis=1)
    sin = jnp.concatenate([sin_ref[...]] * (width // LANES), axis=1)
    lane = lax.broadcasted_iota(jnp.int32, (1, width), 1)
    first_half = (lane % DF_QK) < (DF_QK // 2)

    def rot(x):
        partner = jnp.where(first_half, pltpu.roll(x, width - DF_QK // 2, 1),
                            pltpu.roll(x, DF_QK // 2, 1))
        return x * cos + partner * sin

    q = rot(qq_ref[0]) * (DF_QK ** -0.5 * LOG2E)
    is_a = (lane % (2 * DF_QK)) < DF_QK
    qa_ref[0] = jnp.where(is_a, q, 0.0).astype(BF16)
    qb_ref[0] = jnp.where(is_a, 0.0, q).astype(BF16)
    ko_ref[0] = rot(kk_ref[0]).astype(BF16)
    vo_ref[0] = v_ref[0].astype(BF16)


def _rope_prep(proj, cos_t, sin_t):
    b, s, _ = proj.shape
    tile = min(TILE_ROPE, s)
    w = DF_WIDTH
    spec = lambda cb: pl.BlockSpec((1, tile, w), lambda bi, i: (bi, i, cb))
    out = pl.BlockSpec((1, tile, w), lambda bi, i: (bi, i, 0))
    tab = pl.BlockSpec((tile, LANES), lambda bi, i: (i, 0))
    return pl.pallas_call(
        _rope_kernel,
        grid=(b, s // tile),
        in_specs=[spec(COL_QQ // w), spec(COL_KK // w), spec(COL_DV // w), tab, tab],
        out_specs=[out, out, out, out],
        out_shape=[jax.ShapeDtypeStruct((b, s, w), BF16)] * 4,
        compiler_params=_cparams("parallel", "parallel"),
        name="rope_prep",
    )(proj, proj, proj, cos_t, sin_t)


def _diff_attn_kernel(qa_ref, qb_ref, k_ref, v_ref, lq1_ref, lk1_ref, lq2_ref, lk2_ref, ng_ref,
                      o_ref, s_ref, *, tq, tk, nk, lam_init):
    q2 = (qa_ref[0], qb_ref[0])
    nt_dims = (((1,), (1,)), ((), ()))

    def fold(x, op, init):
        for j in range(tk // LANES):
            init = op(init, x[:, j * LANES:(j + 1) * LANES])
        return init

    m = [jnp.full((tq, LANES), -jnp.inf, F32)] * 2
    for c in range(nk):
        kc = k_ref[0, c * tk:(c + 1) * tk, :]
        for i in range(2):
            sc = lax.dot_general(q2[i], kc, nt_dims, preferred_element_type=F32)
            s_ref[c, i * tq:(i + 1) * tq, :] = sc
            m[i] = fold(sc, jnp.maximum, m[i])
    m = [jnp.max(x, axis=-1, keepdims=True) for x in m]

    acc = [jnp.zeros((tq, 2 * DF_V), F32)] * 2
    ones = jnp.ones((tk, DF_V), BF16)
    for c in range(nk):
        vc = jnp.concatenate([v_ref[0, c * tk:(c + 1) * tk, :], ones], axis=1)
        for i in range(2):
            e = jnp.exp2((s_ref[c, i * tq:(i + 1) * tq, :] - m[i]).astype(BF16))
            acc[i] = acc[i] + jnp.dot(e, vc, preferred_element_type=F32)
    lam = (jnp.exp(jnp.sum(lq1_ref[...] * lk1_ref[...], axis=-1, keepdims=True))
           - jnp.exp(jnp.sum(lq2_ref[...] * lk2_ref[...], axis=-1, keepdims=True)) + lam_init)
    acc = (acc[0][:, :DF_V] * (1.0 / acc[0][:, DF_V:DF_V + 1])
           - acc[1][:, :DF_V] * (lam / acc[1][:, DF_V:DF_V + 1]))
    ms = jnp.mean(acc * acc, axis=-1, keepdims=True)
    o_ref[0] = acc * lax.rsqrt(ms + RMS_EPS) * ng_ref[...] * (1.0 - lam_init)


def _diff_attn(qa, qb, kk, vv, lq1, lk1, lq2, lk2, norm_g, lam_init):
    b, s, _ = qa.shape
    tq = min(TILE_Q, s)
    tk = min(TILE_K, s)
    nk = s // tk
    kern = functools.partial(_diff_attn_kernel, tq=tq, tk=tk, nk=nk, lam_init=lam_init)
    qspec = pl.BlockSpec((1, tq, LANES), lambda bi, h, i: (bi, i, h))
    kspec = pl.BlockSpec((1, s, LANES), lambda bi, h, i: (bi, 0, h))
    vec64 = pl.BlockSpec((1, DF_QK), lambda bi, h, i: (0, 0))
    return pl.pallas_call(
        kern,
        grid=(b, DF_HEADS, s // tq),
        in_specs=[qspec, qspec, kspec, kspec, vec64, vec64, vec64, vec64,
                  pl.BlockSpec((1, DF_V), lambda bi, h, i: (0, 0))],
        out_specs=pl.BlockSpec((1, tq, DF_V), lambda bi, h, i: (bi, i, h)),
        out_shape=jax.ShapeDtypeStruct((b, s, DF_WIDTH), F32),
        scratch_shapes=[pltpu.VMEM((nk, 2 * tq, tk), F32)],
        compiler_params=_cparams("parallel", "parallel", "arbitrary"),
        name="diff_attn",
    )(qa, qb, kk, vv, lq1, lk1, lq2, lk2, norm_g)


def _layer_norm(x, g, b):
    mu = jnp.mean(x, axis=-1, keepdims=True)
    xc = x - mu
    var = jnp.mean(xc * xc, axis=-1, keepdims=True)
    return xc * lax.rsqrt(var + LN_EPS) * g + b


def _merge_kernel(x_ref, of_ref, ob_ref, z_ref, y5_ref, oc_ref, g0_ref, g1_ref, g2_ref,
                  bgate_ref, dnn_ref, gluw_ref, glub_ref, wbr_ref, wout_ref, lng_ref, lnb_ref,
                  o_ref):
    o = of_ref[0] + ob_ref[0]
    z = z_ref[0]
    parts = []
    for h in range(DN_HEADS):
        oh = o[:, h * DN_DIM:(h + 1) * DN_DIM]
        ms = jnp.mean(oh * oh, axis=-1, keepdims=True)
        parts.append(oh * lax.rsqrt(ms + RMS_EPS) * dnn_ref[...])
    o_a = jnp.concatenate(parts, axis=1) * _silu(z)
    y = jax.nn.gelu(y5_ref[0])
    o_b = y * _sigmoid(_bdot(y, gluw_ref[...]) + glub_ref[...])
    o_c = oc_ref[0]
    bg = bgate_ref[...]
    merged = (_sigmoid(g0_ref[0] + bg[:, 0:D_MODEL]) * _bdot(o_a, wbr_ref[0])
              + _sigmoid(g1_ref[0] + bg[:, D_MODEL:2 * D_MODEL]) * _bdot(o_b, wbr_ref[1])
              + _sigmoid(g2_ref[0] + bg[:, 2 * D_MODEL:]) * _bdot(o_c, wbr_ref[2]))
    hres = DEEPNORM_ALPHA * x_ref[0] + _bdot(merged, wout_ref[...])
    o_ref[0] = _layer_norm(hres, lng_ref[...], lnb_ref[...])


def _merge(x, o_f, o_b, proj, y5, o_c, b_gate, dn_norm, glu_w, glu_b, w_branch, w_out, ln_g, ln_b):
    b, s, _ = x.shape
    tile = min(TILE_MERGE, s)
    row = lambda w, cb=0: pl.BlockSpec((1, tile, w), lambda bi, i: (bi, i, cb))
    full = lambda shape: pl.BlockSpec(shape, lambda bi, i: (0,) * len(shape))
    gcb = COL_GATE // D_MODEL
    return pl.pallas_call(
        _merge_kernel,
        grid=(b, s // tile),
        in_specs=[row(D_MODEL), row(DN_WIDTH), row(DN_WIDTH), row(DN_WIDTH, COL_DN_Z // DN_WIDTH),
                  row(S5_WIDTH), row(DF_WIDTH),
                  row(D_MODEL, gcb), row(D_MODEL, gcb + 1), row(D_MODEL, gcb + 2),
                  full((1, N_BRANCH * D_MODEL)), full((1, DN_DIM)),
                  full((S5_WIDTH, S5_WIDTH)), full((1, S5_WIDTH)),
                  full((N_BRANCH, DN_WIDTH, D_MODEL)), full((D_MODEL, D_MODEL)),
                  full((1, D_MODEL)), full((1, D_MODEL))],
        out_specs=row(D_MODEL),
        out_shape=jax.ShapeDtypeStruct((b, s, D_MODEL), F32),
        compiler_params=_cparams("parallel", "parallel"),
        name="merge",
    )(x, o_f, o_b, proj, y5, o_c, proj, proj, proj, b_gate, dn_norm, glu_w, glu_b,
      w_branch, w_out, ln_g, ln_b)


def _xattn_kernel(x_ref, kv_ref, wq_ref, wo_ref, lng_ref, lnb_ref, o_ref):
    x = x_ref[0]
    q = _bdot(x, wq_ref[...])
    outs = []
    for h in range(XA_HEADS):
        qh = q[:, h * XA_DIM:(h + 1) * XA_DIM]
        kh = kv_ref[0, :, h * XA_DIM:(h + 1) * XA_DIM]
        vh = kv_ref[0, :, XA_WIDTH + h * XA_DIM:XA_WIDTH + (h + 1) * XA_DIM]
        sc = _bdot_nt(qh, kh) * (XA_DIM ** -0.5)
        sc = sc - jnp.max(sc, axis=-1, keepdims=True)
        e = jnp.exp(sc)
        pr = e / jnp.sum(e, axis=-1, keepdims=True)
        outs.append(_bdot(pr, vh))
    o = jnp.concatenate(outs, axis=1)
    hres = DEEPNORM_ALPHA * x + _bdot(o, wo_ref[...])
    o_ref[0] = _layer_norm(hres, lng_ref[...], lnb_ref[...])


def _xattn(x, kv, wq, wo, ln_g, ln_b):
    b, s, _ = x.shape
    tile = min(TILE_XA, s)
    mlen = kv.shape[1]
    row = pl.BlockSpec((1, tile, D_MODEL), lambda bi, i: (bi, i, 0))
    full = lambda shape: pl.BlockSpec(shape, lambda bi, i: (0,) * len(shape))
    return pl.pallas_call(
        _xattn_kernel,
        grid=(b, s // tile),
        in_specs=[row, pl.BlockSpec((1, mlen, 2 * XA_WIDTH), lambda bi, i: (bi, 0, 0)),
                  full((D_MODEL, XA_WIDTH)), full((XA_WIDTH, D_MODEL)),
                  full((1, D_MODEL)), full((1, D_MODEL))],
        out_specs=row,
        out_shape=jax.ShapeDtypeStruct((b, s, D_MODEL), F32),
        compiler_params=_cparams("parallel", "parallel"),
        name="xattn",
    )(x, kv, wq, wo, ln_g, ln_b)


def _router_kernel(x_ref, rw_ref, rb_ref, idx_ref, gate_ref, xb_ref):
    x = x_ref[...]
    xb_ref[...] = x.astype(BF16)
    logits = lax.dot_general(rw_ref[...], x, (((1,), (1,)), ((), ())), precision=HI,
                             preferred_element_type=F32) + rb_ref[...]
    tn = logits.shape[1]
    eid = lax.broadcasted_iota(jnp.int32, (N_EXPERTS, tn), 0)
    vals, ids = [], []
    cur = logits
    for _ in range(TOP_K):
        m = jnp.max(cur, axis=0, keepdims=True)
        sel = jnp.min(jnp.where(cur == m, eid, N_EXPERTS), axis=0, keepdims=True)
        vals.append(m)
        ids.append(sel)
        cur = jnp.where(eid == sel, -jnp.inf, cur)
    top_v = jnp.concatenate(vals, axis=0)
    e = jnp.exp(top_v - top_v[0:1])
    gate_ref[...] = e / jnp.sum(e, axis=0, keepdims=True)
    idx_ref[...] = jnp.concatenate(ids, axis=0)


def _router(x2d, router_w_t, router_b, row0, n):
    tile = min(TILE_ROUTER, n)
    t0 = row0 // tile
    return pl.pallas_call(
        _router_kernel,
        grid=(n // tile,),
        in_specs=[pl.BlockSpec((tile, D_MODEL), lambda i: (i + t0, 0)),
                  pl.BlockSpec((N_EXPERTS, D_MODEL), lambda i: (0, 0)),
                  pl.BlockSpec((N_EXPERTS, 1), lambda i: (0, 0))],
        out_specs=[pl.BlockSpec((TOP_K, tile), lambda i: (0, i)),
                   pl.BlockSpec((TOP_K, tile), lambda i: (0, i)),
                   pl.BlockSpec((tile, D_MODEL), lambda i: (i, 0))],
        out_shape=[jax.ShapeDtypeStruct((TOP_K, n), jnp.int32),
                   jax.ShapeDtypeStruct((TOP_K, n), F32),
                   jax.ShapeDtypeStruct((n, D_MODEL), BF16)],
        compiler_params=_cparams("parallel"),
        name="router",
    )(x2d, router_w_t, router_b)


def _rank_kernel(idx_ref, rank_ref, cnt_ref, run_ref, *, tile):
    @pl.when(pl.program_id(0) == 0)
    def _():
        run_ref[...] = jnp.zeros_like(run_ref)

    idx = idx_ref[...]
    eid = lax.broadcasted_iota(jnp.int32, (N_EXPERTS, tile), 0)
    si = lax.broadcasted_iota(jnp.int32, (tile, tile), 0)
    ti = lax.broadcasted_iota(jnp.int32, (tile, tile), 1)
    before = (si < ti).astype(BF16)
    run = run_ref[...]
    ranks = []
    for k in range(TOP_K):
        onehot = eid == idx[k:k + 1]
        oh = onehot.astype(BF16)
        within = jnp.dot(oh, before, preferred_element_type=F32)
        ranks.append(jnp.sum(jnp.where(onehot, within + run, 0.0), axis=0, keepdims=True))
        run = run + jnp.sum(onehot.astype(F32), axis=1, keepdims=True)
    rank_ref[...] = jnp.concatenate(ranks, axis=0).astype(jnp.int32)
    run_ref[...] = run
    cnt_ref[...] = run.astype(jnp.int32)


def _rank(top_i):
    _, n = top_i.shape
    tile = min(TILE_ROUTER, n)
    kern = functools.partial(_rank_kernel, tile=tile)
    return pl.pallas_call(
        kern,
        grid=(n // tile,),
        in_specs=[pl.BlockSpec((TOP_K, tile), lambda i: (0, i))],
        out_specs=[pl.BlockSpec((TOP_K, tile), lambda i: (0, i)),
                   pl.BlockSpec((N_EXPERTS, 1), lambda i: (0, 0))],
        out_shape=[jax.ShapeDtypeStruct((TOP_K, n), jnp.int32),
                   jax.ShapeDtypeStruct((N_EXPERTS, 1), jnp.int32)],
        scratch_shapes=[pltpu.VMEM((N_EXPERTS, 1), F32)],
        compiler_params=_cparams("arbitrary"),
        name="moe_rank",
    )(top_i)


def _expert_kernel(be_ref, on_ref, x_ref, wg_ref, wu_ref, bg_ref, bu_ref, wd_ref, bd_ref, o_ref,
                   wd_bf_ref):
    i = pl.program_id(0)
    live = on_ref[i] != 0

    @pl.when(jnp.logical_or(i == 0, be_ref[i] != be_ref[jnp.maximum(i - 1, 0)]))
    def _():
        wd_bf_ref[...] = wd_ref[0, 0].astype(BF16)

    @pl.when(live)
    def _():
        x = x_ref[...]
        hg = jnp.dot(x, wg_ref[0], preferred_element_type=F32) + bg_ref[0]
        hu = jnp.dot(x, wu_ref[0], preferred_element_type=F32) + bu_ref[0]
        gt = jnp.minimum(hg, SWIGLU_LIMIT)
        up = jnp.clip(hu, -SWIGLU_LIMIT, SWIGLU_LIMIT)
        act = (up + 1.0) * gt * _sigmoid(SWIGLU_ALPHA * gt)
        y = jnp.dot(act.astype(BF16), wd_bf_ref[...], preferred_element_type=F32) + bd_ref[0]
        o_ref[...] = y.astype(o_ref.dtype)

    @pl.when(jnp.logical_not(live))
    def _():
        o_ref[...] = jnp.zeros_like(o_ref)


def _experts(x_rows, blk_e, blk_on, w_g, w_u, b_g, b_u, w_d, l, b_d, *, block):
    rows, _ = x_rows.shape
    wspec = lambda k, m: pl.BlockSpec((1, k, m), lambda i, be, on: (be[i], 0, 0))
    grid_spec = pltpu.PrefetchScalarGridSpec(
        num_scalar_prefetch=2,
        grid=(rows // block,),
        in_specs=[pl.BlockSpec((block, D_MODEL), lambda i, be, on: (i, 0)),
                  wspec(D_MODEL, D_FF), wspec(D_MODEL, D_FF), wspec(1, D_FF), wspec(1, D_FF),
                  pl.BlockSpec((1, 1, D_FF, D_MODEL), lambda i, be, on: (l, be[i], 0, 0)),
                  wspec(1, D_MODEL)],
        out_specs=pl.BlockSpec((block, D_MODEL), lambda i, be, on: (i, 0)),
        scratch_shapes=[pltpu.VMEM((D_FF, D_MODEL), BF16)],
    )
    return pl.pallas_call(
        _expert_kernel,
        grid_spec=grid_spec,
        out_shape=jax.ShapeDtypeStruct((rows, D_MODEL), BF16),
        compiler_params=_cparams("arbitrary"),
        name="moe_experts",
    )(blk_e, blk_on, x_rows, w_g, w_u, b_g, b_u, w_d, b_d)


def _combine_kernel(x_ref, y_ref, gate_ref, lng_ref, lnb_ref, o_ref):
    gate = gate_ref[...]
    h = y_ref[0].astype(F32) * gate[:, 0:1]
    for k in range(1, TOP_K):
        h = h + y_ref[k].astype(F32) * gate[:, k:k + 1]
    o_ref[...] = _layer_norm(DEEPNORM_ALPHA * x_ref[...] + h, lng_ref[...], lnb_ref[...])


def _combine_kernel_into(prev_ref, *refs):
    del prev_ref
    _combine_kernel(*refs)


def _combine(x2d, y_g, gate_t, ln_g, ln_b, row0, prev):
    n, _ = x2d.shape
    ng = gate_t.shape[0]
    tile = min(TILE_COMBINE, ng)
    t0 = row0 // tile
    in_specs = [pl.BlockSpec((tile, D_MODEL), lambda i: (i + t0, 0)),
                pl.BlockSpec((TOP_K, tile, D_MODEL), lambda i: (0, i, 0)),
                pl.BlockSpec((tile, TOP_K), lambda i: (i, 0)),
                pl.BlockSpec((1, D_MODEL), lambda i: (0, 0)),
                pl.BlockSpec((1, D_MODEL), lambda i: (0, 0))]
    args = (x2d, y_g, gate_t, ln_g, ln_b)
    kern, aliases = _combine_kernel, {}
    if prev is not None:
        kern, aliases = _combine_kernel_into, {0: 0}
        in_specs = [pl.BlockSpec(memory_space=pl.ANY)] + in_specs
        args = (prev,) + args
    return pl.pallas_call(
        kern,
        grid=(ng // tile,),
        in_specs=in_specs,
        out_specs=pl.BlockSpec((tile, D_MODEL), lambda i: (i + t0, 0)),
        out_shape=jax.ShapeDtypeStruct((n, D_MODEL), F32),
        input_output_aliases=aliases,
        compiler_params=_cparams("parallel"),
        name="moe_combine",
    )(*args)


def _slot_kernel(dest_ref, out_ref, *, tile, rows):
    base = pl.program_id(0) * tile

    @pl.when(pl.program_id(0) == 0)
    def _():
        def zero(r, carry):
            out_ref[r] = 0
            return carry
        lax.fori_loop(0, rows, zero, 0, unroll=16)

    def body(t, carry):
        for k in range(TOP_K):
            out_ref[dest_ref[k, t]] = base + t
        return carry

    lax.fori_loop(0, tile, body, 0, unroll=8)


def _slot_tokens(dest, rows):
    _, n = dest.shape
    tile = min(TILE_SLOTS, n)
    kern = functools.partial(_slot_kernel, tile=tile, rows=rows)
    return pl.pallas_call(
        kern,
        grid=(n // tile,),
        in_specs=[pl.BlockSpec((TOP_K, tile), lambda i: (0, i), memory_space=pltpu.SMEM)],
        out_specs=pl.BlockSpec((rows,), lambda i: (0,), memory_space=pltpu.SMEM),
        out_shape=jax.ShapeDtypeStruct((rows,), jnp.int32),
        compiler_params=_cparams("arbitrary"),
        name="moe_slot_tokens",
    )(dest)


def _split_kernel(w_ref, sel_ref, g_ref, u_ref):
    w = w_ref[0, 0].astype(BF16)
    for j in range(w.shape[1] // (2 * LANES)):
        r = jnp.dot(w[:, 2 * LANES * j:2 * LANES * (j + 1)], sel_ref[...],
                    preferred_element_type=F32).astype(BF16)
        g_ref[0, :, LANES * j:LANES * (j + 1)] = r[:, :LANES]
        u_ref[0, :, LANES * j:LANES * (j + 1)] = r[:, LANES:]


def _split_gate_up(w_gu, l):
    _, e, dm, f2 = w_gu.shape
    f = f2 // 2
    tr = 256
    col = jnp.arange(2 * LANES, dtype=jnp.int32)[:, None]
    out = jnp.arange(2 * LANES, dtype=jnp.int32)[None, :]
    sel = jnp.where(out < LANES, col == 2 * out, col == 2 * (out - LANES) + 1).astype(BF16)
    ospec = pl.BlockSpec((1, tr, f), lambda ei, i: (ei, i, 0))
    return pl.pallas_call(
        _split_kernel,
        grid=(e, dm // tr),
        in_specs=[pl.BlockSpec((1, 1, tr, f2), lambda ei, i: (l, ei, i, 0)),
                  pl.BlockSpec((2 * LANES, 2 * LANES), lambda ei, i: (0, 0))],
        out_specs=[ospec, ospec],
        out_shape=[jax.ShapeDtypeStruct((e, dm, f), BF16)] * 2,
        compiler_params=_cparams("parallel", "parallel"),
        name="split_gate_up",
    )(w_gu, sel)


def _moe_group(x2d, row0, n, w, prev):
    dm = x2d.shape[1]
    top_i, gate, xb = _router(x2d, w['router_w_t'], w['router_b'], row0, n)
    rank, counts = _rank(top_i)
    block = MOE_BLOCK
    m = n * TOP_K
    nblk = -(-m // block) + N_EXPERTS
    rows = nblk * block
    counts = counts.reshape(N_EXPERTS)
    padded = (counts + block - 1) // block * block
    pad_end = jnp.cumsum(padded)
    pad_start = pad_end - padded
    experts = jnp.arange(N_EXPERTS, dtype=top_i.dtype)
    dest = rank + jnp.sum(jnp.where(top_i[:, :, None] == experts, pad_start.astype(jnp.int32), 0),
                          axis=-1)
    row_tok = _slot_tokens(dest, rows)
    blk_start = jnp.arange(nblk, dtype=pad_end.dtype) * block
    blk_e = jnp.minimum(jnp.sum(pad_end[None, :] <= blk_start[:, None], axis=1),
                        N_EXPERTS - 1).astype(jnp.int32)
    blk_on = (blk_start < pad_end[-1]).astype(jnp.int32)
    x_rows = xb.at[row_tok].get(mode='promise_in_bounds')
    y_rows = _experts(x_rows, blk_e, blk_on, w['e_w_g'], w['e_w_u'], w['e_b_g'], w['e_b_u'], w['e_w_d'],
                      w['layer'], w['e_b_d'], block=block)
    y_g = y_rows.at[dest].get(mode='promise_in_bounds')
    return _combine(x2d, y_g, gate.T, w['ln3_g'], w['ln3_b'], row0, prev)


def _moe(x, w):
    b, s, dm = x.shape
    n = b * s
    x2d = x.reshape(n, dm)
    groups = MOE_TOKEN_GROUPS if n % (MOE_TOKEN_GROUPS * TILE_ROUTER) == 0 else 1
    out = None
    for gi in range(groups):
        out = _moe_group(x2d, gi * (n // groups), n // groups, w, out)
    return out.reshape(b, s, dm)


def _rope_tables(s):
    half = DF_QK // 2
    inv_freq = jnp.power(ROPE_THETA, -jnp.arange(half, dtype=F32) / half)
    ang = jnp.arange(s, dtype=F32)[:, None] * inv_freq[None, :]
    cos = jnp.cos(ang)
    sin = jnp.sin(ang)
    cos_t = jnp.concatenate([cos, cos, cos, cos], axis=1)
    sin_t = jnp.concatenate([-sin, sin, -sin, sin], axis=1)
    return cos_t, sin_t


def _prepare_layer(l, p, s):
    dm = D_MODEL
    w = {}
    w['w_in'] = _pack_in_proj(p['w_in'][l]).astype(BF16)
    pad8 = jnp.zeros((8,), F32)
    pad112 = jnp.zeros((LANES - 16,), F32)
    w['nega'] = jnp.concatenate([pad8, -jnp.exp(p['dn_A_log'][l].reshape(8)), pad112]).reshape(1, LANES)
    w['dtb'] = jnp.concatenate([pad8, p['dn_dt_bias'][l].reshape(8), pad112]).reshape(1, LANES)
    w['dn_conv'] = p['dn_conv'][l]
    t5 = min(S5_CHUNK, s)
    w['s5'] = _s5_operators(p['s5_lam_re'][l], p['s5_lam_im'][l], p['s5_log_dt'][l],
                            p['s5_B_re'][l], p['s5_B_im'][l], p['s5_C_re'][l],
                            p['s5_C_im'][l], p['s5_D'][l], chunk=t5, nc=s // t5)
    for name in ('df_lq1', 'df_lk1', 'df_lq2', 'df_lk2'):
        w[name] = p[name][l].reshape(1, DF_QK)
    w['df_norm'] = p['df_norm'][l].reshape(1, DF_V)
    w['lam_init'] = 0.8 - 0.6 * math.exp(-0.3 * l)
    w['b_gate'] = p['b_gate'][l].reshape(1, -1)
    w['dn_norm'] = p['dn_norm'][l].reshape(1, DN_DIM)
    w['s5_glu_w'] = p['s5_glu_w'][l].astype(BF16)
    w['s5_glu_b'] = p['s5_glu_b'][l].reshape(1, -1)
    w['w_branch'] = p['w_branch'][l].astype(BF16)
    w['w_out'] = p['w_out'][l].astype(BF16)
    for name in ('ln1_g', 'ln1_b', 'ln2_g', 'ln2_b', 'ln3_g', 'ln3_b'):
        w[name] = p[name][l].reshape(1, dm)
    w['xa_wkv'] = p['xa_wkv'][l].astype(BF16)
    w['xa_wq'] = p['xa_wq'][l].astype(BF16)
    w['xa_wo'] = p['xa_wo'][l].astype(BF16)
    w['router_w_t'] = p['router_w'][l].T
    w['router_b'] = p['router_b'][l].reshape(N_EXPERTS, 1)
    w['e_w_g'], w['e_w_u'] = _split_gate_up(p['e_w_gu'], l)
    w['e_b_g'] = p['e_b_gu'][l][:, None, 0::2]
    w['e_b_u'] = p['e_b_gu'][l][:, None, 1::2]
    w['e_w_d'] = p['e_w_down']
    w['layer'] = l
    w['e_b_d'] = p['e_b_down'][l][:, None, :]
    return w


def _layer(x, mem, w, tabs):
    b, s, dm = x.shape
    n = b * s
    proj = _matmul(x.reshape(n, dm), w['w_in'], tm=TILE_MM, tn=LANES * 19)
    proj = proj.reshape(b, s, PROJ_WIDTH)

    qkv, bg = _dn_prep(proj, w['dn_conv'], w['nega'], w['dtb'], chunk=DN_CHUNK)
    bgr = bg[:, :, :16].reshape(b, s // DN_CHUNK, DN_CHUNK, 16).transpose(0, 1, 3, 2)
    o_f, o_b = _dn_scan(qkv, bg, bgr, chunk=DN_CHUNK)

    t5 = min(S5_CHUNK, s)
    nc = s // t5
    u5 = proj[:, :, COL_S5_U:COL_S5_U + S5_WIDTH].reshape(b, nc, t5, S5_GROUPS, S5_GROUP)
    u5 = u5.transpose(3, 0, 1, 2, 4).reshape(S5_GROUPS, b, nc, t5 * S5_GROUP).astype(BF16)
    y5 = _s5_ssm(u5, *w['s5'])
    y5 = y5.reshape(S5_GROUPS, b, nc, t5, S5_GROUP).transpose(1, 2, 3, 0, 4).reshape(b, s, S5_WIDTH)

    qa, qb, kk, vv = _rope_prep(proj, *tabs)
    o_c = _diff_attn(qa, qb, kk, vv, w['df_lq1'], w['df_lk1'], w['df_lq2'], w['df_lk2'],
                     w['df_norm'], w['lam_init'])

    x = _merge(x, o_f, o_b, proj, y5, o_c, w['b_gate'], w['dn_norm'], w['s5_glu_w'], w['s5_glu_b'],
               w['w_branch'], w['w_out'], w['ln1_g'], w['ln1_b'])

    mlen = mem.shape[1]
    kv = _matmul(mem.reshape(b * mlen, dm), w['xa_wkv'], tm=512, tn=1024)
    x = _xattn(x, kv.reshape(b, mlen, 2 * XA_WIDTH), w['xa_wq'], w['xa_wo'], w['ln2_g'], w['ln2_b'])
    return _moe(x, w)


def kernel(x_prompt, x_sample, mem_prompt, mem_sample, w_in, b_gate, dn_conv, dn_A_log, dn_dt_bias,
           dn_norm, s5_lam_re, s5_lam_im, s5_log_dt, s5_B_re, s5_B_im, s5_C_re, s5_C_im, s5_D,
           s5_glu_w, s5_glu_b, df_lq1, df_lk1, df_lq2, df_lk2, df_norm, w_branch, w_out,
           ln1_g, ln1_b, xa_wq, xa_wkv, xa_wo, ln2_g, ln2_b, router_w, router_b,
           e_w_gu, e_b_gu, e_w_down, e_b_down, ln3_g, ln3_b):
    p = dict(w_in=w_in, b_gate=b_gate, dn_conv=dn_conv, dn_A_log=dn_A_log, dn_dt_bias=dn_dt_bias,
             dn_norm=dn_norm, s5_lam_re=s5_lam_re, s5_lam_im=s5_lam_im, s5_log_dt=s5_log_dt,
             s5_B_re=s5_B_re, s5_B_im=s5_B_im, s5_C_re=s5_C_re, s5_C_im=s5_C_im, s5_D=s5_D,
             s5_glu_w=s5_glu_w, s5_glu_b=s5_glu_b, df_lq1=df_lq1, df_lk1=df_lk1, df_lq2=df_lq2,
             df_lk2=df_lk2, df_norm=df_norm, w_branch=w_branch, w_out=w_out, ln1_g=ln1_g,
             ln1_b=ln1_b, xa_wq=xa_wq, xa_wkv=xa_wkv, xa_wo=xa_wo, ln2_g=ln2_g, ln2_b=ln2_b,
             router_w=router_w, router_b=router_b, e_w_gu=e_w_gu, e_b_gu=e_b_gu,
             e_w_down=e_w_down, e_b_down=e_b_down, ln3_g=ln3_g, ln3_b=ln3_b)
    s = x_prompt.shape[1]
    assert x_sample.shape[1] == s
    tabs = _rope_tables(s)
    layers = [_prepare_layer(l, p, s) for l in range(DEPTH)]
    nb = x_prompt.shape[0]
    x = jnp.concatenate([x_prompt, x_sample], axis=0)
    mem = jnp.concatenate([mem_prompt, mem_sample], axis=0)
    for w in layers:
        x = _layer(x, mem, w, tabs)
    return (x[:nb], x[nb:])
```

```python
import functools
import math

import numpy as np
import jax
import jax.numpy as jnp
from jax import lax
from jax.experimental import pallas as pl
from jax.experimental.pallas import tpu as pltpu

F32 = jnp.float32
BF16 = jnp.bfloat16
HI = lax.Precision.HIGHEST

D_MODEL = 1024
DN_HEADS = 4
DN_DIM = 128
DN_WIDTH = DN_HEADS * DN_DIM
DN_CONV = 4
S5_GROUP = 16
S5_WIDTH = 512
S5_GROUPS = S5_WIDTH // S5_GROUP
S5_STATE = 64
DF_HEADS = 4
DF_QK = 64
DF_V = 128
DF_WIDTH = DF_HEADS * DF_V
ROPE_THETA = 10000.0
XA_HEADS = 4
XA_DIM = 128
XA_WIDTH = XA_HEADS * XA_DIM
N_EXPERTS = 32
TOP_K = 4
D_FF = 1024
SWIGLU_LIMIT = 7.0
SWIGLU_ALPHA = 1.702
N_BRANCH = 3
DEPTH = 2
DEEPNORM_ALPHA = (2 * DEPTH) ** 0.25
LN_EPS = 1e-5
RMS_EPS = 1e-6
LOG2E = 1.4426950408889634

LANES = 128
SUBLANES = 8
VMEM_LIMIT = 56 * 1024 * 1024

COL_DN_QKV = 0
COL_DN_Z = 1536
COL_S5_U = 2048
COL_QQ = 2560
COL_KK = 3072
COL_DV = 3584
COL_GATE = 4096
COL_BA = 7168
PROJ_WIDTH = 7296

TILE_MM = 1024
TILE_DN_PREP = 512
TILE_DN_SCAN = 256
DN_CHUNK = 128
DN_CHUNKS_PER_TRIP = 2
S5_CHUNK = 32
TILE_ROPE = 512
TILE_Q = 256
TILE_K = 512
TILE_MERGE = 256
TILE_XA = 512
TILE_ROUTER = 512
MOE_BLOCK = 512
MOE_TOKEN_GROUPS = 2
TILE_SLOTS = 2048
TILE_COMBINE = 256


def _in_proj_perm():
    o_dn_b, o_dn_a, o_s5 = 2048, 2056, 2064
    o_q1, o_q2, o_k1, o_k2, o_dv, o_gate = 2576, 2832, 3088, 3344, 3600, 4112
    perm = np.full((PROJ_WIDTH,), -1, np.int64)
    perm[0:2048] = np.arange(2048)
    perm[COL_S5_U:COL_S5_U + 512] = o_s5 + np.arange(512)
    for h in range(DF_HEADS):
        j = np.arange(DF_QK)
        perm[COL_QQ + h * 128 + j] = o_q1 + h * DF_QK + j
        perm[COL_QQ + h * 128 + 64 + j] = o_q2 + h * DF_QK + j
        perm[COL_KK + h * 128 + j] = o_k1 + h * DF_QK + j
        perm[COL_KK + h * 128 + 64 + j] = o_k2 + h * DF_QK + j
    perm[COL_DV:COL_DV + 512] = o_dv + np.arange(512)
    perm[COL_GATE:COL_GATE + 3072] = o_gate + np.arange(3072)
    perm[COL_BA:COL_BA + 8] = o_dn_b + np.arange(8)
    perm[COL_BA + 8:COL_BA + 16] = o_dn_a + np.arange(8)
    return perm


_IN_PERM = _in_proj_perm()


def _pack_in_proj(w_in):
    runs, start = [], 0
    for i in range(1, PROJ_WIDTH + 1):
        if i == PROJ_WIDTH or _IN_PERM[i] != _IN_PERM[i - 1] + (1 if _IN_PERM[i - 1] >= 0 else 0):
            runs.append((start, i))
            start = i
    pieces = []
    for a, b in runs:
        src = int(_IN_PERM[a])
        pieces.append(jnp.zeros((w_in.shape[0], b - a), w_in.dtype) if src < 0
                      else w_in[:, src:src + (b - a)])
    return jnp.concatenate(pieces, axis=1)


def _cparams(*sem):
    return pltpu.CompilerParams(dimension_semantics=sem, vmem_limit_bytes=VMEM_LIMIT)


def _bdot(a, b):
    return jnp.dot(a.astype(BF16), b.astype(BF16), preferred_element_type=F32)


def _bdot_nt(a, b):
    return lax.dot_general(a.astype(BF16), b.astype(BF16), (((1,), (1,)), ((), ())),
                           preferred_element_type=F32)


def _bdot_tn(a, b):
    return lax.dot_general(a.astype(BF16), b.astype(BF16), (((0,), (0,)), ((), ())),
                           preferred_element_type=F32)


def _sigmoid(x):
    return 0.5 * jnp.tanh(0.5 * x) + 0.5


def _silu(x):
    return x * _sigmoid(x)


def _mm_kernel(x_ref, w_ref, o_ref):
    o_ref[...] = jnp.dot(x_ref[...].astype(BF16), w_ref[...],
                         preferred_element_type=F32).astype(o_ref.dtype)


def _matmul(x, w, *, tm, tn, out_dtype=F32):
    n, k = x.shape
    m = w.shape[1]
    tm = min(tm, n)
    tn = min(tn, m)
    return pl.pallas_call(
        _mm_kernel,
        grid=(n // tm, m // tn),
        in_specs=[pl.BlockSpec((tm, k), lambda i, j: (i, 0)),
                  pl.BlockSpec((k, tn), lambda i, j: (0, j))],
        out_specs=pl.BlockSpec((tm, tn), lambda i, j: (i, j)),
        out_shape=jax.ShapeDtypeStruct((n, m), out_dtype),
        compiler_params=_cparams("parallel", "arbitrary"),
        name="matmul",
    )(x, w)


def _dn_prep_kernel(x_ref, prev_ref, next_ref, ba_ref, cw_ref, nega_ref, dtb_ref,
                    qkv_ref, bg_ref, *, tile, chunk):
    i = pl.program_id(1)
    last = pl.num_programs(1) - 1
    x = x_ref[0]
    prev = jnp.where(i > 0, prev_ref[0], 0.0)
    nxt = jnp.where(i < last, next_ref[0], 0.0)
    ext = jnp.concatenate([prev, x, nxt], axis=0)
    cw = cw_ref[...]
    acc = cw[0:1] * ext[6:6 + tile]
    for j in range(1, DN_CONV):
        acc = acc + cw[j:j + 1] * ext[6 + j:6 + j + tile]
    act = _silu(acc)
    for h in range(DN_HEADS):
        lo = h * DN_DIM
        qh = act[:, lo:lo + DN_DIM]
        qn = qh * lax.rsqrt(jnp.sum(qh * qh, axis=-1, keepdims=True) + 1e-6)
        qkv_ref[0, :, lo:lo + DN_DIM] = qn * (DN_DIM ** -0.5)
        kh = act[:, DN_WIDTH + lo:DN_WIDTH + lo + DN_DIM]
        qkv_ref[0, :, DN_WIDTH + lo:DN_WIDTH + lo + DN_DIM] = kh * lax.rsqrt(
            jnp.sum(kh * kh, axis=-1, keepdims=True) + 1e-6)
    qkv_ref[0, :, 2 * DN_WIDTH:] = act[:, 2 * DN_WIDTH:]

    ba = ba_ref[0]
    beta = _sigmoid(ba)
    z = ba + dtb_ref[...]
    softplus = jnp.maximum(z, 0.0) + jnp.log(1.0 + jnp.exp(-jnp.abs(z)))
    g = nega_ref[...] * softplus
    row = lax.broadcasted_iota(jnp.int32, (chunk, chunk), 0)
    col = lax.broadcasted_iota(jnp.int32, (chunk, chunk), 1)
    lower = (row >= col).astype(F32)
    upper = (row <= col).astype(F32)
    lane = lax.broadcasted_iota(jnp.int32, (chunk, LANES), 1)
    fwd_lane = lane < 8 + DN_HEADS
    for c in range(tile // chunk):
        gc = g[c * chunk:(c + 1) * chunk]
        cf = jnp.dot(lower, gc, precision=HI, preferred_element_type=F32)
        cb = jnp.dot(upper, gc, precision=HI, preferred_element_type=F32)
        gcum = jnp.where(fwd_lane, cf, cb)
        bg_ref[0, c * chunk:(c + 1) * chunk, :] = jnp.where(
            lane < 8, beta[c * chunk:(c + 1) * chunk], gcum)


def _dn_prep(proj, conv_w, nega, dtb, *, chunk):
    b, s, _ = proj.shape
    tile = min(TILE_DN_PREP, s)
    nt = s // tile
    r8 = tile // SUBLANES
    kern = functools.partial(_dn_prep_kernel, tile=tile, chunk=chunk)
    return pl.pallas_call(
        kern,
        grid=(b, nt),
        in_specs=[
            pl.BlockSpec((1, tile, 3 * DN_WIDTH), lambda bi, i: (bi, i, 0)),
            pl.BlockSpec((1, SUBLANES, 3 * DN_WIDTH),
                         lambda bi, i: (bi, jnp.maximum(i * r8 - 1, 0), 0)),
            pl.BlockSpec((1, SUBLANES, 3 * DN_WIDTH),
                         lambda bi, i: (bi, jnp.minimum((i + 1) * r8, s // SUBLANES - 1), 0)),
            pl.BlockSpec((1, tile, LANES), lambda bi, i: (bi, i, COL_BA // LANES)),
            pl.BlockSpec((DN_CONV, 3 * DN_WIDTH), lambda bi, i: (0, 0)),
            pl.BlockSpec((1, LANES), lambda bi, i: (0, 0)),
            pl.BlockSpec((1, LANES), lambda bi, i: (0, 0)),
        ],
        out_specs=[pl.BlockSpec((1, tile, 3 * DN_WIDTH), lambda bi, i: (bi, i, 0)),
                   pl.BlockSpec((1, tile, LANES), lambda bi, i: (bi, i, 0))],
        out_shape=[jax.ShapeDtypeStruct((b, s, 3 * DN_WIDTH), F32),
                   jax.ShapeDtypeStruct((b, s, LANES), F32)],
        compiler_params=_cparams("parallel", "parallel"),
        name="dn_prep",
    )(proj, proj, proj, proj, conv_w, nega, dtb)


def _dn_local(chains, chunk):
    ri = lax.broadcasted_iota(jnp.int32, (chunk, chunk), 0)
    ci = lax.broadcasted_iota(jnp.int32, (chunk, chunk), 1)
    pre = []
    for d, h, q, k, v, bgc, rows in chains:
        li = d * DN_HEADS + h
        beta = bgc[:, li:li + 1]
        gcol = bgc[:, 8 + li:8 + li + 1]
        grow = rows[8 + li:8 + li + 1, :]
        gtot = gcol[chunk - 1:chunk] if d == 0 else gcol[0:1]
        incl = (ri >= ci) if d == 0 else (ri <= ci)
        strict = (ri > ci) if d == 0 else (ri < ci)
        eg = jnp.exp(gcol)
        kb = k * beta
        decay = jnp.where(incl, jnp.exp(jnp.where(incl, gcol - grow, 0.0)), 0.0)
        pre.append(dict(k=k, qg=q * eg, kd=k * jnp.exp(gtot - gcol), decay=decay, strict=strict,
                        x2=jnp.concatenate([v * beta, kb * eg], axis=1), sdec=jnp.exp(gtot),
                        qkb=jnp.concatenate([q, kb], axis=0)))
    qk = [_bdot_nt(c['qkb'], c['k']) for c in pre]
    intra = [x[:chunk] * c['decay'] for x, c in zip(qk, pre)]
    p = [-jnp.where(c['strict'], x[chunk:] * c['decay'], 0.0) for x, c in zip(qk, pre)]
    r = p
    p = [_bdot(x, x) for x in p]
    for _ in range(int(math.log2(chunk)) - 2):
        rp = [_bdot(jnp.concatenate([ri_, pi_], axis=0), pi_) for ri_, pi_ in zip(r, p)]
        r = [ri_ + pi_ + x[:chunk] for ri_, pi_, x in zip(r, p, rp)]
        p = [x[chunk:] for x in rp]
    r = [ri_ + pi_ + _bdot(ri_, pi_) for ri_, pi_ in zip(r, p)]
    uw = [c['x2'] + _bdot(ri_, c['x2']) for ri_, c in zip(r, pre)]
    return [dict(u=x[:, :DN_DIM], wq=jnp.concatenate([x[:, DN_DIM:], c['qg']], axis=0),
                 intra=it, kd=c['kd'], sdec=c['sdec']) for x, it, c in zip(uw, intra, pre)]


def _dn_state(local, states, chunk):
    ws = [_bdot(c['wq'], st) for c, st in zip(local, states)]
    v_new = [c['u'] - y[:chunk] for c, y in zip(local, ws)]
    out = [y[chunk:] + _bdot(c['intra'], vn) for y, c, vn in zip(ws, local, v_new)]
    st_new = [st * c['sdec'] + _bdot_tn(c['kd'], vn) for st, c, vn in zip(states, local, v_new)]
    return out, st_new


def _dn_scan_kernel(xf_ref, xb_ref, gf_ref, gb_ref, rf_ref, rb_ref, of_ref, ob_ref, st_ref,
                    *, tile, chunk):
    @pl.when(pl.program_id(1) == 0)
    def _():
        st_ref[...] = jnp.zeros_like(st_ref)

    nct = tile // chunk
    nchain = 2 * DN_HEADS

    def body(j, carry):
        chains, where = [], []
        for slot in range(DN_CHUNKS_PER_TRIP):
            for d in range(2):
                c = DN_CHUNKS_PER_TRIP * j + slot
                c = c if d == 0 else nct - 1 - c
                r0 = pl.multiple_of(c * chunk, chunk)
                x_ref, g_ref, r_ref = (xf_ref, gf_ref, rf_ref) if d == 0 else (xb_ref, gb_ref, rb_ref)
                bgc = g_ref[0, pl.ds(r0, chunk), :]
                rows = r_ref[0, c]
                for h in range(DN_HEADS):
                    lo = h * DN_DIM
                    q = x_ref[0, pl.ds(r0, chunk), lo:lo + DN_DIM]
                    k = x_ref[0, pl.ds(r0, chunk), DN_WIDTH + lo:DN_WIDTH + lo + DN_DIM]
                    v = x_ref[0, pl.ds(r0, chunk), 2 * DN_WIDTH + lo:2 * DN_WIDTH + lo + DN_DIM]
                    chains.append((d, h, q, k, v, bgc, rows))
                    where.append((d, h, r0))
        states = [st_ref[i] for i in range(nchain)]
        local = _dn_local(chains, chunk)
        outs = []
        for slot in range(DN_CHUNKS_PER_TRIP):
            out, states = _dn_state(local[slot * nchain:(slot + 1) * nchain], states, chunk)
            outs += out
        for (d, h, r0), out in zip(where, outs):
            o_ref = of_ref if d == 0 else ob_ref
            o_ref[0, pl.ds(r0, chunk), h * DN_DIM:(h + 1) * DN_DIM] = out
        for i in range(nchain):
            st_ref[i] = states[i]
        return carry

    lax.fori_loop(0, nct // DN_CHUNKS_PER_TRIP, body, 0)


def _dn_scan(qkv, bg, bgr, *, chunk):
    b, s, _ = qkv.shape
    tile = min(TILE_DN_SCAN, s)
    nt = s // tile
    nct = tile // chunk
    kern = functools.partial(_dn_scan_kernel, tile=tile, chunk=chunk)
    fwd = lambda bi, i: (bi, i, 0)
    bwd = lambda bi, i: (bi, nt - 1 - i, 0)
    fwd4 = lambda bi, i: (bi, i, 0, 0)
    bwd4 = lambda bi, i: (bi, nt - 1 - i, 0, 0)
    return pl.pallas_call(
        kern,
        grid=(b, nt),
        in_specs=[pl.BlockSpec((1, tile, 3 * DN_WIDTH), fwd),
                  pl.BlockSpec((1, tile, 3 * DN_WIDTH), bwd),
                  pl.BlockSpec((1, tile, LANES), fwd),
                  pl.BlockSpec((1, tile, LANES), bwd),
                  pl.BlockSpec((1, nct, 16, chunk), fwd4),
                  pl.BlockSpec((1, nct, 16, chunk), bwd4)],
        out_specs=[pl.BlockSpec((1, tile, DN_WIDTH), fwd),
                   pl.BlockSpec((1, tile, DN_WIDTH), bwd)],
        out_shape=[jax.ShapeDtypeStruct((b, s, DN_WIDTH), F32),
                   jax.ShapeDtypeStruct((b, s, DN_WIDTH), F32)],
        scratch_shapes=[pltpu.VMEM((2 * DN_HEADS, DN_DIM, DN_DIM), F32)],
        compiler_params=_cparams("parallel", "arbitrary"),
        name="dn_scan",
    )(qkv, qkv, bg, bg, bgr, bgr)


def _s5_operators(lam_re, lam_im, log_dt, b_re, b_im, c_re, c_im, d_skip, *, chunk, nc):
    t = chunk
    levels = max(1, int(math.log2(nc)))
    tau = jnp.arange(t + 1, dtype=F32)
    kern = []
    lamp = []
    for d in range(2):
        lam = lax.complex(lam_re[d], lam_im[d])
        dt = jnp.exp(log_dt[d])[:, None]
        lam_dt = lam * dt
        lam_bar = jnp.exp(lam_dt)
        bbar = ((lam_bar - 1.0) / lam)[:, :, None] * lax.complex(b_re[d], b_im[d])
        cc = lax.complex(c_re[d], c_im[d])
        pw = jnp.exp(lam_dt[:, None, :] * tau[None, :, None].astype(jnp.complex64))
        kern.append((pw, bbar, cc))
        steps = (t * 2.0 ** jnp.arange(levels, dtype=F32)).astype(jnp.complex64)
        lamp.append(jnp.exp(lam_dt[:, None, :] * steps[None, :, None]))
    (pwf, bbf, ccf), (pwb, bbb, ccb) = kern
    ef = jnp.einsum('gsp,gpc->gscp', pwf[:, t - 1::-1, :], bbf)
    eb = jnp.einsum('gsp,gpc->gscp', pwb[:, :t, :], bbb)
    w1 = jnp.concatenate([jnp.real(ef), jnp.real(eb), jnp.imag(ef), jnp.imag(eb)], axis=-1)
    w1 = w1.reshape(S5_GROUPS, t * S5_GROUP, 4 * S5_STATE)
    kf = jnp.real(jnp.einsum('gcp,gtp,gpd->gtcd', ccf, pwf[:, :t, :], bbf))
    kb = jnp.real(jnp.einsum('gcp,gtp,gpd->gtcd', ccb, pwb[:, :t, :], bbb))
    si = jnp.arange(t)[:, None]
    ti = jnp.arange(t)[None, :]
    lag_f = jnp.clip(ti - si, 0, t - 1)
    lag_b = jnp.clip(si - ti, 0, t - 1)
    toep = (jnp.where((ti >= si)[None, :, :, None, None], kf[:, lag_f], 0.0)
            + jnp.where((si >= ti)[None, :, :, None, None], kb[:, lag_b], 0.0))
    eye_t = jnp.eye(t, dtype=F32)[None, :, :, None, None]
    eye_c = jnp.eye(S5_GROUP, dtype=F32)[None, None, None, :, :]
    toep = toep + eye_t * eye_c * d_skip.reshape(S5_GROUPS, 1, 1, S5_GROUP, 1)
    w_intra = toep.transpose(0, 1, 4, 2, 3).reshape(S5_GROUPS, t * S5_GROUP, t * S5_GROUP)
    mf = jnp.einsum('gcp,gtp->gptc', ccf, pwf[:, 1:t + 1, :])
    mb = jnp.einsum('gcp,gtp->gptc', ccb, pwb[:, t:0:-1, :])
    w_state = jnp.concatenate([jnp.real(mf), jnp.real(mb), -jnp.imag(mf), -jnp.imag(mb)], axis=1)
    w_state = w_state.reshape(S5_GROUPS, 4 * S5_STATE, t * S5_GROUP)
    w3 = jnp.concatenate([w_intra, w_state], axis=1)
    lam_r = jnp.concatenate([jnp.real(lamp[0]), jnp.real(lamp[1])], axis=-1)
    lam_i = jnp.concatenate([jnp.imag(lamp[0]), jnp.imag(lamp[1])], axis=-1)
    lamp = jnp.stack([lam_r, lam_i], axis=2)
    return w1.astype(BF16), w3.astype(BF16), lamp.astype(F32)


def _s5_kernel(u_ref, w1_ref, w3_ref, lam_ref, y_ref, *, nc, width, levels):
    u = u_ref[0, 0]
    loc = jnp.dot(u, w1_ref[0], preferred_element_type=F32)
    re = loc[:, :2 * S5_STATE]
    im = loc[:, 2 * S5_STATE:]
    row = lax.broadcasted_iota(jnp.int32, (nc, 2 * S5_STATE), 0)
    fwd = lax.broadcasted_iota(jnp.int32, (nc, 2 * S5_STATE), 1) < S5_STATE

    def shifted(x, d):
        down = jnp.where(row >= d, pltpu.roll(x, d, 0), 0.0)
        up = jnp.where(row < nc - d, pltpu.roll(x, nc - d, 0), 0.0)
        return jnp.where(fwd, down, up)

    for k in range(levels):
        d = 2 ** k
        if d >= nc:
            break
        lr = lam_ref[0, k, 0:1, :]
        li = lam_ref[0, k, 1:2, :]
        sr = shifted(re, d)
        si = shifted(im, d)
        re, im = re + sr * lr - si * li, im + sr * li + si * lr
    ent = jnp.concatenate([shifted(re, 1), shifted(im, 1)], axis=1)
    y = jnp.dot(u, w3_ref[0, :width, :], preferred_element_type=F32)
    y = y + jnp.dot(ent.astype(BF16), w3_ref[0, width:, :], preferred_element_type=F32)
    y_ref[0, 0] = y


def _s5_ssm(u5, w1, w3, lamp):
    g, b, nc, width = u5.shape
    levels = lamp.shape[1]
    kern = functools.partial(_s5_kernel, nc=nc, width=width, levels=levels)
    return pl.pallas_call(
        kern,
        grid=(g, b),
        in_specs=[pl.BlockSpec((1, 1, nc, width), lambda gi, bi: (gi, bi, 0, 0)),
                  pl.BlockSpec((1, width, 4 * S5_STATE), lambda gi, bi: (gi, 0, 0)),
                  pl.BlockSpec((1, width + 4 * S5_STATE, width), lambda gi, bi: (gi, 0, 0)),
                  pl.BlockSpec((1, levels, 2, 2 * S5_STATE), lambda gi, bi: (gi, 0, 0, 0))],
        out_specs=pl.BlockSpec((1, 1, nc, width), lambda gi, bi: (gi, bi, 0, 0)),
        out_shape=jax.ShapeDtypeStruct((g, b, nc, width), F32),
        compiler_params=_cparams("parallel", "parallel"),
        name="s5_ssm",
    )(u5, w1, w3, lamp)


def _rope_kernel(qq_ref, kk_ref, v_ref, cos_ref, sin_ref, qa_ref, qb_ref, ko_ref, vo_ref):
    width = DF_HEADS * 2 * DF_QK
    cos = jnp.concatenate([cos_ref[...]] * (width // LANES), axis=1)
    sin = jnp.concatenate([sin_ref[...]] * (width // LANES), axis=1)
    lane = lax.broadcasted_iota(jnp.int32, (1, width), 1)
    first_half = (lane % DF_QK) < (DF_QK // 2)

    def rot(x):
        partner = jnp.where(first_half, pltpu.roll(x, width - DF_QK // 2, 1),
                            pltpu.roll(x, DF_QK // 2, 1))
        return x * cos + partner * sin

    q = rot(qq_ref[0]) * (DF_QK ** -0.5 * LOG2E)
    is_a = (lane % (2 * DF_QK)) < DF_QK
    qa_ref[0] = jnp.where(is_a, q, 0.0).astype(BF16)
    qb_ref[0] = jnp.where(is_a, 0.0, q).astype(BF16)
    ko_ref[0] = rot(kk_ref[0]).astype(BF16)
    vo_ref[0] = v_ref[0].astype(BF16)


def _rope_prep(proj, cos_t, sin_t):
    b, s, _ = proj.shape
    tile = min(TILE_ROPE, s)
    w = DF_WIDTH
    spec = lambda cb: pl.BlockSpec((1, tile, w), lambda bi, i: (bi, i, cb))
    out = pl.BlockSpec((1, tile, w), lambda bi, i: (bi, i, 0))
    tab = pl.BlockSpec((tile, LANES), lambda bi, i: (i, 0))
    return pl.pallas_call(
        _rope_kernel,
        grid=(b, s // tile),
        in_specs=[spec(COL_QQ // w), spec(COL_KK // w), spec(COL_DV // w), tab, tab],
        out_specs=[out, out, out, out],
        out_shape=[jax.ShapeDtypeStruct((b, s, w), BF16)] * 4,
        compiler_params=_cparams("parallel", "parallel"),
        name="rope_prep",
    )(proj, proj, proj, cos_t, sin_t)


def _diff_attn_kernel(qa_ref, qb_ref, k_ref, v_ref, lq1_ref, lk1_ref, lq2_ref, lk2_ref, ng_ref,
                      o_ref, s_ref, *, tq, tk, nk, lam_init):
    q2 = (qa_ref[0], qb_ref[0])
    nt_dims = (((1,), (1,)), ((), ()))

    def fold(x, op, init):
        for j in range(tk // LANES):
            init = op(init, x[:, j * LANES:(j + 1) * LANES])
        return init

    m = [jnp.full((tq, LANES), -jnp.inf, F32)] * 2
    for c in range(nk):
        kc = k_ref[0, c * tk:(c + 1) * tk, :]
        for i in range(2):
            sc = lax.dot_general(q2[i], kc, nt_dims, preferred_element_type=F32)
            s_ref[c, i * tq:(i + 1) * tq, :] = sc
            m[i] = fold(sc, jnp.maximum, m[i])
    m = [jnp.max(x, axis=-1, keepdims=True) for x in m]

    acc = [jnp.zeros((tq, 2 * DF_V), F32)] * 2
    ones = jnp.ones((tk, DF_V), BF16)
    for c in range(nk):
        vc = jnp.concatenate([v_ref[0, c * tk:(c + 1) * tk, :], ones], axis=1)
        for i in range(2):
            e = jnp.exp2((s_ref[c, i * tq:(i + 1) * tq, :] - m[i]).astype(BF16))
            acc[i] = acc[i] + jnp.dot(e, vc, preferred_element_type=F32)
    lam = (jnp.exp(jnp.sum(lq1_ref[...] * lk1_ref[...], axis=-1, keepdims=True))
           - jnp.exp(jnp.sum(lq2_ref[...] * lk2_ref[...], axis=-1, keepdims=True)) + lam_init)
    acc = (acc[0][:, :DF_V] * (1.0 / acc[0][:, DF_V:DF_V + 1])
           - acc[1][:, :DF_V] * (lam / acc[1][:, DF_V:DF_V + 1]))
    ms = jnp.mean(acc * acc, axis=-1, keepdims=True)
    o_ref[0] = acc * lax.rsqrt(ms + RMS_EPS) * ng_ref[...] * (1.0 - lam_init)


def _diff_attn(qa, qb, kk, vv, lq1, lk1, lq2, lk2, norm_g, lam_init):
    b, s, _ = qa.shape
    tq = min(TILE_Q, s)
    tk = min(TILE_K, s)
    nk = s // tk
    kern = functools.partial(_diff_attn_kernel, tq=tq, tk=tk, nk=nk, lam_init=lam_init)
    qspec = pl.BlockSpec((1, tq, LANES), lambda bi, h, i: (bi, i, h))
    kspec = pl.BlockSpec((1, s, LANES), lambda bi, h, i: (bi, 0, h))
    vec64 = pl.BlockSpec((1, DF_QK), lambda bi, h, i: (0, 0))
    return pl.pallas_call(
        kern,
        grid=(b, DF_HEADS, s // tq),
        in_specs=[qspec, qspec, kspec, kspec, vec64, vec64, vec64, vec64,
                  pl.BlockSpec((1, DF_V), lambda bi, h, i: (0, 0))],
        out_specs=pl.BlockSpec((1, tq, DF_V), lambda bi, h, i: (bi, i, h)),
        out_shape=jax.ShapeDtypeStruct((b, s, DF_WIDTH), F32),
        scratch_shapes=[pltpu.VMEM((nk, 2 * tq, tk), F32)],
        compiler_params=_cparams("parallel", "parallel", "arbitrary"),
        name="diff_attn",
    )(qa, qb, kk, vv, lq1, lk1, lq2, lk2, norm_g)


def _layer_norm(x, g, b):
    mu = jnp.mean(x, axis=-1, keepdims=True)
    xc = x - mu
    var = jnp.mean(xc * xc, axis=-1, keepdims=True)
    return xc * lax.rsqrt(var + LN_EPS) * g + b


def _merge_kernel(x_ref, of_ref, ob_ref, z_ref, y5_ref, oc_ref, g0_ref, g1_ref, g2_ref,
                  bgate_ref, dnn_ref, gluw_ref, glub_ref, wbr_ref, wout_ref, lng_ref, lnb_ref,
                  o_ref):
    o = of_ref[0] + ob_ref[0]
    z = z_ref[0]
    parts = []
    for h in range(DN_HEADS):
        oh = o[:, h * DN_DIM:(h + 1) * DN_DIM]
        ms = jnp.mean(oh * oh, axis=-1, keepdims=True)
        parts.append(oh * lax.rsqrt(ms + RMS_EPS) * dnn_ref[...])
    o_a = jnp.concatenate(parts, axis=1) * _silu(z)
    y = jax.nn.gelu(y5_ref[0])
    o_b = y * _sigmoid(_bdot(y, gluw_ref[...]) + glub_ref[...])
    o_c = oc_ref[0]
    bg = bgate_ref[...]
    merged = (_sigmoid(g0_ref[0] + bg[:, 0:D_MODEL]) * _bdot(o_a, wbr_ref[0])
              + _sigmoid(g1_ref[0] + bg[:, D_MODEL:2 * D_MODEL]) * _bdot(o_b, wbr_ref[1])
              + _sigmoid(g2_ref[0] + bg[:, 2 * D_MODEL:]) * _bdot(o_c, wbr_ref[2]))
    hres = DEEPNORM_ALPHA * x_ref[0] + _bdot(merged, wout_ref[...])
    o_ref[0] = _layer_norm(hres, lng_ref[...], lnb_ref[...])


def _merge(x, o_f, o_b, proj, y5, o_c, b_gate, dn_norm, glu_w, glu_b, w_branch, w_out, ln_g, ln_b):
    b, s, _ = x.shape
    tile = min(TILE_MERGE, s)
    row = lambda w, cb=0: pl.BlockSpec((1, tile, w), lambda bi, i: (bi, i, cb))
    full = lambda shape: pl.BlockSpec(shape, lambda bi, i: (0,) * len(shape))
    gcb = COL_GATE // D_MODEL
    return pl.pallas_call(
        _merge_kernel,
        grid=(b, s // tile),
        in_specs=[row(D_MODEL), row(DN_WIDTH), row(DN_WIDTH), row(DN_WIDTH, COL_DN_Z // DN_WIDTH),
                  row(S5_WIDTH), row(DF_WIDTH),
                  row(D_MODEL, gcb), row(D_MODEL, gcb + 1), row(D_MODEL, gcb + 2),
                  full((1, N_BRANCH * D_MODEL)), full((1, DN_DIM)),
                  full((S5_WIDTH, S5_WIDTH)), full((1, S5_WIDTH)),
                  full((N_BRANCH, DN_WIDTH, D_MODEL)), full((D_MODEL, D_MODEL)),
                  full((1, D_MODEL)), full((1, D_MODEL))],
        out_specs=row(D_MODEL),
        out_shape=jax.ShapeDtypeStruct((b, s, D_MODEL), F32),
        compiler_params=_cparams("parallel", "parallel"),
        name="merge",
    )(x, o_f, o_b, proj, y5, o_c, proj, proj, proj, b_gate, dn_norm, glu_w, glu_b,
      w_branch, w_out, ln_g, ln_b)


def _xattn_kernel(x_ref, kv_ref, wq_ref, wo_ref, lng_ref, lnb_ref, o_ref):
    x = x_ref[0]
    q = _bdot(x, wq_ref[...])
    outs = []
    for h in range(XA_HEADS):
        qh = q[:, h * XA_DIM:(h + 1) * XA_DIM]
        kh = kv_ref[0, :, h * XA_DIM:(h + 1) * XA_DIM]
        vh = kv_ref[0, :, XA_WIDTH + h * XA_DIM:XA_WIDTH + (h + 1) * XA_DIM]
        sc = _bdot_nt(qh, kh) * (XA_DIM ** -0.5)
        sc = sc - jnp.max(sc, axis=-1, keepdims=True)
        e = jnp.exp(sc)
        pr = e / jnp.sum(e, axis=-1, keepdims=True)
        outs.append(_bdot(pr, vh))
    o = jnp.concatenate(outs, axis=1)
    hres = DEEPNORM_ALPHA * x + _bdot(o, wo_ref[...])
    o_ref[0] = _layer_norm(hres, lng_ref[...], lnb_ref[...])


def _xattn(x, kv, wq, wo, ln_g, ln_b):
    b, s, _ = x.shape
    tile = min(TILE_XA, s)
    mlen = kv.shape[1]
    row = pl.BlockSpec((1, tile, D_MODEL), lambda bi, i: (bi, i, 0))
    full = lambda shape: pl.BlockSpec(shape, lambda bi, i: (0,) * len(shape))
    return pl.pallas_call(
        _xattn_kernel,
        grid=(b, s // tile),
        in_specs=[row, pl.BlockSpec((1, mlen, 2 * XA_WIDTH), lambda bi, i: (bi, 0, 0)),
                  full((D_MODEL, XA_WIDTH)), full((XA_WIDTH, D_MODEL)),
                  full((1, D_MODEL)), full((1, D_MODEL))],
        out_specs=row,
        out_shape=jax.ShapeDtypeStruct((b, s, D_MODEL), F32),
        compiler_params=_cparams("parallel", "parallel"),
        name="xattn",
    )(x, kv, wq, wo, ln_g, ln_b)


def _router_kernel(x_ref, rw_ref, rb_ref, idx_ref, gate_ref, xb_ref):
    x = x_ref[...]
    xb_ref[...] = x.astype(BF16)
    logits = lax.dot_general(rw_ref[...], x, (((1,), (1,)), ((), ())), precision=HI,
                             preferred_element_type=F32) + rb_ref[...]
    tn = logits.shape[1]
    eid = lax.broadcasted_iota(jnp.int32, (N_EXPERTS, tn), 0)
    vals, ids = [], []
    cur = logits
    for _ in range(TOP_K):
        m = jnp.max(cur, axis=0, keepdims=True)
        sel = jnp.min(jnp.where(cur == m, eid, N_EXPERTS), axis=0, keepdims=True)
        vals.append(m)
        ids.append(sel)
        cur = jnp.where(eid == sel, -jnp.inf, cur)
    top_v = jnp.concatenate(vals, axis=0)
    e = jnp.exp(top_v - top_v[0:1])
    gate_ref[...] = e / jnp.sum(e, axis=0, keepdims=True)
    idx_ref[...] = jnp.concatenate(ids, axis=0)


def _router(x2d, router_w_t, router_b, row0, n):
    tile = min(TILE_ROUTER, n)
    t0 = row0 // tile
    return pl.pallas_call(
        _router_kernel,
        grid=(n // tile,),
        in_specs=[pl.BlockSpec((tile, D_MODEL), lambda i: (i + t0, 0)),
                  pl.BlockSpec((N_EXPERTS, D_MODEL), lambda i: (0, 0)),
                  pl.BlockSpec((N_EXPERTS, 1), lambda i: (0, 0))],
        out_specs=[pl.BlockSpec((TOP_K, tile), lambda i: (0, i)),
                   pl.BlockSpec((TOP_K, tile), lambda i: (0, i)),
                   pl.BlockSpec((tile, D_MODEL), lambda i: (i, 0))],
        out_shape=[jax.ShapeDtypeStruct((TOP_K, n), jnp.int32),
                   jax.ShapeDtypeStruct((TOP_K, n), F32),
                   jax.ShapeDtypeStruct((n, D_MODEL), BF16)],
        compiler_params=_cparams("parallel"),
        name="router",
    )(x2d, router_w_t, router_b)


def _rank_kernel(idx_ref, rank_ref, cnt_ref, run_ref, *, tile):
    @pl.when(pl.program_id(0) == 0)
    def _():
        run_ref[...] = jnp.zeros_like(run_ref)

    idx = idx_ref[...]
    eid = lax.broadcasted_iota(jnp.int32, (N_EXPERTS, tile), 0)
    si = lax.broadcasted_iota(jnp.int32, (tile, tile), 0)
    ti = lax.broadcasted_iota(jnp.int32, (tile, tile), 1)
    before = (si < ti).astype(BF16)
    run = run_ref[...]
    ranks = []
    for k in range(TOP_K):
        onehot = eid == idx[k:k + 1]
        oh = onehot.astype(BF16)
        within = jnp.dot(oh, before, preferred_element_type=F32)
        ranks.append(jnp.sum(jnp.where(onehot, within + run, 0.0), axis=0, keepdims=True))
        run = run + jnp.sum(onehot.astype(F32), axis=1, keepdims=True)
    rank_ref[...] = jnp.concatenate(ranks, axis=0).astype(jnp.int32)
    run_ref[...] = run
    cnt_ref[...] = run.astype(jnp.int32)


def _rank(top_i):
    _, n = top_i.shape
    tile = min(TILE_ROUTER, n)
    kern = functools.partial(_rank_kernel, tile=tile)
    return pl.pallas_call(
        kern,
        grid=(n // tile,),
        in_specs=[pl.BlockSpec((TOP_K, tile), lambda i: (0, i))],
        out_specs=[pl.BlockSpec((TOP_K, tile), lambda i: (0, i)),
                   pl.BlockSpec((N_EXPERTS, 1), lambda i: (0, 0))],
        out_shape=[jax.ShapeDtypeStruct((TOP_K, n), jnp.int32),
                   jax.ShapeDtypeStruct((N_EXPERTS, 1), jnp.int32)],
        scratch_shapes=[pltpu.VMEM((N_EXPERTS, 1), F32)],
        compiler_params=_cparams("arbitrary"),
        name="moe_rank",
    )(top_i)


def _expert_kernel(be_ref, on_ref, x_ref, wg_ref, wu_ref, bg_ref, bu_ref, wd_ref, bd_ref, o_ref,
                   wd_bf_ref):
    i = pl.program_id(0)
    live = on_ref[i] != 0

    @pl.when(jnp.logical_or(i == 0, be_ref[i] != be_ref[jnp.maximum(i - 1, 0)]))
    def _():
        wd_bf_ref[...] = wd_ref[0, 0].astype(BF16)

    @pl.when(live)
    def _():
        x = x_ref[...]
        hg = jnp.dot(x, wg_ref[0], preferred_element_type=F32) + bg_ref[0]
        hu = jnp.dot(x, wu_ref[0], preferred_element_type=F32) + bu_ref[0]
        gt = jnp.minimum(hg, SWIGLU_LIMIT)
        up = jnp.clip(hu, -SWIGLU_LIMIT, SWIGLU_LIMIT)
        act = (up + 1.0) * gt * _sigmoid(SWIGLU_ALPHA * gt)
        y = jnp.dot(act.astype(BF16), wd_bf_ref[...], preferred_element_type=F32) + bd_ref[0]
        o_ref[...] = y.astype(o_ref.dtype)

    @pl.when(jnp.logical_not(live))
    def _():
        o_ref[...] = jnp.zeros_like(o_ref)


def _experts(x_rows, blk_e, blk_on, w_g, w_u, b_g, b_u, w_d, l, b_d, *, block):
    rows, _ = x_rows.shape
    wspec = lambda k, m: pl.BlockSpec((1, k, m), lambda i, be, on: (be[i], 0, 0))
    grid_spec = pltpu.PrefetchScalarGridSpec(
        num_scalar_prefetch=2,
        grid=(rows // block,),
        in_specs=[pl.BlockSpec((block, D_MODEL), lambda i, be, on: (i, 0)),
                  wspec(D_MODEL, D_FF), wspec(D_MODEL, D_FF), wspec(1, D_FF), wspec(1, D_FF),
                  pl.BlockSpec((1, 1, D_FF, D_MODEL), lambda i, be, on: (l, be[i], 0, 0)),
                  wspec(1, D_MODEL)],
        out_specs=pl.BlockSpec((block, D_MODEL), lambda i, be, on: (i, 0)),
        scratch_shapes=[pltpu.VMEM((D_FF, D_MODEL), BF16)],
    )
    return pl.pallas_call(
        _expert_kernel,
        grid_spec=grid_spec,
        out_shape=jax.ShapeDtypeStruct((rows, D_MODEL), BF16),
        compiler_params=_cparams("arbitrary"),
        name="moe_experts",
    )(blk_e, blk_on, x_rows, w_g, w_u, b_g, b_u, w_d, b_d)


def _combine_kernel(x_ref, y_ref, gate_ref, lng_ref, lnb_ref, o_ref):
    gate = gate_ref[...]
    h = y_ref[0].astype(F32) * gate[:, 0:1]
    for k in range(1, TOP_K):
        h = h + y_ref[k].astype(F32) * gate[:, k:k + 1]
    o_ref[...] = _layer_norm(DEEPNORM_ALPHA * x_ref[...] + h, lng_ref[...], lnb_ref[...])


def _combine_kernel_into(prev_ref, *refs):
    del prev_ref
    _combine_kernel(*refs)


def _combine(x2d, y_g, gate_t, ln_g, ln_b, row0, prev):
    n, _ = x2d.shape
    ng = gate_t.shape[0]
    tile = min(TILE_COMBINE, ng)
    t0 = row0 // tile
    in_specs = [pl.BlockSpec((tile, D_MODEL), lambda i: (i + t0, 0)),
                pl.BlockSpec((TOP_K, tile, D_MODEL), lambda i: (0, i, 0)),
                pl.BlockSpec((tile, TOP_K), lambda i: (i, 0)),
                pl.BlockSpec((1, D_MODEL), lambda i: (0, 0)),
                pl.BlockSpec((1, D_MODEL), lambda i: (0, 0))]
    args = (x2d, y_g, gate_t, ln_g, ln_b)
    kern, aliases = _combine_kernel, {}
    if prev is not None:
        kern, aliases = _combine_kernel_into, {0: 0}
        in_specs = [pl.BlockSpec(memory_space=pl.ANY)] + in_specs
        args = (prev,) + args
    return pl.pallas_call(
        kern,
        grid=(ng // tile,),
        in_specs=in_specs,
        out_specs=pl.BlockSpec((tile, D_MODEL), lambda i: (i + t0, 0)),
        out_shape=jax.ShapeDtypeStruct((n, D_MODEL), F32),
        input_output_aliases=aliases,
        compiler_params=_cparams("parallel"),
        name="moe_combine",
    )(*args)


def _slot_kernel(dest_ref, out_ref, *, tile, rows):
    base = pl.program_id(0) * tile

    @pl.when(pl.program_id(0) == 0)
    def _():
        def zero(r, carry):
            out_ref[r] = 0
            return carry
        lax.fori_loop(0, rows, zero, 0, unroll=16)

    def body(t, carry):
        for k in range(TOP_K):
            out_ref[dest_ref[k, t]] = base + t
        return carry

    lax.fori_loop(0, tile, body, 0, unroll=8)


def _slot_tokens(dest, rows):
    _, n = dest.shape
    tile = min(TILE_SLOTS, n)
    kern = functools.partial(_slot_kernel, tile=tile, rows=rows)
    return pl.pallas_call(
        kern,
        grid=(n // tile,),
        in_specs=[pl.BlockSpec((TOP_K, tile), lambda i: (0, i), memory_space=pltpu.SMEM)],
        out_specs=pl.BlockSpec((rows,), lambda i: (0,), memory_space=pltpu.SMEM),
        out_shape=jax.ShapeDtypeStruct((rows,), jnp.int32),
        compiler_params=_cparams("arbitrary"),
        name="moe_slot_tokens",
    )(dest)


def _split_kernel(w_ref, sel_ref, g_ref, u_ref):
    w = w_ref[0, 0].astype(BF16)
    for j in range(w.shape[1] // (2 * LANES)):
        r = jnp.dot(w[:, 2 * LANES * j:2 * LANES * (j + 1)], sel_ref[...],
                    preferred_element_type=F32).astype(BF16)
        g_ref[0, :, LANES * j:LANES * (j + 1)] = r[:, :LANES]
        u_ref[0, :, LANES * j:LANES * (j + 1)] = r[:, LANES:]


def _split_gate_up(w_gu, l):
    _, e, dm, f2 = w_gu.shape
    f = f2 // 2
    tr = 256
    col = jnp.arange(2 * LANES, dtype=jnp.int32)[:, None]
    out = jnp.arange(2 * LANES, dtype=jnp.int32)[None, :]
    sel = jnp.where(out < LANES, col == 2 * out, col == 2 * (out - LANES) + 1).astype(BF16)
    ospec = pl.BlockSpec((1, tr, f), lambda ei, i: (ei, i, 0))
    return pl.pallas_call(
        _split_kernel,
        grid=(e, dm // tr),
        in_specs=[pl.BlockSpec((1, 1, tr, f2), lambda ei, i: (l, ei, i, 0)),
                  pl.BlockSpec((2 * LANES, 2 * LANES), lambda ei, i: (0, 0))],
        out_specs=[ospec, ospec],
        out_shape=[jax.ShapeDtypeStruct((e, dm, f), BF16)] * 2,
        compiler_params=_cparams("parallel", "parallel"),
        name="split_gate_up",
    )(w_gu, sel)


def _moe_group(x2d, row0, n, w, prev):
    dm = x2d.shape[1]
    top_i, gate, xb = _router(x2d, w['router_w_t'], w['router_b'], row0, n)
    rank, counts = _rank(top_i)
    block = MOE_BLOCK
    m = n * TOP_K
    nblk = -(-m // block) + N_EXPERTS
    rows = nblk * block
    counts = counts.reshape(N_EXPERTS)
    padded = (counts + block - 1) // block * block
    pad_end = jnp.cumsum(padded)
    pad_start = pad_end - padded
    experts = jnp.arange(N_EXPERTS, dtype=top_i.dtype)
    dest = rank + jnp.sum(jnp.where(top_i[:, :, None] == experts, pad_start.astype(jnp.int32), 0),
                          axis=-1)
    row_tok = _slot_tokens(dest, rows)
    blk_start = jnp.arange(nblk, dtype=pad_end.dtype) * block
    blk_e = jnp.minimum(jnp.sum(pad_end[None, :] <= blk_start[:, None], axis=1),
                        N_EXPERTS - 1).astype(jnp.int32)
    blk_on = (blk_start < pad_end[-1]).astype(jnp.int32)
    x_rows = xb.at[row_tok].get(mode='promise_in_bounds')
    y_rows = _experts(x_rows, blk_e, blk_on, w['e_w_g'], w['e_w_u'], w['e_b_g'], w['e_b_u'], w['e_w_d'],
                      w['layer'], w['e_b_d'], block=block)
    y_g = y_rows.at[dest].get(mode='promise_in_bounds')
    return _combine(x2d, y_g, gate.T, w['ln3_g'], w['ln3_b'], row0, prev)


def _moe(x, w):
    b, s, dm = x.shape
    n = b * s
    x2d = x.reshape(n, dm)
    groups = MOE_TOKEN_GROUPS if n % (MOE_TOKEN_GROUPS * TILE_ROUTER) == 0 else 1
    out = None
    for gi in range(groups):
        out = _moe_group(x2d, gi * (n // groups), n // groups, w, out)
    return out.reshape(b, s, dm)


def _rope_tables(s):
    half = DF_QK // 2
    inv_freq = jnp.power(ROPE_THETA, -jnp.arange(half, dtype=F32) / half)
    ang = jnp.arange(s, dtype=F32)[:, None] * inv_freq[None, :]
    cos = jnp.cos(ang)
    sin = jnp.sin(ang)
    cos_t = jnp.concatenate([cos, cos, cos, cos], axis=1)
    sin_t = jnp.concatenate([-sin, sin, -sin, sin], axis=1)
    return cos_t, sin_t


def _prepare_layer(l, p, s):
    dm = D_MODEL
    w = {}
    w['w_in'] = _pack_in_proj(p['w_in'][l]).astype(BF16)
    pad8 = jnp.zeros((8,), F32)
    pad112 = jnp.zeros((LANES - 16,), F32)
    w['nega'] = jnp.concatenate([pad8, -jnp.exp(p['dn_A_log'][l].reshape(8)), pad112]).reshape(1, LANES)
    w['dtb'] = jnp.concatenate([pad8, p['dn_dt_bias'][l].reshape(8), pad112]).reshape(1, LANES)
    w['dn_conv'] = p['dn_conv'][l]
    t5 = min(S5_CHUNK, s)
    w['s5'] = _s5_operators(p['s5_lam_re'][l], p['s5_lam_im'][l], p['s5_log_dt'][l],
                            p['s5_B_re'][l], p['s5_B_im'][l], p['s5_C_re'][l],
                            p['s5_C_im'][l], p['s5_D'][l], chunk=t5, nc=s // t5)
    for name in ('df_lq1', 'df_lk1', 'df_lq2', 'df_lk2'):
        w[name] = p[name][l].reshape(1, DF_QK)
    w['df_norm'] = p['df_norm'][l].reshape(1, DF_V)
    w['lam_init'] = 0.8 - 0.6 * math.exp(-0.3 * l)
    w['b_gate'] = p['b_gate'][l].reshape(1, -1)
    w['dn_norm'] = p['dn_norm'][l].reshape(1, DN_DIM)
    w['s5_glu_w'] = p['s5_glu_w'][l].astype(BF16)
    w['s5_glu_b'] = p['s5_glu_b'][l].reshape(1, -1)
    w['w_branch'] = p['w_branch'][l].astype(BF16)
    w['w_out'] = p['w_out'][l].astype(BF16)
    for name in ('ln1_g', 'ln1_b', 'ln2_g', 'ln2_b', 'ln3_g', 'ln3_b'):
        w[name] = p[name][l].reshape(1, dm)
    w['xa_wkv'] = p['xa_wkv'][l].astype(BF16)
    w['xa_wq'] = p['xa_wq'][l].astype(BF16)
    w['xa_wo'] = p['xa_wo'][l].astype(BF16)
    w['router_w_t'] = p['router_w'][l].T
    w['router_b'] = p['router_b'][l].reshape(N_EXPERTS, 1)
    w['e_w_g'], w['e_w_u'] = _split_gate_up(p['e_w_gu'], l)
    w['e_b_g'] = p['e_b_gu'][l][:, None, 0::2]
    w['e_b_u'] = p['e_b_gu'][l][:, None, 1::2]
    w['e_w_d'] = p['e_w_down']
    w['layer'] = l
    w['e_b_d'] = p['e_b_down'][l][:, None, :]
    return w


def _layer(x, mem, w, tabs):
    b, s, dm = x.shape
    n = b * s
    proj = _matmul(x.reshape(n, dm), w['w_in'], tm=TILE_MM, tn=LANES * 19)
    proj = proj.reshape(b, s, PROJ_WIDTH)

    qkv, bg = _dn_prep(proj, w['dn_conv'], w['nega'], w['dtb'], chunk=DN_CHUNK)
    bgr = bg[:, :, :16].reshape(b, s // DN_CHUNK, DN_CHUNK, 16).transpose(0, 1, 3, 2)
    o_f, o_b = _dn_scan(qkv, bg, bgr, chunk=DN_CHUNK)

    t5 = min(S5_CHUNK, s)
    nc = s // t5
    u5 = proj[:, :, COL_S5_U:COL_S5_U + S5_WIDTH].reshape(b, nc, t5, S5_GROUPS, S5_GROUP)
    u5 = u5.transpose(3, 0, 1, 2, 4).reshape(S5_GROUPS, b, nc, t5 * S5_GROUP).astype(BF16)
    y5 = _s5_ssm(u5, *w['s5'])
    y5 = y5.reshape(S5_GROUPS, b, nc, t5, S5_GROUP).transpose(1, 2, 3, 0, 4).reshape(b, s, S5_WIDTH)

    qa, qb, kk, vv = _rope_prep(proj, *tabs)
    o_c = _diff_attn(qa, qb, kk, vv, w['df_lq1'], w['df_lk1'], w['df_lq2'], w['df_lk2'],
                     w['df_norm'], w['lam_init'])

    x = _merge(x, o_f, o_b, proj, y5, o_c, w['b_gate'], w['dn_norm'], w['s5_glu_w'], w['s5_glu_b'],
               w['w_branch'], w['w_out'], w['ln1_g'], w['ln1_b'])

    mlen = mem.shape[1]
    kv = _matmul(mem.reshape(b * mlen, dm), w['xa_wkv'], tm=512, tn=1024)
    x = _xattn(x, kv.reshape(b, mlen, 2 * XA_WIDTH), w['xa_wq'], w['xa_wo'], w['ln2_g'], w['ln2_b'])
    return _moe(x, w)


def kernel(x_prompt, x_sample, mem_prompt, mem_sample, w_in, b_gate, dn_conv, dn_A_log, dn_dt_bias,
           dn_norm, s5_lam_re, s5_lam_im, s5_log_dt, s5_B_re, s5_B_im, s5_C_re, s5_C_im, s5_D,
           s5_glu_w, s5_glu_b, df_lq1, df_lk1, df_lq2, df_lk2, df_norm, w_branch, w_out,
           ln1_g, ln1_b, xa_wq, xa_wkv, xa_wo, ln2_g, ln2_b, router_w, router_b,
           e_w_gu, e_b_gu, e_w_down, e_b_down, ln3_g, ln3_b):
    p = dict(w_in=w_in, b_gate=b_gate, dn_conv=dn_conv, dn_A_log=dn_A_log, dn_dt_bias=dn_dt_bias,
             dn_norm=dn_norm, s5_lam_re=s5_lam_re, s5_lam_im=s5_lam_im, s5_log_dt=s5_log_dt,
             s5_B_re=s5_B_re, s5_B_im=s5_B_im, s5_C_re=s5_C_re, s5_C_im=s5_C_im, s5_D=s5_D,
             s5_glu_w=s5_glu_w, s5_glu_b=s5_glu_b, df_lq1=df_lq1, df_lk1=df_lk1, df_lq2=df_lq2,
             df_lk2=df_lk2, df_norm=df_norm, w_branch=w_branch, w_out=w_out, ln1_g=ln1_g,
             ln1_b=ln1_b, xa_wq=xa_wq, xa_wkv=xa_wkv, xa_wo=xa_wo, ln2_g=ln2_g, ln2_b=ln2_b,
             router_w=router_w, router_b=router_b, e_w_gu=e_w_gu, e_b_gu=e_b_gu,
             e_w_down=e_w_down, e_b_down=e_b_down, ln3_g=ln3_g, ln3_b=ln3_b)
    s = x_prompt.shape[1]
    assert x_sample.shape[1] == s
    tabs = _rope_tables(s)
    layers = [_prepare_layer(l, p, s) for l in range(DEPTH)]
    nb = x_prompt.shape[0]
    x = jnp.concatenate([x_prompt, x_sample], axis=0)
    mem = jnp.concatenate([mem_prompt, mem_sample], axis=0)
    for w in layers:
        x = _layer(x, mem, w, tabs)
    return (x[:nb], x[nb:])
```
